```python
import math
import jax, jax.numpy as jnp
from jax import lax
import numpy as np

D_MODEL = 1024
BATCH = 2
SEQ = 8192
DEPTH = 2
DEC_BATCH = 32
DEC_SEQ = 8
PAST_LEN = 16384
PAGE_SIZE = 128

D_PLE = 256
RET_HEADS = 4
RET_DK = 128
RET_DV = 256
RET_CHUNK = 128
ATT_GROUPS = ((128, 1), (512, 4), (2048, 16))
N_GROUPS = 3
HPG = 4
ATT_HEAD_DIM = 128
ATT_HEADS = N_GROUPS * HPG
Q_BLOCK = 128
D_FF = 4 * D_MODEL
EPS = 1e-6

RET_QK_W = RET_HEADS * RET_DK
RET_V_W = RET_HEADS * RET_DV
ATT_W = ATT_HEADS * ATT_HEAD_DIM
ATT_OUT_W = HPG * ATT_HEAD_DIM
D_IN = 2 * RET_QK_W + 2 * RET_V_W + 3 * ATT_W + 2 * D_MODEL

kernel_name = 'hybrid_retention_dilated_attn_decode_step'


def _split_points():
    sizes = (RET_QK_W, RET_QK_W, RET_V_W, RET_V_W, ATT_W, ATT_W, ATT_W, D_MODEL, D_MODEL)
    pts, acc = [], 0
    for s in sizes[:-1]:
        acc += s
        pts.append(acc)
    return pts


def _ret_log_gamma():
    return jnp.log1p(-jnp.exp(jnp.linspace(math.log(1.0 / 32), math.log(1.0 / 512), RET_HEADS))).astype(jnp.float32)


def _alibi_slopes():
    return (2.0 ** (-8.0 * (jnp.arange(ATT_HEADS, dtype=jnp.float32) + 1.0) / ATT_HEADS)).astype(jnp.float32)


def rmsnorm(x, g):
    xf = x.astype(jnp.float32)
    y = xf * lax.rsqrt(jnp.mean(xf * xf, axis=-1, keepdims=True) + EPS) * g.astype(jnp.float32)
    return y.astype(x.dtype)


def head_layernorm(o):
    of = o.astype(jnp.float32)
    mu = jnp.mean(of, axis=-1, keepdims=True)
    var = jnp.mean((of - mu) ** 2, axis=-1, keepdims=True)
    return ((of - mu) * lax.rsqrt(var + EPS)).astype(o.dtype)


def retention(q, k, v, state0):
    B, T, H, DK = q.shape
    DV = v.shape[-1]
    dt = v.dtype
    C = math.gcd(T, RET_CHUNK)
    NC = T // C
    lg = _ret_log_gamma()
    pos = jnp.arange(C, dtype=jnp.float32)
    diff = pos[:, None] - pos[None, :]
    intra = jnp.where(diff[None] >= 0, jnp.exp(lg[:, None, None] * jnp.maximum(diff, 0.0)[None]), 0.0)
    xi = jnp.exp(lg[:, None] * (pos[None] + 1.0)).T
    zeta = jnp.exp(lg[:, None] * (C - 1.0 - pos)[None]).T
    chunk_decay = jnp.exp(lg * C).astype(dt)
    qc = q.reshape(B, NC, C, H, DK)
    kc = (k * (DK ** -0.5)).reshape(B, NC, C, H, DK)
    vc = v.reshape(B, NC, C, H, DV)
    scores = jnp.einsum('bnihd,bnjhd->bnhij', qc, kc) * intra.astype(dt)[None, None]
    o_intra = jnp.einsum('bnhij,bnjhv->bnihv', scores, vc)
    u = jnp.einsum('bnjhd,bnjhv->bnhdv', kc, vc * zeta.astype(dt)[None, None, :, :, None])

    def step(R, u_c):
        return chunk_decay[None, :, None, None] * R + u_c, R

    R_final, R_prev = lax.scan(step, state0.astype(dt), jnp.moveaxis(u, 1, 0))
    R_prev = jnp.moveaxis(R_prev, 0, 1)
    o_cross = jnp.einsum('bnihd,bnhdv->bnihv', qc * xi.astype(dt)[None, None, :, :, None], R_prev)
    return (o_intra + o_cross).reshape(B, T, H, DV), R_final


def dilated_group_attention(q, k, v, q_start, window, dil, slopes):
    B, Tq, H, Dh = q.shape
    n_taps = window // dil + 1
    QB = math.gcd(Tq, Q_BLOCK)
    nb = Tq // QB
    taps = jnp.arange(n_taps, dtype=jnp.int32) * dil
    alibi = -slopes[:, None] * taps.astype(jnp.float32)[None]
    qb = jnp.moveaxis(q.reshape(B, nb, QB, H, Dh), 1, 0)
    starts = q_start + jnp.arange(nb, dtype=jnp.int32) * QB
    scale = Dh ** -0.5

    def block(args):
        qblk, s0 = args
        qi = s0 + jnp.arange(QB, dtype=jnp.int32)
        idx = qi[:, None] - taps[None, :]
        valid = idx >= 0
        idx = jnp.maximum(idx, 0)
        kg = jnp.take(k, idx, axis=1)
        vg = jnp.take(v, idx, axis=1)
        s = jnp.einsum('bqhd,bqnhd->bhqn', qblk, kg).astype(jnp.float32) * scale + alibi[None, :, None, :]
        s = jnp.where(valid[None, None], s, -jnp.inf)
        m = jnp.max(s, axis=-1, keepdims=True)
        e = jnp.exp(s - m)
        den = jnp.sum(e, axis=-1, keepdims=True)
        o = jnp.einsum('bhqn,bqnhd->bqhd', (e / den).astype(v.dtype), vg)
        lse = (m + jnp.log(den))[..., 0]
        return o, lse

    o, lse = lax.map(block, (qb, starts))
    o = jnp.moveaxis(o, 0, 1).reshape(B, Tq, H, Dh)
    lse = jnp.transpose(lse, (1, 0, 3, 2)).reshape(B, Tq, H)
    return o, lse


def layer(x, p_l, ret_state0, win_bufs, lw):
    (norm_mix, w_in, w_ret_br, w_att_br, w_out, norm_ffn, w_up, w_down, w_ple, w_ple_gate) = lw
    B, T, _ = x.shape
    dt = x.dtype
    h = rmsnorm(x, norm_mix)
    z = h @ w_in
    rq, rk, rv, rg, aq, ak, av, ga, gb = jnp.split(z, _split_points(), axis=-1)
    o_ret, ret_state = retention(rq.reshape(B, T, RET_HEADS, RET_DK), rk.reshape(B, T, RET_HEADS, RET_DK),
                                 rv.reshape(B, T, RET_HEADS, RET_DV), ret_state0)
    o_ret = head_layernorm(o_ret).reshape(B, T, RET_V_W) * jax.nn.silu(rg)
    br_ret = o_ret @ w_ret_br
    aq = aq.reshape(B, T, N_GROUPS, HPG, ATT_HEAD_DIM)
    ak = ak.reshape(B, T, N_GROUPS, HPG, ATT_HEAD_DIM)
    av = av.reshape(B, T, N_GROUPS, HPG, ATT_HEAD_DIM)
    slopes = _alibi_slopes()
    outs, lses, new_bufs = [], [], []
    for g, (W, dil) in enumerate(ATT_GROUPS):
        kg, vg = ak[:, :, g], av[:, :, g]
        if win_bufs is None:
            k_all, v_all, q_start = kg, vg, 0
        else:
            kb, vb = win_bufs[g]
            k_all = jnp.concatenate([kb.astype(dt), kg], axis=1)
            v_all = jnp.concatenate([vb.astype(dt), vg], axis=1)
            q_start = kb.shape[1]
        o_g, lse_g = dilated_group_attention(aq[:, :, g], k_all, v_all, q_start, W, dil,
                                             slopes[g * HPG:(g + 1) * HPG])
        outs.append(o_g)
        lses.append(lse_g)
        L = min(W, k_all.shape[1])
        new_bufs.append((k_all[:, -L:], v_all[:, -L:]))
    wts = jax.nn.softmax(jnp.stack(lses, axis=0), axis=0)
    o_att = jnp.einsum('gbth,gbthd->bthd', wts, jnp.stack(outs, axis=0).astype(jnp.float32)).astype(dt)
    br_att = o_att.reshape(B, T, ATT_OUT_W) @ w_att_br
    x = x + (jax.nn.sigmoid(ga) * br_ret + jax.nn.sigmoid(gb) * br_att) @ w_out
    u = jax.nn.relu(rmsnorm(x, norm_ffn) @ w_up)
    x = x + (u * u) @ w_down
    x = x + (p_l @ w_ple) * jax.nn.sigmoid(x @ w_ple_gate)
    return x, ret_state, new_bufs


def setup_inputs(seed: int = 0) -> dict:
    key = jax.random.key(seed)
    ks = jax.random.split(key, 24)
    f32 = jnp.float32
    nrm = lambda k, s, sc=1.0: jax.random.normal(k, s, f32) * sc
    d = {}
    d['x_prompt'] = nrm(ks[0], (BATCH, SEQ, D_MODEL))
    d['x_sample'] = nrm(ks[1], (DEC_BATCH, DEC_SEQ, D_MODEL))
    for g, (W, dil) in enumerate(ATT_GROUPS):
        L = min(W, PAST_LEN)
        d['cache_win_k%d' % g] = nrm(ks[2 + 2 * g], (DEPTH, DEC_BATCH, L, HPG, ATT_HEAD_DIM))
        d['cache_win_v%d' % g] = nrm(ks[3 + 2 * g], (DEPTH, DEC_BATCH, L, HPG, ATT_HEAD_DIM))
    d['state_ret'] = nrm(ks[8], (DEPTH, DEC_BATCH, RET_HEADS, RET_DK, RET_DV))
    d['p_prompt'] = nrm(ks[9], (DEPTH, BATCH, SEQ, D_PLE))
    d['p_sample'] = nrm(ks[10], (DEPTH, DEC_BATCH, DEC_SEQ, D_PLE))
    d['norm_mix'] = 1.0 + nrm(ks[11], (DEPTH, D_MODEL), 0.02)
    d['w_in'] = nrm(ks[12], (DEPTH, D_MODEL, D_IN), D_MODEL ** -0.5)
    d['w_ret_br'] = nrm(ks[13], (DEPTH, RET_V_W, D_MODEL), RET_V_W ** -0.5)
    d['w_att_br'] = nrm(ks[14], (DEPTH, ATT_OUT_W, D_MODEL), ATT_OUT_W ** -0.5)
    d['w_out'] = nrm(ks[15], (DEPTH, D_MODEL, D_MODEL), D_MODEL ** -0.5)
    d['norm_ffn'] = 1.0 + nrm(ks[16], (DEPTH, D_MODEL), 0.02)
    d['w_up'] = nrm(ks[17], (DEPTH, D_MODEL, D_FF), D_MODEL ** -0.5)
    d['w_down'] = nrm(ks[18], (DEPTH, D_FF, D_MODEL), D_FF ** -0.5)
    d['w_ple'] = nrm(ks[19], (DEPTH, D_PLE, D_MODEL), D_PLE ** -0.5)
    d['w_ple_gate'] = nrm(ks[20], (DEPTH, D_MODEL, D_MODEL), D_MODEL ** -0.5)
    d['norm_final'] = 1.0 + nrm(ks[21], (D_MODEL,), 0.02)
    return d


def reference(x_prompt, x_sample, cache_win_k0, cache_win_v0, cache_win_k1, cache_win_v1, cache_win_k2, cache_win_v2,
              state_ret, p_prompt, p_sample, norm_mix, w_in, w_ret_br, w_att_br, w_out, norm_ffn, w_up, w_down,
              w_ple, w_ple_gate, norm_final):
    xp, xs = x_prompt, x_sample
    cache_k = (cache_win_k0, cache_win_k1, cache_win_k2)
    cache_v = (cache_win_v0, cache_win_v1, cache_win_v2)
    pk = [[] for _ in range(N_GROUPS)]
    pv = [[] for _ in range(N_GROUPS)]
    sk = [[] for _ in range(N_GROUPS)]
    sv = [[] for _ in range(N_GROUPS)]
    pr, sr = [], []
    for i in range(DEPTH):
        lw = (norm_mix[i], w_in[i], w_ret_br[i], w_att_br[i], w_out[i], norm_ffn[i], w_up[i], w_down[i],
              w_ple[i], w_ple_gate[i])
        zero_state = jnp.zeros((xp.shape[0], RET_HEADS, RET_DK, RET_DV), xp.dtype)
        xp, rp, bp = layer(xp, p_prompt[i], zero_state, None, lw)
        bufs = [(cache_k[g][i], cache_v[g][i]) for g in range(N_GROUPS)]
        xs, rs, bs = layer(xs, p_sample[i], state_ret[i], bufs, lw)
        pr.append(rp)
        sr.append(rs)
        for g in range(N_GROUPS):
            pk[g].append(bp[g][0]); pv[g].append(bp[g][1])
            sk[g].append(bs[g][0]); sv[g].append(bs[g][1])
    y_prompt = rmsnorm(xp, norm_final)
    y_sample = rmsnorm(xs, norm_final)
    prompt_win_k0, prompt_win_v0 = jnp.stack(pk[0]), jnp.stack(pv[0])
    prompt_win_k1, prompt_win_v1 = jnp.stack(pk[1]), jnp.stack(pv[1])
    prompt_win_k2, prompt_win_v2 = jnp.stack(pk[2]), jnp.stack(pv[2])
    sample_win_k0, sample_win_v0 = jnp.stack(sk[0]), jnp.stack(sv[0])
    sample_win_k1, sample_win_v1 = jnp.stack(sk[1]), jnp.stack(sv[1])
    sample_win_k2, sample_win_v2 = jnp.stack(sk[2]), jnp.stack(sv[2])
    prompt_ret = jnp.stack(pr)
    sample_ret = jnp.stack(sr)
    return (y_prompt, y_sample, prompt_win_k0, prompt_win_v0, prompt_win_k1, prompt_win_v1, prompt_win_k2, prompt_win_v2,
            prompt_ret, sample_win_k0, sample_win_v0, sample_win_k1, sample_win_v1, sample_win_k2, sample_win_v2,
            sample_ret)
```

```python
import functools
import math

import jax
import jax.numpy as jnp
import numpy as np
from jax import lax
from jax.experimental import pallas as pl
from jax.experimental.pallas import tpu as pltpu

D_MODEL = 1024
D_PLE = 256
RET_HEADS = 4
RET_DK = 128
RET_DV = 256
RET_CHUNK = 128
ATT_GROUPS = ((128, 1), (512, 4), (2048, 16))
N_GROUPS = 3
HPG = 4
ATT_HEAD_DIM = 128
ATT_HEADS = N_GROUPS * HPG
ATT_TAPS = 128
Q_BLOCK = 128
D_FF = 4 * D_MODEL
EPS = 1e-6

RET_QK_W = RET_HEADS * RET_DK
RET_V_W = RET_HEADS * RET_DV
GROUP_W = HPG * ATT_HEAD_DIM
ATT_W = ATT_HEADS * ATT_HEAD_DIM
D_IN = 2 * RET_QK_W + 2 * RET_V_W + 3 * ATT_W + 2 * D_MODEL

COL_GATES = 0
COL_RQ = 2 * D_MODEL
COL_RK = COL_RQ + RET_QK_W
COL_RV = COL_RK + RET_QK_W
COL_RG = COL_RV + RET_V_W
COL_AQ = COL_RG + RET_V_W
COL_AK = COL_AQ + ATT_W
COL_AV = COL_AK + ATT_W
COL_BLOCK = 512
N_COL_BLOCKS = D_IN // COL_BLOCK

LSE_LANES = 128
VMEM_LIMIT = 48 * 1024 * 1024

_BF16 = jnp.bfloat16
_F32 = jnp.float32


def _params(*sem):
    return pltpu.CompilerParams(dimension_semantics=sem, vmem_limit_bytes=VMEM_LIMIT)


def _rmsnorm(x, g):
    return x * lax.rsqrt(jnp.mean(x * x, axis=-1, keepdims=True) + EPS) * g


def _dot(a, b):
    return jnp.dot(a, b, preferred_element_type=_F32)


def _dot_nt(a, b):
    return lax.dot_general(a, b, (((1,), (1,)), ((), ())), preferred_element_type=_F32)


def _dot_tn(a, b):
    return lax.dot_general(a, b, (((0,), (0,)), ((), ())), preferred_element_type=_F32)


def _norm_proj_kernel(x_ref, g_ref, w_ref, z_ref, h_ref):
    @pl.when(pl.program_id(1) == 0)
    def _():
        h_ref[...] = _rmsnorm(x_ref[...], g_ref[...]).astype(_BF16)

    z_ref[...] = _dot(h_ref[...], w_ref[...])


def _norm_proj(x, g, w, *, tm, tn=COL_BLOCK):
    n, d = x.shape
    d_out = w.shape[1]
    return pl.pallas_call(
        _norm_proj_kernel,
        grid=(n // tm, d_out // tn),
        in_specs=[
            pl.BlockSpec((tm, d), lambda i, j: (i, 0)),
            pl.BlockSpec((1, d), lambda i, j: (0, 0)),
            pl.BlockSpec((d, tn), lambda i, j: (0, j)),
        ],
        out_specs=pl.BlockSpec((tm, tn), lambda i, j: (i, j)),
        out_shape=jax.ShapeDtypeStruct((n, d_out), _F32),
        scratch_shapes=[pltpu.VMEM((tm, d), _BF16)],
        compiler_params=_params("parallel", "arbitrary"),
        name="norm_proj",
    )(x, g, w)


def _ret_log_gamma():
    return jnp.log1p(-jnp.exp(jnp.linspace(math.log(1.0 / 32), math.log(1.0 / 512), RET_HEADS))).astype(_F32)


def _retention_tables(chunk):
    lg = _ret_log_gamma()
    pos = jnp.arange(chunk, dtype=_F32)
    diff = pos[:, None] - pos[None, :]
    intra = jnp.where(diff[None] >= 0, jnp.exp(lg[:, None, None] * jnp.maximum(diff, 0.0)[None]), 0.0)
    xi = jnp.exp(lg[:, None] * (pos[None] + 1.0))
    zeta = jnp.exp(lg[:, None] * (chunk - 1.0 - pos)[None])
    decay = jnp.exp(lg * chunk)
    return (intra * (RET_DK ** -0.5),
            jnp.broadcast_to(xi[:, :, None], (RET_HEADS, chunk, RET_DK)),
            jnp.broadcast_to(zeta[:, :, None], (RET_HEADS, chunk, RET_DV)),
            jnp.broadcast_to(decay[:, None, None], (RET_HEADS, 8, RET_DV)))


def _retention_kernel(q_ref, k_ref, v_ref, g_ref, s0_ref, intra_ref, xi_ref, zeta_ref, decay_ref,
                      o_ref, sfin_ref, state, *, chunk, n_chunks):
    j = pl.program_id(1)

    @pl.when(j == 0)
    def _():
        state[...] = s0_ref[...]

    for c in range(n_chunks):
        rows = slice(c * chunk, (c + 1) * chunk)
        for h in range(RET_HEADS):
            qk_cols = slice(h * RET_DK, (h + 1) * RET_DK)
            v_cols = slice(h * RET_DV, (h + 1) * RET_DV)
            q = q_ref[rows, qk_cols]
            kb = k_ref[rows, qk_cols].astype(_BF16)
            v = v_ref[rows, v_cols]
            g = g_ref[rows, v_cols]
            r_prev = state[h]
            scores = _dot_nt(q.astype(_BF16), kb) * intra_ref[h]
            o = _dot(scores.astype(_BF16), v.astype(_BF16))
            o = o + _dot((q * xi_ref[h]).astype(_BF16), r_prev.astype(_BF16))
            u = _dot_tn(kb, (v * zeta_ref[h]).astype(_BF16)) * (RET_DK ** -0.5)
            state[h] = decay_ref[h, 0:1, :] * r_prev + u
            mu = jnp.mean(o, axis=-1, keepdims=True)
            oc = o - mu
            var = jnp.mean(oc * oc, axis=-1, keepdims=True)
            on = oc * lax.rsqrt(var + EPS)
            o_ref[rows, v_cols] = (on * (g * jax.nn.sigmoid(g))).astype(o_ref.dtype)

    @pl.when(j == pl.num_programs(1) - 1)
    def _():
        sfin_ref[...] = state[...]


def _retention(z3, state0, *, chunk, n_chunks):
    b, t, _ = z3.shape
    tc = chunk * n_chunks
    intra, xi, zeta, decay = _retention_tables(chunk)
    const = lambda shape: pl.BlockSpec(shape, lambda bi, j: (0,) * len(shape))
    state_spec = pl.BlockSpec((None, RET_HEADS, RET_DK, RET_DV), lambda bi, j: (bi, 0, 0, 0))
    kernel = functools.partial(_retention_kernel, chunk=chunk, n_chunks=n_chunks)
    return pl.pallas_call(
        kernel,
        grid=(b, t // tc),
        in_specs=[
            pl.BlockSpec((None, tc, RET_QK_W), lambda bi, j: (bi, j, COL_RQ // RET_QK_W)),
            pl.BlockSpec((None, tc, RET_QK_W), lambda bi, j: (bi, j, COL_RK // RET_QK_W)),
            pl.BlockSpec((None, tc, RET_V_W), lambda bi, j: (bi, j, COL_RV // RET_V_W)),
            pl.BlockSpec((None, tc, RET_V_W), lambda bi, j: (bi, j, COL_RG // RET_V_W)),
            state_spec,
            const((RET_HEADS, chunk, chunk)),
            const((RET_HEADS, chunk, RET_DK)),
            const((RET_HEADS, chunk, RET_DV)),
            const((RET_HEADS, 8, RET_DV)),
        ],
        out_specs=[
            pl.BlockSpec((None, tc, RET_V_W), lambda bi, j: (bi, j, 0)),
            state_spec,
        ],
        out_shape=[
            jax.ShapeDtypeStruct((b, t, RET_V_W), _BF16),
            jax.ShapeDtypeStruct((b, RET_HEADS, RET_DK, RET_DV), _F32),
        ],
        scratch_shapes=[pltpu.VMEM((RET_HEADS, RET_DK, RET_DV), _F32)],
        compiler_params=_params("parallel", "arbitrary"),
        name="retention",
    )(z3, z3, z3, z3, state0, intra, xi, zeta, decay)


def _alibi_slope(head):
    return 2.0 ** (-8.0 * (head + 1.0) / ATT_HEADS)


def _pack_head_stats(cols):
    rows = cols[0].shape[0]
    lane = lax.broadcasted_iota(jnp.int32, (rows, LSE_LANES), 1)
    out = jnp.zeros((rows, LSE_LANES), _F32)
    for h, c in enumerate(cols):
        out = jnp.where(lane == h, c, out)
    return out


def _band_attn_kernel(q_ref, k_ref, kh_ref, v_ref, vh_ref, o_ref, lse_ref, kbuf, vbuf, *, tq, dil, group):
    i = pl.program_id(2)
    blk = Q_BLOCK
    kbuf[0:blk, :] = kh_ref[...].astype(_BF16)
    kbuf[blk:, :] = k_ref[...].astype(_BF16)
    vbuf[0:blk, :] = vh_ref[...].astype(_BF16)
    vbuf[blk:, :] = v_ref[...].astype(_BF16)

    row = lax.broadcasted_iota(jnp.int32, (blk, 2 * blk), 0)
    col = lax.broadcasted_iota(jnp.int32, (blk, 2 * blk), 1)
    delta = row + blk - col
    in_band = (delta >= 0) & (delta <= ATT_TAPS)
    first_valid = in_band & (col >= jnp.where(i > 0, 0, blk))
    dist = (delta * dil).astype(_F32)
    scale = ATT_HEAD_DIM ** -0.5

    for qb in range(tq // blk):
        rows = slice(qb * blk, (qb + 1) * blk)
        keys = slice(qb * blk, (qb + 2) * blk)
        valid = first_valid if qb == 0 else in_band
        stats = []
        for h in range(HPG):
            cols = slice(h * ATT_HEAD_DIM, (h + 1) * ATT_HEAD_DIM)
            q = q_ref[rows, cols].astype(_BF16)
            s = _dot_nt(q, kbuf[keys, cols]) * scale - _alibi_slope(group * HPG + h) * dist
            s = jnp.where(valid, s, -jnp.inf)
            m = jnp.max(s, axis=-1, keepdims=True)
            e = jnp.exp(s - m)
            den = jnp.sum(e, axis=-1, keepdims=True)
            o_ref[rows, cols] = _dot((e / den).astype(_BF16), vbuf[keys, cols])
            stats.append(m + jnp.log(den))
        lse_ref[rows, :] = _pack_head_stats(stats)


def _band_attention(z3, group, *, tq):
    b, t, _ = z3.shape
    _, dil = ATT_GROUPS[group]
    ts = t // dil
    tq = min(tq, ts)
    zr = z3.reshape(b, ts, dil * D_IN)
    cq = (COL_AQ + group * GROUP_W) // COL_BLOCK
    ck = (COL_AK + group * GROUP_W) // COL_BLOCK
    cv = (COL_AV + group * GROUP_W) // COL_BLOCK
    per = tq // Q_BLOCK

    def main(c):
        return pl.BlockSpec((None, tq, COL_BLOCK), lambda bi, r, i: (bi, i, r * N_COL_BLOCKS + c))

    def halo(c):
        return pl.BlockSpec((None, Q_BLOCK, COL_BLOCK),
                            lambda bi, r, i: (bi, jnp.maximum(i * per - 1, 0), r * N_COL_BLOCKS + c))

    kernel = functools.partial(_band_attn_kernel, tq=tq, dil=dil, group=group)
    o, lse = pl.pallas_call(
        kernel,
        grid=(b, dil, ts // tq),
        in_specs=[main(cq), main(ck), halo(ck), main(cv), halo(cv)],
        out_specs=[
            pl.BlockSpec((None, tq, GROUP_W), lambda bi, r, i: (bi, i, r)),
            pl.BlockSpec((None, tq, LSE_LANES), lambda bi, r, i: (bi, i, r)),
        ],
        out_shape=[
            jax.ShapeDtypeStruct((b, ts, dil * GROUP_W), _F32),
            jax.ShapeDtypeStruct((b, ts, dil * LSE_LANES), _F32),
        ],
        scratch_shapes=[pltpu.VMEM((tq + Q_BLOCK, GROUP_W), _BF16), pltpu.VMEM((tq + Q_BLOCK, GROUP_W), _BF16)],
        compiler_params=_params("parallel", "parallel", "arbitrary"),
        name="band_attention_g%d" % group,
    )(zr, zr, zr, zr, zr)
    return o.reshape(b, t, GROUP_W), lse.reshape(b, t, LSE_LANES)


def _step_attn_kernel(q_ref, kn_ref, vn_ref, kc_ref, vc_ref, o_ref, lse_ref, ko_ref, vo_ref, *,
                      t_new, cache_len, dil, group):
    n_res = min(dil, t_new)
    taps = cache_len // dil
    rows_all = HPG * t_new
    head_cols = [slice(h * ATT_HEAD_DIM, (h + 1) * ATT_HEAD_DIM) for h in range(HPG)]
    head_rows = [slice(h * t_new, (h + 1) * t_new) for h in range(HPG)]
    q = [q_ref[:, c].astype(_BF16) for c in head_cols]
    k_new = kn_ref[...]
    v_new = vn_ref[...]

    def cache_taps(ref, r, h):
        return ref[pl.ds(r * HPG + h, taps, stride=HPG * dil), :].astype(_BF16)

    scale = ATT_HEAD_DIM ** -0.5
    log2_dil = dil.bit_length() - 1
    log2_new = t_new.bit_length() - 1

    def row_terms(width):
        row_id = lax.broadcasted_iota(jnp.int32, (rows_all, width), 0)
        slope = jnp.zeros((rows_all, width), _F32)
        for h in range(HPG):
            slope = jnp.where((row_id >> log2_new) == h, _alibi_slope(group * HPG + h), slope)
        return row_id & (t_new - 1), slope

    query, slope = row_terms(taps)
    residue = query & (dil - 1)
    taps_back = taps + (query >> log2_dil) - lax.broadcasted_iota(jnp.int32, (rows_all, taps), 1)
    s_cache = None
    for r in range(n_res):
        s_r = jnp.concatenate([_dot_nt(q[h], cache_taps(kc_ref, r, h)) for h in range(HPG)], axis=0)
        s_cache = s_r if s_cache is None else jnp.where(residue == r, s_r, s_cache)
    s_cache = s_cache * scale - slope * (taps_back << log2_dil).astype(_F32)
    s_cache = jnp.where(taps_back <= ATT_TAPS, s_cache, -jnp.inf)

    query_n, slope_n = row_terms(t_new)
    back = query_n - lax.broadcasted_iota(jnp.int32, (rows_all, t_new), 1)
    s_new = jnp.concatenate([_dot_nt(q[h], k_new[:, head_cols[h]].astype(_BF16)) for h in range(HPG)], axis=0)
    s_new = s_new * scale - slope_n * back.astype(_F32)
    s_new = jnp.where((back >= 0) & ((back & (dil - 1)) == 0), s_new, -jnp.inf)

    m = jnp.maximum(jnp.max(s_cache, axis=-1, keepdims=True), jnp.max(s_new, axis=-1, keepdims=True))
    e_cache = jnp.exp(s_cache - m)
    e_new = jnp.exp(s_new - m)
    den = jnp.sum(e_cache, axis=-1, keepdims=True) + jnp.sum(e_new, axis=-1, keepdims=True)
    p_cache = e_cache / den
    p_new = (e_new / den).astype(_BF16)
    lse = m + jnp.log(den)

    for h in range(HPG):
        out = _dot(p_new[head_rows[h]], v_new[:, head_cols[h]].astype(_BF16))
        for r in range(n_res):
            p_r = p_cache[head_rows[h]]
            if n_res > 1:
                p_r = jnp.where(residue[head_rows[h]] == r, p_r, 0.0)
            out = out + _dot(p_r.astype(_BF16), cache_taps(vc_ref, r, h))
        o_ref[:, head_cols[h]] = out
    lse_ref[...] = _pack_head_stats([lse[rows] for rows in head_rows])

    keep = (cache_len - t_new) * HPG
    ko_ref[0:keep, :] = kc_ref[t_new * HPG:, :]
    vo_ref[0:keep, :] = vc_ref[t_new * HPG:, :]
    for h in range(HPG):
        ko_ref[pl.ds(keep + h, t_new, stride=HPG), :] = k_new[:, head_cols[h]]
        vo_ref[pl.ds(keep + h, t_new, stride=HPG), :] = v_new[:, head_cols[h]]


def _step_attention(z3, cache_k, cache_v, prev_k, prev_v, layer, group):
    b, t_new, _ = z3.shape
    cache_len = cache_k.shape[2] // HPG
    _, dil = ATT_GROUPS[group]
    cq = (COL_AQ + group * GROUP_W) // COL_BLOCK
    ck = (COL_AK + group * GROUP_W) // COL_BLOCK
    cv = (COL_AV + group * GROUP_W) // COL_BLOCK
    zspec = lambda c: pl.BlockSpec((None, t_new, COL_BLOCK), lambda bi: (bi, 0, c))
    cache_spec = pl.BlockSpec((None, None, cache_len * HPG, ATT_HEAD_DIM), lambda bi: (layer, bi, 0, 0))
    kernel = functools.partial(_step_attn_kernel, t_new=t_new, cache_len=cache_len, dil=dil, group=group)
    in_specs = [zspec(cq), zspec(ck), zspec(cv), cache_spec, cache_spec]
    args = [z3, z3, z3, cache_k, cache_v]
    aliases = {}
    if prev_k is not None:
        in_specs += [pl.BlockSpec(memory_space=pl.ANY)] * 2
        args += [prev_k, prev_v]
        aliases = {5: 2, 6: 3}
        kernel = functools.partial(_drop_refs, kernel, 5, 2)
    cache_shape = jax.ShapeDtypeStruct(cache_k.shape, cache_k.dtype)
    return pl.pallas_call(
        kernel,
        grid=(b,),
        in_specs=in_specs,
        out_specs=[
            pl.BlockSpec((None, t_new, GROUP_W), lambda bi: (bi, 0, 0)),
            pl.BlockSpec((None, t_new, LSE_LANES), lambda bi: (bi, 0, 0)),
            cache_spec, cache_spec,
        ],
        out_shape=[
            jax.ShapeDtypeStruct((b, t_new, GROUP_W), _F32),
            jax.ShapeDtypeStruct((b, t_new, LSE_LANES), _F32),
            cache_shape, cache_shape,
        ],
        input_output_aliases=aliases,
        compiler_params=_params("parallel"),
        name="step_attention_g%d" % group,
    )(*args)


def _drop_refs(kernel, start, count, *refs):
    return kernel(*refs[:start], *refs[start + count:])


def _merge_kernel(x_ref, oret_ref, og0_ref, og1_ref, og2_ref, l0_ref, l1_ref, l2_ref, gates_ref,
                  wret_ref, watt_ref, wout_ref, y_ref):
    l0, l1, l2 = l0_ref[...], l1_ref[...], l2_ref[...]
    mx = jnp.maximum(jnp.maximum(l0, l1), l2)
    e0, e1, e2 = jnp.exp(l0 - mx), jnp.exp(l1 - mx), jnp.exp(l2 - mx)
    tot = e0 + e1 + e2
    w0, w1, w2 = e0 / tot, e1 / tot, e2 / tot
    parts = []
    for h in range(HPG):
        cols = slice(h * ATT_HEAD_DIM, (h + 1) * ATT_HEAD_DIM)
        parts.append(w0[:, h:h + 1] * og0_ref[:, cols] + w1[:, h:h + 1] * og1_ref[:, cols]
                     + w2[:, h:h + 1] * og2_ref[:, cols])
    o_att = jnp.concatenate(parts, axis=-1).astype(_BF16)
    br_ret = _dot(oret_ref[...], wret_ref[...])
    br_att = _dot(o_att, watt_ref[...])
    ga = gates_ref[:, 0:D_MODEL]
    gb = gates_ref[:, D_MODEL:2 * D_MODEL]
    mix = jax.nn.sigmoid(ga) * br_ret + jax.nn.sigmoid(gb) * br_att
    y_ref[...] = x_ref[...] + _dot(mix.astype(_BF16), wout_ref[...])


def _merge(x, o_ret, ogs, lses, z, w_ret, w_att, w_out, *, tm):
    n, d = x.shape
    row = lambda w: pl.BlockSpec((tm, w), lambda i: (i, 0))
    full = lambda a: pl.BlockSpec(a.shape, lambda i: (0, 0))
    return pl.pallas_call(
        _merge_kernel,
        grid=(n // tm,),
        in_specs=[row(d), row(RET_V_W), row(GROUP_W), row(GROUP_W), row(GROUP_W),
                  row(LSE_LANES), row(LSE_LANES), row(LSE_LANES), row(2 * D_MODEL),
                  full(w_ret), full(w_att), full(w_out)],
        out_specs=row(d),
        out_shape=jax.ShapeDtypeStruct((n, d), _F32),
        compiler_params=_params("parallel"),
        name="merge_out_proj",
    )(x, o_ret, *ogs, *lses, z, w_ret, w_att, w_out)


def _mlp_kernel(x_ref, g_ref, wup_ref, wdown_ref, p_ref, wple_ref, wgate_ref, gfin_ref, y_ref, h_ref, acc_ref, *,
                final_norm):
    j = pl.program_id(1)

    @pl.when(j == 0)
    def _():
        h_ref[...] = _rmsnorm(x_ref[...], g_ref[...]).astype(_BF16)
        acc_ref[...] = jnp.zeros_like(acc_ref)

    u = jnp.maximum(_dot(h_ref[...], wup_ref[...]), 0.0)
    acc_ref[...] += _dot((u * u).astype(_BF16), wdown_ref[...])

    @pl.when(j == pl.num_programs(1) - 1)
    def _():
        x = x_ref[...] + acc_ref[...]
        gate = jax.nn.sigmoid(_dot(x.astype(_BF16), wgate_ref[...]))
        x = x + _dot(p_ref[...].astype(_BF16), wple_ref[...]) * gate
        if final_norm:
            x = _rmsnorm(x, gfin_ref[...])
        y_ref[...] = x


def _mlp(x, g, w_up, w_down, p, w_ple, w_gate, g_final, *, tm, tf, final_norm):
    n, d = x.shape
    row = lambda w: pl.BlockSpec((tm, w), lambda i, j: (i, 0))
    full = lambda a: pl.BlockSpec(a.shape, lambda i, j: (0, 0))
    return pl.pallas_call(
        functools.partial(_mlp_kernel, final_norm=final_norm),
        grid=(n // tm, D_FF // tf),
        in_specs=[row(d), full(g),
                  pl.BlockSpec((d, tf), lambda i, j: (0, j)),
                  pl.BlockSpec((tf, d), lambda i, j: (j, 0)),
                  row(D_PLE), full(w_ple), full(w_gate), full(g_final)],
        out_specs=row(d),
        out_shape=jax.ShapeDtypeStruct((n, d), _F32),
        scratch_shapes=[pltpu.VMEM((tm, d), _BF16), pltpu.VMEM((tm, d), _F32)],
        compiler_params=_params("parallel", "arbitrary"),
        name="mlp_ple",
    )(x, g, w_up, w_down, p, w_ple, w_gate, g_final)


def _layer_weights(norm_mix, w_in, w_ret_br, w_att_br, w_out, norm_ffn, w_up, w_down, w_ple, w_ple_gate, i):
    gates_first = jnp.concatenate([w_in[i][:, D_IN - 2 * D_MODEL:], w_in[i][:, :D_IN - 2 * D_MODEL]], axis=1)
    return dict(
        norm_mix=norm_mix[i][None, :], w_in=gates_first.astype(_BF16),
        w_ret_br=w_ret_br[i].astype(_BF16), w_att_br=w_att_br[i].astype(_BF16), w_out=w_out[i].astype(_BF16),
        norm_ffn=norm_ffn[i][None, :], w_up=w_up[i].astype(_BF16), w_down=w_down[i].astype(_BF16),
        w_ple=w_ple[i].astype(_BF16), w_ple_gate=w_ple_gate[i].astype(_BF16))


def _mix_and_ffn(x, z, o_ret, ogs, lses, p_l, lw, g_final, final_norm, *, tm_merge, tm_mlp, tf):
    n = x.shape[0]
    flat = lambda a: a.reshape(n, a.shape[-1])
    x = _merge(x, flat(o_ret), [flat(o) for o in ogs], [flat(l) for l in lses], z,
               lw["w_ret_br"], lw["w_att_br"], lw["w_out"], tm=tm_merge)
    return _mlp(x, lw["norm_ffn"], lw["w_up"], lw["w_down"], p_l, lw["w_ple"], lw["w_ple_gate"], g_final,
                tm=tm_mlp, tf=tf, final_norm=final_norm)


def kernel(x_prompt, x_sample, cache_win_k0, cache_win_v0, cache_win_k1, cache_win_v1, cache_win_k2, cache_win_v2,
           state_ret, p_prompt, p_sample, norm_mix, w_in, w_ret_br, w_att_br, w_out, norm_ffn, w_up, w_down,
           w_ple, w_ple_gate, norm_final):
    depth = w_in.shape[0]
    bp, tp, d = x_prompt.shape
    bs, ts, _ = x_sample.shape
    xp = x_prompt.reshape(bp * tp, d)
    xs = x_sample.reshape(bs * ts, d)
    g_final = norm_final[None, :]
    pos_head_rows = lambda c: c.reshape(c.shape[:2] + (c.shape[2] * HPG, ATT_HEAD_DIM))
    caches_k = [pos_head_rows(c) for c in (cache_win_k0, cache_win_k1, cache_win_k2)]
    caches_v = [pos_head_rows(c) for c in (cache_win_v0, cache_win_v1, cache_win_v2)]
    new_k = [None] * N_GROUPS
    new_v = [None] * N_GROUPS
    pk = [[] for _ in range(N_GROUPS)]
    pv = [[] for _ in range(N_GROUPS)]
    prompt_ret, sample_ret = [], []
    sample_chunk = math.gcd(ts, RET_CHUNK)

    for i in range(depth):
        lw = _layer_weights(norm_mix, w_in, w_ret_br, w_att_br, w_out, norm_ffn, w_up, w_down, w_ple, w_ple_gate, i)
        last = i == depth - 1

        z = _norm_proj(xp, lw["norm_mix"], lw["w_in"], tm=1024)
        z3 = z.reshape(bp, tp, D_IN)
        o_ret, ret_state = _retention(z3, jnp.zeros((bp, RET_HEADS, RET_DK, RET_DV), _F32),
                                      chunk=RET_CHUNK, n_chunks=4)
        prompt_ret.append(ret_state)
        ogs, lses = zip(*[_band_attention(z3, g, tq=512) for g in range(N_GROUPS)])
        for g, (window, _) in enumerate(ATT_GROUPS):
            keep = min(window, tp)
            ck = COL_AK + g * GROUP_W
            cv = COL_AV + g * GROUP_W
            pk[g].append(z3[:, tp - keep:, ck:ck + GROUP_W].reshape(bp, keep, HPG, ATT_HEAD_DIM))
            pv[g].append(z3[:, tp - keep:, cv:cv + GROUP_W].reshape(bp, keep, HPG, ATT_HEAD_DIM))
        xp = _mix_and_ffn(xp, z, o_ret, ogs, lses, p_prompt[i].reshape(bp * tp, D_PLE), lw, g_final, last,
                          tm_merge=256, tm_mlp=512, tf=1024)

        z = _norm_proj(xs, lw["norm_mix"], lw["w_in"], tm=bs * ts)
        z3 = z.reshape(bs, ts, D_IN)
        o_ret, ret_state = _retention(z3, state_ret[i], chunk=sample_chunk, n_chunks=ts // sample_chunk)
        sample_ret.append(ret_state)
        ogs, lses = [], []
        for g in range(N_GROUPS):
            o_g, lse_g, new_k[g], new_v[g] = _step_attention(z3, caches_k[g], caches_v[g], new_k[g], new_v[g], i, g)
            ogs.append(o_g)
            lses.append(lse_g)
        xs = _mix_and_ffn(xs, z, o_ret, ogs, lses, p_sample[i].reshape(bs * ts, D_PLE), lw, g_final, last,
                          tm_merge=bs * ts, tm_mlp=bs * ts, tf=512)

    as_heads = lambda a: a.reshape(a.shape[:2] + (a.shape[2] // HPG, HPG, ATT_HEAD_DIM))
    prompt_windows = [jnp.stack(a) for g in range(N_GROUPS) for a in (pk[g], pv[g])]
    sample_windows = [as_heads(a) for g in range(N_GROUPS) for a in (new_k[g], new_v[g])]
    return (xp.reshape(bp, tp, d), xs.reshape(bs, ts, d), *prompt_windows, jnp.stack(prompt_ret),
            *sample_windows, jnp.stack(sample_ret))
```

```python
import functools
import math

import jax
import jax.numpy as jnp
import numpy as np
from jax import lax
from jax.experimental import pallas as pl
from jax.experimental.pallas import tpu as pltpu

D_MODEL = 1024
D_PLE = 256
RET_HEADS = 4
RET_DK = 128
RET_DV = 256
RET_CHUNK = 128
ATT_GROUPS = ((128, 1), (512, 4), (2048, 16))
N_GROUPS = 3
HPG = 4
ATT_HEAD_DIM = 128
ATT_HEADS = N_GROUPS * HPG
ATT_TAPS = 128
Q_BLOCK = 128
D_FF = 4 * D_MODEL
EPS = 1e-6

RET_QK_W = RET_HEADS * RET_DK
RET_V_W = RET_HEADS * RET_DV
GROUP_W = HPG * ATT_HEAD_DIM
ATT_W = ATT_HEADS * ATT_HEAD_DIM
D_IN = 2 * RET_QK_W + 2 * RET_V_W + 3 * ATT_W + 2 * D_MODEL

COL_GATES = 0
COL_RQ = 2 * D_MODEL
COL_RK = COL_RQ + RET_QK_W
COL_RV = COL_RK + RET_QK_W
COL_RG = COL_RV + RET_V_W
COL_AQ = COL_RG + RET_V_W
COL_AK = COL_AQ + ATT_W
COL_AV = COL_AK + ATT_W
COL_BLOCK = 512
N_COL_BLOCKS = D_IN // COL_BLOCK

LSE_LANES = 128
VMEM_LIMIT = 48 * 1024 * 1024

_BF16 = jnp.bfloat16
_F32 = jnp.float32


def _params(*sem):
    return pltpu.CompilerParams(dimension_semantics=sem, vmem_limit_bytes=VMEM_LIMIT)


def _rmsnorm(x, g):
    return x * lax.rsqrt(jnp.mean(x * x, axis=-1, keepdims=True) + EPS) * g


def _dot(a, b):
    return jnp.dot(a, b, preferred_element_type=_F32)


def _dot_nt(a, b):
    return lax.dot_general(a, b, (((1,), (1,)), ((), ())), preferred_element_type=_F32)


def _dot_tn(a, b):
    return lax.dot_general(a, b, (((0,), (0,)), ((), ())), preferred_element_type=_F32)


def _norm_proj_kernel(x_ref, g_ref, w_ref, z_ref, h_ref):
    @pl.when(pl.program_id(1) == 0)
    def _():
        h_ref[...] = _rmsnorm(x_ref[...], g_ref[...]).astype(_BF16)

    z_ref[...] = _dot(h_ref[...], w_ref[...])


def _norm_proj(x, g, w, *, tm, tn=COL_BLOCK):
    n, d = x.shape
    d_out = w.shape[1]
    return pl.pallas_call(
        _norm_proj_kernel,
        grid=(n // tm, d_out // tn),
        in_specs=[
            pl.BlockSpec((tm, d), lambda i, j: (i, 0)),
            pl.BlockSpec((1, d), lambda i, j: (0, 0)),
            pl.BlockSpec((d, tn), lambda i, j: (0, j)),
        ],
        out_specs=pl.BlockSpec((tm, tn), lambda i, j: (i, j)),
        out_shape=jax.ShapeDtypeStruct((n, d_out), _F32),
        scratch_shapes=[pltpu.VMEM((tm, d), _BF16)],
        compiler_params=_params("parallel", "arbitrary"),
        name="norm_proj",
    )(x, g, w)


def _ret_log_gamma():
    return jnp.log1p(-jnp.exp(jnp.linspace(math.log(1.0 / 32), math.log(1.0 / 512), RET_HEADS))).astype(_F32)


def _retention_tables(chunk):
    lg = _ret_log_gamma()
    pos = jnp.arange(chunk, dtype=_F32)
    diff = pos[:, None] - pos[None, :]
    intra = jnp.where(diff[None] >= 0, jnp.exp(lg[:, None, None] * jnp.maximum(diff, 0.0)[None]), 0.0)
    xi = jnp.exp(lg[:, None] * (pos[None] + 1.0))
    zeta = jnp.exp(lg[:, None] * (chunk - 1.0 - pos)[None])
    decay = jnp.exp(lg * chunk)
    return (intra * (RET_DK ** -0.5),
            jnp.broadcast_to(xi[:, :, None], (RET_HEADS, chunk, RET_DK)),
            jnp.broadcast_to(zeta[:, :, None], (RET_HEADS, chunk, RET_DV)),
            jnp.broadcast_to(decay[:, None, None], (RET_HEADS, 8, RET_DV)))


def _retention_kernel(q_ref, k_ref, v_ref, g_ref, s0_ref, intra_ref, xi_ref, zeta_ref, decay_ref,
                      o_ref, sfin_ref, state, *, chunk, n_chunks):
    j = pl.program_id(1)

    @pl.when(j == 0)
    def _():
        state[...] = s0_ref[...]

    for c in range(n_chunks):
        rows = slice(c * chunk, (c + 1) * chunk)
        for h in range(RET_HEADS):
            qk_cols = slice(h * RET_DK, (h + 1) * RET_DK)
            v_cols = slice(h * RET_DV, (h + 1) * RET_DV)
            q = q_ref[rows, qk_cols]
            kb = k_ref[rows, qk_cols].astype(_BF16)
            v = v_ref[rows, v_cols]
            g = g_ref[rows, v_cols]
            r_prev = state[h]
            scores = _dot_nt(q.astype(_BF16), kb) * intra_ref[h]
            o = _dot(scores.astype(_BF16), v.astype(_BF16))
            o = o + _dot((q * xi_ref[h]).astype(_BF16), r_prev.astype(_BF16))
            u = _dot_tn(kb, (v * zeta_ref[h]).astype(_BF16)) * (RET_DK ** -0.5)
            state[h] = decay_ref[h, 0:1, :] * r_prev + u
            mu = jnp.mean(o, axis=-1, keepdims=True)
            oc = o - mu
            var = jnp.mean(oc * oc, axis=-1, keepdims=True)
            on = oc * lax.rsqrt(var + EPS)
            o_ref[rows, v_cols] = (on * (g * jax.nn.sigmoid(g))).astype(o_ref.dtype)

    @pl.when(j == pl.num_programs(1) - 1)
    def _():
        sfin_ref[...] = state[...]


def _retention(z3, state0, *, chunk, n_chunks):
    b, t, _ = z3.shape
    tc = chunk * n_chunks
    intra, xi, zeta, decay = _retention_tables(chunk)
    const = lambda shape: pl.BlockSpec(shape, lambda bi, j: (0,) * len(shape))
    state_spec = pl.BlockSpec((None, RET_HEADS, RET_DK, RET_DV), lambda bi, j: (bi, 0, 0, 0))
    kernel = functools.partial(_retention_kernel, chunk=chunk, n_chunks=n_chunks)
    return pl.pallas_call(
        kernel,
        grid=(b, t // tc),
        in_specs=[
            pl.BlockSpec((None, tc, RET_QK_W), lambda bi, j: (bi, j, COL_RQ // RET_QK_W)),
            pl.BlockSpec((None, tc, RET_QK_W), lambda bi, j: (bi, j, COL_RK // RET_QK_W)),
            pl.BlockSpec((None, tc, RET_V_W), lambda bi, j: (bi, j, COL_RV // RET_V_W)),
            pl.BlockSpec((None, tc, RET_V_W), lambda bi, j: (bi, j, COL_RG // RET_V_W)),
            state_spec,
            const((RET_HEADS, chunk, chunk)),
            const((RET_HEADS, chunk, RET_DK)),
            const((RET_HEADS, chunk, RET_DV)),
            const((RET_HEADS, 8, RET_DV)),
        ],
        out_specs=[
            pl.BlockSpec((None, tc, RET_V_W), lambda bi, j: (bi, j, 0)),
            state_spec,
        ],
        out_shape=[
            jax.ShapeDtypeStruct((b, t, RET_V_W), _BF16),
            jax.ShapeDtypeStruct((b, RET_HEADS, RET_DK, RET_DV), _F32),
        ],
        scratch_shapes=[pltpu.VMEM((RET_HEADS, RET_DK, RET_DV), _F32)],
        compiler_params=_params("parallel", "arbitrary"),
        name="retention",
    )(z3, z3, z3, z3, state0, intra, xi, zeta, decay)


def _alibi_slope(head):
    return 2.0 ** (-8.0 * (head + 1.0) / ATT_HEADS)


def _pack_head_stats(cols):
    rows = cols[0].shape[0]
    lane = lax.broadcasted_iota(jnp.int32, (rows, LSE_LANES), 1)
    out = jnp.zeros((rows, LSE_LANES), _F32)
    for h, c in enumerate(cols):
        out = jnp.where(lane == h, c, out)
    return out


def _band_attn_kernel(q_ref, k_ref, kh_ref, v_ref, vh_ref, o_ref, lse_ref, *, n_sub, dil, group):
    i = pl.program_id(1)
    h = pl.program_id(2)
    blk = Q_BLOCK
    span = blk * dil
    row = lax.broadcasted_iota(jnp.int32, (blk, 2 * blk), 0)
    col = lax.broadcasted_iota(jnp.int32, (blk, 2 * blk), 1)
    delta = row + blk - col
    slope = jnp.float32(0.0)
    for hh in range(HPG):
        slope = jnp.where(h == hh, jnp.float32(_alibi_slope(group * HPG + hh)), slope)
    in_band = (delta >= 0) & (delta <= ATT_TAPS)
    bias = jnp.where(in_band, -slope * (delta * dil).astype(_F32), -jnp.inf)
    bias_first = jnp.where(col >= jnp.where(i > 0, 0, blk), bias, -jnp.inf)
    lane = lax.broadcasted_iota(jnp.int32, (blk, LSE_LANES), 1)
    scale = ATT_HEAD_DIM ** -0.5

    def strided(start, size):
        return pl.ds(start, size, stride=dil) if dil > 1 else pl.ds(start, size)

    def attend(rows, q, keys, values, bias_blk):
        s = _dot_nt(q.astype(_BF16), keys.astype(_BF16)) * scale + bias_blk
        m = jnp.max(s, axis=-1, keepdims=True)
        e = jnp.exp(s - m)
        den = jnp.sum(e, axis=-1, keepdims=True)
        o_ref[rows, :] = _dot((e / den).astype(_BF16), values.astype(_BF16))
        prev = jnp.where(h == 0, 0.0, lse_ref[rows, :])
        lse_ref[rows, :] = jnp.where(lane == h, m + jnp.log(den), prev)

    def residue_body(r, carry):
        rows = strided(r, blk)
        keys = jnp.concatenate([kh_ref[rows, :], k_ref[rows, :]], axis=0)
        values = jnp.concatenate([vh_ref[rows, :], v_ref[rows, :]], axis=0)
        attend(rows, q_ref[rows, :], keys, values, bias_first)

        def block_body(sb, carry):
            rows = strided(r + sb * span, blk)
            both = strided(r + (sb - 1) * span, 2 * blk)
            attend(rows, q_ref[rows, :], k_ref[both, :], v_ref[both, :], bias)
            return carry

        return lax.fori_loop(1, n_sub, block_body, carry)

    lax.fori_loop(0, dil, residue_body, 0)


def _band_attention(z3, group, *, tile):
    b, t, _ = z3.shape
    _, dil = ATT_GROUPS[group]
    span = Q_BLOCK * dil
    tile = max(min(tile, t), span)
    n_sub = tile // span
    tiles = t // tile
    cq = (COL_AQ + group * GROUP_W) // ATT_HEAD_DIM
    ck = (COL_AK + group * GROUP_W) // ATT_HEAD_DIM
    cv = (COL_AV + group * GROUP_W) // ATT_HEAD_DIM

    def main(c):
        return pl.BlockSpec((None, tile, ATT_HEAD_DIM), lambda bi, i, h: (bi, i, c + h))

    def halo(c):
        return pl.BlockSpec((None, span, ATT_HEAD_DIM), lambda bi, i, h: (bi, jnp.maximum(i * n_sub - 1, 0), c + h))

    kernel = functools.partial(_band_attn_kernel, n_sub=n_sub, dil=dil, group=group)
    return pl.pallas_call(
        kernel,
        grid=(b, tiles, HPG),
        in_specs=[main(cq), main(ck), halo(ck), main(cv), halo(cv)],
        out_specs=[
            pl.BlockSpec((None, tile, ATT_HEAD_DIM), lambda bi, i, h: (h, bi * tiles + i, 0)),
            pl.BlockSpec((tile, LSE_LANES), lambda bi, i, h: (bi * tiles + i, 0)),
        ],
        out_shape=[
            jax.ShapeDtypeStruct((HPG, b * t, ATT_HEAD_DIM), _F32),
            jax.ShapeDtypeStruct((b * t, LSE_LANES), _F32),
        ],
        compiler_params=_params("parallel", "parallel", "arbitrary"),
        name="band_attention_g%d" % group,
    )(z3, z3, z3, z3, z3)


def _step_attn_kernel(q_ref, kn_ref, vn_ref, kc_ref, vc_ref, o_ref, lse_ref, ko_ref, vo_ref, *,
                      t_new, cache_len, dil, group):
    n_res = min(dil, t_new)
    taps = cache_len // dil
    rows_all = HPG * t_new
    head_cols = [slice(h * ATT_HEAD_DIM, (h + 1) * ATT_HEAD_DIM) for h in range(HPG)]
    head_rows = [slice(h * t_new, (h + 1) * t_new) for h in range(HPG)]
    q = [q_ref[:, c].astype(_BF16) for c in head_cols]
    k_new = kn_ref[...]
    v_new = vn_ref[...]

    def cache_taps(ref, r, h):
        return ref[pl.ds(r * HPG + h, taps, stride=HPG * dil), :].astype(_BF16)

    scale = ATT_HEAD_DIM ** -0.5
    log2_dil = dil.bit_length() - 1
    log2_new = t_new.bit_length() - 1

    def row_terms(width):
        row_id = lax.broadcasted_iota(jnp.int32, (rows_all, width), 0)
        slope = jnp.zeros((rows_all, width), _F32)
        for h in range(HPG):
            slope = jnp.where((row_id >> log2_new) == h, _alibi_slope(group * HPG + h), slope)
        return row_id & (t_new - 1), slope

    query, slope = row_terms(taps)
    residue = query & (dil - 1)
    taps_back = taps + (query >> log2_dil) - lax.broadcasted_iota(jnp.int32, (rows_all, taps), 1)
    s_cache = None
    for r in range(n_res):
        s_r = jnp.concatenate([_dot_nt(q[h], cache_taps(kc_ref, r, h)) for h in range(HPG)], axis=0)
        s_cache = s_r if s_cache is None else jnp.where(residue == r, s_r, s_cache)
    s_cache = s_cache * scale - slope * (taps_back << log2_dil).astype(_F32)
    s_cache = jnp.where(taps_back <= ATT_TAPS, s_cache, -jnp.inf)

    query_n, slope_n = row_terms(t_new)
    back = query_n - lax.broadcasted_iota(jnp.int32, (rows_all, t_new), 1)
    s_new = jnp.concatenate([_dot_nt(q[h], k_new[:, head_cols[h]].astype(_BF16)) for h in range(HPG)], axis=0)
    s_new = s_new * scale - slope_n * back.astype(_F32)
    s_new = jnp.where((back >= 0) & ((back & (dil - 1)) == 0), s_new, -jnp.inf)

    m = jnp.maximum(jnp.max(s_cache, axis=-1, keepdims=True), jnp.max(s_new, axis=-1, keepdims=True))
    e_cache = jnp.exp(s_cache - m)
    e_new = jnp.exp(s_new - m)
    den = jnp.sum(e_cache, axis=-1, keepdims=True) + jnp.sum(e_new, axis=-1, keepdims=True)
    p_cache = e_cache / den
    p_new = (e_new / den).astype(_BF16)
    lse = m + jnp.log(den)

    for h in range(HPG):
        out = _dot(p_new[head_rows[h]], v_new[:, head_cols[h]].astype(_BF16))
        for r in range(n_res):
            p_r = p_cache[head_rows[h]]
            if n_res > 1:
                p_r = jnp.where(residue[head_rows[h]] == r, p_r, 0.0)
            out = out + _dot(p_r.astype(_BF16), cache_taps(vc_ref, r, h))
        o_ref[h] = out
    lse_ref[...] = _pack_head_stats([lse[rows] for rows in head_rows])

    keep = (cache_len - t_new) * HPG
    ko_ref[0:keep, :] = kc_ref[t_new * HPG:, :]
    vo_ref[0:keep, :] = vc_ref[t_new * HPG:, :]
    for h in range(HPG):
        ko_ref[pl.ds(keep + h, t_new, stride=HPG), :] = k_new[:, head_cols[h]]
        vo_ref[pl.ds(keep + h, t_new, stride=HPG), :] = v_new[:, head_cols[h]]


def _step_attention(z3, cache_k, cache_v, prev_k, prev_v, layer, group):
    b, t_new, _ = z3.shape
    cache_len = cache_k.shape[2] // HPG
    _, dil = ATT_GROUPS[group]
    cq = (COL_AQ + group * GROUP_W) // COL_BLOCK
    ck = (COL_AK + group * GROUP_W) // COL_BLOCK
    cv = (COL_AV + group * GROUP_W) // COL_BLOCK
    zspec = lambda c: pl.BlockSpec((None, t_new, COL_BLOCK), lambda bi: (bi, 0, c))
    cache_spec = pl.BlockSpec((None, None, cache_len * HPG, ATT_HEAD_DIM), lambda bi: (layer, bi, 0, 0))
    kernel = functools.partial(_step_attn_kernel, t_new=t_new, cache_len=cache_len, dil=dil, group=group)
    in_specs = [zspec(cq), zspec(ck), zspec(cv), cache_spec, cache_spec]
    args = [z3, z3, z3, cache_k, cache_v]
    aliases = {}
    if prev_k is not None:
        in_specs += [pl.BlockSpec(memory_space=pl.ANY)] * 2
        args += [prev_k, prev_v]
        aliases = {5: 2, 6: 3}
        kernel = functools.partial(_drop_refs, kernel, 5, 2)
    cache_shape = jax.ShapeDtypeStruct(cache_k.shape, cache_k.dtype)
    return pl.pallas_call(
        kernel,
        grid=(b,),
        in_specs=in_specs,
        out_specs=[
            pl.BlockSpec((HPG, t_new, ATT_HEAD_DIM), lambda bi: (0, bi, 0)),
            pl.BlockSpec((t_new, LSE_LANES), lambda bi: (bi, 0)),
            cache_spec, cache_spec,
        ],
        out_shape=[
            jax.ShapeDtypeStruct((HPG, b * t_new, ATT_HEAD_DIM), _F32),
            jax.ShapeDtypeStruct((b * t_new, LSE_LANES), _F32),
            cache_shape, cache_shape,
        ],
        input_output_aliases=aliases,
        compiler_params=_params("parallel"),
        name="step_attention_g%d" % group,
    )(*args)


def _drop_refs(kernel, start, count, *refs):
    return kernel(*refs[:start], *refs[start + count:])


def _merge_kernel(x_ref, oret_ref, og0_ref, og1_ref, og2_ref, l0_ref, l1_ref, l2_ref, gates_ref,
                  wret_ref, watt_ref, wout_ref, y_ref):
    l0, l1, l2 = l0_ref[...], l1_ref[...], l2_ref[...]
    mx = jnp.maximum(jnp.maximum(l0, l1), l2)
    e0, e1, e2 = jnp.exp(l0 - mx), jnp.exp(l1 - mx), jnp.exp(l2 - mx)
    tot = e0 + e1 + e2
    w0, w1, w2 = e0 / tot, e1 / tot, e2 / tot
    parts = []
    for h in range(HPG):
        parts.append(w0[:, h:h + 1] * og0_ref[h] + w1[:, h:h + 1] * og1_ref[h] + w2[:, h:h + 1] * og2_ref[h])
    o_att = jnp.concatenate(parts, axis=-1).astype(_BF16)
    br_ret = _dot(oret_ref[...], wret_ref[...])
    br_att = _dot(o_att, watt_ref[...])
    ga = gates_ref[:, 0:D_MODEL]
    gb = gates_ref[:, D_MODEL:2 * D_MODEL]
    mix = jax.nn.sigmoid(ga) * br_ret + jax.nn.sigmoid(gb) * br_att
    y_ref[...] = x_ref[...] + _dot(mix.astype(_BF16), wout_ref[...])


def _merge(x, o_ret, ogs, lses, z, w_ret, w_att, w_out, *, tm):
    n, d = x.shape
    row = lambda w: pl.BlockSpec((tm, w), lambda i: (i, 0))
    full = lambda a: pl.BlockSpec(a.shape, lambda i: (0, 0))
    heads = pl.BlockSpec((HPG, tm, ATT_HEAD_DIM), lambda i: (0, i, 0))
    return pl.pallas_call(
        _merge_kernel,
        grid=(n // tm,),
        in_specs=[row(d), row(RET_V_W), heads, heads, heads,
                  row(LSE_LANES), row(LSE_LANES), row(LSE_LANES), row(2 * D_MODEL),
                  full(w_ret), full(w_att), full(w_out)],
        out_specs=row(d),
        out_shape=jax.ShapeDtypeStruct((n, d), _F32),
        compiler_params=_params("parallel"),
        name="merge_out_proj",
    )(x, o_ret, *ogs, *lses, z, w_ret, w_att, w_out)


def _mlp_kernel(x_ref, g_ref, wup_ref, wdown_ref, p_ref, wple_ref, wgate_ref, gfin_ref, y_ref, h_ref, acc_ref, *,
                final_norm):
    j = pl.program_id(1)

    @pl.when(j == 0)
    def _():
        h_ref[...] = _rmsnorm(x_ref[...], g_ref[...]).astype(_BF16)
        acc_ref[...] = jnp.zeros_like(acc_ref)

    u = jnp.maximum(_dot(h_ref[...], wup_ref[...]), 0.0)
    acc_ref[...] += _dot((u * u).astype(_BF16), wdown_ref[...])

    @pl.when(j == pl.num_programs(1) - 1)
    def _():
        x = x_ref[...] + acc_ref[...]
        gate = jax.nn.sigmoid(_dot(x.astype(_BF16), wgate_ref[...]))
        x = x + _dot(p_ref[...].astype(_BF16), wple_ref[...]) * gate
        if final_norm:
            x = _rmsnorm(x, gfin_ref[...])
        y_ref[...] = x


def _mlp(x, g, w_up, w_down, p, w_ple, w_gate, g_final, *, tm, tf, final_norm):
    n, d = x.shape
    row = lambda w: pl.BlockSpec((tm, w), lambda i, j: (i, 0))
    full = lambda a: pl.BlockSpec(a.shape, lambda i, j: (0, 0))
    return pl.pallas_call(
        functools.partial(_mlp_kernel, final_norm=final_norm),
        grid=(n // tm, D_FF // tf),
        in_specs=[row(d), full(g),
                  pl.BlockSpec((d, tf), lambda i, j: (0, j)),
                  pl.BlockSpec((tf, d), lambda i, j: (j, 0)),
                  row(D_PLE), full(w_ple), full(w_gate), full(g_final)],
        out_specs=row(d),
        out_shape=jax.ShapeDtypeStruct((n, d), _F32),
        scratch_shapes=[pltpu.VMEM((tm, d), _BF16), pltpu.VMEM((tm, d), _F32)],
        compiler_params=_params("parallel", "arbitrary"),
        name="mlp_ple",
    )(x, g, w_up, w_down, p, w_ple, w_gate, g_final)


def _layer_weights(norm_mix, w_in, w_ret_br, w_att_br, w_out, norm_ffn, w_up, w_down, w_ple, w_ple_gate, i):
    gates_first = jnp.concatenate([w_in[i][:, D_IN - 2 * D_MODEL:], w_in[i][:, :D_IN - 2 * D_MODEL]], axis=1)
    return dict(
        norm_mix=norm_mix[i][None, :], w_in=gates_first.astype(_BF16),
        w_ret_br=w_ret_br[i].astype(_BF16), w_att_br=w_att_br[i].astype(_BF16), w_out=w_out[i].astype(_BF16),
        norm_ffn=norm_ffn[i][None, :], w_up=w_up[i].astype(_BF16), w_down=w_down[i].astype(_BF16),
        w_ple=w_ple[i].astype(_BF16), w_ple_gate=w_ple_gate[i].astype(_BF16))


def _mix_and_ffn(x, z, o_ret, ogs, lses, p_l, lw, g_final, final_norm, *, tm_merge, tm_mlp, tf):
    n = x.shape[0]
    flat = lambda a: a.reshape(n, a.shape[-1])
    x = _merge(x, flat(o_ret), ogs, lses, z,
               lw["w_ret_br"], lw["w_att_br"], lw["w_out"], tm=tm_merge)
    return _mlp(x, lw["norm_ffn"], lw["w_up"], lw["w_down"], p_l, lw["w_ple"], lw["w_ple_gate"], g_final,
                tm=tm_mlp, tf=tf, final_norm=final_norm)


def kernel(x_prompt, x_sample, cache_win_k0, cache_win_v0, cache_win_k1, cache_win_v1, cache_win_k2, cache_win_v2,
           state_ret, p_prompt, p_sample, norm_mix, w_in, w_ret_br, w_att_br, w_out, norm_ffn, w_up, w_down,
           w_ple, w_ple_gate, norm_final):
    depth = w_in.shape[0]
    bp, tp, d = x_prompt.shape
    bs, ts, _ = x_sample.shape
    xp = x_prompt.reshape(bp * tp, d)
    xs = x_sample.reshape(bs * ts, d)
    g_final = norm_final[None, :]
    pos_head_rows = lambda c: c.reshape(c.shape[:2] + (c.shape[2] * HPG, ATT_HEAD_DIM))
    caches_k = [pos_head_rows(c) for c in (cache_win_k0, cache_win_k1, cache_win_k2)]
    caches_v = [pos_head_rows(c) for c in (cache_win_v0, cache_win_v1, cache_win_v2)]
    new_k = [None] * N_GROUPS
    new_v = [None] * N_GROUPS
    pk = [[] for _ in range(N_GROUPS)]
    pv = [[] for _ in range(N_GROUPS)]
    prompt_ret, sample_ret = [], []
    sample_chunk = math.gcd(ts, RET_CHUNK)

    for i in range(depth):
        lw = _layer_weights(norm_mix, w_in, w_ret_br, w_att_br, w_out, norm_ffn, w_up, w_down, w_ple, w_ple_gate, i)
        last = i == depth - 1

        z = _norm_proj(xp, lw["norm_mix"], lw["w_in"], tm=1024)
        z3 = z.reshape(bp, tp, D_IN)
        o_ret, ret_state = _retention(z3, jnp.zeros((bp, RET_HEADS, RET_DK, RET_DV), _F32),
                                      chunk=RET_CHUNK, n_chunks=4)
        prompt_ret.append(ret_state)
        ogs, lses = zip(*[_band_attention(z3, g, tile=2048) for g in range(N_GROUPS)])
        for g, (window, _) in enumerate(ATT_GROUPS):
            keep = min(window, tp)
            ck = COL_AK + g * GROUP_W
            cv = COL_AV + g * GROUP_W
            pk[g].append(z3[:, tp - keep:, ck:ck + GROUP_W].reshape(bp, keep, HPG, ATT_HEAD_DIM))
            pv[g].append(z3[:, tp - keep:, cv:cv + GROUP_W].reshape(bp, keep, HPG, ATT_HEAD_DIM))
        xp = _mix_and_ffn(xp, z, o_ret, ogs, lses, p_prompt[i].reshape(bp * tp, D_PLE), lw, g_final, last,
                          tm_merge=256, tm_mlp=512, tf=1024)

        z = _norm_proj(xs, lw["norm_mix"], lw["w_in"], tm=bs * ts)
        z3 = z.reshape(bs, ts, D_IN)
        o_ret, ret_state = _retention(z3, state_ret[i], chunk=sample_chunk, n_chunks=ts // sample_chunk)
        sample_ret.append(ret_state)
        ogs, lses = [], []
        for g in range(N_GROUPS):
            o_g, lse_g, new_k[g], new_v[g] = _step_attention(z3, caches_k[g], caches_v[g], new_k[g], new_v[g], i, g)
            ogs.append(o_g)
            lses.append(lse_g)
        xs = _mix_and_ffn(xs, z, o_ret, ogs, lses, p_sample[i].reshape(bs * ts, D_PLE), lw, g_final, last,
                          tm_merge=bs * ts, tm_mlp=bs * ts, tf=512)

    as_heads = lambda a: a.reshape(a.shape[:2] + (a.shape[2] // HPG, HPG, ATT_HEAD_DIM))
    prompt_windows = [jnp.stack(a) for g in range(N_GROUPS) for a in (pk[g], pv[g])]
    sample_windows = [as_heads(a) for g in range(N_GROUPS) for a in (new_k[g], new_v[g])]
    return (xp.reshape(bp, tp, d), xs.reshape(bs, ts, d), *prompt_windows, jnp.stack(prompt_ret),
            *sample_windows, jnp.stack(sample_ret))
```

```python
import functools
import math

import jax
import jax.numpy as jnp
import numpy as np
from jax import lax
from jax.experimental import pallas as pl
from jax.experimental.pallas import tpu as pltpu

D_MODEL = 1024
D_PLE = 256
RET_HEADS = 4
RET_DK = 128
RET_DV = 256
RET_CHUNK = 128
ATT_GROUPS = ((128, 1), (512, 4), (2048, 16))
N_GROUPS = 3
HPG = 4
ATT_HEAD_DIM = 128
ATT_HEADS = N_GROUPS * HPG
ATT_TAPS = 128
Q_BLOCK = 128
BAND_BLOCKS_IN_FLIGHT = 16
D_FF = 4 * D_MODEL
EPS = 1e-6

RET_QK_W = RET_HEADS * RET_DK
RET_V_W = RET_HEADS * RET_DV
GROUP_W = HPG * ATT_HEAD_DIM
ATT_W = ATT_HEADS * ATT_HEAD_DIM
D_IN = 2 * RET_QK_W + 2 * RET_V_W + 3 * ATT_W + 2 * D_MODEL

MIX_W = 2 * D_MODEL + 2 * RET_QK_W + 2 * RET_V_W
COL_GATES = 0
COL_RQ = 2 * D_MODEL
COL_RK = COL_RQ + RET_QK_W
COL_RV = COL_RK + RET_QK_W
COL_RG = COL_RV + RET_V_W
COL_AQ = 0
COL_AK = ATT_W
COL_AV = 2 * ATT_W
COL_BLOCK = 512

LSE_LANES = 128
VMEM_LIMIT = 48 * 1024 * 1024

_BF16 = jnp.bfloat16
_F32 = jnp.float32


def _params(*sem):
    return pltpu.CompilerParams(dimension_semantics=sem, vmem_limit_bytes=VMEM_LIMIT)


def _rmsnorm(x, g):
    return x * lax.rsqrt(jnp.mean(x * x, axis=-1, keepdims=True) + EPS) * g


def _dot(a, b):
    return jnp.dot(a, b, preferred_element_type=_F32)


def _dot_nt(a, b):
    return lax.dot_general(a, b, (((1,), (1,)), ((), ())), preferred_element_type=_F32)


def _dot_tn(a, b):
    return lax.dot_general(a, b, (((0,), (0,)), ((), ())), preferred_element_type=_F32)


def _norm_proj_kernel(x_ref, g_ref, w_ref, z_ref, h_ref):
    @pl.when(pl.program_id(1) == 0)
    def _():
        h_ref[...] = _rmsnorm(x_ref[...], g_ref[...]).astype(_BF16)

    z_ref[...] = _dot(h_ref[...], w_ref[...]).astype(z_ref.dtype)


def _norm_proj(x, g, w, out_dtype, *, tm, tn=COL_BLOCK):
    n, d = x.shape
    d_out = w.shape[1]
    return pl.pallas_call(
        _norm_proj_kernel,
        grid=(n // tm, d_out // tn),
        in_specs=[
            pl.BlockSpec((tm, d), lambda i, j: (i, 0)),
            pl.BlockSpec((1, d), lambda i, j: (0, 0)),
            pl.BlockSpec((d, tn), lambda i, j: (0, j)),
        ],
        out_specs=pl.BlockSpec((tm, tn), lambda i, j: (i, j)),
        out_shape=jax.ShapeDtypeStruct((n, d_out), out_dtype),
        scratch_shapes=[pltpu.VMEM((tm, d), _BF16)],
        compiler_params=_params("parallel", "arbitrary"),
        name="norm_proj",
    )(x, g, w)


def _ret_log_gamma():
    return jnp.log1p(-jnp.exp(jnp.linspace(math.log(1.0 / 32), math.log(1.0 / 512), RET_HEADS))).astype(_F32)


def _retention_tables(chunk):
    lg = _ret_log_gamma()
    pos = jnp.arange(chunk, dtype=_F32)
    diff = pos[:, None] - pos[None, :]
    intra = jnp.where(diff[None] >= 0, jnp.exp(lg[:, None, None] * jnp.maximum(diff, 0.0)[None]), 0.0)
    xi = jnp.exp(lg[:, None] * (pos[None] + 1.0))
    zeta = jnp.exp(lg[:, None] * (chunk - 1.0 - pos)[None])
    decay = jnp.exp(lg * chunk)
    return (intra * (RET_DK ** -0.5),
            jnp.broadcast_to(xi[:, :, None], (RET_HEADS, chunk, RET_DK)),
            jnp.broadcast_to(zeta[:, :, None], (RET_HEADS, chunk, RET_DV)),
            jnp.broadcast_to(decay[:, None, None], (RET_HEADS, 8, RET_DV)))


def _retention_kernel(q_ref, k_ref, v_ref, g_ref, s0_ref, intra_ref, xi_ref, zeta_ref, decay_ref,
                      o_ref, sfin_ref, state, *, chunk, n_chunks):
    j = pl.program_id(1)

    @pl.when(j == 0)
    def _():
        state[...] = s0_ref[...]

    for c in range(n_chunks):
        rows = slice(c * chunk, (c + 1) * chunk)
        for h in range(RET_HEADS):
            qk_cols = slice(h * RET_DK, (h + 1) * RET_DK)
            v_cols = slice(h * RET_DV, (h + 1) * RET_DV)
            q = q_ref[rows, qk_cols]
            kb = k_ref[rows, qk_cols]
            v = v_ref[rows, v_cols]
            g = g_ref[rows, v_cols].astype(_F32)
            r_prev = state[h]
            scores = _dot_nt(q, kb) * intra_ref[h]
            o = _dot(scores.astype(_BF16), v)
            o = o + _dot((q * xi_ref[h]).astype(_BF16), r_prev.astype(_BF16))
            u = _dot_tn(kb, (v * zeta_ref[h]).astype(_BF16)) * (RET_DK ** -0.5)
            state[h] = decay_ref[h, 0:1, :] * r_prev + u
            mu = jnp.mean(o, axis=-1, keepdims=True)
            oc = o - mu
            var = jnp.mean(oc * oc, axis=-1, keepdims=True)
            on = oc * lax.rsqrt(var + EPS)
            o_ref[rows, v_cols] = (on * (g * jax.nn.sigmoid(g))).astype(o_ref.dtype)

    @pl.when(j == pl.num_programs(1) - 1)
    def _():
        sfin_ref[...] = state[...]


def _retention(z3, state0, *, chunk, n_chunks):
    b, t, _ = z3.shape
    tc = chunk * n_chunks
    intra, xi, zeta, decay = _retention_tables(chunk)
    const = lambda shape: pl.BlockSpec(shape, lambda bi, j: (0,) * len(shape))
    state_spec = pl.BlockSpec((None, RET_HEADS, RET_DK, RET_DV), lambda bi, j: (bi, 0, 0, 0))
    kernel = functools.partial(_retention_kernel, chunk=chunk, n_chunks=n_chunks)
    return pl.pallas_call(
        kernel,
        grid=(b, t // tc),
        in_specs=[
            pl.BlockSpec((None, tc, RET_QK_W), lambda bi, j: (bi, j, COL_RQ // RET_QK_W)),
            pl.BlockSpec((None, tc, RET_QK_W), lambda bi, j: (bi, j, COL_RK // RET_QK_W)),
            pl.BlockSpec((None, tc, RET_V_W), lambda bi, j: (bi, j, COL_RV // RET_V_W)),
            pl.BlockSpec((None, tc, RET_V_W), lambda bi, j: (bi, j, COL_RG // RET_V_W)),
            state_spec,
            const((RET_HEADS, chunk, chunk)),
            const((RET_HEADS, chunk, RET_DK)),
            const((RET_HEADS, chunk, RET_DV)),
            const((RET_HEADS, 8, RET_DV)),
        ],
        out_specs=[
            pl.BlockSpec((None, tc, RET_V_W), lambda bi, j: (bi, j, 0)),
            state_spec,
        ],
        out_shape=[
            jax.ShapeDtypeStruct((b, t, RET_V_W), _BF16),
            jax.ShapeDtypeStruct((b, RET_HEADS, RET_DK, RET_DV), _F32),
        ],
        scratch_shapes=[pltpu.VMEM((RET_HEADS, RET_DK, RET_DV), _F32)],
        compiler_params=_params("parallel", "arbitrary"),
        name="retention",
    )(z3, z3, z3, z3, state0, intra, xi, zeta, decay)


def _alibi_slope(head):
    return 2.0 ** (-8.0 * (head + 1.0) / ATT_HEADS)


def _pack_head_stats(cols):
    rows = cols[0].shape[0]
    lane = lax.broadcasted_iota(jnp.int32, (rows, LSE_LANES), 1)
    out = jnp.zeros((rows, LSE_LANES), _F32)
    for h, c in enumerate(cols):
        out = jnp.where(lane == h, c, out)
    return out


def _band_attn_kernel(q_ref, k_ref, kh_ref, v_ref, vh_ref, o_ref, lse_ref, *, n_sub, dil, group):
    i = pl.program_id(1)
    h = pl.program_id(2)
    blk = Q_BLOCK
    span = blk * dil
    row = lax.broadcasted_iota(jnp.int32, (blk, 2 * blk), 0)
    col = lax.broadcasted_iota(jnp.int32, (blk, 2 * blk), 1)
    delta = row + blk - col
    slope = jnp.float32(0.0)
    for hh in range(HPG):
        slope = jnp.where(h == hh, jnp.float32(_alibi_slope(group * HPG + hh)), slope)
    in_band = (delta >= 0) & (delta <= ATT_TAPS)
    bias = jnp.where(in_band, -slope * (delta * dil).astype(_F32), -jnp.inf)
    bias_first = jnp.where(col >= jnp.where(i > 0, 0, blk), bias, -jnp.inf)
    lane = lax.broadcasted_iota(jnp.int32, (blk, LSE_LANES), 1)
    scale = ATT_HEAD_DIM ** -0.5

    def strided(start, size):
        return pl.ds(start, size, stride=dil) if dil > 1 else pl.ds(start, size)

    def attend(rows, q, keys, values, bias_blk):
        s = _dot_nt(q.astype(_BF16), keys.astype(_BF16)) * scale + bias_blk
        m = jnp.max(s, axis=-1, keepdims=True)
        e = jnp.exp(s - m)
        den = jnp.sum(e, axis=-1, keepdims=True)
        o_ref[rows, :] = _dot((e / den).astype(_BF16), values.astype(_BF16))
        prev = jnp.where(h == 0, 0.0, lse_ref[rows, :])
        lse_ref[rows, :] = jnp.where(lane == h, m + jnp.log(den), prev)

    def residue_body(r, carry):
        rows = strided(r, blk)
        keys = jnp.concatenate([kh_ref[rows, :], k_ref[rows, :]], axis=0)
        values = jnp.concatenate([vh_ref[rows, :], v_ref[rows, :]], axis=0)
        attend(rows, q_ref[rows, :], keys, values, bias_first)

        for sb in range(1, n_sub):
            rows = strided(r + sb * span, blk)
            both = strided(r + (sb - 1) * span, 2 * blk)
            attend(rows, q_ref[rows, :], k_ref[both, :], v_ref[both, :], bias)
        return carry

    lax.fori_loop(0, dil, residue_body, 0, unroll=min(dil, max(1, BAND_BLOCKS_IN_FLIGHT // n_sub)))


def _band_attention(z3, group, *, tile):
    b, t, _ = z3.shape
    _, dil = ATT_GROUPS[group]
    span = Q_BLOCK * dil
    tile = max(min(tile, t), span)
    n_sub = tile // span
    tiles = t // tile
    cq = (COL_AQ + group * GROUP_W) // ATT_HEAD_DIM
    ck = (COL_AK + group * GROUP_W) // ATT_HEAD_DIM
    cv = (COL_AV + group * GROUP_W) // ATT_HEAD_DIM

    def main(c):
        return pl.BlockSpec((None, tile, ATT_HEAD_DIM), lambda bi, i, h: (bi, i, c + h))

    def halo(c):
        return pl.BlockSpec((None, span, ATT_HEAD_DIM), lambda bi, i, h: (bi, jnp.maximum(i * n_sub - 1, 0), c + h))

    kernel = functools.partial(_band_attn_kernel, n_sub=n_sub, dil=dil, group=group)
    return pl.pallas_call(
        kernel,
        grid=(b, tiles, HPG),
        in_specs=[main(cq), main(ck), halo(ck), main(cv), halo(cv)],
        out_specs=[
            pl.BlockSpec((None, tile, ATT_HEAD_DIM), lambda bi, i, h: (h, bi * tiles + i, 0)),
            pl.BlockSpec((tile, LSE_LANES), lambda bi, i, h: (bi * tiles + i, 0)),
        ],
        out_shape=[
            jax.ShapeDtypeStruct((HPG, b * t, ATT_HEAD_DIM), _F32),
            jax.ShapeDtypeStruct((b * t, LSE_LANES), _F32),
        ],
        compiler_params=_params("parallel", "parallel", "arbitrary"),
        name="band_attention_g%d" % group,
    )(z3, z3, z3, z3, z3)


def _step_attn_kernel(q_ref, kn_ref, vn_ref, kc_ref, vc_ref, o_ref, lse_ref, ko_ref, vo_ref, *,
                      t_new, cache_len, dil, group):
    n_res = min(dil, t_new)
    taps = cache_len // dil
    rows_all = HPG * t_new
    head_cols = [slice(h * ATT_HEAD_DIM, (h + 1) * ATT_HEAD_DIM) for h in range(HPG)]
    head_rows = [slice(h * t_new, (h + 1) * t_new) for h in range(HPG)]
    q = [q_ref[:, c].astype(_BF16) for c in head_cols]
    k_new = kn_ref[...]
    v_new = vn_ref[...]

    def cache_taps(ref, r, h):
        return ref[pl.ds(r * HPG + h, taps, stride=HPG * dil), :].astype(_BF16)

    scale = ATT_HEAD_DIM ** -0.5
    log2_dil = dil.bit_length() - 1
    log2_new = t_new.bit_length() - 1

    def row_terms(width):
        row_id = lax.broadcasted_iota(jnp.int32, (rows_all, width), 0)
        slope = jnp.zeros((rows_all, width), _F32)
        for h in range(HPG):
            slope = jnp.where((row_id >> log2_new) == h, _alibi_slope(group * HPG + h), slope)
        return row_id & (t_new - 1), slope

    query, slope = row_terms(taps)
    residue = query & (dil - 1)
    taps_back = taps + (query >> log2_dil) - lax.broadcasted_iota(jnp.int32, (rows_all, taps), 1)
    s_cache = None
    for r in range(n_res):
        s_r = jnp.concatenate([_dot_nt(q[h], cache_taps(kc_ref, r, h)) for h in range(HPG)], axis=0)
        s_cache = s_r if s_cache is None else jnp.where(residue == r, s_r, s_cache)
    s_cache = s_cache * scale - slope * (taps_back << log2_dil).astype(_F32)
    s_cache = jnp.where(taps_back <= ATT_TAPS, s_cache, -jnp.inf)

    query_n, slope_n = row_terms(t_new)
    back = query_n - lax.broadcasted_iota(jnp.int32, (rows_all, t_new), 1)
    s_new = jnp.concatenate([_dot_nt(q[h], k_new[:, head_cols[h]].astype(_BF16)) for h in range(HPG)], axis=0)
    s_new = s_new * scale - slope_n * back.astype(_F32)
    s_new = jnp.where((back >= 0) & ((back & (dil - 1)) == 0), s_new, -jnp.inf)

    m = jnp.maximum(jnp.max(s_cache, axis=-1, keepdims=True), jnp.max(s_new, axis=-1, keepdims=True))
    e_cache = jnp.exp(s_cache - m)
    e_new = jnp.exp(s_new - m)
    den = jnp.sum(e_cache, axis=-1, keepdims=True) + jnp.sum(e_new, axis=-1, keepdims=True)
    p_cache = e_cache / den
    p_new = (e_new / den).astype(_BF16)
    lse = m + jnp.log(den)

    for h in range(HPG):
        out = _dot(p_new[head_rows[h]], v_new[:, head_cols[h]].astype(_BF16))
        for r in range(n_res):
            p_r = p_cache[head_rows[h]]
            if n_res > 1:
                p_r = jnp.where(residue[head_rows[h]] == r, p_r, 0.0)
            out = out + _dot(p_r.astype(_BF16), cache_taps(vc_ref, r, h))
        o_ref[h] = out
    lse_ref[...] = _pack_head_stats([lse[rows] for rows in head_rows])

    keep = (cache_len - t_new) * HPG
    ko_ref[0:keep, :] = kc_ref[t_new * HPG:, :]
    vo_ref[0:keep, :] = vc_ref[t_new * HPG:, :]
    for h in range(HPG):
        ko_ref[pl.ds(keep + h, t_new, stride=HPG), :] = k_new[:, head_cols[h]]
        vo_ref[pl.ds(keep + h, t_new, stride=HPG), :] = v_new[:, head_cols[h]]


def _step_attention(z3, cache_k, cache_v, prev_k, prev_v, layer, group):
    b, t_new, _ = z3.shape
    cache_len = cache_k.shape[2] // HPG
    _, dil = ATT_GROUPS[group]
    cq = (COL_AQ + group * GROUP_W) // COL_BLOCK
    ck = (COL_AK + group * GROUP_W) // COL_BLOCK
    cv = (COL_AV + group * GROUP_W) // COL_BLOCK
    zspec = lambda c: pl.BlockSpec((None, t_new, COL_BLOCK), lambda bi: (bi, 0, c))
    cache_spec = pl.BlockSpec((None, None, cache_len * HPG, ATT_HEAD_DIM), lambda bi: (layer, bi, 0, 0))
    kernel = functools.partial(_step_attn_kernel, t_new=t_new, cache_len=cache_len, dil=dil, group=group)
    in_specs = [zspec(cq), zspec(ck), zspec(cv), cache_spec, cache_spec]
    args = [z3, z3, z3, cache_k, cache_v]
    aliases = {}
    if prev_k is not None:
        in_specs += [pl.BlockSpec(memory_space=pl.ANY)] * 2
        args += [prev_k, prev_v]
        aliases = {5: 2, 6: 3}
        kernel = functools.partial(_drop_refs, kernel, 5, 2)
    cache_shape = jax.ShapeDtypeStruct(cache_k.shape, cache_k.dtype)
    return pl.pallas_call(
        kernel,
        grid=(b,),
        in_specs=in_specs,
        out_specs=[
            pl.BlockSpec((HPG, t_new, ATT_HEAD_DIM), lambda bi: (0, bi, 0)),
            pl.BlockSpec((t_new, LSE_LANES), lambda bi: (bi, 0)),
            cache_spec, cache_spec,
        ],
        out_shape=[
            jax.ShapeDtypeStruct((HPG, b * t_new, ATT_HEAD_DIM), _F32),
            jax.ShapeDtypeStruct((b * t_new, LSE_LANES), _F32),
            cache_shape, cache_shape,
        ],
        input_output_aliases=aliases,
        compiler_params=_params("parallel"),
        name="step_attention_g%d" % group,
    )(*args)


def _drop_refs(kernel, start, count, *refs):
    return kernel(*refs[:start], *refs[start + count:])


def _merge_kernel(x_ref, oret_ref, og0_ref, og1_ref, og2_ref, l0_ref, l1_ref, l2_ref, gates_ref,
                  wret_ref, watt_ref, wout_ref, y_ref):
    l0, l1, l2 = l0_ref[...], l1_ref[...], l2_ref[...]
    mx = jnp.maximum(jnp.maximum(l0, l1), l2)
    e0, e1, e2 = jnp.exp(l0 - mx), jnp.exp(l1 - mx), jnp.exp(l2 - mx)
    tot = e0 + e1 + e2
    w0, w1, w2 = e0 / tot, e1 / tot, e2 / tot
    parts = []
    for h in range(HPG):
        parts.append(w0[:, h:h + 1] * og0_ref[h] + w1[:, h:h + 1] * og1_ref[h] + w2[:, h:h + 1] * og2_ref[h])
    o_att = jnp.concatenate(parts, axis=-1).astype(_BF16)
    br_ret = _dot(oret_ref[...], wret_ref[...])
    br_att = _dot(o_att, watt_ref[...])
    ga = gates_ref[:, 0:D_MODEL].astype(_F32)
    gb = gates_ref[:, D_MODEL:2 * D_MODEL].astype(_F32)
    mix = jax.nn.sigmoid(ga) * br_ret + jax.nn.sigmoid(gb) * br_att
    y_ref[...] = x_ref[...] + _dot(mix.astype(_BF16), wout_ref[...])


def _merge(x, o_ret, ogs, lses, z, w_ret, w_att, w_out, *, tm):
    n, d = x.shape
    row = lambda w: pl.BlockSpec((tm, w), lambda i: (i, 0))
    full = lambda a: pl.BlockSpec(a.shape, lambda i: (0, 0))
    heads = pl.BlockSpec((HPG, tm, ATT_HEAD_DIM), lambda i: (0, i, 0))
    return pl.pallas_call(
        _merge_kernel,
        grid=(n // tm,),
        in_specs=[row(d), row(RET_V_W), heads, heads, heads,
                  row(LSE_LANES), row(LSE_LANES), row(LSE_LANES), row(2 * D_MODEL),
                  full(w_ret), full(w_att), full(w_out)],
        out_specs=row(d),
        out_shape=jax.ShapeDtypeStruct((n, d), _F32),
        compiler_params=_params("parallel"),
        name="merge_out_proj",
    )(x, o_ret, *ogs, *lses, z, w_ret, w_att, w_out)


def _mlp_kernel(x_ref, g_ref, wup_ref, wdown_ref, p_ref, wple_ref, wgate_ref, gfin_ref, y_ref, h_ref, acc_ref, *,
                final_norm):
    j = pl.program_id(1)

    @pl.when(j == 0)
    def _():
        h_ref[...] = _rmsnorm(x_ref[...], g_ref[...]).astype(_BF16)
        acc_ref[...] = jnp.zeros_like(acc_ref)

    u = jnp.maximum(_dot(h_ref[...], wup_ref[...]), 0.0)
    acc_ref[...] += _dot((u * u).astype(_BF16), wdown_ref[...])

    @pl.when(j == pl.num_programs(1) - 1)
    def _():
        x = x_ref[...] + acc_ref[...]
        gate = jax.nn.sigmoid(_dot(x.astype(_BF16), wgate_ref[...]))
        x = x + _dot(p_ref[...].astype(_BF16), wple_ref[...]) * gate
        if final_norm:
            x = _rmsnorm(x, gfin_ref[...])
        y_ref[...] = x


def _mlp(x, g, w_up, w_down, p, w_ple, w_gate, g_final, *, tm, tf, final_norm):
    n, d = x.shape
    row = lambda w: pl.BlockSpec((tm, w), lambda i, j: (i, 0))
    full = lambda a: pl.BlockSpec(a.shape, lambda i, j: (0, 0))
    return pl.pallas_call(
        functools.partial(_mlp_kernel, final_norm=final_norm),
        grid=(n // tm, D_FF // tf),
        in_specs=[row(d), full(g),
                  pl.BlockSpec((d, tf), lambda i, j: (0, j)),
                  pl.BlockSpec((tf, d), lambda i, j: (j, 0)),
                  row(D_PLE), full(w_ple), full(w_gate), full(g_final)],
        out_specs=row(d),
        out_shape=jax.ShapeDtypeStruct((n, d), _F32),
        scratch_shapes=[pltpu.VMEM((tm, d), _BF16), pltpu.VMEM((tm, d), _F32)],
        compiler_params=_params("parallel", "arbitrary"),
        name="mlp_ple",
    )(x, g, w_up, w_down, p, w_ple, w_gate, g_final)


def _layer_weights(norm_mix, w_in, w_ret_br, w_att_br, w_out, norm_ffn, w_up, w_down, w_ple, w_ple_gate, i):
    w = w_in[i]
    att_start = 2 * RET_QK_W + 2 * RET_V_W
    w_mix = jnp.concatenate([w[:, D_IN - 2 * D_MODEL:], w[:, :att_start]], axis=1)
    return dict(
        norm_mix=norm_mix[i][None, :], w_in_mix=w_mix.astype(_BF16),
        w_in_att=w[:, att_start:att_start + 3 * ATT_W].astype(_BF16),
        w_ret_br=w_ret_br[i].astype(_BF16), w_att_br=w_att_br[i].astype(_BF16), w_out=w_out[i].astype(_BF16),
        norm_ffn=norm_ffn[i][None, :], w_up=w_up[i].astype(_BF16), w_down=w_down[i].astype(_BF16),
        w_ple=w_ple[i].astype(_BF16), w_ple_gate=w_ple_gate[i].astype(_BF16))


def _mix_and_ffn(x, z_mix, o_ret, ogs, lses, p_l, lw, g_final, final_norm, *, tm_merge, tm_mlp, tf):
    n = x.shape[0]
    flat = lambda a: a.reshape(n, a.shape[-1])
    x = _merge(x, flat(o_ret), ogs, lses, z_mix,
               lw["w_ret_br"], lw["w_att_br"], lw["w_out"], tm=tm_merge)
    return _mlp(x, lw["norm_ffn"], lw["w_up"], lw["w_down"], p_l, lw["w_ple"], lw["w_ple_gate"], g_final,
                tm=tm_mlp, tf=tf, final_norm=final_norm)


def kernel(x_prompt, x_sample, cache_win_k0, cache_win_v0, cache_win_k1, cache_win_v1, cache_win_k2, cache_win_v2,
           state_ret, p_prompt, p_sample, norm_mix, w_in, w_ret_br, w_att_br, w_out, norm_ffn, w_up, w_down,
           w_ple, w_ple_gate, norm_final):
    depth = w_in.shape[0]
    bp, tp, d = x_prompt.shape
    bs, ts, _ = x_sample.shape
    xp = x_prompt.reshape(bp * tp, d)
    xs = x_sample.reshape(bs * ts, d)
    g_final = norm_final[None, :]
    pos_head_rows = lambda c: c.reshape(c.shape[:2] + (c.shape[2] * HPG, ATT_HEAD_DIM))
    caches_k = [pos_head_rows(c) for c in (cache_win_k0, cache_win_k1, cache_win_k2)]
    caches_v = [pos_head_rows(c) for c in (cache_win_v0, cache_win_v1, cache_win_v2)]
    new_k = [None] * N_GROUPS
    new_v = [None] * N_GROUPS
    pk = [[] for _ in range(N_GROUPS)]
    pv = [[] for _ in range(N_GROUPS)]
    prompt_ret, sample_ret = [], []
    sample_chunk = math.gcd(ts, RET_CHUNK)

    for i in range(depth):
        lw = _layer_weights(norm_mix, w_in, w_ret_br, w_att_br, w_out, norm_ffn, w_up, w_down, w_ple, w_ple_gate, i)
        last = i == depth - 1

        z_mix = _norm_proj(xp, lw["norm_mix"], lw["w_in_mix"], _BF16, tm=2048)
        z3 = _norm_proj(xp, lw["norm_mix"], lw["w_in_att"], _F32, tm=2048).reshape(bp, tp, 3 * ATT_W)
        o_ret, ret_state = _retention(z_mix.reshape(bp, tp, MIX_W), jnp.zeros((bp, RET_HEADS, RET_DK, RET_DV), _F32),
                                      chunk=RET_CHUNK, n_chunks=4)
        prompt_ret.append(ret_state)
        ogs, lses = zip(*[_band_attention(z3, g, tile=2048) for g in range(N_GROUPS)])
        for g, (window, _) in enumerate(ATT_GROUPS):
            keep = min(window, tp)
            ck = COL_AK + g * GROUP_W
            cv = COL_AV + g * GROUP_W
            pk[g].append(z3[:, tp - keep:, ck:ck + GROUP_W].reshape(bp, keep, HPG, ATT_HEAD_DIM))
            pv[g].append(z3[:, tp - keep:, cv:cv + GROUP_W].reshape(bp, keep, HPG, ATT_HEAD_DIM))
        xp = _mix_and_ffn(xp, z_mix, o_ret, ogs, lses, p_prompt[i].reshape(bp * tp, D_PLE), lw, g_final, last,
                          tm_merge=256, tm_mlp=512, tf=1024)

        z_mix = _norm_proj(xs, lw["norm_mix"], lw["w_in_mix"], _BF16, tm=bs * ts)
        z3 = _norm_proj(xs, lw["norm_mix"], lw["w_in_att"], _F32, tm=bs * ts).reshape(bs, ts, 3 * ATT_W)
        o_ret, ret_state = _retention(z_mix.reshape(bs, ts, MIX_W), state_ret[i], chunk=sample_chunk, n_chunks=ts // sample_chunk)
        sample_ret.append(ret_state)
        ogs, lses = [], []
        for g in range(N_GROUPS):
            o_g, lse_g, new_k[g], new_v[g] = _step_attention(z3, caches_k[g], caches_v[g], new_k[g], new_v[g], i, g)
            ogs.append(o_g)
            lses.append(lse_g)
        xs = _mix_and_ffn(xs, z_mix, o_ret, ogs, lses, p_sample[i].reshape(bs * ts, D_PLE), lw, g_final, last,
                          tm_merge=bs * ts, tm_mlp=bs * ts, tf=512)

    as_heads = lambda a: a.reshape(a.shape[:2] + (a.shape[2] // HPG, HPG, ATT_HEAD_DIM))
    prompt_windows = [jnp.stack(a) for g in range(N_GROUPS) for a in (pk[g], pv[g])]
    sample_windows = [as_heads(a) for g in range(N_GROUPS) for a in (new_k[g], new_v[g])]
    return (xp.reshape(bp, tp, d), xs.reshape(bs, ts, d), *prompt_windows, jnp.stack(prompt_ret),
            *sample_windows, jnp.stack(sample_ret))
```

```python
import functools
import math

import jax
import jax.numpy as jnp
import numpy as np
from jax import lax
from jax.experimental import pallas as pl
from jax.experimental.pallas import tpu as pltpu

D_MODEL = 1024
D_PLE = 256
RET_HEADS = 4
RET_DK = 128
RET_DV = 256
RET_CHUNK = 128
ATT_GROUPS = ((128, 1), (512, 4), (2048, 16))
N_GROUPS = 3
HPG = 4
ATT_HEAD_DIM = 128
ATT_HEADS = N_GROUPS * HPG
ATT_TAPS = 128
Q_BLOCK = 128
BAND_BLOCKS_IN_FLIGHT = 16
D_FF = 4 * D_MODEL
EPS = 1e-6

RET_QK_W = RET_HEADS * RET_DK
RET_V_W = RET_HEADS * RET_DV
GROUP_W = HPG * ATT_HEAD_DIM
ATT_W = ATT_HEADS * ATT_HEAD_DIM
D_IN = 2 * RET_QK_W + 2 * RET_V_W + 3 * ATT_W + 2 * D_MODEL

MIX_W = 2 * D_MODEL + 2 * RET_QK_W + 2 * RET_V_W
COL_GATES = 0
COL_RQ = 2 * D_MODEL
COL_RK = COL_RQ + RET_QK_W
COL_RV = COL_RK + RET_QK_W
COL_RG = COL_RV + RET_V_W
COL_AQ = 0
COL_AK = ATT_W
COL_AV = 2 * ATT_W
COL_BLOCK = 512

LSE_LANES = 128
VMEM_LIMIT = 48 * 1024 * 1024

_BF16 = jnp.bfloat16
_F32 = jnp.float32


def _params(*sem):
    return pltpu.CompilerParams(dimension_semantics=sem, vmem_limit_bytes=VMEM_LIMIT)


def _rmsnorm(x, g):
    return x * lax.rsqrt(jnp.mean(x * x, axis=-1, keepdims=True) + EPS) * g


def _dot(a, b):
    return jnp.dot(a, b, preferred_element_type=_F32)


def _dot_nt(a, b):
    return lax.dot_general(a, b, (((1,), (1,)), ((), ())), preferred_element_type=_F32)


def _dot_tn(a, b):
    return lax.dot_general(a, b, (((0,), (0,)), ((), ())), preferred_element_type=_F32)


def _norm_proj_kernel(x_ref, g_ref, w_ref, z_ref, h_ref):
    @pl.when(pl.program_id(1) == 0)
    def _():
        h_ref[...] = _rmsnorm(x_ref[...], g_ref[...]).astype(_BF16)

    z_ref[...] = _dot(h_ref[...], w_ref[...]).astype(z_ref.dtype)


def _norm_proj(x, g, w, out_dtype, *, tm, tn=COL_BLOCK):
    n, d = x.shape
    d_out = w.shape[1]
    return pl.pallas_call(
        _norm_proj_kernel,
        grid=(n // tm, d_out // tn),
        in_specs=[
            pl.BlockSpec((tm, d), lambda i, j: (i, 0)),
            pl.BlockSpec((1, d), lambda i, j: (0, 0)),
            pl.BlockSpec((d, tn), lambda i, j: (0, j)),
        ],
        out_specs=pl.BlockSpec((tm, tn), lambda i, j: (i, j)),
        out_shape=jax.ShapeDtypeStruct((n, d_out), out_dtype),
        scratch_shapes=[pltpu.VMEM((tm, d), _BF16)],
        compiler_params=_params("parallel", "arbitrary"),
        name="norm_proj",
    )(x, g, w)


def _ret_log_gamma():
    return jnp.log1p(-jnp.exp(jnp.linspace(math.log(1.0 / 32), math.log(1.0 / 512), RET_HEADS))).astype(_F32)


def _retention_tables(chunk):
    lg = _ret_log_gamma()
    pos = jnp.arange(chunk, dtype=_F32)
    diff = pos[:, None] - pos[None, :]
    intra = jnp.where(diff[None] >= 0, jnp.exp(lg[:, None, None] * jnp.maximum(diff, 0.0)[None]), 0.0)
    xi = jnp.exp(lg[:, None] * (pos[None] + 1.0))
    zeta = jnp.exp(lg[:, None] * (chunk - 1.0 - pos)[None])
    decay = jnp.exp(lg * chunk)
    return (intra * (RET_DK ** -0.5),
            jnp.broadcast_to(xi[:, :, None], (RET_HEADS, chunk, RET_DK)),
            jnp.broadcast_to(zeta[:, :, None], (RET_HEADS, chunk, RET_DV)),
            jnp.broadcast_to(decay[:, None, None], (RET_HEADS, 8, RET_DV)))


def _retention_kernel(q_ref, k_ref, v_ref, g_ref, s0_ref, intra_ref, xi_ref, zeta_ref, decay_ref,
                      o_ref, sfin_ref, state, *, chunk, n_chunks):
    j = pl.program_id(1)

    @pl.when(j == 0)
    def _():
        state[...] = s0_ref[...]

    for c in range(n_chunks):
        rows = slice(c * chunk, (c + 1) * chunk)
        for h in range(RET_HEADS):
            qk_cols = slice(h * RET_DK, (h + 1) * RET_DK)
            v_cols = slice(h * RET_DV, (h + 1) * RET_DV)
            q = q_ref[rows, qk_cols]
            kb = k_ref[rows, qk_cols]
            v = v_ref[rows, v_cols]
            g = g_ref[rows, v_cols].astype(_F32)
            r_prev = state[h]
            scores = _dot_nt(q, kb) * intra_ref[h]
            o = _dot(scores.astype(_BF16), v)
            o = o + _dot((q * xi_ref[h]).astype(_BF16), r_prev.astype(_BF16))
            u = _dot_tn(kb, (v * zeta_ref[h]).astype(_BF16)) * (RET_DK ** -0.5)
            state[h] = decay_ref[h, 0:1, :] * r_prev + u
            mu = jnp.mean(o, axis=-1, keepdims=True)
            oc = o - mu
            var = jnp.mean(oc * oc, axis=-1, keepdims=True)
            on = oc * lax.rsqrt(var + EPS)
            o_ref[rows, v_cols] = (on * (g * jax.nn.sigmoid(g))).astype(o_ref.dtype)

    @pl.when(j == pl.num_programs(1) - 1)
    def _():
        sfin_ref[...] = state[...]


def _retention(z3, states0, layer0, prev_states, layer, depth, *, chunk, n_chunks):
    b, t, _ = z3.shape
    tc = chunk * n_chunks
    intra, xi, zeta, decay = _retention_tables(chunk)
    const = lambda shape: pl.BlockSpec(shape, lambda bi, j: (0,) * len(shape))
    state_block = (None, None, RET_HEADS, RET_DK, RET_DV)
    kernel = functools.partial(_retention_kernel, chunk=chunk, n_chunks=n_chunks)
    args = [z3, z3, z3, z3, states0, intra, xi, zeta, decay]
    extra_specs, aliases = [], {}
    if prev_states is not None:
        extra_specs = [pl.BlockSpec(memory_space=pl.ANY)]
        aliases = {len(args): 1}
        kernel = functools.partial(_drop_refs, kernel, len(args), 1)
        args.append(prev_states)
    return pl.pallas_call(
        kernel,
        grid=(b, t // tc),
        in_specs=[
            pl.BlockSpec((None, tc, RET_QK_W), lambda bi, j: (bi, j, COL_RQ // RET_QK_W)),
            pl.BlockSpec((None, tc, RET_QK_W), lambda bi, j: (bi, j, COL_RK // RET_QK_W)),
            pl.BlockSpec((None, tc, RET_V_W), lambda bi, j: (bi, j, COL_RV // RET_V_W)),
            pl.BlockSpec((None, tc, RET_V_W), lambda bi, j: (bi, j, COL_RG // RET_V_W)),
            pl.BlockSpec(state_block, lambda bi, j: (layer0, bi, 0, 0, 0)),
            const((RET_HEADS, chunk, chunk)),
            const((RET_HEADS, chunk, RET_DK)),
            const((RET_HEADS, chunk, RET_DV)),
            const((RET_HEADS, 8, RET_DV)),
        ] + extra_specs,
        out_specs=[
            pl.BlockSpec((None, tc, RET_V_W), lambda bi, j: (bi, j, 0)),
            pl.BlockSpec(state_block, lambda bi, j: (layer, bi, 0, 0, 0)),
        ],
        out_shape=[
            jax.ShapeDtypeStruct((b, t, RET_V_W), _BF16),
            jax.ShapeDtypeStruct((depth, b, RET_HEADS, RET_DK, RET_DV), _F32),
        ],
        scratch_shapes=[pltpu.VMEM((RET_HEADS, RET_DK, RET_DV), _F32)],
        input_output_aliases=aliases,
        compiler_params=_params("parallel", "arbitrary"),
        name="retention",
    )(*args)


def _alibi_slope(head):
    return 2.0 ** (-8.0 * (head + 1.0) / ATT_HEADS)


def _pack_head_stats(cols):
    rows = cols[0].shape[0]
    lane = lax.broadcasted_iota(jnp.int32, (rows, LSE_LANES), 1)
    out = jnp.zeros((rows, LSE_LANES), _F32)
    for h, c in enumerate(cols):
        out = jnp.where(lane == h, c, out)
    return out


def _band_attn_kernel(q_ref, k_ref, kh_ref, v_ref, vh_ref, o_ref, lse_ref, *, n_sub, dil, group):
    i = pl.program_id(1)
    h = pl.program_id(2)
    blk = Q_BLOCK
    span = blk * dil
    row = lax.broadcasted_iota(jnp.int32, (blk, 2 * blk), 0)
    col = lax.broadcasted_iota(jnp.int32, (blk, 2 * blk), 1)
    delta = row + blk - col
    slope = jnp.float32(0.0)
    for hh in range(HPG):
        slope = jnp.where(h == hh, jnp.float32(_alibi_slope(group * HPG + hh)), slope)
    in_band = (delta >= 0) & (delta <= ATT_TAPS)
    bias = jnp.where(in_band, -slope * (delta * dil).astype(_F32), -jnp.inf)
    bias_first = jnp.where(col >= jnp.where(i > 0, 0, blk), bias, -jnp.inf)
    lane = lax.broadcasted_iota(jnp.int32, (blk, LSE_LANES), 1)
    scale = ATT_HEAD_DIM ** -0.5

    def strided(start, size):
        return pl.ds(start, size, stride=dil) if dil > 1 else pl.ds(start, size)

    def attend(rows, q, keys, values, bias_blk):
        s = _dot_nt(q.astype(_BF16), keys.astype(_BF16)) * scale + bias_blk
        m = jnp.max(s, axis=-1, keepdims=True)
        e = jnp.exp(s - m)
        den = jnp.sum(e, axis=-1, keepdims=True)
        o_ref[rows, :] = _dot((e / den).astype(_BF16), values.astype(_BF16))
        prev = jnp.where(h == 0, 0.0, lse_ref[rows, :])
        lse_ref[rows, :] = jnp.where(lane == h, m + jnp.log(den), prev)

    def residue_body(r, carry):
        rows = strided(r, blk)
        keys = jnp.concatenate([kh_ref[rows, :], k_ref[rows, :]], axis=0)
        values = jnp.concatenate([vh_ref[rows, :], v_ref[rows, :]], axis=0)
        attend(rows, q_ref[rows, :], keys, values, bias_first)

        for sb in range(1, n_sub):
            rows = strided(r + sb * span, blk)
            both = strided(r + (sb - 1) * span, 2 * blk)
            attend(rows, q_ref[rows, :], k_ref[both, :], v_ref[both, :], bias)
        return carry

    lax.fori_loop(0, dil, residue_body, 0, unroll=min(dil, max(1, BAND_BLOCKS_IN_FLIGHT // n_sub)))


def _band_attention(z3, group, *, tile):
    b, t, _ = z3.shape
    _, dil = ATT_GROUPS[group]
    span = Q_BLOCK * dil
    tile = max(min(tile, t), span)
    n_sub = tile // span
    tiles = t // tile
    cq = (COL_AQ + group * GROUP_W) // ATT_HEAD_DIM
    ck = (COL_AK + group * GROUP_W) // ATT_HEAD_DIM
    cv = (COL_AV + group * GROUP_W) // ATT_HEAD_DIM

    def main(c):
        return pl.BlockSpec((None, tile, ATT_HEAD_DIM), lambda bi, i, h: (bi, i, c + h))

    def halo(c):
        return pl.BlockSpec((None, span, ATT_HEAD_DIM), lambda bi, i, h: (bi, jnp.maximum(i * n_sub - 1, 0), c + h))

    kernel = functools.partial(_band_attn_kernel, n_sub=n_sub, dil=dil, group=group)
    return pl.pallas_call(
        kernel,
        grid=(b, tiles, HPG),
        in_specs=[main(cq), main(ck), halo(ck), main(cv), halo(cv)],
        out_specs=[
            pl.BlockSpec((None, tile, ATT_HEAD_DIM), lambda bi, i, h: (h, bi * tiles + i, 0)),
            pl.BlockSpec((tile, LSE_LANES), lambda bi, i, h: (bi * tiles + i, 0)),
        ],
        out_shape=[
            jax.ShapeDtypeStruct((HPG, b * t, ATT_HEAD_DIM), _F32),
            jax.ShapeDtypeStruct((b * t, LSE_LANES), _F32),
        ],
        compiler_params=_params("parallel", "parallel", "arbitrary"),
        name="band_attention_g%d" % group,
    )(z3, z3, z3, z3, z3)


def _step_attn_kernel(q_ref, kn_ref, vn_ref, kc_ref, vc_ref, o_ref, lse_ref, ko_ref, vo_ref, *,
                      t_new, cache_len, dil, group):
    n_res = min(dil, t_new)
    taps = cache_len // dil
    rows_all = HPG * t_new
    head_cols = [slice(h * ATT_HEAD_DIM, (h + 1) * ATT_HEAD_DIM) for h in range(HPG)]
    head_rows = [slice(h * t_new, (h + 1) * t_new) for h in range(HPG)]
    q = [q_ref[:, c].astype(_BF16) for c in head_cols]
    k_new = kn_ref[...]
    v_new = vn_ref[...]

    def cache_taps(ref, r, h):
        return ref[pl.ds(r * HPG + h, taps, stride=HPG * dil), :].astype(_BF16)

    scale = ATT_HEAD_DIM ** -0.5
    log2_dil = dil.bit_length() - 1
    log2_new = t_new.bit_length() - 1

    def row_terms(width):
        row_id = lax.broadcasted_iota(jnp.int32, (rows_all, width), 0)
        slope = jnp.zeros((rows_all, width), _F32)
        for h in range(HPG):
            slope = jnp.where((row_id >> log2_new) == h, _alibi_slope(group * HPG + h), slope)
        return row_id & (t_new - 1), slope

    query, slope = row_terms(taps)
    residue = query & (dil - 1)
    taps_back = taps + (query >> log2_dil) - lax.broadcasted_iota(jnp.int32, (rows_all, taps), 1)
    s_cache = None
    for r in range(n_res):
        s_r = jnp.concatenate([_dot_nt(q[h], cache_taps(kc_ref, r, h)) for h in range(HPG)], axis=0)
        s_cache = s_r if s_cache is None else jnp.where(residue == r, s_r, s_cache)
    s_cache = s_cache * scale - slope * (taps_back << log2_dil).astype(_F32)
    s_cache = jnp.where(taps_back <= ATT_TAPS, s_cache, -jnp.inf)

    query_n, slope_n = row_terms(t_new)
    back = query_n - lax.broadcasted_iota(jnp.int32, (rows_all, t_new), 1)
    s_new = jnp.concatenate([_dot_nt(q[h], k_new[:, head_cols[h]].astype(_BF16)) for h in range(HPG)], axis=0)
    s_new = s_new * scale - slope_n * back.astype(_F32)
    s_new = jnp.where((back >= 0) & ((back & (dil - 1)) == 0), s_new, -jnp.inf)

    m = jnp.maximum(jnp.max(s_cache, axis=-1, keepdims=True), jnp.max(s_new, axis=-1, keepdims=True))
    e_cache = jnp.exp(s_cache - m)
    e_new = jnp.exp(s_new - m)
    den = jnp.sum(e_cache, axis=-1, keepdims=True) + jnp.sum(e_new, axis=-1, keepdims=True)
    p_cache = e_cache / den
    p_new = (e_new / den).astype(_BF16)
    lse = m + jnp.log(den)

    for h in range(HPG):
        out = _dot(p_new[head_rows[h]], v_new[:, head_cols[h]].astype(_BF16))
        for r in range(n_res):
            p_r = p_cache[head_rows[h]]
            if n_res > 1:
                p_r = jnp.where(residue[head_rows[h]] == r, p_r, 0.0)
            out = out + _dot(p_r.astype(_BF16), cache_taps(vc_ref, r, h))
        o_ref[h] = out
    lse_ref[...] = _pack_head_stats([lse[rows] for rows in head_rows])

    keep = (cache_len - t_new) * HPG
    ko_ref[0:keep, :] = kc_ref[t_new * HPG:, :]
    vo_ref[0:keep, :] = vc_ref[t_new * HPG:, :]
    for h in range(HPG):
        ko_ref[pl.ds(keep + h, t_new, stride=HPG), :] = k_new[:, head_cols[h]]
        vo_ref[pl.ds(keep + h, t_new, stride=HPG), :] = v_new[:, head_cols[h]]


def _step_attention(z3, cache_k, cache_v, prev_k, prev_v, layer, group):
    b, t_new, _ = z3.shape
    cache_len = cache_k.shape[2] // HPG
    _, dil = ATT_GROUPS[group]
    cq = (COL_AQ + group * GROUP_W) // COL_BLOCK
    ck = (COL_AK + group * GROUP_W) // COL_BLOCK
    cv = (COL_AV + group * GROUP_W) // COL_BLOCK
    zspec = lambda c: pl.BlockSpec((None, t_new, COL_BLOCK), lambda bi: (bi, 0, c))
    cache_spec = pl.BlockSpec((None, None, cache_len * HPG, ATT_HEAD_DIM), lambda bi: (layer, bi, 0, 0))
    kernel = functools.partial(_step_attn_kernel, t_new=t_new, cache_len=cache_len, dil=dil, group=group)
    in_specs = [zspec(cq), zspec(ck), zspec(cv), cache_spec, cache_spec]
    args = [z3, z3, z3, cache_k, cache_v]
    aliases = {}
    if prev_k is not None:
        in_specs += [pl.BlockSpec(memory_space=pl.ANY)] * 2
        args += [prev_k, prev_v]
        aliases = {5: 2, 6: 3}
        kernel = functools.partial(_drop_refs, kernel, 5, 2)
    cache_shape = jax.ShapeDtypeStruct(cache_k.shape, cache_k.dtype)
    return pl.pallas_call(
        kernel,
        grid=(b,),
        in_specs=in_specs,
        out_specs=[
            pl.BlockSpec((HPG, t_new, ATT_HEAD_DIM), lambda bi: (0, bi, 0)),
            pl.BlockSpec((t_new, LSE_LANES), lambda bi: (bi, 0)),
            cache_spec, cache_spec,
        ],
        out_shape=[
            jax.ShapeDtypeStruct((HPG, b * t_new, ATT_HEAD_DIM), _F32),
            jax.ShapeDtypeStruct((b * t_new, LSE_LANES), _F32),
            cache_shape, cache_shape,
        ],
        input_output_aliases=aliases,
        compiler_params=_params("parallel"),
        name="step_attention_g%d" % group,
    )(*args)


def _drop_refs(kernel, start, count, *refs):
    return kernel(*refs[:start], *refs[start + count:])


def _merge_kernel(x_ref, oret_ref, og0_ref, og1_ref, og2_ref, l0_ref, l1_ref, l2_ref, gates_ref,
                  wret_ref, watt_ref, wout_ref, y_ref):
    l0, l1, l2 = l0_ref[...], l1_ref[...], l2_ref[...]
    mx = jnp.maximum(jnp.maximum(l0, l1), l2)
    e0, e1, e2 = jnp.exp(l0 - mx), jnp.exp(l1 - mx), jnp.exp(l2 - mx)
    tot = e0 + e1 + e2
    w0, w1, w2 = e0 / tot, e1 / tot, e2 / tot
    parts = []
    for h in range(HPG):
        parts.append(w0[:, h:h + 1] * og0_ref[h] + w1[:, h:h + 1] * og1_ref[h] + w2[:, h:h + 1] * og2_ref[h])
    o_att = jnp.concatenate(parts, axis=-1).astype(_BF16)
    br_ret = _dot(oret_ref[...], wret_ref[...])
    br_att = _dot(o_att, watt_ref[...])
    ga = gates_ref[:, 0:D_MODEL].astype(_F32)
    gb = gates_ref[:, D_MODEL:2 * D_MODEL].astype(_F32)
    mix = jax.nn.sigmoid(ga) * br_ret + jax.nn.sigmoid(gb) * br_att
    y_ref[...] = x_ref[...] + _dot(mix.astype(_BF16), wout_ref[...])


def _merge(x, o_ret, ogs, lses, z, w_ret, w_att, w_out, *, tm):
    n, d = x.shape
    row = lambda w: pl.BlockSpec((tm, w), lambda i: (i, 0))
    full = lambda a: pl.BlockSpec(a.shape, lambda i: (0, 0))
    heads = pl.BlockSpec((HPG, tm, ATT_HEAD_DIM), lambda i: (0, i, 0))
    return pl.pallas_call(
        _merge_kernel,
        grid=(n // tm,),
        in_specs=[row(d), row(RET_V_W), heads, heads, heads,
                  row(LSE_LANES), row(LSE_LANES), row(LSE_LANES), row(2 * D_MODEL),
                  full(w_ret), full(w_att), full(w_out)],
        out_specs=row(d),
        out_shape=jax.ShapeDtypeStruct((n, d), _F32),
        compiler_params=_params("parallel"),
        name="merge_out_proj",
    )(x, o_ret, *ogs, *lses, z, w_ret, w_att, w_out)


def _mlp_kernel(x_ref, g_ref, wup_ref, wdown_ref, p_ref, wple_ref, wgate_ref, gfin_ref, y_ref, h_ref, acc_ref, *,
                final_norm):
    j = pl.program_id(1)

    @pl.when(j == 0)
    def _():
        h_ref[...] = _rmsnorm(x_ref[...], g_ref[...]).astype(_BF16)
        acc_ref[...] = jnp.zeros_like(acc_ref)

    u = jnp.maximum(_dot(h_ref[...], wup_ref[...]), 0.0)
    acc_ref[...] += _dot((u * u).astype(_BF16), wdown_ref[...])

    @pl.when(j == pl.num_programs(1) - 1)
    def _():
        x = x_ref[...] + acc_ref[...]
        gate = jax.nn.sigmoid(_dot(x.astype(_BF16), wgate_ref[...]))
        x = x + _dot(p_ref[...].astype(_BF16), wple_ref[...]) * gate
        if final_norm:
            x = _rmsnorm(x, gfin_ref[...])
        y_ref[...] = x


def _mlp(x, g, w_up, w_down, p, w_ple, w_gate, g_final, *, tm, tf, final_norm):
    n, d = x.shape
    row = lambda w: pl.BlockSpec((tm, w), lambda i, j: (i, 0))
    full = lambda a: pl.BlockSpec(a.shape, lambda i, j: (0, 0))
    return pl.pallas_call(
        functools.partial(_mlp_kernel, final_norm=final_norm),
        grid=(n // tm, D_FF // tf),
        in_specs=[row(d), full(g),
                  pl.BlockSpec((d, tf), lambda i, j: (0, j)),
                  pl.BlockSpec((tf, d), lambda i, j: (j, 0)),
                  row(D_PLE), full(w_ple), full(w_gate), full(g_final)],
        out_specs=row(d),
        out_shape=jax.ShapeDtypeStruct((n, d), _F32),
        scratch_shapes=[pltpu.VMEM((tm, d), _BF16), pltpu.VMEM((tm, d), _F32)],
        compiler_params=_params("parallel", "arbitrary"),
        name="mlp_ple",
    )(x, g, w_up, w_down, p, w_ple, w_gate, g_final)


def _layer_weights(norm_mix, w_in, w_ret_br, w_att_br, w_out, norm_ffn, w_up, w_down, w_ple, w_ple_gate, i):
    w = w_in[i]
    att_start = 2 * RET_QK_W + 2 * RET_V_W
    w_mix = jnp.concatenate([w[:, D_IN - 2 * D_MODEL:], w[:, :att_start]], axis=1)
    return dict(
        norm_mix=norm_mix[i][None, :], w_in_mix=w_mix.astype(_BF16),
        w_in_att=w[:, att_start:att_start + 3 * ATT_W].astype(_BF16),
        w_ret_br=w_ret_br[i].astype(_BF16), w_att_br=w_att_br[i].astype(_BF16), w_out=w_out[i].astype(_BF16),
        norm_ffn=norm_ffn[i][None, :], w_up=w_up[i].astype(_BF16), w_down=w_down[i].astype(_BF16),
        w_ple=w_ple[i].astype(_BF16), w_ple_gate=w_ple_gate[i].astype(_BF16))


def _mix_and_ffn(x, z_mix, o_ret, ogs, lses, p_l, lw, g_final, final_norm, *, tm_merge, tm_mlp, tf):
    n = x.shape[0]
    flat = lambda a: a.reshape(n, a.shape[-1])
    x = _merge(x, flat(o_ret), ogs, lses, z_mix,
               lw["w_ret_br"], lw["w_att_br"], lw["w_out"], tm=tm_merge)
    return _mlp(x, lw["norm_ffn"], lw["w_up"], lw["w_down"], p_l, lw["w_ple"], lw["w_ple_gate"], g_final,
                tm=tm_mlp, tf=tf, final_norm=final_norm)


def kernel(x_prompt, x_sample, cache_win_k0, cache_win_v0, cache_win_k1, cache_win_v1, cache_win_k2, cache_win_v2,
           state_ret, p_prompt, p_sample, norm_mix, w_in, w_ret_br, w_att_br, w_out, norm_ffn, w_up, w_down,
           w_ple, w_ple_gate, norm_final):
    depth = w_in.shape[0]
    bp, tp, d = x_prompt.shape
    bs, ts, _ = x_sample.shape
    xp = x_prompt.reshape(bp * tp, d)
    xs = x_sample.reshape(bs * ts, d)
    g_final = norm_final[None, :]
    pos_head_rows = lambda c: c.reshape(c.shape[:2] + (c.shape[2] * HPG, ATT_HEAD_DIM))
    caches_k = [pos_head_rows(c) for c in (cache_win_k0, cache_win_k1, cache_win_k2)]
    caches_v = [pos_head_rows(c) for c in (cache_win_v0, cache_win_v1, cache_win_v2)]
    new_k = [None] * N_GROUPS
    new_v = [None] * N_GROUPS
    pk = [[] for _ in range(N_GROUPS)]
    pv = [[] for _ in range(N_GROUPS)]
    prompt_ret = sample_ret = None
    zero_state = jnp.zeros((1, bp, RET_HEADS, RET_DK, RET_DV), _F32)
    sample_chunk = math.gcd(ts, RET_CHUNK)

    for i in range(depth):
        lw = _layer_weights(norm_mix, w_in, w_ret_br, w_att_br, w_out, norm_ffn, w_up, w_down, w_ple, w_ple_gate, i)
        last = i == depth - 1

        z_mix = _norm_proj(xp, lw["norm_mix"], lw["w_in_mix"], _BF16, tm=2048, tn=1024)
        z3 = _norm_proj(xp, lw["norm_mix"], lw["w_in_att"], _F32, tm=1024, tn=ATT_W).reshape(bp, tp, 3 * ATT_W)
        o_ret, prompt_ret = _retention(z_mix.reshape(bp, tp, MIX_W), zero_state, 0, prompt_ret, i, depth,
                                       chunk=RET_CHUNK, n_chunks=4)
        ogs, lses = zip(*[_band_attention(z3, g, tile=2048) for g in range(N_GROUPS)])
        for g, (window, _) in enumerate(ATT_GROUPS):
            keep = min(window, tp)
            ck = COL_AK + g * GROUP_W
            cv = COL_AV + g * GROUP_W
            pk[g].append(z3[:, tp - keep:, ck:ck + GROUP_W].reshape(bp, keep, HPG, ATT_HEAD_DIM))
            pv[g].append(z3[:, tp - keep:, cv:cv + GROUP_W].reshape(bp, keep, HPG, ATT_HEAD_DIM))
        xp = _mix_and_ffn(xp, z_mix, o_ret, ogs, lses, p_prompt[i].reshape(bp * tp, D_PLE), lw, g_final, last,
                          tm_merge=256, tm_mlp=1024, tf=512)

        z_mix = _norm_proj(xs, lw["norm_mix"], lw["w_in_mix"], _BF16, tm=bs * ts)
        z3 = _norm_proj(xs, lw["norm_mix"], lw["w_in_att"], _F32, tm=bs * ts).reshape(bs, ts, 3 * ATT_W)
        o_ret, sample_ret = _retention(z_mix.reshape(bs, ts, MIX_W), state_ret, i, sample_ret, i, depth,
                                       chunk=sample_chunk, n_chunks=ts // sample_chunk)
        ogs, lses = [], []
        for g in range(N_GROUPS):
            o_g, lse_g, new_k[g], new_v[g] = _step_attention(z3, caches_k[g], caches_v[g], new_k[g], new_v[g], i, g)
            ogs.append(o_g)
            lses.append(lse_g)
        xs = _mix_and_ffn(xs, z_mix, o_ret, ogs, lses, p_sample[i].reshape(bs * ts, D_PLE), lw, g_final, last,
                          tm_merge=bs * ts, tm_mlp=bs * ts, tf=512)

    as_heads = lambda a: a.reshape(a.shape[:2] + (a.shape[2] // HPG, HPG, ATT_HEAD_DIM))
    prompt_windows = [jnp.stack(a) for g in range(N_GROUPS) for a in (pk[g], pv[g])]
    sample_windows = [as_heads(a) for g in range(N_GROUPS) for a in (new_k[g], new_v[g])]
    return (xp.reshape(bp, tp, d), xs.reshape(bs, ts, d), *prompt_windows, prompt_ret, *sample_windows, sample_ret)
```

```python
import functools
import math

import jax
import jax.numpy as jnp
import numpy as np
from jax import lax
from jax.experimental import pallas as pl
from jax.experimental.pallas import tpu as pltpu

D_MODEL = 1024
D_PLE = 256
RET_HEADS = 4
RET_DK = 128
RET_DV = 256
RET_CHUNK = 128
ATT_GROUPS = ((128, 1), (512, 4), (2048, 16))
N_GROUPS = 3
HPG = 4
ATT_HEAD_DIM = 128
ATT_HEADS = N_GROUPS * HPG
ATT_TAPS = 128
Q_BLOCK = 128
BAND_BLOCKS_IN_FLIGHT = 16
CACHE_COPY_CHUNKS = 4
D_FF = 4 * D_MODEL
EPS = 1e-6

RET_QK_W = RET_HEADS * RET_DK
RET_V_W = RET_HEADS * RET_DV
GROUP_W = HPG * ATT_HEAD_DIM
ATT_W = ATT_HEADS * ATT_HEAD_DIM
D_IN = 2 * RET_QK_W + 2 * RET_V_W + 3 * ATT_W + 2 * D_MODEL

MIX_W = 2 * D_MODEL + 2 * RET_QK_W + 2 * RET_V_W
COL_GATES = 0
COL_RQ = 2 * D_MODEL
COL_RK = COL_RQ + RET_QK_W
COL_RV = COL_RK + RET_QK_W
COL_RG = COL_RV + RET_V_W
COL_AQ = 0
COL_AK = ATT_W
COL_AV = 2 * ATT_W
COL_BLOCK = 512

LSE_LANES = 128
VMEM_LIMIT = 48 * 1024 * 1024

_BF16 = jnp.bfloat16
_F32 = jnp.float32


def _params(*sem):
    return pltpu.CompilerParams(dimension_semantics=sem, vmem_limit_bytes=VMEM_LIMIT)


def _rmsnorm(x, g):
    return x * lax.rsqrt(jnp.mean(x * x, axis=-1, keepdims=True) + EPS) * g


def _dot(a, b):
    return jnp.dot(a, b, preferred_element_type=_F32)


def _dot_nt(a, b):
    return lax.dot_general(a, b, (((1,), (1,)), ((), ())), preferred_element_type=_F32)


def _dot_tn(a, b):
    return lax.dot_general(a, b, (((0,), (0,)), ((), ())), preferred_element_type=_F32)


def _norm_proj_kernel(x_ref, g_ref, w_ref, z_ref, h_ref):
    @pl.when(pl.program_id(1) == 0)
    def _():
        h_ref[...] = _rmsnorm(x_ref[...], g_ref[...]).astype(_BF16)

    z_ref[...] = _dot(h_ref[...], w_ref[...]).astype(z_ref.dtype)


def _norm_proj(x, g, w, out_dtype, *, tm, tn=COL_BLOCK):
    n, d = x.shape
    d_out = w.shape[1]
    return pl.pallas_call(
        _norm_proj_kernel,
        grid=(n // tm, d_out // tn),
        in_specs=[
            pl.BlockSpec((tm, d), lambda i, j: (i, 0)),
            pl.BlockSpec((1, d), lambda i, j: (0, 0)),
            pl.BlockSpec((d, tn), lambda i, j: (0, j)),
        ],
        out_specs=pl.BlockSpec((tm, tn), lambda i, j: (i, j)),
        out_shape=jax.ShapeDtypeStruct((n, d_out), out_dtype),
        scratch_shapes=[pltpu.VMEM((tm, d), _BF16)],
        compiler_params=_params("parallel", "arbitrary"),
        name="norm_proj",
    )(x, g, w)


def _ret_log_gamma():
    return jnp.log1p(-jnp.exp(jnp.linspace(math.log(1.0 / 32), math.log(1.0 / 512), RET_HEADS))).astype(_F32)


def _retention_tables(chunk):
    lg = _ret_log_gamma()
    pos = jnp.arange(chunk, dtype=_F32)
    diff = pos[:, None] - pos[None, :]
    intra = jnp.where(diff[None] >= 0, jnp.exp(lg[:, None, None] * jnp.maximum(diff, 0.0)[None]), 0.0)
    xi = jnp.exp(lg[:, None] * (pos[None] + 1.0))
    zeta = jnp.exp(lg[:, None] * (chunk - 1.0 - pos)[None])
    decay = jnp.exp(lg * chunk)
    return (intra * (RET_DK ** -0.5),
            jnp.broadcast_to(xi[:, :, None], (RET_HEADS, chunk, RET_DK)),
            jnp.broadcast_to(zeta[:, :, None], (RET_HEADS, chunk, RET_DV)),
            jnp.broadcast_to(decay[:, None, None], (RET_HEADS, 8, RET_DV)))


def _retention_kernel(q_ref, k_ref, v_ref, g_ref, s0_ref, intra_ref, xi_ref, zeta_ref, decay_ref,
                      o_ref, sfin_ref, state, *, chunk, n_chunks):
    j = pl.program_id(1)

    @pl.when(j == 0)
    def _():
        state[...] = s0_ref[...]

    for c in range(n_chunks):
        rows = slice(c * chunk, (c + 1) * chunk)
        for h in range(RET_HEADS):
            qk_cols = slice(h * RET_DK, (h + 1) * RET_DK)
            v_cols = slice(h * RET_DV, (h + 1) * RET_DV)
            q = q_ref[rows, qk_cols]
            kb = k_ref[rows, qk_cols]
            v = v_ref[rows, v_cols]
            g = g_ref[rows, v_cols].astype(_F32)
            r_prev = state[h]
            scores = _dot_nt(q, kb) * intra_ref[h]
            o = _dot(scores.astype(_BF16), v)
            o = o + _dot((q * xi_ref[h]).astype(_BF16), r_prev.astype(_BF16))
            u = _dot_tn(kb, (v * zeta_ref[h]).astype(_BF16)) * (RET_DK ** -0.5)
            state[h] = decay_ref[h, 0:1, :] * r_prev + u
            mu = jnp.mean(o, axis=-1, keepdims=True)
            oc = o - mu
            var = jnp.mean(oc * oc, axis=-1, keepdims=True)
            on = oc * lax.rsqrt(var + EPS)
            o_ref[rows, v_cols] = (on * (g * jax.nn.sigmoid(g))).astype(o_ref.dtype)

    @pl.when(j == pl.num_programs(1) - 1)
    def _():
        sfin_ref[...] = state[...]


def _retention(z3, states0, layer0, prev_states, layer, depth, *, chunk, n_chunks):
    b, t, _ = z3.shape
    tc = chunk * n_chunks
    intra, xi, zeta, decay = _retention_tables(chunk)
    const = lambda shape: pl.BlockSpec(shape, lambda bi, j: (0,) * len(shape))
    state_block = (None, None, RET_HEADS, RET_DK, RET_DV)
    kernel = functools.partial(_retention_kernel, chunk=chunk, n_chunks=n_chunks)
    args = [z3, z3, z3, z3, states0, intra, xi, zeta, decay]
    extra_specs, aliases = [], {}
    if prev_states is not None:
        extra_specs = [pl.BlockSpec(memory_space=pl.ANY)]
        aliases = {len(args): 1}
        kernel = functools.partial(_drop_refs, kernel, len(args), 1)
        args.append(prev_states)
    return pl.pallas_call(
        kernel,
        grid=(b, t // tc),
        in_specs=[
            pl.BlockSpec((None, tc, RET_QK_W), lambda bi, j: (bi, j, COL_RQ // RET_QK_W)),
            pl.BlockSpec((None, tc, RET_QK_W), lambda bi, j: (bi, j, COL_RK // RET_QK_W)),
            pl.BlockSpec((None, tc, RET_V_W), lambda bi, j: (bi, j, COL_RV // RET_V_W)),
            pl.BlockSpec((None, tc, RET_V_W), lambda bi, j: (bi, j, COL_RG // RET_V_W)),
            pl.BlockSpec(state_block, lambda bi, j: (layer0, bi, 0, 0, 0)),
            const((RET_HEADS, chunk, chunk)),
            const((RET_HEADS, chunk, RET_DK)),
            const((RET_HEADS, chunk, RET_DV)),
            const((RET_HEADS, 8, RET_DV)),
        ] + extra_specs,
        out_specs=[
            pl.BlockSpec((None, tc, RET_V_W), lambda bi, j: (bi, j, 0)),
            pl.BlockSpec(state_block, lambda bi, j: (layer, bi, 0, 0, 0)),
        ],
        out_shape=[
            jax.ShapeDtypeStruct((b, t, RET_V_W), _BF16),
            jax.ShapeDtypeStruct((depth, b, RET_HEADS, RET_DK, RET_DV), _F32),
        ],
        scratch_shapes=[pltpu.VMEM((RET_HEADS, RET_DK, RET_DV), _F32)],
        input_output_aliases=aliases,
        compiler_params=_params("parallel", "arbitrary"),
        name="retention",
    )(*args)


def _alibi_slope(head):
    return 2.0 ** (-8.0 * (head + 1.0) / ATT_HEADS)


def _pack_head_stats(cols):
    rows = cols[0].shape[0]
    lane = lax.broadcasted_iota(jnp.int32, (rows, LSE_LANES), 1)
    out = jnp.zeros((rows, LSE_LANES), _F32)
    for h, c in enumerate(cols):
        out = jnp.where(lane == h, c, out)
    return out


def _band_attn_kernel(q_ref, k_ref, kh_ref, v_ref, vh_ref, o_ref, lse_ref, *, n_sub, dil, group):
    i = pl.program_id(1)
    h = pl.program_id(2)
    blk = Q_BLOCK
    span = blk * dil
    row = lax.broadcasted_iota(jnp.int32, (blk, 2 * blk), 0)
    col = lax.broadcasted_iota(jnp.int32, (blk, 2 * blk), 1)
    delta = row + blk - col
    slope = jnp.float32(0.0)
    for hh in range(HPG):
        slope = jnp.where(h == hh, jnp.float32(_alibi_slope(group * HPG + hh)), slope)
    in_band = (delta >= 0) & (delta <= ATT_TAPS)
    bias = jnp.where(in_band, -slope * (delta * dil).astype(_F32), -jnp.inf)
    bias_first = jnp.where(col >= jnp.where(i > 0, 0, blk), bias, -jnp.inf)
    lane = lax.broadcasted_iota(jnp.int32, (blk, LSE_LANES), 1)
    scale = ATT_HEAD_DIM ** -0.5

    def strided(start, size):
        return pl.ds(start, size, stride=dil) if dil > 1 else pl.ds(start, size)

    def attend(rows, q, keys, values, bias_blk):
        s = _dot_nt(q.astype(_BF16), keys.astype(_BF16)) * scale + bias_blk
        m = jnp.max(s, axis=-1, keepdims=True)
        e = jnp.exp(s - m)
        den = jnp.sum(e, axis=-1, keepdims=True)
        o_ref[rows, :] = _dot((e / den).astype(_BF16), values.astype(_BF16))
        prev = jnp.where(h == 0, 0.0, lse_ref[rows, :])
        lse_ref[rows, :] = jnp.where(lane == h, m + jnp.log(den), prev)

    def residue_body(r, carry):
        rows = strided(r, blk)
        keys = jnp.concatenate([kh_ref[rows, :], k_ref[rows, :]], axis=0)
        values = jnp.concatenate([vh_ref[rows, :], v_ref[rows, :]], axis=0)
        attend(rows, q_ref[rows, :], keys, values, bias_first)

        for sb in range(1, n_sub):
            rows = strided(r + sb * span, blk)
            both = strided(r + (sb - 1) * span, 2 * blk)
            attend(rows, q_ref[rows, :], k_ref[both, :], v_ref[both, :], bias)
        return carry

    lax.fori_loop(0, dil, residue_body, 0, unroll=min(dil, max(1, BAND_BLOCKS_IN_FLIGHT // n_sub)))


def _band_attention(z3, group, *, tile):
    b, t, _ = z3.shape
    _, dil = ATT_GROUPS[group]
    span = Q_BLOCK * dil
    tile = max(min(tile, t), span)
    n_sub = tile // span
    tiles = t // tile
    cq = (COL_AQ + group * GROUP_W) // ATT_HEAD_DIM
    ck = (COL_AK + group * GROUP_W) // ATT_HEAD_DIM
    cv = (COL_AV + group * GROUP_W) // ATT_HEAD_DIM

    def main(c):
        return pl.BlockSpec((None, tile, ATT_HEAD_DIM), lambda bi, i, h: (bi, i, c + h))

    def halo(c):
        return pl.BlockSpec((None, span, ATT_HEAD_DIM), lambda bi, i, h: (bi, jnp.maximum(i * n_sub - 1, 0), c + h))

    kernel = functools.partial(_band_attn_kernel, n_sub=n_sub, dil=dil, group=group)
    return pl.pallas_call(
        kernel,
        grid=(b, tiles, HPG),
        in_specs=[main(cq), main(ck), halo(ck), main(cv), halo(cv)],
        out_specs=[
            pl.BlockSpec((None, tile, ATT_HEAD_DIM), lambda bi, i, h: (h, bi * tiles + i, 0)),
            pl.BlockSpec((tile, LSE_LANES), lambda bi, i, h: (bi * tiles + i, 0)),
        ],
        out_shape=[
            jax.ShapeDtypeStruct((HPG, b * t, ATT_HEAD_DIM), _F32),
            jax.ShapeDtypeStruct((b * t, LSE_LANES), _F32),
        ],
        compiler_params=_params("parallel", "parallel", "arbitrary"),
        name="band_attention_g%d" % group,
    )(z3, z3, z3, z3, z3)


def _step_attn_kernel(q_ref, kn_ref, vn_ref, kc_ref, vc_ref, o_ref, lse_ref, ko_ref, vo_ref, *,
                      t_new, cache_len, dil, group):
    n_res = min(dil, t_new)
    taps = cache_len // dil
    rows_all = HPG * t_new
    head_cols = [slice(h * ATT_HEAD_DIM, (h + 1) * ATT_HEAD_DIM) for h in range(HPG)]
    head_rows = [slice(h * t_new, (h + 1) * t_new) for h in range(HPG)]
    q = [q_ref[:, c].astype(_BF16) for c in head_cols]
    k_new = kn_ref[...]
    v_new = vn_ref[...]

    def cache_taps(ref, r, h):
        return ref[pl.ds(r * HPG + h, taps, stride=HPG * dil), :].astype(_BF16)

    scale = ATT_HEAD_DIM ** -0.5
    log2_dil = dil.bit_length() - 1
    log2_new = t_new.bit_length() - 1

    def row_terms(width):
        row_id = lax.broadcasted_iota(jnp.int32, (rows_all, width), 0)
        slope = jnp.zeros((rows_all, width), _F32)
        for h in range(HPG):
            slope = jnp.where((row_id >> log2_new) == h, _alibi_slope(group * HPG + h), slope)
        return row_id & (t_new - 1), slope

    query, slope = row_terms(taps)
    residue = query & (dil - 1)
    taps_back = taps + (query >> log2_dil) - lax.broadcasted_iota(jnp.int32, (rows_all, taps), 1)
    s_cache = None
    for r in range(n_res):
        s_r = jnp.concatenate([_dot_nt(q[h], cache_taps(kc_ref, r, h)) for h in range(HPG)], axis=0)
        s_cache = s_r if s_cache is None else jnp.where(residue == r, s_r, s_cache)
    s_cache = s_cache * scale - slope * (taps_back << log2_dil).astype(_F32)
    s_cache = jnp.where(taps_back <= ATT_TAPS, s_cache, -jnp.inf)

    query_n, slope_n = row_terms(t_new)
    back = query_n - lax.broadcasted_iota(jnp.int32, (rows_all, t_new), 1)
    s_new = jnp.concatenate([_dot_nt(q[h], k_new[:, head_cols[h]].astype(_BF16)) for h in range(HPG)], axis=0)
    s_new = s_new * scale - slope_n * back.astype(_F32)
    s_new = jnp.where((back >= 0) & ((back & (dil - 1)) == 0), s_new, -jnp.inf)

    m = jnp.maximum(jnp.max(s_cache, axis=-1, keepdims=True), jnp.max(s_new, axis=-1, keepdims=True))
    e_cache = jnp.exp(s_cache - m)
    e_new = jnp.exp(s_new - m)
    den = jnp.sum(e_cache, axis=-1, keepdims=True) + jnp.sum(e_new, axis=-1, keepdims=True)
    p_cache = e_cache / den
    p_new = (e_new / den).astype(_BF16)
    lse = m + jnp.log(den)

    for h in range(HPG):
        out = _dot(p_new[head_rows[h]], v_new[:, head_cols[h]].astype(_BF16))
        for r in range(n_res):
            p_r = p_cache[head_rows[h]]
            if n_res > 1:
                p_r = jnp.where(residue[head_rows[h]] == r, p_r, 0.0)
            out = out + _dot(p_r.astype(_BF16), cache_taps(vc_ref, r, h))
        o_ref[h] = out
    lse_ref[...] = _pack_head_stats([lse[rows] for rows in head_rows])

    for h in range(HPG):
        ko_ref[pl.ds(h, t_new, stride=HPG), :] = k_new[:, head_cols[h]]
        vo_ref[pl.ds(h, t_new, stride=HPG), :] = v_new[:, head_cols[h]]


def _step_attention(z3, cache_k, cache_v, shifted_k, shifted_v, layer, group):
    b, t_new, _ = z3.shape
    cache_len = cache_k.shape[2] // HPG
    _, dil = ATT_GROUPS[group]
    cq = (COL_AQ + group * GROUP_W) // COL_BLOCK
    ck = (COL_AK + group * GROUP_W) // COL_BLOCK
    cv = (COL_AV + group * GROUP_W) // COL_BLOCK
    zspec = lambda c: pl.BlockSpec((None, t_new, COL_BLOCK), lambda bi: (bi, 0, c))
    cache_spec = pl.BlockSpec((None, None, cache_len * HPG, ATT_HEAD_DIM), lambda bi: (layer, bi, 0, 0))
    kernel = functools.partial(_step_attn_kernel, t_new=t_new, cache_len=cache_len, dil=dil, group=group)
    new_rows = t_new * HPG
    new_spec = pl.BlockSpec((None, None, new_rows, ATT_HEAD_DIM),
                            lambda bi: (layer, bi, cache_len * HPG // new_rows - 1, 0))
    cache_shape = jax.ShapeDtypeStruct(cache_k.shape, cache_k.dtype)
    return pl.pallas_call(
        functools.partial(_drop_refs, kernel, 5, 2),
        grid=(b,),
        in_specs=[zspec(cq), zspec(ck), zspec(cv), cache_spec, cache_spec] + [pl.BlockSpec(memory_space=pl.ANY)] * 2,
        out_specs=[
            pl.BlockSpec((HPG, t_new, ATT_HEAD_DIM), lambda bi: (0, bi, 0)),
            pl.BlockSpec((t_new, LSE_LANES), lambda bi: (bi, 0)),
            new_spec, new_spec,
        ],
        out_shape=[
            jax.ShapeDtypeStruct((HPG, b * t_new, ATT_HEAD_DIM), _F32),
            jax.ShapeDtypeStruct((b * t_new, LSE_LANES), _F32),
            cache_shape, cache_shape,
        ],
        input_output_aliases={5: 2, 6: 3},
        compiler_params=_params("parallel"),
        name="step_attention_g%d" % group,
    )(z3, z3, z3, cache_k, cache_v, shifted_k, shifted_v)


def _drop_refs(kernel, start, count, *refs):
    return kernel(*refs[:start], *refs[start + count:])


def _merge_kernel(x_ref, oret_ref, og0_ref, og1_ref, og2_ref, l0_ref, l1_ref, l2_ref, gates_ref,
                  wret_ref, watt_ref, wout_ref, y_ref):
    l0, l1, l2 = l0_ref[...], l1_ref[...], l2_ref[...]
    mx = jnp.maximum(jnp.maximum(l0, l1), l2)
    e0, e1, e2 = jnp.exp(l0 - mx), jnp.exp(l1 - mx), jnp.exp(l2 - mx)
    tot = e0 + e1 + e2
    w0, w1, w2 = e0 / tot, e1 / tot, e2 / tot
    parts = []
    for h in range(HPG):
        parts.append(w0[:, h:h + 1] * og0_ref[h] + w1[:, h:h + 1] * og1_ref[h] + w2[:, h:h + 1] * og2_ref[h])
    o_att = jnp.concatenate(parts, axis=-1).astype(_BF16)
    br_ret = _dot(oret_ref[...], wret_ref[...])
    br_att = _dot(o_att, watt_ref[...])
    ga = gates_ref[:, 0:D_MODEL].astype(_F32)
    gb = gates_ref[:, D_MODEL:2 * D_MODEL].astype(_F32)
    mix = jax.nn.sigmoid(ga) * br_ret + jax.nn.sigmoid(gb) * br_att
    y_ref[...] = x_ref[...] + _dot(mix.astype(_BF16), wout_ref[...])


def _merge(x, o_ret, ogs, lses, z, w_ret, w_att, w_out, *, tm):
    n, d = x.shape
    row = lambda w: pl.BlockSpec((tm, w), lambda i: (i, 0))
    full = lambda a: pl.BlockSpec(a.shape, lambda i: (0, 0))
    heads = pl.BlockSpec((HPG, tm, ATT_HEAD_DIM), lambda i: (0, i, 0))
    return pl.pallas_call(
        _merge_kernel,
        grid=(n // tm,),
        in_specs=[row(d), row(RET_V_W), heads, heads, heads,
                  row(LSE_LANES), row(LSE_LANES), row(LSE_LANES), row(2 * D_MODEL),
                  full(w_ret), full(w_att), full(w_out)],
        out_specs=row(d),
        out_shape=jax.ShapeDtypeStruct((n, d), _F32),
        compiler_params=_params("parallel"),
        name="merge_out_proj",
    )(x, o_ret, *ogs, *lses, z, w_ret, w_att, w_out)


def _shift_copies(src_refs, dst_refs, sem, layer, n_drop):
    copies = []
    for a, (src, dst) in enumerate(zip(src_refs, dst_refs)):
        _, batch, rows, _ = src.shape
        step = batch // CACHE_COPY_CHUNKS
        for c in range(CACHE_COPY_CHUNKS):
            copies.append(pltpu.make_async_copy(
                src.at[layer, pl.ds(c * step, step), pl.ds(n_drop, rows - n_drop), :],
                dst.at[layer, pl.ds(c * step, step), pl.ds(0, rows - n_drop), :],
                sem.at[a * CACHE_COPY_CHUNKS + c]))
    return copies


def _mlp_kernel(*refs, final_norm, n_shift, n_prev, layer, n_drop):
    x_ref, g_ref, wup_ref, wdown_ref, p_ref, wple_ref, wgate_ref, gfin_ref = refs[:8]
    src_refs = refs[8:8 + n_shift]
    y_ref = refs[8 + n_shift + n_prev]
    dst_refs = refs[9 + n_shift + n_prev:9 + 2 * n_shift + n_prev]
    h_ref, acc_ref = refs[9 + 2 * n_shift + n_prev:11 + 2 * n_shift + n_prev]
    i, j = pl.program_id(0), pl.program_id(1)
    copies = _shift_copies(src_refs, dst_refs, refs[-1], layer, n_drop) if n_shift else []

    if copies:
        @pl.when((i == 0) & (j == 0))
        def _():
            for c in copies:
                c.start()

    @pl.when(j == 0)
    def _():
        h_ref[...] = _rmsnorm(x_ref[...], g_ref[...]).astype(_BF16)
        acc_ref[...] = jnp.zeros_like(acc_ref)

    u = jnp.maximum(_dot(h_ref[...], wup_ref[...]), 0.0)
    acc_ref[...] += _dot((u * u).astype(_BF16), wdown_ref[...])

    @pl.when(j == pl.num_programs(1) - 1)
    def _():
        x = x_ref[...] + acc_ref[...]
        gate = jax.nn.sigmoid(_dot(x.astype(_BF16), wgate_ref[...]))
        x = x + _dot(p_ref[...].astype(_BF16), wple_ref[...]) * gate
        if final_norm:
            x = _rmsnorm(x, gfin_ref[...])
        y_ref[...] = x

    if copies:
        @pl.when((i == pl.num_programs(0) - 1) & (j == pl.num_programs(1) - 1))
        def _():
            for c in copies:
                c.wait()


def _mlp(x, g, w_up, w_down, p, w_ple, w_gate, g_final, *, tm, tf, final_norm, shift=None):
    n, d = x.shape
    row = lambda w: pl.BlockSpec((tm, w), lambda i, j: (i, 0))
    full = lambda a: pl.BlockSpec(a.shape, lambda i, j: (0, 0))
    anywhere = pl.BlockSpec(memory_space=pl.ANY)
    sources, previous, layer, n_drop = shift if shift is not None else ([], None, 0, 0)
    previous = list(previous) if previous is not None else []
    n_shift, n_prev = len(sources), len(previous)
    args = [x, g, w_up, w_down, p, w_ple, w_gate, g_final, *sources, *previous]
    out = pl.pallas_call(
        functools.partial(_mlp_kernel, final_norm=final_norm, n_shift=n_shift, n_prev=n_prev, layer=layer,
                          n_drop=n_drop),
        grid=(n // tm, D_FF // tf),
        in_specs=[row(d), full(g),
                  pl.BlockSpec((d, tf), lambda i, j: (0, j)),
                  pl.BlockSpec((tf, d), lambda i, j: (j, 0)),
                  row(D_PLE), full(w_ple), full(w_gate), full(g_final)] + [anywhere] * (n_shift + n_prev),
        out_specs=[row(d)] + [anywhere] * n_shift,
        out_shape=[jax.ShapeDtypeStruct((n, d), _F32)] + [jax.ShapeDtypeStruct(s.shape, s.dtype) for s in sources],
        scratch_shapes=[pltpu.VMEM((tm, d), _BF16), pltpu.VMEM((tm, d), _F32)]
        + ([pltpu.SemaphoreType.DMA((n_shift * CACHE_COPY_CHUNKS,))] if n_shift else []),
        input_output_aliases={8 + n_shift + k: 1 + k for k in range(n_prev)},
        compiler_params=_params("arbitrary", "arbitrary"),
        name="mlp_ple",
    )(*args)
    return out[0], list(out[1:])


def _layer_weights(norm_mix, w_in, w_ret_br, w_att_br, w_out, norm_ffn, w_up, w_down, w_ple, w_ple_gate, i):
    w = w_in[i]
    att_start = 2 * RET_QK_W + 2 * RET_V_W
    w_mix = jnp.concatenate([w[:, D_IN - 2 * D_MODEL:], w[:, :att_start]], axis=1)
    return dict(
        norm_mix=norm_mix[i][None, :], w_in_mix=w_mix.astype(_BF16),
        w_in_att=w[:, att_start:att_start + 3 * ATT_W].astype(_BF16),
        w_ret_br=w_ret_br[i].astype(_BF16), w_att_br=w_att_br[i].astype(_BF16), w_out=w_out[i].astype(_BF16),
        norm_ffn=norm_ffn[i][None, :], w_up=w_up[i].astype(_BF16), w_down=w_down[i].astype(_BF16),
        w_ple=w_ple[i].astype(_BF16), w_ple_gate=w_ple_gate[i].astype(_BF16))


def _mix_and_ffn(x, z_mix, o_ret, ogs, lses, p_l, lw, g_final, final_norm, *, tm_merge, tm_mlp, tf, shift=None):
    n = x.shape[0]
    flat = lambda a: a.reshape(n, a.shape[-1])
    x = _merge(x, flat(o_ret), ogs, lses, z_mix,
               lw["w_ret_br"], lw["w_att_br"], lw["w_out"], tm=tm_merge)
    return _mlp(x, lw["norm_ffn"], lw["w_up"], lw["w_down"], p_l, lw["w_ple"], lw["w_ple_gate"], g_final,
                tm=tm_mlp, tf=tf, final_norm=final_norm, shift=shift)


def kernel(x_prompt, x_sample, cache_win_k0, cache_win_v0, cache_win_k1, cache_win_v1, cache_win_k2, cache_win_v2,
           state_ret, p_prompt, p_sample, norm_mix, w_in, w_ret_br, w_att_br, w_out, norm_ffn, w_up, w_down,
           w_ple, w_ple_gate, norm_final):
    depth = w_in.shape[0]
    bp, tp, d = x_prompt.shape
    bs, ts, _ = x_sample.shape
    xp = x_prompt.reshape(bp * tp, d)
    xs = x_sample.reshape(bs * ts, d)
    g_final = norm_final[None, :]
    pos_head_rows = lambda c: c.reshape(c.shape[:2] + (c.shape[2] * HPG, ATT_HEAD_DIM))
    caches_k = [pos_head_rows(c) for c in (cache_win_k0, cache_win_k1, cache_win_k2)]
    caches_v = [pos_head_rows(c) for c in (cache_win_v0, cache_win_v1, cache_win_v2)]
    caches = [c for kv in zip(caches_k, caches_v) for c in kv]
    new_caches = None
    pk = [[] for _ in range(N_GROUPS)]
    pv = [[] for _ in range(N_GROUPS)]
    prompt_ret = sample_ret = None
    zero_state = jnp.zeros((1, bp, RET_HEADS, RET_DK, RET_DV), _F32)
    sample_chunk = math.gcd(ts, RET_CHUNK)

    for i in range(depth):
        lw = _layer_weights(norm_mix, w_in, w_ret_br, w_att_br, w_out, norm_ffn, w_up, w_down, w_ple, w_ple_gate, i)
        last = i == depth - 1

        z_mix = _norm_proj(xp, lw["norm_mix"], lw["w_in_mix"], _BF16, tm=2048, tn=1024)
        z3 = _norm_proj(xp, lw["norm_mix"], lw["w_in_att"], _F32, tm=1024, tn=ATT_W).reshape(bp, tp, 3 * ATT_W)
        o_ret, prompt_ret = _retention(z_mix.reshape(bp, tp, MIX_W), zero_state, 0, prompt_ret, i, depth,
                                       chunk=RET_CHUNK, n_chunks=4)
        ogs, lses = zip(*[_band_attention(z3, g, tile=2048) for g in range(N_GROUPS)])
        for g, (window, _) in enumerate(ATT_GROUPS):
            keep = min(window, tp)
            ck = COL_AK + g * GROUP_W
            cv = COL_AV + g * GROUP_W
            pk[g].append(z3[:, tp - keep:, ck:ck + GROUP_W].reshape(bp, keep, HPG, ATT_HEAD_DIM))
            pv[g].append(z3[:, tp - keep:, cv:cv + GROUP_W].reshape(bp, keep, HPG, ATT_HEAD_DIM))
        xp, new_caches = _mix_and_ffn(xp, z_mix, o_ret, ogs, lses, p_prompt[i].reshape(bp * tp, D_PLE), lw, g_final,
                                      last, tm_merge=256, tm_mlp=1024, tf=512,
                                      shift=(caches, new_caches, i, ts * HPG))

        z_mix = _norm_proj(xs, lw["norm_mix"], lw["w_in_mix"], _BF16, tm=bs * ts)
        z3 = _norm_proj(xs, lw["norm_mix"], lw["w_in_att"], _F32, tm=bs * ts).reshape(bs, ts, 3 * ATT_W)
        o_ret, sample_ret = _retention(z_mix.reshape(bs, ts, MIX_W), state_ret, i, sample_ret, i, depth,
                                       chunk=sample_chunk, n_chunks=ts // sample_chunk)
        ogs, lses = [], []
        for g in range(N_GROUPS):
            o_g, lse_g, new_caches[2 * g], new_caches[2 * g + 1] = _step_attention(
                z3, caches_k[g], caches_v[g], new_caches[2 * g], new_caches[2 * g + 1], i, g)
            ogs.append(o_g)
            lses.append(lse_g)
        xs, _ = _mix_and_ffn(xs, z_mix, o_ret, ogs, lses, p_sample[i].reshape(bs * ts, D_PLE), lw, g_final, last,
                             tm_merge=bs * ts, tm_mlp=bs * ts, tf=512)

    as_heads = lambda a: a.reshape(a.shape[:2] + (a.shape[2] // HPG, HPG, ATT_HEAD_DIM))
    prompt_windows = [jnp.stack(a) for g in range(N_GROUPS) for a in (pk[g], pv[g])]
    sample_windows = [as_heads(a) for a in new_caches]
    return (xp.reshape(bp, tp, d), xs.reshape(bs, ts, d), *prompt_windows, prompt_ret, *sample_windows, sample_ret)
```

```python
import functools
import math

import jax
import jax.numpy as jnp
import numpy as np
from jax import lax
from jax.experimental import pallas as pl
from jax.experimental.pallas import tpu as pltpu
from jax.experimental.pallas import tpu_sc as plsc

D_MODEL = 1024
D_PLE = 256
RET_HEADS = 4
RET_DK = 128
RET_DV = 256
RET_CHUNK = 128
ATT_GROUPS = ((128, 1), (512, 4), (2048, 16))
N_GROUPS = 3
HPG = 4
ATT_HEAD_DIM = 128
ATT_HEADS = N_GROUPS * HPG
ATT_TAPS = 128
Q_BLOCK = 128
BAND_BLOCKS_IN_FLIGHT = 16
SHIFT_CHUNK_ROWS = 240
D_FF = 4 * D_MODEL
EPS = 1e-6

RET_QK_W = RET_HEADS * RET_DK
RET_V_W = RET_HEADS * RET_DV
GROUP_W = HPG * ATT_HEAD_DIM
ATT_W = ATT_HEADS * ATT_HEAD_DIM
D_IN = 2 * RET_QK_W + 2 * RET_V_W + 3 * ATT_W + 2 * D_MODEL

MIX_W = 2 * D_MODEL + 2 * RET_QK_W + 2 * RET_V_W
COL_GATES = 0
COL_RQ = 2 * D_MODEL
COL_RK = COL_RQ + RET_QK_W
COL_RV = COL_RK + RET_QK_W
COL_RG = COL_RV + RET_V_W
COL_AQ = 0
COL_AK = ATT_W
COL_AV = 2 * ATT_W
COL_BLOCK = 512

LSE_LANES = 128
VMEM_LIMIT = 48 * 1024 * 1024

_BF16 = jnp.bfloat16
_F32 = jnp.float32


def _params(*sem):
    return pltpu.CompilerParams(dimension_semantics=sem, vmem_limit_bytes=VMEM_LIMIT)


def _rmsnorm(x, g):
    return x * lax.rsqrt(jnp.mean(x * x, axis=-1, keepdims=True) + EPS) * g


def _dot(a, b):
    return jnp.dot(a, b, preferred_element_type=_F32)


def _dot_nt(a, b):
    return lax.dot_general(a, b, (((1,), (1,)), ((), ())), preferred_element_type=_F32)


def _dot_tn(a, b):
    return lax.dot_general(a, b, (((0,), (0,)), ((), ())), preferred_element_type=_F32)


def _norm_proj_kernel(x_ref, g_ref, w_ref, z_ref, h_ref):
    @pl.when(pl.program_id(1) == 0)
    def _():
        h_ref[...] = _rmsnorm(x_ref[...], g_ref[...]).astype(_BF16)

    z_ref[...] = _dot(h_ref[...], w_ref[...]).astype(z_ref.dtype)


def _norm_proj(x, g, w, out_dtype, *, tm, tn=COL_BLOCK):
    n, d = x.shape
    d_out = w.shape[1]
    return pl.pallas_call(
        _norm_proj_kernel,
        grid=(n // tm, d_out // tn),
        in_specs=[
            pl.BlockSpec((tm, d), lambda i, j: (i, 0)),
            pl.BlockSpec((1, d), lambda i, j: (0, 0)),
            pl.BlockSpec((d, tn), lambda i, j: (0, j)),
        ],
        out_specs=pl.BlockSpec((tm, tn), lambda i, j: (i, j)),
        out_shape=jax.ShapeDtypeStruct((n, d_out), out_dtype),
        scratch_shapes=[pltpu.VMEM((tm, d), _BF16)],
        compiler_params=_params("parallel", "arbitrary"),
        name="norm_proj",
    )(x, g, w)


def _ret_log_gamma():
    return jnp.log1p(-jnp.exp(jnp.linspace(math.log(1.0 / 32), math.log(1.0 / 512), RET_HEADS))).astype(_F32)


def _retention_tables(chunk):
    lg = _ret_log_gamma()
    pos = jnp.arange(chunk, dtype=_F32)
    diff = pos[:, None] - pos[None, :]
    intra = jnp.where(diff[None] >= 0, jnp.exp(lg[:, None, None] * jnp.maximum(diff, 0.0)[None]), 0.0)
    xi = jnp.exp(lg[:, None] * (pos[None] + 1.0))
    zeta = jnp.exp(lg[:, None] * (chunk - 1.0 - pos)[None])
    decay = jnp.exp(lg * chunk)
    return (intra * (RET_DK ** -0.5),
            jnp.broadcast_to(xi[:, :, None], (RET_HEADS, chunk, RET_DK)),
            jnp.broadcast_to(zeta[:, :, None], (RET_HEADS, chunk, RET_DV)),
            jnp.broadcast_to(decay[:, None, None], (RET_HEADS, 8, RET_DV)))


def _retention_kernel(q_ref, k_ref, v_ref, g_ref, s0_ref, intra_ref, xi_ref, zeta_ref, decay_ref,
                      o_ref, sfin_ref, state, *, chunk, n_chunks):
    j = pl.program_id(1)

    @pl.when(j == 0)
    def _():
        state[...] = s0_ref[...]

    for c in range(n_chunks):
        rows = slice(c * chunk, (c + 1) * chunk)
        for h in range(RET_HEADS):
            qk_cols = slice(h * RET_DK, (h + 1) * RET_DK)
            v_cols = slice(h * RET_DV, (h + 1) * RET_DV)
            q = q_ref[rows, qk_cols]
            kb = k_ref[rows, qk_cols]
            v = v_ref[rows, v_cols]
            g = g_ref[rows, v_cols].astype(_F32)
            r_prev = state[h]
            scores = _dot_nt(q, kb) * intra_ref[h]
            o = _dot(scores.astype(_BF16), v)
            o = o + _dot((q * xi_ref[h]).astype(_BF16), r_prev.astype(_BF16))
            u = _dot_tn(kb, (v * zeta_ref[h]).astype(_BF16)) * (RET_DK ** -0.5)
            state[h] = decay_ref[h, 0:1, :] * r_prev + u
            mu = jnp.mean(o, axis=-1, keepdims=True)
            oc = o - mu
            var = jnp.mean(oc * oc, axis=-1, keepdims=True)
            on = oc * lax.rsqrt(var + EPS)
            o_ref[rows, v_cols] = (on * (g * jax.nn.sigmoid(g))).astype(o_ref.dtype)

    @pl.when(j == pl.num_programs(1) - 1)
    def _():
        sfin_ref[...] = state[...]


def _retention(z3, states0, layer0, prev_states, layer, depth, *, chunk, n_chunks):
    b, t, _ = z3.shape
    tc = chunk * n_chunks
    intra, xi, zeta, decay = _retention_tables(chunk)
    const = lambda shape: pl.BlockSpec(shape, lambda bi, j: (0,) * len(shape))
    state_block = (None, None, RET_HEADS, RET_DK, RET_DV)
    kernel = functools.partial(_retention_kernel, chunk=chunk, n_chunks=n_chunks)
    args = [z3, z3, z3, z3, states0, intra, xi, zeta, decay]
    extra_specs, aliases = [], {}
    if prev_states is not None:
        extra_specs = [pl.BlockSpec(memory_space=pl.ANY)]
        aliases = {len(args): 1}
        kernel = functools.partial(_drop_refs, kernel, len(args), 1)
        args.append(prev_states)
    return pl.pallas_call(
        kernel,
        grid=(b, t // tc),
        in_specs=[
            pl.BlockSpec((None, tc, RET_QK_W), lambda bi, j: (bi, j, COL_RQ // RET_QK_W)),
            pl.BlockSpec((None, tc, RET_QK_W), lambda bi, j: (bi, j, COL_RK // RET_QK_W)),
            pl.BlockSpec((None, tc, RET_V_W), lambda bi, j: (bi, j, COL_RV // RET_V_W)),
            pl.BlockSpec((None, tc, RET_V_W), lambda bi, j: (bi, j, COL_RG // RET_V_W)),
            pl.BlockSpec(state_block, lambda bi, j: (layer0, bi, 0, 0, 0)),
            const((RET_HEADS, chunk, chunk)),
            const((RET_HEADS, chunk, RET_DK)),
            const((RET_HEADS, chunk, RET_DV)),
            const((RET_HEADS, 8, RET_DV)),
        ] + extra_specs,
        out_specs=[
            pl.BlockSpec((None, tc, RET_V_W), lambda bi, j: (bi, j, 0)),
            pl.BlockSpec(state_block, lambda bi, j: (layer, bi, 0, 0, 0)),
        ],
        out_shape=[
            jax.ShapeDtypeStruct((b, t, RET_V_W), _BF16),
            jax.ShapeDtypeStruct((depth, b, RET_HEADS, RET_DK, RET_DV), _F32),
        ],
        scratch_shapes=[pltpu.VMEM((RET_HEADS, RET_DK, RET_DV), _F32)],
        input_output_aliases=aliases,
        compiler_params=_params("parallel", "arbitrary"),
        name="retention",
    )(*args)


def _alibi_slope(head):
    return 2.0 ** (-8.0 * (head + 1.0) / ATT_HEADS)


def _pack_head_stats(cols):
    rows = cols[0].shape[0]
    lane = lax.broadcasted_iota(jnp.int32, (rows, LSE_LANES), 1)
    out = jnp.zeros((rows, LSE_LANES), _F32)
    for h, c in enumerate(cols):
        out = jnp.where(lane == h, c, out)
    return out


def _band_attn_kernel(q_ref, k_ref, kh_ref, v_ref, vh_ref, o_ref, lse_ref, *, n_sub, dil, group):
    i = pl.program_id(1)
    h = pl.program_id(2)
    blk = Q_BLOCK
    span = blk * dil
    row = lax.broadcasted_iota(jnp.int32, (blk, 2 * blk), 0)
    col = lax.broadcasted_iota(jnp.int32, (blk, 2 * blk), 1)
    delta = row + blk - col
    slope = jnp.float32(0.0)
    for hh in range(HPG):
        slope = jnp.where(h == hh, jnp.float32(_alibi_slope(group * HPG + hh)), slope)
    in_band = (delta >= 0) & (delta <= ATT_TAPS)
    bias = jnp.where(in_band, -slope * (delta * dil).astype(_F32), -jnp.inf)
    bias_first = jnp.where(col >= jnp.where(i > 0, 0, blk), bias, -jnp.inf)
    lane = lax.broadcasted_iota(jnp.int32, (blk, LSE_LANES), 1)
    scale = ATT_HEAD_DIM ** -0.5

    def strided(start, size):
        return pl.ds(start, size, stride=dil) if dil > 1 else pl.ds(start, size)

    def attend(rows, q, keys, values, bias_blk):
        s = _dot_nt(q.astype(_BF16), keys.astype(_BF16)) * scale + bias_blk
        m = jnp.max(s, axis=-1, keepdims=True)
        e = jnp.exp(s - m)
        den = jnp.sum(e, axis=-1, keepdims=True)
        o_ref[rows, :] = _dot((e / den).astype(_BF16), values.astype(_BF16))
        prev = jnp.where(h == 0, 0.0, lse_ref[rows, :])
        lse_ref[rows, :] = jnp.where(lane == h, m + jnp.log(den), prev)

    def residue_body(r, carry):
        rows = strided(r, blk)
        keys = jnp.concatenate([kh_ref[rows, :], k_ref[rows, :]], axis=0)
        values = jnp.concatenate([vh_ref[rows, :], v_ref[rows, :]], axis=0)
        attend(rows, q_ref[rows, :], keys, values, bias_first)

        for sb in range(1, n_sub):
            rows = strided(r + sb * span, blk)
            both = strided(r + (sb - 1) * span, 2 * blk)
            attend(rows, q_ref[rows, :], k_ref[both, :], v_ref[both, :], bias)
        return carry

    lax.fori_loop(0, dil, residue_body, 0, unroll=min(dil, max(1, BAND_BLOCKS_IN_FLIGHT // n_sub)))


def _band_attention(z3, group, *, tile):
    b, t, _ = z3.shape
    _, dil = ATT_GROUPS[group]
    span = Q_BLOCK * dil
    tile = max(min(tile, t), span)
    n_sub = tile // span
    tiles = t // tile
    cq = (COL_AQ + group * GROUP_W) // ATT_HEAD_DIM
    ck = (COL_AK + group * GROUP_W) // ATT_HEAD_DIM
    cv = (COL_AV + group * GROUP_W) // ATT_HEAD_DIM

    def main(c):
        return pl.BlockSpec((None, tile, ATT_HEAD_DIM), lambda bi, i, h: (bi, i, c + h))

    def halo(c):
        return pl.BlockSpec((None, span, ATT_HEAD_DIM), lambda bi, i, h: (bi, jnp.maximum(i * n_sub - 1, 0), c + h))

    kernel = functools.partial(_band_attn_kernel, n_sub=n_sub, dil=dil, group=group)
    return pl.pallas_call(
        kernel,
        grid=(b, tiles, HPG),
        in_specs=[main(cq), main(ck), halo(ck), main(cv), halo(cv)],
        out_specs=[
            pl.BlockSpec((None, tile, ATT_HEAD_DIM), lambda bi, i, h: (h, bi * tiles + i, 0)),
            pl.BlockSpec((tile, LSE_LANES), lambda bi, i, h: (bi * tiles + i, 0)),
        ],
        out_shape=[
            jax.ShapeDtypeStruct((HPG, b * t, ATT_HEAD_DIM), _F32),
            jax.ShapeDtypeStruct((b * t, LSE_LANES), _F32),
        ],
        compiler_params=_params("parallel", "parallel", "arbitrary"),
        name="band_attention_g%d" % group,
    )(z3, z3, z3, z3, z3)


def _step_attn_kernel(q_ref, kn_ref, vn_ref, kc_ref, vc_ref, o_ref, lse_ref, ko_ref, vo_ref, *,
                      t_new, cache_len, dil, group):
    n_res = min(dil, t_new)
    taps = cache_len // dil
    rows_all = HPG * t_new
    head_cols = [slice(h * ATT_HEAD_DIM, (h + 1) * ATT_HEAD_DIM) for h in range(HPG)]
    head_rows = [slice(h * t_new, (h + 1) * t_new) for h in range(HPG)]
    q = [q_ref[:, c].astype(_BF16) for c in head_cols]
    k_new = kn_ref[...]
    v_new = vn_ref[...]

    def cache_taps(ref, r, h):
        return ref[pl.ds(r * HPG + h, taps, stride=HPG * dil), :].astype(_BF16)

    scale = ATT_HEAD_DIM ** -0.5
    log2_dil = dil.bit_length() - 1
    log2_new = t_new.bit_length() - 1

    def row_terms(width):
        row_id = lax.broadcasted_iota(jnp.int32, (rows_all, width), 0)
        slope = jnp.zeros((rows_all, width), _F32)
        for h in range(HPG):
            slope = jnp.where((row_id >> log2_new) == h, _alibi_slope(group * HPG + h), slope)
        return row_id & (t_new - 1), slope

    query, slope = row_terms(taps)
    residue = query & (dil - 1)
    taps_back = taps + (query >> log2_dil) - lax.broadcasted_iota(jnp.int32, (rows_all, taps), 1)
    s_cache = None
    for r in range(n_res):
        s_r = jnp.concatenate([_dot_nt(q[h], cache_taps(kc_ref, r, h)) for h in range(HPG)], axis=0)
        s_cache = s_r if s_cache is None else jnp.where(residue == r, s_r, s_cache)
    s_cache = s_cache * scale - slope * (taps_back << log2_dil).astype(_F32)
    s_cache = jnp.where(taps_back <= ATT_TAPS, s_cache, -jnp.inf)

    query_n, slope_n = row_terms(t_new)
    back = query_n - lax.broadcasted_iota(jnp.int32, (rows_all, t_new), 1)
    s_new = jnp.concatenate([_dot_nt(q[h], k_new[:, head_cols[h]].astype(_BF16)) for h in range(HPG)], axis=0)
    s_new = s_new * scale - slope_n * back.astype(_F32)
    s_new = jnp.where((back >= 0) & ((back & (dil - 1)) == 0), s_new, -jnp.inf)

    m = jnp.maximum(jnp.max(s_cache, axis=-1, keepdims=True), jnp.max(s_new, axis=-1, keepdims=True))
    e_cache = jnp.exp(s_cache - m)
    e_new = jnp.exp(s_new - m)
    den = jnp.sum(e_cache, axis=-1, keepdims=True) + jnp.sum(e_new, axis=-1, keepdims=True)
    p_cache = e_cache / den
    p_new = (e_new / den).astype(_BF16)
    lse = m + jnp.log(den)

    for h in range(HPG):
        out = _dot(p_new[head_rows[h]], v_new[:, head_cols[h]].astype(_BF16))
        for r in range(n_res):
            p_r = p_cache[head_rows[h]]
            if n_res > 1:
                p_r = jnp.where(residue[head_rows[h]] == r, p_r, 0.0)
            out = out + _dot(p_r.astype(_BF16), cache_taps(vc_ref, r, h))
        o_ref[h] = out
    lse_ref[...] = _pack_head_stats([lse[rows] for rows in head_rows])

    for h in range(HPG):
        ko_ref[pl.ds(h, t_new, stride=HPG), :] = k_new[:, head_cols[h]]
        vo_ref[pl.ds(h, t_new, stride=HPG), :] = v_new[:, head_cols[h]]


def _step_attention(z3, cache_k, cache_v, shifted_k, shifted_v, layer, group):
    b, t_new, _ = z3.shape
    cache_len = cache_k.shape[2] // HPG
    _, dil = ATT_GROUPS[group]
    cq = (COL_AQ + group * GROUP_W) // COL_BLOCK
    ck = (COL_AK + group * GROUP_W) // COL_BLOCK
    cv = (COL_AV + group * GROUP_W) // COL_BLOCK
    zspec = lambda c: pl.BlockSpec((None, t_new, COL_BLOCK), lambda bi: (bi, 0, c))
    cache_spec = pl.BlockSpec((None, None, cache_len * HPG, ATT_HEAD_DIM), lambda bi: (layer, bi, 0, 0))
    kernel = functools.partial(_step_attn_kernel, t_new=t_new, cache_len=cache_len, dil=dil, group=group)
    new_rows = t_new * HPG
    new_spec = pl.BlockSpec((None, None, new_rows, ATT_HEAD_DIM),
                            lambda bi: (layer, bi, cache_len * HPG // new_rows - 1, 0))
    cache_shape = jax.ShapeDtypeStruct(cache_k.shape, cache_k.dtype)
    return pl.pallas_call(
        functools.partial(_drop_refs, kernel, 5, 2),
        grid=(b,),
        in_specs=[zspec(cq), zspec(ck), zspec(cv), cache_spec, cache_spec] + [pl.BlockSpec(memory_space=pl.ANY)] * 2,
        out_specs=[
            pl.BlockSpec((HPG, t_new, ATT_HEAD_DIM), lambda bi: (0, bi, 0)),
            pl.BlockSpec((t_new, LSE_LANES), lambda bi: (bi, 0)),
            new_spec, new_spec,
        ],
        out_shape=[
            jax.ShapeDtypeStruct((HPG, b * t_new, ATT_HEAD_DIM), _F32),
            jax.ShapeDtypeStruct((b * t_new, LSE_LANES), _F32),
            cache_shape, cache_shape,
        ],
        input_output_aliases={5: 2, 6: 3},
        compiler_params=_params("parallel"),
        name="step_attention_g%d" % group,
    )(z3, z3, z3, cache_k, cache_v, shifted_k, shifted_v)


def _shift_caches(caches, n_drop):
    depth, batch = caches[0].shape[:2]
    mesh = plsc.VectorSubcoreMesh(core_axis_name="core", subcore_axis_name="subcore")
    workers = mesh.num_cores * mesh.num_subcores
    per_worker = batch // workers
    n = len(caches)

    @pl.kernel(out_type=[jax.ShapeDtypeStruct(c.shape, c.dtype) for c in caches], mesh=mesh,
               scratch_types=[pltpu.VMEM((SHIFT_CHUNK_ROWS, caches[0].shape[3]), caches[0].dtype)])
    def shift(*refs):
        srcs, dsts, buf = refs[:n], refs[n:2 * n], refs[2 * n]
        worker = lax.axis_index("core") * mesh.num_subcores + lax.axis_index("subcore")

        def move(src, dst, layer, b, row, count):
            stage = buf if count == SHIFT_CHUNK_ROWS else buf.at[pl.ds(0, count), :]
            pltpu.sync_copy(src.at[layer, b, pl.ds(row + n_drop, count), :], stage)
            pltpu.sync_copy(stage, dst.at[layer, b, pl.ds(row, count), :])

        for src, dst in zip(srcs, dsts):
            full, rest = divmod(src.shape[2] - n_drop, SHIFT_CHUNK_ROWS)
            for layer in range(depth):
                for local in range(per_worker):
                    b = worker * per_worker + local

                    @pl.loop(0, full)
                    def _(k):
                        move(src, dst, layer, b, k * SHIFT_CHUNK_ROWS, SHIFT_CHUNK_ROWS)

                    if rest:
                        move(src, dst, layer, b, full * SHIFT_CHUNK_ROWS, rest)

    return list(shift(*caches))


def _drop_refs(kernel, start, count, *refs):
    return kernel(*refs[:start], *refs[start + count:])


def _merge_kernel(x_ref, oret_ref, og0_ref, og1_ref, og2_ref, l0_ref, l1_ref, l2_ref, gates_ref,
                  wret_ref, watt_ref, wout_ref, y_ref):
    l0, l1, l2 = l0_ref[...], l1_ref[...], l2_ref[...]
    mx = jnp.maximum(jnp.maximum(l0, l1), l2)
    e0, e1, e2 = jnp.exp(l0 - mx), jnp.exp(l1 - mx), jnp.exp(l2 - mx)
    tot = e0 + e1 + e2
    w0, w1, w2 = e0 / tot, e1 / tot, e2 / tot
    parts = []
    for h in range(HPG):
        parts.append(w0[:, h:h + 1] * og0_ref[h] + w1[:, h:h + 1] * og1_ref[h] + w2[:, h:h + 1] * og2_ref[h])
    o_att = jnp.concatenate(parts, axis=-1).astype(_BF16)
    br_ret = _dot(oret_ref[...], wret_ref[...])
    br_att = _dot(o_att, watt_ref[...])
    ga = gates_ref[:, 0:D_MODEL].astype(_F32)
    gb = gates_ref[:, D_MODEL:2 * D_MODEL].astype(_F32)
    mix = jax.nn.sigmoid(ga) * br_ret + jax.nn.sigmoid(gb) * br_att
    y_ref[...] = x_ref[...] + _dot(mix.astype(_BF16), wout_ref[...])


def _merge(x, o_ret, ogs, lses, z, w_ret, w_att, w_out, *, tm):
    n, d = x.shape
    row = lambda w: pl.BlockSpec((tm, w), lambda i: (i, 0))
    full = lambda a: pl.BlockSpec(a.shape, lambda i: (0, 0))
    heads = pl.BlockSpec((HPG, tm, ATT_HEAD_DIM), lambda i: (0, i, 0))
    return pl.pallas_call(
        _merge_kernel,
        grid=(n // tm,),
        in_specs=[row(d), row(RET_V_W), heads, heads, heads,
                  row(LSE_LANES), row(LSE_LANES), row(LSE_LANES), row(2 * D_MODEL),
                  full(w_ret), full(w_att), full(w_out)],
        out_specs=row(d),
        out_shape=jax.ShapeDtypeStruct((n, d), _F32),
        compiler_params=_params("parallel"),
        name="merge_out_proj",
    )(x, o_ret, *ogs, *lses, z, w_ret, w_att, w_out)


def _mlp_kernel(x_ref, g_ref, wup_ref, wdown_ref, p_ref, wple_ref, wgate_ref, gfin_ref, y_ref, h_ref, acc_ref, *,
                final_norm):
    j = pl.program_id(1)

    @pl.when(j == 0)
    def _():
        h_ref[...] = _rmsnorm(x_ref[...], g_ref[...]).astype(_BF16)
        acc_ref[...] = jnp.zeros_like(acc_ref)

    u = jnp.maximum(_dot(h_ref[...], wup_ref[...]), 0.0)
    acc_ref[...] += _dot((u * u).astype(_BF16), wdown_ref[...])

    @pl.when(j == pl.num_programs(1) - 1)
    def _():
        x = x_ref[...] + acc_ref[...]
        gate = jax.nn.sigmoid(_dot(x.astype(_BF16), wgate_ref[...]))
        x = x + _dot(p_ref[...].astype(_BF16), wple_ref[...]) * gate
        if final_norm:
            x = _rmsnorm(x, gfin_ref[...])
        y_ref[...] = x


def _mlp(x, g, w_up, w_down, p, w_ple, w_gate, g_final, *, tm, tf, final_norm):
    n, d = x.shape
    row = lambda w: pl.BlockSpec((tm, w), lambda i, j: (i, 0))
    full = lambda a: pl.BlockSpec(a.shape, lambda i, j: (0, 0))
    return pl.pallas_call(
        functools.partial(_mlp_kernel, final_norm=final_norm),
        grid=(n // tm, D_FF // tf),
        in_specs=[row(d), full(g),
                  pl.BlockSpec((d, tf), lambda i, j: (0, j)),
                  pl.BlockSpec((tf, d), lambda i, j: (j, 0)),
                  row(D_PLE), full(w_ple), full(w_gate), full(g_final)],
        out_specs=row(d),
        out_shape=jax.ShapeDtypeStruct((n, d), _F32),
        scratch_shapes=[pltpu.VMEM((tm, d), _BF16), pltpu.VMEM((tm, d), _F32)],
        compiler_params=_params("parallel", "arbitrary"),
        name="mlp_ple",
    )(x, g, w_up, w_down, p, w_ple, w_gate, g_final)


def _layer_weights(norm_mix, w_in, w_ret_br, w_att_br, w_out, norm_ffn, w_up, w_down, w_ple, w_ple_gate, i):
    w = w_in[i]
    att_start = 2 * RET_QK_W + 2 * RET_V_W
    w_mix = jnp.concatenate([w[:, D_IN - 2 * D_MODEL:], w[:, :att_start]], axis=1)
    return dict(
        norm_mix=norm_mix[i][None, :], w_in_mix=w_mix.astype(_BF16),
        w_in_att=w[:, att_start:att_start + 3 * ATT_W].astype(_BF16),
        w_ret_br=w_ret_br[i].astype(_BF16), w_att_br=w_att_br[i].astype(_BF16), w_out=w_out[i].astype(_BF16),
        norm_ffn=norm_ffn[i][None, :], w_up=w_up[i].astype(_BF16), w_down=w_down[i].astype(_BF16),
        w_ple=w_ple[i].astype(_BF16), w_ple_gate=w_ple_gate[i].astype(_BF16))


def _mix_and_ffn(x, z_mix, o_ret, ogs, lses, p_l, lw, g_final, final_norm, *, tm_merge, tm_mlp, tf):
    n = x.shape[0]
    flat = lambda a: a.reshape(n, a.shape[-1])
    x = _merge(x, flat(o_ret), ogs, lses, z_mix,
               lw["w_ret_br"], lw["w_att_br"], lw["w_out"], tm=tm_merge)
    return _mlp(x, lw["norm_ffn"], lw["w_up"], lw["w_down"], p_l, lw["w_ple"], lw["w_ple_gate"], g_final,
                tm=tm_mlp, tf=tf, final_norm=final_norm)


def kernel(x_prompt, x_sample, cache_win_k0, cache_win_v0, cache_win_k1, cache_win_v1, cache_win_k2, cache_win_v2,
           state_ret, p_prompt, p_sample, norm_mix, w_in, w_ret_br, w_att_br, w_out, norm_ffn, w_up, w_down,
           w_ple, w_ple_gate, norm_final):
    depth = w_in.shape[0]
    bp, tp, d = x_prompt.shape
    bs, ts, _ = x_sample.shape
    xp = x_prompt.reshape(bp * tp, d)
    xs = x_sample.reshape(bs * ts, d)
    g_final = norm_final[None, :]
    pos_head_rows = lambda c: c.reshape(c.shape[:2] + (c.shape[2] * HPG, ATT_HEAD_DIM))
    caches_k = [pos_head_rows(c) for c in (cache_win_k0, cache_win_k1, cache_win_k2)]
    caches_v = [pos_head_rows(c) for c in (cache_win_v0, cache_win_v1, cache_win_v2)]
    new_caches = _shift_caches([c for kv in zip(caches_k, caches_v) for c in kv], ts * HPG)
    pk = [[] for _ in range(N_GROUPS)]
    pv = [[] for _ in range(N_GROUPS)]
    prompt_ret = sample_ret = None
    zero_state = jnp.zeros((1, bp, RET_HEADS, RET_DK, RET_DV), _F32)
    sample_chunk = math.gcd(ts, RET_CHUNK)

    for i in range(depth):
        lw = _layer_weights(norm_mix, w_in, w_ret_br, w_att_br, w_out, norm_ffn, w_up, w_down, w_ple, w_ple_gate, i)
        last = i == depth - 1

        z_mix = _norm_proj(xp, lw["norm_mix"], lw["w_in_mix"], _BF16, tm=2048, tn=1024)
        z3 = _norm_proj(xp, lw["norm_mix"], lw["w_in_att"], _F32, tm=1024, tn=ATT_W).reshape(bp, tp, 3 * ATT_W)
        o_ret, prompt_ret = _retention(z_mix.reshape(bp, tp, MIX_W), zero_state, 0, prompt_ret, i, depth,
                                       chunk=RET_CHUNK, n_chunks=4)
        ogs, lses = zip(*[_band_attention(z3, g, tile=2048) for g in range(N_GROUPS)])
        for g, (window, _) in enumerate(ATT_GROUPS):
            keep = min(window, tp)
            ck = COL_AK + g * GROUP_W
            cv = COL_AV + g * GROUP_W
            pk[g].append(z3[:, tp - keep:, ck:ck + GROUP_W].reshape(bp, keep, HPG, ATT_HEAD_DIM))
            pv[g].append(z3[:, tp - keep:, cv:cv + GROUP_W].reshape(bp, keep, HPG, ATT_HEAD_DIM))
        xp = _mix_and_ffn(xp, z_mix, o_ret, ogs, lses, p_prompt[i].reshape(bp * tp, D_PLE), lw, g_final, last,
                          tm_merge=256, tm_mlp=1024, tf=512)

        z_mix = _norm_proj(xs, lw["norm_mix"], lw["w_in_mix"], _BF16, tm=bs * ts)
        z3 = _norm_proj(xs, lw["norm_mix"], lw["w_in_att"], _F32, tm=bs * ts).reshape(bs, ts, 3 * ATT_W)
        o_ret, sample_ret = _retention(z_mix.reshape(bs, ts, MIX_W), state_ret, i, sample_ret, i, depth,
                                       chunk=sample_chunk, n_chunks=ts // sample_chunk)
        ogs, lses = [], []
        for g in range(N_GROUPS):
            o_g, lse_g, new_caches[2 * g], new_caches[2 * g + 1] = _step_attention(
                z3, caches_k[g], caches_v[g], new_caches[2 * g], new_caches[2 * g + 1], i, g)
            ogs.append(o_g)
            lses.append(lse_g)
        xs = _mix_and_ffn(xs, z_mix, o_ret, ogs, lses, p_sample[i].reshape(bs * ts, D_PLE), lw, g_final, last,
                          tm_merge=bs * ts, tm_mlp=bs * ts, tf=512)

    as_heads = lambda a: a.reshape(a.shape[:2] + (a.shape[2] // HPG, HPG, ATT_HEAD_DIM))
    prompt_windows = [jnp.stack(a) for g in range(N_GROUPS) for a in (pk[g], pv[g])]
    sample_windows = [as_heads(a) for a in new_caches]
    return (xp.reshape(bp, tp, d), xs.reshape(bs, ts, d), *prompt_windows, prompt_ret, *sample_windows, sample_ret)
```

```python
import functools
import math

import jax
import jax.numpy as jnp
import numpy as np
from jax import lax
from jax.experimental import pallas as pl
from jax.experimental.pallas import tpu as pltpu
from jax.experimental.pallas import tpu_sc as plsc

D_MODEL = 1024
D_PLE = 256
RET_HEADS = 4
RET_DK = 128
RET_DV = 256
RET_CHUNK = 128
ATT_GROUPS = ((128, 1), (512, 4), (2048, 16))
N_GROUPS = 3
HPG = 4
ATT_HEAD_DIM = 128
ATT_HEADS = N_GROUPS * HPG
ATT_TAPS = 128
Q_BLOCK = 128
BAND_BLOCKS_IN_FLIGHT = 16
SHIFT_CHUNK_ROWS = 240
D_FF = 4 * D_MODEL
EPS = 1e-6

RET_QK_W = RET_HEADS * RET_DK
RET_V_W = RET_HEADS * RET_DV
GROUP_W = HPG * ATT_HEAD_DIM
ATT_W = ATT_HEADS * ATT_HEAD_DIM
D_IN = 2 * RET_QK_W + 2 * RET_V_W + 3 * ATT_W + 2 * D_MODEL

MIX_W = 2 * D_MODEL + 2 * RET_QK_W + 2 * RET_V_W
COL_GATES = 0
COL_RQ = 2 * D_MODEL
COL_RK = COL_RQ + RET_QK_W
COL_RV = COL_RK + RET_QK_W
COL_RG = COL_RV + RET_V_W
COL_AQ = 0
COL_AK = ATT_W
COL_AV = 2 * ATT_W
COL_BLOCK = 512

LSE_LANES = 128
VMEM_LIMIT = 48 * 1024 * 1024

_BF16 = jnp.bfloat16
_F32 = jnp.float32


def _params(*sem):
    return pltpu.CompilerParams(dimension_semantics=sem, vmem_limit_bytes=VMEM_LIMIT)


def _rmsnorm(x, g):
    return x * lax.rsqrt(jnp.mean(x * x, axis=-1, keepdims=True) + EPS) * g


def _dot(a, b):
    return jnp.dot(a, b, preferred_element_type=_F32)


def _dot_nt(a, b):
    return lax.dot_general(a, b, (((1,), (1,)), ((), ())), preferred_element_type=_F32)


def _dot_tn(a, b):
    return lax.dot_general(a, b, (((0,), (0,)), ((), ())), preferred_element_type=_F32)


def _norm_proj_kernel(x_ref, g_ref, w_ref, z_ref, h_ref):
    @pl.when(pl.program_id(1) == 0)
    def _():
        h_ref[...] = _rmsnorm(x_ref[...], g_ref[...]).astype(_BF16)

    z_ref[...] = _dot(h_ref[...], w_ref[...]).astype(z_ref.dtype)


def _norm_proj(x, g, w, out_dtype, *, tm, tn=COL_BLOCK):
    n, d = x.shape
    d_out = w.shape[1]
    return pl.pallas_call(
        _norm_proj_kernel,
        grid=(n // tm, d_out // tn),
        in_specs=[
            pl.BlockSpec((tm, d), lambda i, j: (i, 0)),
            pl.BlockSpec((1, d), lambda i, j: (0, 0)),
            pl.BlockSpec((d, tn), lambda i, j: (0, j)),
        ],
        out_specs=pl.BlockSpec((tm, tn), lambda i, j: (i, j)),
        out_shape=jax.ShapeDtypeStruct((n, d_out), out_dtype),
        scratch_shapes=[pltpu.VMEM((tm, d), _BF16)],
        compiler_params=_params("parallel", "arbitrary"),
        name="norm_proj",
    )(x, g, w)


def _ret_log_gamma():
    return jnp.log1p(-jnp.exp(jnp.linspace(math.log(1.0 / 32), math.log(1.0 / 512), RET_HEADS))).astype(_F32)


def _retention_tables(chunk):
    lg = _ret_log_gamma()
    pos = jnp.arange(chunk, dtype=_F32)
    diff = pos[:, None] - pos[None, :]
    intra = jnp.where(diff[None] >= 0, jnp.exp(lg[:, None, None] * jnp.maximum(diff, 0.0)[None]), 0.0)
    xi = jnp.exp(lg[:, None] * (pos[None] + 1.0))
    zeta = jnp.exp(lg[:, None] * (chunk - 1.0 - pos)[None])
    decay = jnp.exp(lg * chunk)
    return (intra * (RET_DK ** -0.5),
            jnp.broadcast_to(xi[:, :, None], (RET_HEADS, chunk, RET_DK)),
            jnp.broadcast_to(zeta[:, :, None], (RET_HEADS, chunk, RET_DV)),
            jnp.broadcast_to(decay[:, None, None], (RET_HEADS, 8, RET_DV)))


def _retention_kernel(q_ref, k_ref, v_ref, g_ref, s0_ref, intra_ref, xi_ref, zeta_ref, decay_ref,
                      o_ref, sfin_ref, state, *, chunk, n_chunks):
    j = pl.program_id(1)

    @pl.when(j == 0)
    def _():
        state[...] = s0_ref[...]

    for c in range(n_chunks):
        rows = slice(c * chunk, (c + 1) * chunk)
        for h in range(RET_HEADS):
            qk_cols = slice(h * RET_DK, (h + 1) * RET_DK)
            v_cols = slice(h * RET_DV, (h + 1) * RET_DV)
            q = q_ref[rows, qk_cols]
            kb = k_ref[rows, qk_cols]
            v = v_ref[rows, v_cols]
            g = g_ref[rows, v_cols].astype(_F32)
            r_prev = state[h]
            scores = _dot_nt(q, kb) * intra_ref[h]
            o = _dot(scores.astype(_BF16), v)
            o = o + _dot((q * xi_ref[h]).astype(_BF16), r_prev.astype(_BF16))
            u = _dot_tn(kb, (v * zeta_ref[h]).astype(_BF16)) * (RET_DK ** -0.5)
            state[h] = decay_ref[h, 0:1, :] * r_prev + u
            mu = jnp.mean(o, axis=-1, keepdims=True)
            oc = o - mu
            var = jnp.mean(oc * oc, axis=-1, keepdims=True)
            on = oc * lax.rsqrt(var + EPS)
            o_ref[rows, v_cols] = (on * (g * jax.nn.sigmoid(g))).astype(o_ref.dtype)

    @pl.when(j == pl.num_programs(1) - 1)
    def _():
        sfin_ref[...] = state[...]


def _retention(z3, states0, layer0, prev_states, layer, depth, *, chunk, n_chunks):
    b, t, _ = z3.shape
    tc = chunk * n_chunks
    intra, xi, zeta, decay = _retention_tables(chunk)
    const = lambda shape: pl.BlockSpec(shape, lambda bi, j: (0,) * len(shape))
    state_block = (None, None, RET_HEADS, RET_DK, RET_DV)
    kernel = functools.partial(_retention_kernel, chunk=chunk, n_chunks=n_chunks)
    args = [z3, z3, z3, z3, states0, intra, xi, zeta, decay]
    extra_specs, aliases = [], {}
    if prev_states is not None:
        extra_specs = [pl.BlockSpec(memory_space=pl.ANY)]
        aliases = {len(args): 1}
        kernel = functools.partial(_drop_refs, kernel, len(args), 1)
        args.append(prev_states)
    return pl.pallas_call(
        kernel,
        grid=(b, t // tc),
        in_specs=[
            pl.BlockSpec((None, tc, RET_QK_W), lambda bi, j: (bi, j, COL_RQ // RET_QK_W)),
            pl.BlockSpec((None, tc, RET_QK_W), lambda bi, j: (bi, j, COL_RK // RET_QK_W)),
            pl.BlockSpec((None, tc, RET_V_W), lambda bi, j: (bi, j, COL_RV // RET_V_W)),
            pl.BlockSpec((None, tc, RET_V_W), lambda bi, j: (bi, j, COL_RG // RET_V_W)),
            pl.BlockSpec(state_block, lambda bi, j: (layer0, bi, 0, 0, 0)),
            const((RET_HEADS, chunk, chunk)),
            const((RET_HEADS, chunk, RET_DK)),
            const((RET_HEADS, chunk, RET_DV)),
            const((RET_HEADS, 8, RET_DV)),
        ] + extra_specs,
        out_specs=[
            pl.BlockSpec((None, tc, RET_V_W), lambda bi, j: (bi, j, 0)),
            pl.BlockSpec(state_block, lambda bi, j: (layer, bi, 0, 0, 0)),
        ],
        out_shape=[
            jax.ShapeDtypeStruct((b, t, RET_V_W), _BF16),
            jax.ShapeDtypeStruct((depth, b, RET_HEADS, RET_DK, RET_DV), _F32),
        ],
        scratch_shapes=[pltpu.VMEM((RET_HEADS, RET_DK, RET_DV), _F32)],
        input_output_aliases=aliases,
        compiler_params=_params("parallel", "arbitrary"),
        name="retention",
    )(*args)


def _alibi_slope(head):
    return 2.0 ** (-8.0 * (head + 1.0) / ATT_HEADS)


def _pack_head_stats(cols):
    rows = cols[0].shape[0]
    lane = lax.broadcasted_iota(jnp.int32, (rows, LSE_LANES), 1)
    out = jnp.zeros((rows, LSE_LANES), _F32)
    for h, c in enumerate(cols):
        out = jnp.where(lane == h, c, out)
    return out


def _band_attn_kernel(q_ref, k_ref, kh_ref, v_ref, vh_ref, o_ref, lse_ref, *, n_sub, dil, group):
    i = pl.program_id(1)
    h = pl.program_id(2)
    blk = Q_BLOCK
    span = blk * dil
    row = lax.broadcasted_iota(jnp.int32, (blk, 2 * blk), 0)
    col = lax.broadcasted_iota(jnp.int32, (blk, 2 * blk), 1)
    delta = row + blk - col
    slope = jnp.float32(0.0)
    for hh in range(HPG):
        slope = jnp.where(h == hh, jnp.float32(_alibi_slope(group * HPG + hh)), slope)
    in_band = (delta >= 0) & (delta <= ATT_TAPS)
    bias = jnp.where(in_band, -slope * (delta * dil).astype(_F32), -jnp.inf)
    bias_first = jnp.where(col >= jnp.where(i > 0, 0, blk), bias, -jnp.inf)
    lane = lax.broadcasted_iota(jnp.int32, (blk, LSE_LANES), 1)
    scale = ATT_HEAD_DIM ** -0.5

    def strided(start, size):
        return pl.ds(start, size, stride=dil) if dil > 1 else pl.ds(start, size)

    def attend(rows, q, keys, values, bias_blk):
        s = _dot_nt(q.astype(_BF16), keys.astype(_BF16)) * scale + bias_blk
        m = jnp.max(s, axis=-1, keepdims=True)
        e = jnp.exp(s - m)
        den = jnp.sum(e, axis=-1, keepdims=True)
        o_ref[rows, :] = _dot((e / den).astype(_BF16), values.astype(_BF16))
        prev = jnp.where(h == 0, 0.0, lse_ref[rows, :])
        lse_ref[rows, :] = jnp.where(lane == h, m + jnp.log(den), prev)

    def residue_body(r, carry):
        rows = strided(r, blk)
        keys = jnp.concatenate([kh_ref[rows, :], k_ref[rows, :]], axis=0)
        values = jnp.concatenate([vh_ref[rows, :], v_ref[rows, :]], axis=0)
        attend(rows, q_ref[rows, :], keys, values, bias_first)

        for sb in range(1, n_sub):
            rows = strided(r + sb * span, blk)
            both = strided(r + (sb - 1) * span, 2 * blk)
            attend(rows, q_ref[rows, :], k_ref[both, :], v_ref[both, :], bias)
        return carry

    lax.fori_loop(0, dil, residue_body, 0, unroll=min(dil, max(1, BAND_BLOCKS_IN_FLIGHT // n_sub)))


def _band_attention(z3, group, *, tile):
    b, t, _ = z3.shape
    _, dil = ATT_GROUPS[group]
    span = Q_BLOCK * dil
    tile = max(min(tile, t), span)
    n_sub = tile // span
    tiles = t // tile
    cq = (COL_AQ + group * GROUP_W) // ATT_HEAD_DIM
    ck = (COL_AK + group * GROUP_W) // ATT_HEAD_DIM
    cv = (COL_AV + group * GROUP_W) // ATT_HEAD_DIM

    def main(c):
        return pl.BlockSpec((None, tile, ATT_HEAD_DIM), lambda bi, i, h: (bi, i, c + h))

    def halo(c):
        return pl.BlockSpec((None, span, ATT_HEAD_DIM), lambda bi, i, h: (bi, jnp.maximum(i * n_sub - 1, 0), c + h))

    kernel = functools.partial(_band_attn_kernel, n_sub=n_sub, dil=dil, group=group)
    return pl.pallas_call(
        kernel,
        grid=(b, tiles, HPG),
        in_specs=[main(cq), main(ck), halo(ck), main(cv), halo(cv)],
        out_specs=[
            pl.BlockSpec((None, tile, ATT_HEAD_DIM), lambda bi, i, h: (h, bi * tiles + i, 0)),
            pl.BlockSpec((tile, LSE_LANES), lambda bi, i, h: (bi * tiles + i, 0)),
        ],
        out_shape=[
            jax.ShapeDtypeStruct((HPG, b * t, ATT_HEAD_DIM), _F32),
            jax.ShapeDtypeStruct((b * t, LSE_LANES), _F32),
        ],
        compiler_params=_params("parallel", "parallel", "arbitrary"),
        name="band_attention_g%d" % group,
    )(z3, z3, z3, z3, z3)


def _step_attn_kernel(q_ref, kn_ref, vn_ref, kc_ref, vc_ref, o_ref, lse_ref, ko_ref, vo_ref, *,
                      t_new, cache_len, dil, group):
    n_res = min(dil, t_new)
    taps = cache_len // dil
    rows_all = HPG * t_new
    head_cols = [slice(h * ATT_HEAD_DIM, (h + 1) * ATT_HEAD_DIM) for h in range(HPG)]
    head_rows = [slice(h * t_new, (h + 1) * t_new) for h in range(HPG)]
    q = [q_ref[:, c].astype(_BF16) for c in head_cols]
    k_new = kn_ref[...]
    v_new = vn_ref[...]

    def cache_taps(ref, r, h):
        return ref[pl.ds(r * HPG + h, taps, stride=HPG * dil), :].astype(_BF16)

    scale = ATT_HEAD_DIM ** -0.5
    log2_dil = dil.bit_length() - 1
    log2_new = t_new.bit_length() - 1

    def row_terms(width):
        row_id = lax.broadcasted_iota(jnp.int32, (rows_all, width), 0)
        slope = jnp.zeros((rows_all, width), _F32)
        for h in range(HPG):
            slope = jnp.where((row_id >> log2_new) == h, _alibi_slope(group * HPG + h), slope)
        return row_id & (t_new - 1), slope

    query, slope = row_terms(taps)
    residue = query & (dil - 1)
    taps_back = taps + (query >> log2_dil) - lax.broadcasted_iota(jnp.int32, (rows_all, taps), 1)
    s_cache = None
    for r in range(n_res):
        s_r = jnp.concatenate([_dot_nt(q[h], cache_taps(kc_ref, r, h)) for h in range(HPG)], axis=0)
        s_cache = s_r if s_cache is None else jnp.where(residue == r, s_r, s_cache)
    s_cache = s_cache * scale - slope * (taps_back << log2_dil).astype(_F32)
    s_cache = jnp.where(taps_back <= ATT_TAPS, s_cache, -jnp.inf)

    query_n, slope_n = row_terms(t_new)
    back = query_n - lax.broadcasted_iota(jnp.int32, (rows_all, t_new), 1)
    s_new = jnp.concatenate([_dot_nt(q[h], k_new[:, head_cols[h]].astype(_BF16)) for h in range(HPG)], axis=0)
    s_new = s_new * scale - slope_n * back.astype(_F32)
    s_new = jnp.where((back >= 0) & ((back & (dil - 1)) == 0), s_new, -jnp.inf)

    m = jnp.maximum(jnp.max(s_cache, axis=-1, keepdims=True), jnp.max(s_new, axis=-1, keepdims=True))
    e_cache = jnp.exp(s_cache - m)
    e_new = jnp.exp(s_new - m)
    den = jnp.sum(e_cache, axis=-1, keepdims=True) + jnp.sum(e_new, axis=-1, keepdims=True)
    p_cache = e_cache / den
    p_new = (e_new / den).astype(_BF16)
    lse = m + jnp.log(den)

    for h in range(HPG):
        out = _dot(p_new[head_rows[h]], v_new[:, head_cols[h]].astype(_BF16))
        for r in range(n_res):
            p_r = p_cache[head_rows[h]]
            if n_res > 1:
                p_r = jnp.where(residue[head_rows[h]] == r, p_r, 0.0)
            out = out + _dot(p_r.astype(_BF16), cache_taps(vc_ref, r, h))
        o_ref[h] = out
    lse_ref[...] = _pack_head_stats([lse[rows] for rows in head_rows])

    for h in range(HPG):
        ko_ref[pl.ds(h, t_new, stride=HPG), :] = k_new[:, head_cols[h]]
        vo_ref[pl.ds(h, t_new, stride=HPG), :] = v_new[:, head_cols[h]]


def _step_attention(z3, cache_k, cache_v, shifted_k, shifted_v, layer, group):
    b, t_new, _ = z3.shape
    cache_len = cache_k.shape[2] // HPG
    _, dil = ATT_GROUPS[group]
    cq = (COL_AQ + group * GROUP_W) // COL_BLOCK
    ck = (COL_AK + group * GROUP_W) // COL_BLOCK
    cv = (COL_AV + group * GROUP_W) // COL_BLOCK
    zspec = lambda c: pl.BlockSpec((None, t_new, COL_BLOCK), lambda bi: (bi, 0, c))
    cache_spec = pl.BlockSpec((None, None, cache_len * HPG, ATT_HEAD_DIM), lambda bi: (layer, bi, 0, 0))
    kernel = functools.partial(_step_attn_kernel, t_new=t_new, cache_len=cache_len, dil=dil, group=group)
    new_rows = t_new * HPG
    new_spec = pl.BlockSpec((None, None, new_rows, ATT_HEAD_DIM),
                            lambda bi: (layer, bi, cache_len * HPG // new_rows - 1, 0))
    cache_shape = jax.ShapeDtypeStruct(cache_k.shape, cache_k.dtype)
    return pl.pallas_call(
        functools.partial(_drop_refs, kernel, 5, 2),
        grid=(b,),
        in_specs=[zspec(cq), zspec(ck), zspec(cv), cache_spec, cache_spec] + [pl.BlockSpec(memory_space=pl.ANY)] * 2,
        out_specs=[
            pl.BlockSpec((HPG, t_new, ATT_HEAD_DIM), lambda bi: (0, bi, 0)),
            pl.BlockSpec((t_new, LSE_LANES), lambda bi: (bi, 0)),
            new_spec, new_spec,
        ],
        out_shape=[
            jax.ShapeDtypeStruct((HPG, b * t_new, ATT_HEAD_DIM), _F32),
            jax.ShapeDtypeStruct((b * t_new, LSE_LANES), _F32),
            cache_shape, cache_shape,
        ],
        input_output_aliases={5: 2, 6: 3},
        compiler_params=_params("parallel"),
        name="step_attention_g%d" % group,
    )(z3, z3, z3, cache_k, cache_v, shifted_k, shifted_v)


def _shift_caches(caches, n_drop, after):
    depth, batch = caches[0].shape[:2]
    mesh = plsc.VectorSubcoreMesh(core_axis_name="core", subcore_axis_name="subcore")
    workers = mesh.num_cores * mesh.num_subcores
    per_worker = batch // workers
    n = len(caches)

    @pl.kernel(out_type=[jax.ShapeDtypeStruct(c.shape, c.dtype) for c in caches], mesh=mesh,
               scratch_types=[pltpu.VMEM((SHIFT_CHUNK_ROWS, caches[0].shape[3]), caches[0].dtype)])
    def shift(*refs):
        srcs, dsts, buf = refs[:n], refs[n + 1:2 * n + 1], refs[2 * n + 1]
        worker = lax.axis_index("core") * mesh.num_subcores + lax.axis_index("subcore")

        def move(src, dst, layer, b, row, count):
            stage = buf if count == SHIFT_CHUNK_ROWS else buf.at[pl.ds(0, count), :]
            pltpu.sync_copy(src.at[layer, b, pl.ds(row + n_drop, count), :], stage)
            pltpu.sync_copy(stage, dst.at[layer, b, pl.ds(row, count), :])

        for src, dst in zip(srcs, dsts):
            full, rest = divmod(src.shape[2] - n_drop, SHIFT_CHUNK_ROWS)
            for layer in range(depth):
                for local in range(per_worker):
                    b = worker * per_worker + local

                    @pl.loop(0, full)
                    def _(k):
                        move(src, dst, layer, b, k * SHIFT_CHUNK_ROWS, SHIFT_CHUNK_ROWS)

                    if rest:
                        move(src, dst, layer, b, full * SHIFT_CHUNK_ROWS, rest)

    return list(shift(*caches, after))


def _drop_refs(kernel, start, count, *refs):
    return kernel(*refs[:start], *refs[start + count:])


def _merge_kernel(x_ref, oret_ref, og0_ref, og1_ref, og2_ref, l0_ref, l1_ref, l2_ref, gates_ref,
                  wret_ref, watt_ref, wout_ref, y_ref):
    l0, l1, l2 = l0_ref[...], l1_ref[...], l2_ref[...]
    mx = jnp.maximum(jnp.maximum(l0, l1), l2)
    e0, e1, e2 = jnp.exp(l0 - mx), jnp.exp(l1 - mx), jnp.exp(l2 - mx)
    tot = e0 + e1 + e2
    w0, w1, w2 = e0 / tot, e1 / tot, e2 / tot
    parts = []
    for h in range(HPG):
        parts.append(w0[:, h:h + 1] * og0_ref[h] + w1[:, h:h + 1] * og1_ref[h] + w2[:, h:h + 1] * og2_ref[h])
    o_att = jnp.concatenate(parts, axis=-1).astype(_BF16)
    br_ret = _dot(oret_ref[...], wret_ref[...])
    br_att = _dot(o_att, watt_ref[...])
    ga = gates_ref[:, 0:D_MODEL].astype(_F32)
    gb = gates_ref[:, D_MODEL:2 * D_MODEL].astype(_F32)
    mix = jax.nn.sigmoid(ga) * br_ret + jax.nn.sigmoid(gb) * br_att
    y_ref[...] = x_ref[...] + _dot(mix.astype(_BF16), wout_ref[...])


def _merge(x, o_ret, ogs, lses, z, w_ret, w_att, w_out, *, tm):
    n, d = x.shape
    row = lambda w: pl.BlockSpec((tm, w), lambda i: (i, 0))
    full = lambda a: pl.BlockSpec(a.shape, lambda i: (0, 0))
    heads = pl.BlockSpec((HPG, tm, ATT_HEAD_DIM), lambda i: (0, i, 0))
    return pl.pallas_call(
        _merge_kernel,
        grid=(n // tm,),
        in_specs=[row(d), row(RET_V_W), heads, heads, heads,
                  row(LSE_LANES), row(LSE_LANES), row(LSE_LANES), row(2 * D_MODEL),
                  full(w_ret), full(w_att), full(w_out)],
        out_specs=row(d),
        out_shape=jax.ShapeDtypeStruct((n, d), _F32),
        compiler_params=_params("parallel"),
        name="merge_out_proj",
    )(x, o_ret, *ogs, *lses, z, w_ret, w_att, w_out)


def _mlp_kernel(x_ref, g_ref, wup_ref, wdown_ref, p_ref, wple_ref, wgate_ref, gfin_ref, y_ref, h_ref, acc_ref, *,
                final_norm):
    j = pl.program_id(1)

    @pl.when(j == 0)
    def _():
        h_ref[...] = _rmsnorm(x_ref[...], g_ref[...]).astype(_BF16)
        acc_ref[...] = jnp.zeros_like(acc_ref)

    u = jnp.maximum(_dot(h_ref[...], wup_ref[...]), 0.0)
    acc_ref[...] += _dot((u * u).astype(_BF16), wdown_ref[...])

    @pl.when(j == pl.num_programs(1) - 1)
    def _():
        x = x_ref[...] + acc_ref[...]
        gate = jax.nn.sigmoid(_dot(x.astype(_BF16), wgate_ref[...]))
        x = x + _dot(p_ref[...].astype(_BF16), wple_ref[...]) * gate
        if final_norm:
            x = _rmsnorm(x, gfin_ref[...])
        y_ref[...] = x


def _mlp(x, g, w_up, w_down, p, w_ple, w_gate, g_final, *, tm, tf, final_norm):
    n, d = x.shape
    row = lambda w: pl.BlockSpec((tm, w), lambda i, j: (i, 0))
    full = lambda a: pl.BlockSpec(a.shape, lambda i, j: (0, 0))
    return pl.pallas_call(
        functools.partial(_mlp_kernel, final_norm=final_norm),
        grid=(n // tm, D_FF // tf),
        in_specs=[row(d), full(g),
                  pl.BlockSpec((d, tf), lambda i, j: (0, j)),
                  pl.BlockSpec((tf, d), lambda i, j: (j, 0)),
                  row(D_PLE), full(w_ple), full(w_gate), full(g_final)],
        out_specs=row(d),
        out_shape=jax.ShapeDtypeStruct((n, d), _F32),
        scratch_shapes=[pltpu.VMEM((tm, d), _BF16), pltpu.VMEM((tm, d), _F32)],
        compiler_params=_params("parallel", "arbitrary"),
        name="mlp_ple",
    )(x, g, w_up, w_down, p, w_ple, w_gate, g_final)


def _layer_weights(norm_mix, w_in, w_ret_br, w_att_br, w_out, norm_ffn, w_up, w_down, w_ple, w_ple_gate, i):
    w = w_in[i]
    att_start = 2 * RET_QK_W + 2 * RET_V_W
    w_mix = jnp.concatenate([w[:, D_IN - 2 * D_MODEL:], w[:, :att_start]], axis=1)
    return dict(
        norm_mix=norm_mix[i][None, :], w_in_mix=w_mix.astype(_BF16),
        w_in_att=w[:, att_start:att_start + 3 * ATT_W].astype(_BF16),
        w_ret_br=w_ret_br[i].astype(_BF16), w_att_br=w_att_br[i].astype(_BF16), w_out=w_out[i].astype(_BF16),
        norm_ffn=norm_ffn[i][None, :], w_up=w_up[i].astype(_BF16), w_down=w_down[i].astype(_BF16),
        w_ple=w_ple[i].astype(_BF16), w_ple_gate=w_ple_gate[i].astype(_BF16))


def _mix_and_ffn(x, z_mix, o_ret, ogs, lses, p_l, lw, g_final, final_norm, *, tm_merge, tm_mlp, tf):
    n = x.shape[0]
    flat = lambda a: a.reshape(n, a.shape[-1])
    x_mid = _merge(x, flat(o_ret), ogs, lses, z_mix,
                   lw["w_ret_br"], lw["w_att_br"], lw["w_out"], tm=tm_merge)
    x = _mlp(x_mid, lw["norm_ffn"], lw["w_up"], lw["w_down"], p_l, lw["w_ple"], lw["w_ple_gate"], g_final,
             tm=tm_mlp, tf=tf, final_norm=final_norm)
    return x, x_mid


def kernel(x_prompt, x_sample, cache_win_k0, cache_win_v0, cache_win_k1, cache_win_v1, cache_win_k2, cache_win_v2,
           state_ret, p_prompt, p_sample, norm_mix, w_in, w_ret_br, w_att_br, w_out, norm_ffn, w_up, w_down,
           w_ple, w_ple_gate, norm_final):
    depth = w_in.shape[0]
    bp, tp, d = x_prompt.shape
    bs, ts, _ = x_sample.shape
    xp = x_prompt.reshape(bp * tp, d)
    xs = x_sample.reshape(bs * ts, d)
    g_final = norm_final[None, :]
    pos_head_rows = lambda c: c.reshape(c.shape[:2] + (c.shape[2] * HPG, ATT_HEAD_DIM))
    caches_k = [pos_head_rows(c) for c in (cache_win_k0, cache_win_k1, cache_win_k2)]
    caches_v = [pos_head_rows(c) for c in (cache_win_v0, cache_win_v1, cache_win_v2)]
    pk = [[] for _ in range(N_GROUPS)]
    pv = [[] for _ in range(N_GROUPS)]
    prompt_ret = sample_ret = None
    zero_state = jnp.zeros((1, bp, RET_HEADS, RET_DK, RET_DV), _F32)
    sample_chunk = math.gcd(ts, RET_CHUNK)
    lws = [_layer_weights(norm_mix, w_in, w_ret_br, w_att_br, w_out, norm_ffn, w_up, w_down, w_ple, w_ple_gate, i)
           for i in range(depth)]
    new_k = new_v = None

    for i, lw in enumerate(lws):
        z_mix = _norm_proj(xp, lw["norm_mix"], lw["w_in_mix"], _BF16, tm=2048, tn=1024)
        z3 = _norm_proj(xp, lw["norm_mix"], lw["w_in_att"], _F32, tm=1024, tn=ATT_W).reshape(bp, tp, 3 * ATT_W)
        o_ret, prompt_ret = _retention(z_mix.reshape(bp, tp, MIX_W), zero_state, 0, prompt_ret, i, depth,
                                       chunk=RET_CHUNK, n_chunks=4)
        ogs, lses = zip(*[_band_attention(z3, g, tile=2048) for g in range(N_GROUPS)])
        for g, (window, _) in enumerate(ATT_GROUPS):
            keep = min(window, tp)
            ck = COL_AK + g * GROUP_W
            cv = COL_AV + g * GROUP_W
            pk[g].append(z3[:, tp - keep:, ck:ck + GROUP_W].reshape(bp, keep, HPG, ATT_HEAD_DIM))
            pv[g].append(z3[:, tp - keep:, cv:cv + GROUP_W].reshape(bp, keep, HPG, ATT_HEAD_DIM))
        xp, x_mid = _mix_and_ffn(xp, z_mix, o_ret, ogs, lses, p_prompt[i].reshape(bp * tp, D_PLE), lw, g_final,
                                 i == depth - 1, tm_merge=256, tm_mlp=1024, tf=512)
        if i == 0:
            new_k = _shift_caches(caches_k, ts * HPG, x_mid)
        if i == depth - 1:
            new_v = _shift_caches(caches_v, ts * HPG, x_mid)

    for i, lw in enumerate(lws):
        z_mix = _norm_proj(xs, lw["norm_mix"], lw["w_in_mix"], _BF16, tm=bs * ts)
        z3 = _norm_proj(xs, lw["norm_mix"], lw["w_in_att"], _F32, tm=bs * ts).reshape(bs, ts, 3 * ATT_W)
        o_ret, sample_ret = _retention(z_mix.reshape(bs, ts, MIX_W), state_ret, i, sample_ret, i, depth,
                                       chunk=sample_chunk, n_chunks=ts // sample_chunk)
        ogs, lses = [], []
        for g in range(N_GROUPS):
            o_g, lse_g, new_k[g], new_v[g] = _step_attention(z3, caches_k[g], caches_v[g], new_k[g], new_v[g], i, g)
            ogs.append(o_g)
            lses.append(lse_g)
        xs, _ = _mix_and_ffn(xs, z_mix, o_ret, ogs, lses, p_sample[i].reshape(bs * ts, D_PLE), lw, g_final,
                             i == depth - 1, tm_merge=bs * ts, tm_mlp=bs * ts, tf=512)

    as_heads = lambda a: a.reshape(a.shape[:2] + (a.shape[2] // HPG, HPG, ATT_HEAD_DIM))
    prompt_windows = [jnp.stack(a) for g in range(N_GROUPS) for a in (pk[g], pv[g])]
    sample_windows = [as_heads(a) for kv in zip(new_k, new_v) for a in kv]
    return (xp.reshape(bp, tp, d), xs.reshape(bs, ts, d), *prompt_windows, prompt_ret, *sample_windows, sample_ret)
```

```python
import functools
import math

import jax
import jax.numpy as jnp
import numpy as np
from jax import lax
from jax.experimental import pallas as pl
from jax.experimental.pallas import tpu as pltpu

D_MODEL = 1024
D_PLE = 256
RET_HEADS = 4
RET_DK = 128
RET_DV = 256
RET_CHUNK = 128
ATT_GROUPS = ((128, 1), (512, 4), (2048, 16))
N_GROUPS = 3
HPG = 4
ATT_HEAD_DIM = 128
ATT_HEADS = N_GROUPS * HPG
ATT_TAPS = 128
Q_BLOCK = 128
BAND_BLOCKS_IN_FLIGHT = 16
D_FF = 4 * D_MODEL
EPS = 1e-6

RET_QK_W = RET_HEADS * RET_DK
RET_V_W = RET_HEADS * RET_DV
GROUP_W = HPG * ATT_HEAD_DIM
ATT_W = ATT_HEADS * ATT_HEAD_DIM
D_IN = 2 * RET_QK_W + 2 * RET_V_W + 3 * ATT_W + 2 * D_MODEL

MIX_W = 2 * D_MODEL + 2 * RET_QK_W + 2 * RET_V_W
COL_GATES = 0
COL_RQ = 2 * D_MODEL
COL_RK = COL_RQ + RET_QK_W
COL_RV = COL_RK + RET_QK_W
COL_RG = COL_RV + RET_V_W
COL_AQ = 0
COL_AK = ATT_W
COL_AV = 2 * ATT_W
COL_BLOCK = 512

LSE_LANES = 128
VMEM_LIMIT = 48 * 1024 * 1024
VMEM_LIMIT_MIX_FFN = 56 * 1024 * 1024

_BF16 = jnp.bfloat16
_F32 = jnp.float32


def _params(*sem, vmem_limit=VMEM_LIMIT):
    return pltpu.CompilerParams(dimension_semantics=sem, vmem_limit_bytes=vmem_limit)


def _rmsnorm(x, g):
    return x * lax.rsqrt(jnp.mean(x * x, axis=-1, keepdims=True) + EPS) * g


def _dot(a, b):
    return jnp.dot(a, b, preferred_element_type=_F32)


def _dot_nt(a, b):
    return lax.dot_general(a, b, (((1,), (1,)), ((), ())), preferred_element_type=_F32)


def _dot_tn(a, b):
    return lax.dot_general(a, b, (((0,), (0,)), ((), ())), preferred_element_type=_F32)


def _norm_proj_kernel(x_ref, g_ref, w_ref, z_ref, h_ref):
    @pl.when(pl.program_id(1) == 0)
    def _():
        h_ref[...] = _rmsnorm(x_ref[...], g_ref[...]).astype(_BF16)

    z_ref[...] = _dot(h_ref[...], w_ref[...]).astype(z_ref.dtype)


def _norm_proj(x, g, w, out_dtype, *, tm, tn=COL_BLOCK):
    n, d = x.shape
    d_out = w.shape[1]
    return pl.pallas_call(
        _norm_proj_kernel,
        grid=(n // tm, d_out // tn),
        in_specs=[
            pl.BlockSpec((tm, d), lambda i, j: (i, 0)),
            pl.BlockSpec((1, d), lambda i, j: (0, 0)),
            pl.BlockSpec((d, tn), lambda i, j: (0, j)),
        ],
        out_specs=pl.BlockSpec((tm, tn), lambda i, j: (i, j)),
        out_shape=jax.ShapeDtypeStruct((n, d_out), out_dtype),
        scratch_shapes=[pltpu.VMEM((tm, d), _BF16)],
        compiler_params=_params("parallel", "arbitrary"),
        name="norm_proj",
    )(x, g, w)


def _ret_log_gamma():
    return jnp.log1p(-jnp.exp(jnp.linspace(math.log(1.0 / 32), math.log(1.0 / 512), RET_HEADS))).astype(_F32)


def _retention_tables(chunk):
    lg = _ret_log_gamma()
    pos = jnp.arange(chunk, dtype=_F32)
    diff = pos[:, None] - pos[None, :]
    intra = jnp.where(diff[None] >= 0, jnp.exp(lg[:, None, None] * jnp.maximum(diff, 0.0)[None]), 0.0)
    xi = jnp.exp(lg[:, None] * (pos[None] + 1.0))
    zeta = jnp.exp(lg[:, None] * (chunk - 1.0 - pos)[None])
    decay = jnp.exp(lg * chunk)
    return (intra * (RET_DK ** -0.5),
            jnp.broadcast_to(xi[:, :, None], (RET_HEADS, chunk, RET_DK)),
            jnp.broadcast_to(zeta[:, :, None], (RET_HEADS, chunk, RET_DV)),
            jnp.broadcast_to(decay[:, None, None], (RET_HEADS, 8, RET_DV)))


def _retention_kernel(q_ref, k_ref, v_ref, g_ref, s0_ref, intra_ref, xi_ref, zeta_ref, decay_ref,
                      o_ref, sfin_ref, state, *, chunk, n_chunks):
    j = pl.program_id(1)

    @pl.when(j == 0)
    def _():
        state[...] = s0_ref[...]

    for c in range(n_chunks):
        rows = slice(c * chunk, (c + 1) * chunk)
        for h in range(RET_HEADS):
            qk_cols = slice(h * RET_DK, (h + 1) * RET_DK)
            v_cols = slice(h * RET_DV, (h + 1) * RET_DV)
            q = q_ref[rows, qk_cols]
            kb = k_ref[rows, qk_cols]
            v = v_ref[rows, v_cols]
            g = g_ref[rows, v_cols].astype(_F32)
            r_prev = state[h]
            scores = _dot_nt(q, kb) * intra_ref[h]
            o = _dot(scores.astype(_BF16), v)
            o = o + _dot((q * xi_ref[h]).astype(_BF16), r_prev.astype(_BF16))
            u = _dot_tn(kb, (v * zeta_ref[h]).astype(_BF16)) * (RET_DK ** -0.5)
            state[h] = decay_ref[h, 0:1, :] * r_prev + u
            mu = jnp.mean(o, axis=-1, keepdims=True)
            oc = o - mu
            var = jnp.mean(oc * oc, axis=-1, keepdims=True)
            on = oc * lax.rsqrt(var + EPS)
            o_ref[rows, v_cols] = (on * (g * jax.nn.sigmoid(g))).astype(o_ref.dtype)

    @pl.when(j == pl.num_programs(1) - 1)
    def _():
        sfin_ref[...] = state[...]


def _retention(z3, states0, layer0, prev_states, layer, depth, *, chunk, n_chunks):
    b, t, _ = z3.shape
    tc = chunk * n_chunks
    intra, xi, zeta, decay = _retention_tables(chunk)
    const = lambda shape: pl.BlockSpec(shape, lambda bi, j: (0,) * len(shape))
    state_block = (None, None, RET_HEADS, RET_DK, RET_DV)
    kernel = functools.partial(_retention_kernel, chunk=chunk, n_chunks=n_chunks)
    args = [z3, z3, z3, z3, states0, intra, xi, zeta, decay]
    extra_specs, aliases = [], {}
    if prev_states is not None:
        extra_specs = [pl.BlockSpec(memory_space=pl.ANY)]
        aliases = {len(args): 1}
        kernel = functools.partial(_drop_refs, kernel, len(args), 1)
        args.append(prev_states)
    return pl.pallas_call(
        kernel,
        grid=(b, t // tc),
        in_specs=[
            pl.BlockSpec((None, tc, RET_QK_W), lambda bi, j: (bi, j, COL_RQ // RET_QK_W)),
            pl.BlockSpec((None, tc, RET_QK_W), lambda bi, j: (bi, j, COL_RK // RET_QK_W)),
            pl.BlockSpec((None, tc, RET_V_W), lambda bi, j: (bi, j, COL_RV // RET_V_W)),
            pl.BlockSpec((None, tc, RET_V_W), lambda bi, j: (bi, j, COL_RG // RET_V_W)),
            pl.BlockSpec(state_block, lambda bi, j: (layer0, bi, 0, 0, 0)),
            const((RET_HEADS, chunk, chunk)),
            const((RET_HEADS, chunk, RET_DK)),
            const((RET_HEADS, chunk, RET_DV)),
            const((RET_HEADS, 8, RET_DV)),
        ] + extra_specs,
        out_specs=[
            pl.BlockSpec((None, tc, RET_V_W), lambda bi, j: (bi, j, 0)),
            pl.BlockSpec(state_block, lambda bi, j: (layer, bi, 0, 0, 0)),
        ],
        out_shape=[
            jax.ShapeDtypeStruct((b, t, RET_V_W), _BF16),
            jax.ShapeDtypeStruct((depth, b, RET_HEADS, RET_DK, RET_DV), _F32),
        ],
        scratch_shapes=[pltpu.VMEM((RET_HEADS, RET_DK, RET_DV), _F32)],
        input_output_aliases=aliases,
        compiler_params=_params("parallel", "arbitrary"),
        name="retention",
    )(*args)


def _alibi_slope(head):
    return 2.0 ** (-8.0 * (head + 1.0) / ATT_HEADS)


def _pack_head_stats(cols):
    rows = cols[0].shape[0]
    lane = lax.broadcasted_iota(jnp.int32, (rows, LSE_LANES), 1)
    out = jnp.zeros((rows, LSE_LANES), _F32)
    for h, c in enumerate(cols):
        out = jnp.where(lane == h, c, out)
    return out


def _band_attn_kernel(q_ref, k_ref, kh_ref, v_ref, vh_ref, o_ref, lse_ref, *, n_sub, dil, group):
    i = pl.program_id(1)
    h = pl.program_id(2)
    blk = Q_BLOCK
    span = blk * dil
    row = lax.broadcasted_iota(jnp.int32, (blk, 2 * blk), 0)
    col = lax.broadcasted_iota(jnp.int32, (blk, 2 * blk), 1)
    delta = row + blk - col
    slope = jnp.float32(0.0)
    for hh in range(HPG):
        slope = jnp.where(h == hh, jnp.float32(_alibi_slope(group * HPG + hh)), slope)
    in_band = (delta >= 0) & (delta <= ATT_TAPS)
    bias = jnp.where(in_band, -slope * (delta * dil).astype(_F32), -jnp.inf)
    bias_first = jnp.where(col >= jnp.where(i > 0, 0, blk), bias, -jnp.inf)
    lane = lax.broadcasted_iota(jnp.int32, (blk, LSE_LANES), 1)
    scale = ATT_HEAD_DIM ** -0.5

    def strided(start, size):
        return pl.ds(start, size, stride=dil) if dil > 1 else pl.ds(start, size)

    def attend(rows, q, keys, values, bias_blk):
        s = _dot_nt(q.astype(_BF16), keys.astype(_BF16)) * scale + bias_blk
        m = jnp.max(s, axis=-1, keepdims=True)
        e = jnp.exp(s - m)
        den = jnp.sum(e, axis=-1, keepdims=True)
        o_ref[rows, :] = _dot((e / den).astype(_BF16), values.astype(_BF16))
        prev = jnp.where(h == 0, 0.0, lse_ref[rows, :])
        lse_ref[rows, :] = jnp.where(lane == h, m + jnp.log(den), prev)

    def residue_body(r, carry):
        rows = strided(r, blk)
        keys = jnp.concatenate([kh_ref[rows, :], k_ref[rows, :]], axis=0)
        values = jnp.concatenate([vh_ref[rows, :], v_ref[rows, :]], axis=0)
        attend(rows, q_ref[rows, :], keys, values, bias_first)

        for sb in range(1, n_sub):
            rows = strided(r + sb * span, blk)
            both = strided(r + (sb - 1) * span, 2 * blk)
            attend(rows, q_ref[rows, :], k_ref[both, :], v_ref[both, :], bias)
        return carry

    lax.fori_loop(0, dil, residue_body, 0, unroll=min(dil, max(1, BAND_BLOCKS_IN_FLIGHT // n_sub)))


def _band_attention(z3, group, *, tile):
    b, t, _ = z3.shape
    _, dil = ATT_GROUPS[group]
    span = Q_BLOCK * dil
    tile = max(min(tile, t), span)
    n_sub = tile // span
    tiles = t // tile
    cq = (COL_AQ + group * GROUP_W) // ATT_HEAD_DIM
    ck = (COL_AK + group * GROUP_W) // ATT_HEAD_DIM
    cv = (COL_AV + group * GROUP_W) // ATT_HEAD_DIM

    def main(c):
        return pl.BlockSpec((None, tile, ATT_HEAD_DIM), lambda bi, i, h: (bi, i, c + h))

    def halo(c):
        return pl.BlockSpec((None, span, ATT_HEAD_DIM), lambda bi, i, h: (bi, jnp.maximum(i * n_sub - 1, 0), c + h))

    kernel = functools.partial(_band_attn_kernel, n_sub=n_sub, dil=dil, group=group)
    return pl.pallas_call(
        kernel,
        grid=(b, tiles, HPG),
        in_specs=[main(cq), main(ck), halo(ck), main(cv), halo(cv)],
        out_specs=[
            pl.BlockSpec((None, tile, ATT_HEAD_DIM), lambda bi, i, h: (h, bi * tiles + i, 0)),
            pl.BlockSpec((tile, LSE_LANES), lambda bi, i, h: (bi * tiles + i, 0)),
        ],
        out_shape=[
            jax.ShapeDtypeStruct((HPG, b * t, ATT_HEAD_DIM), _F32),
            jax.ShapeDtypeStruct((b * t, LSE_LANES), _F32),
        ],
        compiler_params=_params("parallel", "parallel", "arbitrary"),
        name="band_attention_g%d" % group,
    )(z3, z3, z3, z3, z3)


def _step_attn_kernel(q_ref, kn_ref, vn_ref, kc_ref, vc_ref, o_ref, lse_ref, ko_ref, vo_ref, *,
                      t_new, cache_len, dil, group):
    n_res = min(dil, t_new)
    taps = cache_len // dil
    rows_all = HPG * t_new
    head_cols = [slice(h * ATT_HEAD_DIM, (h + 1) * ATT_HEAD_DIM) for h in range(HPG)]
    head_rows = [slice(h * t_new, (h + 1) * t_new) for h in range(HPG)]
    q = [q_ref[:, c].astype(_BF16) for c in head_cols]
    k_new = kn_ref[...]
    v_new = vn_ref[...]

    def cache_taps(ref, r, h):
        return ref[pl.ds(r * HPG + h, taps, stride=HPG * dil), :].astype(_BF16)

    scale = ATT_HEAD_DIM ** -0.5
    log2_dil = dil.bit_length() - 1
    log2_new = t_new.bit_length() - 1

    def row_terms(width):
        row_id = lax.broadcasted_iota(jnp.int32, (rows_all, width), 0)
        slope = jnp.zeros((rows_all, width), _F32)
        for h in range(HPG):
            slope = jnp.where((row_id >> log2_new) == h, _alibi_slope(group * HPG + h), slope)
        return row_id & (t_new - 1), slope

    query, slope = row_terms(taps)
    residue = query & (dil - 1)
    taps_back = taps + (query >> log2_dil) - lax.broadcasted_iota(jnp.int32, (rows_all, taps), 1)
    s_cache = None
    for r in range(n_res):
        s_r = jnp.concatenate([_dot_nt(q[h], cache_taps(kc_ref, r, h)) for h in range(HPG)], axis=0)
        s_cache = s_r if s_cache is None else jnp.where(residue == r, s_r, s_cache)
    s_cache = s_cache * scale - slope * (taps_back << log2_dil).astype(_F32)
    s_cache = jnp.where(taps_back <= ATT_TAPS, s_cache, -jnp.inf)

    query_n, slope_n = row_terms(t_new)
    back = query_n - lax.broadcasted_iota(jnp.int32, (rows_all, t_new), 1)
    s_new = jnp.concatenate([_dot_nt(q[h], k_new[:, head_cols[h]].astype(_BF16)) for h in range(HPG)], axis=0)
    s_new = s_new * scale - slope_n * back.astype(_F32)
    s_new = jnp.where((back >= 0) & ((back & (dil - 1)) == 0), s_new, -jnp.inf)

    m = jnp.maximum(jnp.max(s_cache, axis=-1, keepdims=True), jnp.max(s_new, axis=-1, keepdims=True))
    e_cache = jnp.exp(s_cache - m)
    e_new = jnp.exp(s_new - m)
    den = jnp.sum(e_cache, axis=-1, keepdims=True) + jnp.sum(e_new, axis=-1, keepdims=True)
    p_cache = e_cache / den
    p_new = (e_new / den).astype(_BF16)
    lse = m + jnp.log(den)

    for h in range(HPG):
        out = _dot(p_new[head_rows[h]], v_new[:, head_cols[h]].astype(_BF16))
        for r in range(n_res):
            p_r = p_cache[head_rows[h]]
            if n_res > 1:
                p_r = jnp.where(residue[head_rows[h]] == r, p_r, 0.0)
            out = out + _dot(p_r.astype(_BF16), cache_taps(vc_ref, r, h))
        o_ref[h] = out
    lse_ref[...] = _pack_head_stats([lse[rows] for rows in head_rows])

    keep = (cache_len - t_new) * HPG
    ko_ref[0:keep, :] = kc_ref[t_new * HPG:, :]
    vo_ref[0:keep, :] = vc_ref[t_new * HPG:, :]
    for h in range(HPG):
        ko_ref[pl.ds(keep + h, t_new, stride=HPG), :] = k_new[:, head_cols[h]]
        vo_ref[pl.ds(keep + h, t_new, stride=HPG), :] = v_new[:, head_cols[h]]


def _step_attention(z3, cache_k, cache_v, prev_k, prev_v, layer, group):
    b, t_new, _ = z3.shape
    cache_len = cache_k.shape[2] // HPG
    _, dil = ATT_GROUPS[group]
    cq = (COL_AQ + group * GROUP_W) // COL_BLOCK
    ck = (COL_AK + group * GROUP_W) // COL_BLOCK
    cv = (COL_AV + group * GROUP_W) // COL_BLOCK
    zspec = lambda c: pl.BlockSpec((None, t_new, COL_BLOCK), lambda bi: (bi, 0, c))
    cache_spec = pl.BlockSpec((None, None, cache_len * HPG, ATT_HEAD_DIM), lambda bi: (layer, bi, 0, 0))
    kernel = functools.partial(_step_attn_kernel, t_new=t_new, cache_len=cache_len, dil=dil, group=group)
    in_specs = [zspec(cq), zspec(ck), zspec(cv), cache_spec, cache_spec]
    args = [z3, z3, z3, cache_k, cache_v]
    aliases = {}
    if prev_k is not None:
        in_specs += [pl.BlockSpec(memory_space=pl.ANY)] * 2
        args += [prev_k, prev_v]
        aliases = {5: 2, 6: 3}
        kernel = functools.partial(_drop_refs, kernel, 5, 2)
    cache_shape = jax.ShapeDtypeStruct(cache_k.shape, cache_k.dtype)
    return pl.pallas_call(
        kernel,
        grid=(b,),
        in_specs=in_specs,
        out_specs=[
            pl.BlockSpec((HPG, t_new, ATT_HEAD_DIM), lambda bi: (0, bi, 0)),
            pl.BlockSpec((t_new, LSE_LANES), lambda bi: (bi, 0)),
            cache_spec, cache_spec,
        ],
        out_shape=[
            jax.ShapeDtypeStruct((HPG, b * t_new, ATT_HEAD_DIM), _F32),
            jax.ShapeDtypeStruct((b * t_new, LSE_LANES), _F32),
            cache_shape, cache_shape,
        ],
        input_output_aliases=aliases,
        compiler_params=_params("parallel"),
        name="step_attention_g%d" % group,
    )(*args)


def _drop_refs(kernel, start, count, *refs):
    return kernel(*refs[:start], *refs[start + count:])


def _mix_ffn_kernel(x_ref, oret_ref, og0_ref, og1_ref, og2_ref, l0_ref, l1_ref, l2_ref, gates_ref,
                    wret_ref, watt_ref, wout_ref, g_ref, wup_ref, wdown_ref, p_ref, wple_ref, wgate_ref, gfin_ref,
                    y_ref, xmid_ref, h_ref, acc_ref, *, final_norm):
    j = pl.program_id(1)

    @pl.when(j == 0)
    def _():
        l0, l1, l2 = l0_ref[...], l1_ref[...], l2_ref[...]
        mx = jnp.maximum(jnp.maximum(l0, l1), l2)
        e0, e1, e2 = jnp.exp(l0 - mx), jnp.exp(l1 - mx), jnp.exp(l2 - mx)
        tot = e0 + e1 + e2
        w0, w1, w2 = e0 / tot, e1 / tot, e2 / tot
        parts = []
        for h in range(HPG):
            parts.append(w0[:, h:h + 1] * og0_ref[h] + w1[:, h:h + 1] * og1_ref[h] + w2[:, h:h + 1] * og2_ref[h])
        o_att = jnp.concatenate(parts, axis=-1).astype(_BF16)
        br_ret = _dot(oret_ref[...], wret_ref[...])
        br_att = _dot(o_att, watt_ref[...])
        ga = gates_ref[:, 0:D_MODEL].astype(_F32)
        gb = gates_ref[:, D_MODEL:2 * D_MODEL].astype(_F32)
        mix = jax.nn.sigmoid(ga) * br_ret + jax.nn.sigmoid(gb) * br_att
        x_mid = x_ref[...] + _dot(mix.astype(_BF16), wout_ref[...])
        xmid_ref[...] = x_mid
        h_ref[...] = _rmsnorm(x_mid, g_ref[...]).astype(_BF16)
        acc_ref[...] = jnp.zeros_like(acc_ref)

    u = jnp.maximum(_dot(h_ref[...], wup_ref[...]), 0.0)
    acc_ref[...] += _dot((u * u).astype(_BF16), wdown_ref[...])

    @pl.when(j == pl.num_programs(1) - 1)
    def _():
        x = xmid_ref[...] + acc_ref[...]
        gate = jax.nn.sigmoid(_dot(x.astype(_BF16), wgate_ref[...]))
        x = x + _dot(p_ref[...].astype(_BF16), wple_ref[...]) * gate
        if final_norm:
            x = _rmsnorm(x, gfin_ref[...])
        y_ref[...] = x


def _mix_ffn(x, o_ret, ogs, lses, z_mix, p_all, layer, lw, g_final, *, tm, tf, final_norm):
    n, d = x.shape
    row = lambda w: pl.BlockSpec((tm, w), lambda i, j: (i, 0))
    heads = pl.BlockSpec((HPG, tm, ATT_HEAD_DIM), lambda i, j: (0, i, 0))
    once = lambda a: pl.BlockSpec(a.shape, lambda i, j: (0, 0), pipeline_mode=pl.Buffered(1))
    weights = [lw[k] for k in ("w_ret_br", "w_att_br", "w_out", "norm_ffn")]
    tail = [lw["w_ple"], lw["w_ple_gate"], g_final]
    return pl.pallas_call(
        functools.partial(_mix_ffn_kernel, final_norm=final_norm),
        grid=(n // tm, D_FF // tf),
        in_specs=[row(d), row(RET_V_W), heads, heads, heads,
                  row(LSE_LANES), row(LSE_LANES), row(LSE_LANES), row(2 * D_MODEL)]
        + [once(w) for w in weights]
        + [pl.BlockSpec((d, tf), lambda i, j: (0, j)),
           pl.BlockSpec((tf, d), lambda i, j: (j, 0)),
           pl.BlockSpec((None, tm, D_PLE), lambda i, j: (layer, i, 0))]
        + [once(w) for w in tail],
        out_specs=row(d),
        out_shape=jax.ShapeDtypeStruct((n, d), _F32),
        scratch_shapes=[pltpu.VMEM((tm, d), _F32), pltpu.VMEM((tm, d), _BF16), pltpu.VMEM((tm, d), _F32)],
        compiler_params=_params("parallel", "arbitrary", vmem_limit=VMEM_LIMIT_MIX_FFN),
        name="mix_ffn",
    )(x, o_ret, *ogs, *lses, z_mix, *weights, lw["w_up"], lw["w_down"], p_all, *tail)


def _layer_weights(norm_mix, w_in, w_ret_br, w_att_br, w_out, norm_ffn, w_up, w_down, w_ple, w_ple_gate, i):
    w = w_in[i]
    att_start = 2 * RET_QK_W + 2 * RET_V_W
    w_mix = jnp.concatenate([w[:, D_IN - 2 * D_MODEL:], w[:, :att_start]], axis=1)
    return dict(
        norm_mix=norm_mix[i][None, :], w_in_mix=w_mix.astype(_BF16),
        w_in_att=w[:, att_start:att_start + 3 * ATT_W].astype(_BF16),
        w_ret_br=w_ret_br[i].astype(_BF16), w_att_br=w_att_br[i].astype(_BF16), w_out=w_out[i].astype(_BF16),
        norm_ffn=norm_ffn[i][None, :], w_up=w_up[i].astype(_BF16), w_down=w_down[i].astype(_BF16),
        w_ple=w_ple[i].astype(_BF16), w_ple_gate=w_ple_gate[i].astype(_BF16))


def kernel(x_prompt, x_sample, cache_win_k0, cache_win_v0, cache_win_k1, cache_win_v1, cache_win_k2, cache_win_v2,
           state_ret, p_prompt, p_sample, norm_mix, w_in, w_ret_br, w_att_br, w_out, norm_ffn, w_up, w_down,
           w_ple, w_ple_gate, norm_final):
    depth = w_in.shape[0]
    bp, tp, d = x_prompt.shape
    bs, ts, _ = x_sample.shape
    xp = x_prompt.reshape(bp * tp, d)
    xs = x_sample.reshape(bs * ts, d)
    g_final = norm_final[None, :]
    pp = p_prompt.reshape(depth, bp * tp, D_PLE)
    ps = p_sample.reshape(depth, bs * ts, D_PLE)
    pos_head_rows = lambda c: c.reshape(c.shape[:2] + (c.shape[2] * HPG, ATT_HEAD_DIM))
    caches_k = [pos_head_rows(c) for c in (cache_win_k0, cache_win_k1, cache_win_k2)]
    caches_v = [pos_head_rows(c) for c in (cache_win_v0, cache_win_v1, cache_win_v2)]
    new_k = [None] * N_GROUPS
    new_v = [None] * N_GROUPS
    pk = [[] for _ in range(N_GROUPS)]
    pv = [[] for _ in range(N_GROUPS)]
    prompt_ret = sample_ret = None
    zero_state = jnp.zeros((1, bp, RET_HEADS, RET_DK, RET_DV), _F32)
    sample_chunk = math.gcd(ts, RET_CHUNK)

    for i in range(depth):
        lw = _layer_weights(norm_mix, w_in, w_ret_br, w_att_br, w_out, norm_ffn, w_up, w_down, w_ple, w_ple_gate, i)
        last = i == depth - 1

        z_mix = _norm_proj(xp, lw["norm_mix"], lw["w_in_mix"], _BF16, tm=2048, tn=1024)
        z3 = _norm_proj(xp, lw["norm_mix"], lw["w_in_att"], _F32, tm=1024, tn=ATT_W).reshape(bp, tp, 3 * ATT_W)
        o_ret, prompt_ret = _retention(z_mix.reshape(bp, tp, MIX_W), zero_state, 0, prompt_ret, i, depth,
                                       chunk=RET_CHUNK, n_chunks=4)
        ogs, lses = zip(*[_band_attention(z3, g, tile=2048) for g in range(N_GROUPS)])
        for g, (window, _) in enumerate(ATT_GROUPS):
            keep = min(window, tp)
            ck = COL_AK + g * GROUP_W
            cv = COL_AV + g * GROUP_W
            pk[g].append(z3[:, tp - keep:, ck:ck + GROUP_W].reshape(bp, keep, HPG, ATT_HEAD_DIM))
            pv[g].append(z3[:, tp - keep:, cv:cv + GROUP_W].reshape(bp, keep, HPG, ATT_HEAD_DIM))
        xp = _mix_ffn(xp, o_ret.reshape(bp * tp, RET_V_W), ogs, lses, z_mix, pp, i, lw, g_final,
                      tm=512, tf=1024, final_norm=last)

        z_mix = _norm_proj(xs, lw["norm_mix"], lw["w_in_mix"], _BF16, tm=bs * ts)
        z3 = _norm_proj(xs, lw["norm_mix"], lw["w_in_att"], _F32, tm=bs * ts).reshape(bs, ts, 3 * ATT_W)
        o_ret, sample_ret = _retention(z_mix.reshape(bs, ts, MIX_W), state_ret, i, sample_ret, i, depth,
                                       chunk=sample_chunk, n_chunks=ts // sample_chunk)
        ogs, lses = [], []
        for g in range(N_GROUPS):
            o_g, lse_g, new_k[g], new_v[g] = _step_attention(z3, caches_k[g], caches_v[g], new_k[g], new_v[g], i, g)
            ogs.append(o_g)
            lses.append(lse_g)
        xs = _mix_ffn(xs, o_ret.reshape(bs * ts, RET_V_W), ogs, lses, z_mix, ps, i, lw, g_final,
                      tm=bs * ts, tf=1024, final_norm=last)

    as_heads = lambda a: a.reshape(a.shape[:2] + (a.shape[2] // HPG, HPG, ATT_HEAD_DIM))
    prompt_windows = [jnp.stack(a) for g in range(N_GROUPS) for a in (pk[g], pv[g])]
    sample_windows = [as_heads(a) for g in range(N_GROUPS) for a in (new_k[g], new_v[g])]
    return (xp.reshape(bp, tp, d), xs.reshape(bs, ts, d), *prompt_windows, prompt_ret, *sample_windows, sample_ret)
```

```python
import functools
import math

import jax
import jax.numpy as jnp
import numpy as np
from jax import lax
from jax.experimental import pallas as pl
from jax.experimental.pallas import tpu as pltpu

D_MODEL = 1024
D_PLE = 256
RET_HEADS = 4
RET_DK = 128
RET_DV = 256
RET_CHUNK = 128
ATT_GROUPS = ((128, 1), (512, 4), (2048, 16))
N_GROUPS = 3
HPG = 4
ATT_HEAD_DIM = 128
ATT_HEADS = N_GROUPS * HPG
ATT_TAPS = 128
Q_BLOCK = 128
ATT_TILE = 2048
D_FF = 4 * D_MODEL
EPS = 1e-6

RET_QK_W = RET_HEADS * RET_DK
RET_V_W = RET_HEADS * RET_DV
GROUP_W = HPG * ATT_HEAD_DIM
ATT_W = ATT_HEADS * ATT_HEAD_DIM
D_IN = 2 * RET_QK_W + 2 * RET_V_W + 3 * ATT_W + 2 * D_MODEL

MIX_W = 2 * D_MODEL + 2 * RET_QK_W + 2 * RET_V_W
COL_GATES = 0
COL_RQ = 2 * D_MODEL
COL_RK = COL_RQ + RET_QK_W
COL_RV = COL_RK + RET_QK_W
COL_RG = COL_RV + RET_V_W
COL_AQ = 0
COL_AK = ATT_W
COL_AV = 2 * ATT_W
COL_BLOCK = 512

LSE_LANES = 128
VMEM_LIMIT = 48 * 1024 * 1024
VMEM_LIMIT_MIX_FFN = 56 * 1024 * 1024

_BF16 = jnp.bfloat16
_F32 = jnp.float32


def _params(*sem, vmem_limit=VMEM_LIMIT):
    return pltpu.CompilerParams(dimension_semantics=sem, vmem_limit_bytes=vmem_limit)


def _rmsnorm(x, g):
    return x * lax.rsqrt(jnp.mean(x * x, axis=-1, keepdims=True) + EPS) * g


def _dot(a, b):
    return jnp.dot(a, b, preferred_element_type=_F32)


def _dot_nt(a, b):
    return lax.dot_general(a, b, (((1,), (1,)), ((), ())), preferred_element_type=_F32)


def _dot_tn(a, b):
    return lax.dot_general(a, b, (((0,), (0,)), ((), ())), preferred_element_type=_F32)


def _norm_proj_kernel(x_ref, g_ref, w_ref, z_ref, h_ref):
    @pl.when(pl.program_id(1) == 0)
    def _():
        h_ref[...] = _rmsnorm(x_ref[...], g_ref[...]).astype(_BF16)

    z_ref[...] = _dot(h_ref[...], w_ref[...]).astype(z_ref.dtype)


def _norm_proj(x, g, w, out_dtype, *, tm, tn=COL_BLOCK):
    n, d = x.shape
    d_out = w.shape[1]
    return pl.pallas_call(
        _norm_proj_kernel,
        grid=(n // tm, d_out // tn),
        in_specs=[
            pl.BlockSpec((tm, d), lambda i, j: (i, 0)),
            pl.BlockSpec((1, d), lambda i, j: (0, 0)),
            pl.BlockSpec((d, tn), lambda i, j: (0, j)),
        ],
        out_specs=pl.BlockSpec((tm, tn), lambda i, j: (i, j)),
        out_shape=jax.ShapeDtypeStruct((n, d_out), out_dtype),
        scratch_shapes=[pltpu.VMEM((tm, d), _BF16)],
        compiler_params=_params("parallel", "arbitrary"),
        name="norm_proj",
    )(x, g, w)


def _norm_proj_att_kernel(x_ref, g_ref, w_ref, z_ref, h_ref, stage_ref):
    j = pl.program_id(1)

    @pl.when(j == 0)
    def _():
        h_ref[...] = _rmsnorm(x_ref[...], g_ref[...]).astype(_BF16)

    res = _dot(h_ref[...], w_ref[...])
    for group, (_, dil) in enumerate(ATT_GROUPS):
        @pl.when(j // 3 == group)
        def _(dil=dil):
            if dil == 1:
                z_ref[...] = res.astype(_BF16)
                return
            per = ATT_TILE // dil
            for c in range(HPG):
                stage_ref[c] = res[:, c * ATT_HEAD_DIM:(c + 1) * ATT_HEAD_DIM]
            for r in range(dil):
                for c in range(HPG):
                    z_ref[r * per:(r + 1) * per, c * ATT_HEAD_DIM:(c + 1) * ATT_HEAD_DIM] = (
                        stage_ref[c, pl.ds(r, per, stride=dil), :].astype(_BF16))


def _norm_proj_att(x, g, w):
    n, d = x.shape
    d_out = w.shape[1]
    return pl.pallas_call(
        _norm_proj_att_kernel,
        grid=(n // ATT_TILE, d_out // GROUP_W),
        in_specs=[
            pl.BlockSpec((ATT_TILE, d), lambda i, j: (i, 0)),
            pl.BlockSpec((1, d), lambda i, j: (0, 0)),
            pl.BlockSpec((d, GROUP_W), lambda i, j: (0, j)),
        ],
        out_specs=pl.BlockSpec((ATT_TILE, GROUP_W), lambda i, j: (i, j)),
        out_shape=jax.ShapeDtypeStruct((n, d_out), _BF16),
        scratch_shapes=[pltpu.VMEM((ATT_TILE, d), _BF16), pltpu.VMEM((HPG, ATT_TILE, ATT_HEAD_DIM), _F32)],
        compiler_params=_params("parallel", "arbitrary"),
        name="norm_proj_att",
    )(x, g, w)


def _ret_log_gamma():
    return jnp.log1p(-jnp.exp(jnp.linspace(math.log(1.0 / 32), math.log(1.0 / 512), RET_HEADS))).astype(_F32)


def _retention_tables(chunk):
    lg = _ret_log_gamma()
    pos = jnp.arange(chunk, dtype=_F32)
    diff = pos[:, None] - pos[None, :]
    intra = jnp.where(diff[None] >= 0, jnp.exp(lg[:, None, None] * jnp.maximum(diff, 0.0)[None]), 0.0)
    xi = jnp.exp(lg[:, None] * (pos[None] + 1.0))
    zeta = jnp.exp(lg[:, None] * (chunk - 1.0 - pos)[None])
    decay = jnp.exp(lg * chunk)
    return (intra * (RET_DK ** -0.5),
            jnp.broadcast_to(xi[:, :, None], (RET_HEADS, chunk, RET_DK)),
            jnp.broadcast_to(zeta[:, :, None], (RET_HEADS, chunk, RET_DV)),
            jnp.broadcast_to(decay[:, None, None], (RET_HEADS, 8, RET_DV)))


def _retention_kernel(q_ref, k_ref, v_ref, g_ref, s0_ref, intra_ref, xi_ref, zeta_ref, decay_ref,
                      o_ref, sfin_ref, state, *, chunk, n_chunks):
    j = pl.program_id(1)

    @pl.when(j == 0)
    def _():
        state[...] = s0_ref[...]

    for c in range(n_chunks):
        rows = slice(c * chunk, (c + 1) * chunk)
        for h in range(RET_HEADS):
            qk_cols = slice(h * RET_DK, (h + 1) * RET_DK)
            v_cols = slice(h * RET_DV, (h + 1) * RET_DV)
            q = q_ref[rows, qk_cols]
            kb = k_ref[rows, qk_cols]
            v = v_ref[rows, v_cols]
            g = g_ref[rows, v_cols].astype(_F32)
            r_prev = state[h]
            scores = _dot_nt(q, kb) * intra_ref[h]
            o = _dot(scores.astype(_BF16), v)
            o = o + _dot((q * xi_ref[h]).astype(_BF16), r_prev.astype(_BF16))
            u = _dot_tn(kb, (v * zeta_ref[h]).astype(_BF16)) * (RET_DK ** -0.5)
            state[h] = decay_ref[h, 0:1, :] * r_prev + u
            mu = jnp.mean(o, axis=-1, keepdims=True)
            oc = o - mu
            var = jnp.mean(oc * oc, axis=-1, keepdims=True)
            on = oc * lax.rsqrt(var + EPS)
            o_ref[rows, v_cols] = (on * (g * jax.nn.sigmoid(g))).astype(o_ref.dtype)

    @pl.when(j == pl.num_programs(1) - 1)
    def _():
        sfin_ref[...] = state[...]


def _retention(z3, states0, layer0, prev_states, layer, depth, *, chunk, n_chunks):
    b, t, _ = z3.shape
    tc = chunk * n_chunks
    intra, xi, zeta, decay = _retention_tables(chunk)
    const = lambda shape: pl.BlockSpec(shape, lambda bi, j: (0,) * len(shape))
    state_block = (None, None, RET_HEADS, RET_DK, RET_DV)
    kernel = functools.partial(_retention_kernel, chunk=chunk, n_chunks=n_chunks)
    args = [z3, z3, z3, z3, states0, intra, xi, zeta, decay]
    extra_specs, aliases = [], {}
    if prev_states is not None:
        extra_specs = [pl.BlockSpec(memory_space=pl.ANY)]
        aliases = {len(args): 1}
        kernel = functools.partial(_drop_refs, kernel, len(args), 1)
        args.append(prev_states)
    return pl.pallas_call(
        kernel,
        grid=(b, t // tc),
        in_specs=[
            pl.BlockSpec((None, tc, RET_QK_W), lambda bi, j: (bi, j, COL_RQ // RET_QK_W)),
            pl.BlockSpec((None, tc, RET_QK_W), lambda bi, j: (bi, j, COL_RK // RET_QK_W)),
            pl.BlockSpec((None, tc, RET_V_W), lambda bi, j: (bi, j, COL_RV // RET_V_W)),
            pl.BlockSpec((None, tc, RET_V_W), lambda bi, j: (bi, j, COL_RG // RET_V_W)),
            pl.BlockSpec(state_block, lambda bi, j: (layer0, bi, 0, 0, 0)),
            const((RET_HEADS, chunk, chunk)),
            const((RET_HEADS, chunk, RET_DK)),
            const((RET_HEADS, chunk, RET_DV)),
            const((RET_HEADS, 8, RET_DV)),
        ] + extra_specs,
        out_specs=[
            pl.BlockSpec((None, tc, RET_V_W), lambda bi, j: (bi, j, 0)),
            pl.BlockSpec(state_block, lambda bi, j: (layer, bi, 0, 0, 0)),
        ],
        out_shape=[
            jax.ShapeDtypeStruct((b, t, RET_V_W), _BF16),
            jax.ShapeDtypeStruct((depth, b, RET_HEADS, RET_DK, RET_DV), _F32),
        ],
        scratch_shapes=[pltpu.VMEM((RET_HEADS, RET_DK, RET_DV), _F32)],
        input_output_aliases=aliases,
        compiler_params=_params("parallel", "arbitrary"),
        name="retention",
    )(*args)


def _alibi_slope(head):
    return 2.0 ** (-8.0 * (head + 1.0) / ATT_HEADS)


def _pack_head_stats(cols):
    rows = cols[0].shape[0]
    lane = lax.broadcasted_iota(jnp.int32, (rows, LSE_LANES), 1)
    out = jnp.zeros((rows, LSE_LANES), _F32)
    for h, c in enumerate(cols):
        out = jnp.where(lane == h, c, out)
    return out


def _band_attn_kernel(q_ref, k_ref, v_ref, o_ref, lse_ref, kprev, vprev, *, dil, group):
    i = pl.program_id(1)
    h = pl.program_id(2)
    blk = Q_BLOCK
    per = ATT_TILE // dil
    row = lax.broadcasted_iota(jnp.int32, (blk, 2 * blk), 0)
    col = lax.broadcasted_iota(jnp.int32, (blk, 2 * blk), 1)
    delta = row + blk - col
    slope = jnp.float32(0.0)
    for hh in range(HPG):
        slope = jnp.where(h == hh, jnp.float32(_alibi_slope(group * HPG + hh)), slope)
    in_band = (delta >= 0) & (delta <= ATT_TAPS)
    bias = jnp.where(in_band, -slope * (delta * dil).astype(_F32), -jnp.inf)
    bias_first = jnp.where(col >= jnp.where(i > 0, 0, blk), bias, -jnp.inf)
    lane = lax.broadcasted_iota(jnp.int32, (blk, LSE_LANES), 1)
    scale = ATT_HEAD_DIM ** -0.5

    @pl.when(i == 0)
    def _():
        kprev[h] = jnp.zeros((ATT_TILE, ATT_HEAD_DIM), _BF16)
        vprev[h] = jnp.zeros((ATT_TILE, ATT_HEAD_DIM), _BF16)

    for r in range(dil):
        for sb in range(per // blk):
            lo = r * per + sb * blk
            if sb == 0:
                last = slice((r + 1) * per - blk, (r + 1) * per)
                keys = jnp.concatenate([kprev[h, last, :], k_ref[lo:lo + blk, :]], axis=0)
                values = jnp.concatenate([vprev[h, last, :], v_ref[lo:lo + blk, :]], axis=0)
            else:
                keys = k_ref[lo - blk:lo + blk, :]
                values = v_ref[lo - blk:lo + blk, :]
            s = _dot_nt(q_ref[lo:lo + blk, :], keys) * scale + (bias_first if sb == 0 else bias)
            m = jnp.max(s, axis=-1, keepdims=True)
            e = jnp.exp(s - m)
            den = jnp.sum(e, axis=-1, keepdims=True)
            start = r + sb * blk * dil
            rows = pl.ds(start, blk, stride=dil) if dil > 1 else pl.ds(start, blk)
            o_ref[rows, :] = _dot((e / den).astype(_BF16), values)
            prev = jnp.where(h == 0, 0.0, lse_ref[rows, :])
            lse_ref[rows, :] = jnp.where(lane == h, m + jnp.log(den), prev)

    kprev[h] = k_ref[...]
    vprev[h] = v_ref[...]


def _band_attention(z3, group):
    b, t, _ = z3.shape
    _, dil = ATT_GROUPS[group]
    tiles = t // ATT_TILE
    spec = lambda which: pl.BlockSpec((None, ATT_TILE, ATT_HEAD_DIM),
                                      lambda bi, i, h: (bi, i, (group * 3 + which) * HPG + h))
    carry = pltpu.VMEM((HPG, ATT_TILE, ATT_HEAD_DIM), _BF16)
    return pl.pallas_call(
        functools.partial(_band_attn_kernel, dil=dil, group=group),
        grid=(b, tiles, HPG),
        in_specs=[spec(0), spec(1), spec(2)],
        out_specs=[
            pl.BlockSpec((None, ATT_TILE, ATT_HEAD_DIM), lambda bi, i, h: (h, bi * tiles + i, 0)),
            pl.BlockSpec((ATT_TILE, LSE_LANES), lambda bi, i, h: (bi * tiles + i, 0)),
        ],
        out_shape=[
            jax.ShapeDtypeStruct((HPG, b * t, ATT_HEAD_DIM), _F32),
            jax.ShapeDtypeStruct((b * t, LSE_LANES), _F32),
        ],
        scratch_shapes=[carry, carry],
        compiler_params=_params("parallel", "arbitrary", "arbitrary"),
        name="band_attention_g%d" % group,
    )(z3, z3, z3)


def _step_attn_kernel(q_ref, kn_ref, vn_ref, kc_ref, vc_ref, o_ref, lse_ref, ko_ref, vo_ref, *,
                      t_new, cache_len, dil, group):
    n_res = min(dil, t_new)
    taps = cache_len // dil
    rows_all = HPG * t_new
    head_cols = [slice(h * ATT_HEAD_DIM, (h + 1) * ATT_HEAD_DIM) for h in range(HPG)]
    head_rows = [slice(h * t_new, (h + 1) * t_new) for h in range(HPG)]
    q = [q_ref[:, c].astype(_BF16) for c in head_cols]
    k_new = kn_ref[...]
    v_new = vn_ref[...]

    def cache_taps(ref, r, h):
        return ref[pl.ds(r * HPG + h, taps, stride=HPG * dil), :].astype(_BF16)

    scale = ATT_HEAD_DIM ** -0.5
    log2_dil = dil.bit_length() - 1
    log2_new = t_new.bit_length() - 1

    def row_terms(width):
        row_id = lax.broadcasted_iota(jnp.int32, (rows_all, width), 0)
        slope = jnp.zeros((rows_all, width), _F32)
        for h in range(HPG):
            slope = jnp.where((row_id >> log2_new) == h, _alibi_slope(group * HPG + h), slope)
        return row_id & (t_new - 1), slope

    query, slope = row_terms(taps)
    residue = query & (dil - 1)
    taps_back = taps + (query >> log2_dil) - lax.broadcasted_iota(jnp.int32, (rows_all, taps), 1)
    s_cache = None
    for r in range(n_res):
        s_r = jnp.concatenate([_dot_nt(q[h], cache_taps(kc_ref, r, h)) for h in range(HPG)], axis=0)
        s_cache = s_r if s_cache is None else jnp.where(residue == r, s_r, s_cache)
    s_cache = s_cache * scale - slope * (taps_back << log2_dil).astype(_F32)
    s_cache = jnp.where(taps_back <= ATT_TAPS, s_cache, -jnp.inf)

    query_n, slope_n = row_terms(t_new)
    back = query_n - lax.broadcasted_iota(jnp.int32, (rows_all, t_new), 1)
    s_new = jnp.concatenate([_dot_nt(q[h], k_new[:, head_cols[h]].astype(_BF16)) for h in range(HPG)], axis=0)
    s_new = s_new * scale - slope_n * back.astype(_F32)
    s_new = jnp.where((back >= 0) & ((back & (dil - 1)) == 0), s_new, -jnp.inf)

    m = jnp.maximum(jnp.max(s_cache, axis=-1, keepdims=True), jnp.max(s_new, axis=-1, keepdims=True))
    e_cache = jnp.exp(s_cache - m)
    e_new = jnp.exp(s_new - m)
    den = jnp.sum(e_cache, axis=-1, keepdims=True) + jnp.sum(e_new, axis=-1, keepdims=True)
    p_cache = e_cache / den
    p_new = (e_new / den).astype(_BF16)
    lse = m + jnp.log(den)

    for h in range(HPG):
        out = _dot(p_new[head_rows[h]], v_new[:, head_cols[h]].astype(_BF16))
        for r in range(n_res):
            p_r = p_cache[head_rows[h]]
            if n_res > 1:
                p_r = jnp.where(residue[head_rows[h]] == r, p_r, 0.0)
            out = out + _dot(p_r.astype(_BF16), cache_taps(vc_ref, r, h))
        o_ref[h] = out
    lse_ref[...] = _pack_head_stats([lse[rows] for rows in head_rows])

    keep = (cache_len - t_new) * HPG
    ko_ref[0:keep, :] = kc_ref[t_new * HPG:, :]
    vo_ref[0:keep, :] = vc_ref[t_new * HPG:, :]
    for h in range(HPG):
        ko_ref[pl.ds(keep + h, t_new, stride=HPG), :] = k_new[:, head_cols[h]]
        vo_ref[pl.ds(keep + h, t_new, stride=HPG), :] = v_new[:, head_cols[h]]


def _step_attention(z3, cache_k, cache_v, prev_k, prev_v, layer, group):
    b, t_new, _ = z3.shape
    cache_len = cache_k.shape[2] // HPG
    _, dil = ATT_GROUPS[group]
    cq = (COL_AQ + group * GROUP_W) // COL_BLOCK
    ck = (COL_AK + group * GROUP_W) // COL_BLOCK
    cv = (COL_AV + group * GROUP_W) // COL_BLOCK
    zspec = lambda c: pl.BlockSpec((None, t_new, COL_BLOCK), lambda bi: (bi, 0, c))
    cache_spec = pl.BlockSpec((None, None, cache_len * HPG, ATT_HEAD_DIM), lambda bi: (layer, bi, 0, 0))
    kernel = functools.partial(_step_attn_kernel, t_new=t_new, cache_len=cache_len, dil=dil, group=group)
    in_specs = [zspec(cq), zspec(ck), zspec(cv), cache_spec, cache_spec]
    args = [z3, z3, z3, cache_k, cache_v]
    aliases = {}
    if prev_k is not None:
        in_specs += [pl.BlockSpec(memory_space=pl.ANY)] * 2
        args += [prev_k, prev_v]
        aliases = {5: 2, 6: 3}
        kernel = functools.partial(_drop_refs, kernel, 5, 2)
    cache_shape = jax.ShapeDtypeStruct(cache_k.shape, cache_k.dtype)
    return pl.pallas_call(
        kernel,
        grid=(b,),
        in_specs=in_specs,
        out_specs=[
            pl.BlockSpec((HPG, t_new, ATT_HEAD_DIM), lambda bi: (0, bi, 0)),
            pl.BlockSpec((t_new, LSE_LANES), lambda bi: (bi, 0)),
            cache_spec, cache_spec,
        ],
        out_shape=[
            jax.ShapeDtypeStruct((HPG, b * t_new, ATT_HEAD_DIM), _F32),
            jax.ShapeDtypeStruct((b * t_new, LSE_LANES), _F32),
            cache_shape, cache_shape,
        ],
        input_output_aliases=aliases,
        compiler_params=_params("parallel"),
        name="step_attention_g%d" % group,
    )(*args)


def _drop_refs(kernel, start, count, *refs):
    return kernel(*refs[:start], *refs[start + count:])


def _mix_ffn_kernel(x_ref, oret_ref, og0_ref, og1_ref, og2_ref, l0_ref, l1_ref, l2_ref, gates_ref,
                    wret_ref, watt_ref, wout_ref, g_ref, wup_ref, wdown_ref, p_ref, wple_ref, wgate_ref, gfin_ref,
                    y_ref, xmid_ref, h_ref, acc_ref, *, final_norm):
    j = pl.program_id(1)

    @pl.when(j == 0)
    def _():
        l0, l1, l2 = l0_ref[...], l1_ref[...], l2_ref[...]
        mx = jnp.maximum(jnp.maximum(l0, l1), l2)
        e0, e1, e2 = jnp.exp(l0 - mx), jnp.exp(l1 - mx), jnp.exp(l2 - mx)
        tot = e0 + e1 + e2
        w0, w1, w2 = e0 / tot, e1 / tot, e2 / tot
        parts = []
        for h in range(HPG):
            parts.append(w0[:, h:h + 1] * og0_ref[h] + w1[:, h:h + 1] * og1_ref[h] + w2[:, h:h + 1] * og2_ref[h])
        o_att = jnp.concatenate(parts, axis=-1).astype(_BF16)
        br_ret = _dot(oret_ref[...], wret_ref[...])
        br_att = _dot(o_att, watt_ref[...])
        ga = gates_ref[:, 0:D_MODEL].astype(_F32)
        gb = gates_ref[:, D_MODEL:2 * D_MODEL].astype(_F32)
        mix = jax.nn.sigmoid(ga) * br_ret + jax.nn.sigmoid(gb) * br_att
        x_mid = x_ref[...] + _dot(mix.astype(_BF16), wout_ref[...])
        xmid_ref[...] = x_mid
        h_ref[...] = _rmsnorm(x_mid, g_ref[...]).astype(_BF16)
        acc_ref[...] = jnp.zeros_like(acc_ref)

    u = jnp.maximum(_dot(h_ref[...], wup_ref[...]), 0.0)
    acc_ref[...] += _dot((u * u).astype(_BF16), wdown_ref[...])

    @pl.when(j == pl.num_programs(1) - 1)
    def _():
        x = xmid_ref[...] + acc_ref[...]
        gate = jax.nn.sigmoid(_dot(x.astype(_BF16), wgate_ref[...]))
        x = x + _dot(p_ref[...].astype(_BF16), wple_ref[...]) * gate
        if final_norm:
            x = _rmsnorm(x, gfin_ref[...])
        y_ref[...] = x


def _mix_ffn(x, o_ret, ogs, lses, z_mix, p_all, layer, lw, g_final, *, tm, tf, final_norm):
    n, d = x.shape
    row = lambda w: pl.BlockSpec((tm, w), lambda i, j: (i, 0))
    heads = pl.BlockSpec((HPG, tm, ATT_HEAD_DIM), lambda i, j: (0, i, 0))
    once = lambda a: pl.BlockSpec(a.shape, lambda i, j: (0, 0), pipeline_mode=pl.Buffered(1))
    weights = [lw[k] for k in ("w_ret_br", "w_att_br", "w_out", "norm_ffn")]
    tail = [lw["w_ple"], lw["w_ple_gate"], g_final]
    return pl.pallas_call(
        functools.partial(_mix_ffn_kernel, final_norm=final_norm),
        grid=(n // tm, D_FF // tf),
        in_specs=[row(d), row(RET_V_W), heads, heads, heads,
                  row(LSE_LANES), row(LSE_LANES), row(LSE_LANES), row(2 * D_MODEL)]
        + [once(w) for w in weights]
        + [pl.BlockSpec((d, tf), lambda i, j: (0, j)),
           pl.BlockSpec((tf, d), lambda i, j: (j, 0)),
           pl.BlockSpec((None, tm, D_PLE), lambda i, j: (layer, i, 0))]
        + [once(w) for w in tail],
        out_specs=row(d),
        out_shape=jax.ShapeDtypeStruct((n, d), _F32),
        scratch_shapes=[pltpu.VMEM((tm, d), _F32), pltpu.VMEM((tm, d), _BF16), pltpu.VMEM((tm, d), _F32)],
        compiler_params=_params("parallel", "arbitrary", vmem_limit=VMEM_LIMIT_MIX_FFN),
        name="mix_ffn",
    )(x, o_ret, *ogs, *lses, z_mix, *weights, lw["w_up"], lw["w_down"], p_all, *tail)


def _layer_weights(norm_mix, w_in, w_ret_br, w_att_br, w_out, norm_ffn, w_up, w_down, w_ple, w_ple_gate, i):
    w = w_in[i]
    att_start = 2 * RET_QK_W + 2 * RET_V_W
    w_mix = jnp.concatenate([w[:, D_IN - 2 * D_MODEL:], w[:, :att_start]], axis=1)
    return dict(
        norm_mix=norm_mix[i][None, :], w_in_mix=w_mix.astype(_BF16),
        w_in_att=w[:, att_start:att_start + 3 * ATT_W].astype(_BF16),
        w_in_att_grouped=jnp.concatenate(
            [w[:, att_start + which * ATT_W + g * GROUP_W:att_start + which * ATT_W + (g + 1) * GROUP_W]
             for g in range(N_GROUPS) for which in range(3)], axis=1).astype(_BF16),
        w_ret_br=w_ret_br[i].astype(_BF16), w_att_br=w_att_br[i].astype(_BF16), w_out=w_out[i].astype(_BF16),
        norm_ffn=norm_ffn[i][None, :], w_up=w_up[i].astype(_BF16), w_down=w_down[i].astype(_BF16),
        w_ple=w_ple[i].astype(_BF16), w_ple_gate=w_ple_gate[i].astype(_BF16))


def kernel(x_prompt, x_sample, cache_win_k0, cache_win_v0, cache_win_k1, cache_win_v1, cache_win_k2, cache_win_v2,
           state_ret, p_prompt, p_sample, norm_mix, w_in, w_ret_br, w_att_br, w_out, norm_ffn, w_up, w_down,
           w_ple, w_ple_gate, norm_final):
    depth = w_in.shape[0]
    bp, tp, d = x_prompt.shape
    bs, ts, _ = x_sample.shape
    xp = x_prompt.reshape(bp * tp, d)
    xs = x_sample.reshape(bs * ts, d)
    g_final = norm_final[None, :]
    pp = p_prompt.reshape(depth, bp * tp, D_PLE)
    ps = p_sample.reshape(depth, bs * ts, D_PLE)
    pos_head_rows = lambda c: c.reshape(c.shape[:2] + (c.shape[2] * HPG, ATT_HEAD_DIM))
    caches_k = [pos_head_rows(c) for c in (cache_win_k0, cache_win_k1, cache_win_k2)]
    caches_v = [pos_head_rows(c) for c in (cache_win_v0, cache_win_v1, cache_win_v2)]
    new_k = [None] * N_GROUPS
    new_v = [None] * N_GROUPS
    pk = [[] for _ in range(N_GROUPS)]
    pv = [[] for _ in range(N_GROUPS)]
    prompt_ret = sample_ret = None
    zero_state = jnp.zeros((1, bp, RET_HEADS, RET_DK, RET_DV), _F32)
    sample_chunk = math.gcd(ts, RET_CHUNK)

    for i in range(depth):
        lw = _layer_weights(norm_mix, w_in, w_ret_br, w_att_br, w_out, norm_ffn, w_up, w_down, w_ple, w_ple_gate, i)
        last = i == depth - 1

        z_mix = _norm_proj(xp, lw["norm_mix"], lw["w_in_mix"], _BF16, tm=2048, tn=1024)
        z_att = _norm_proj_att(xp, lw["norm_mix"], lw["w_in_att_grouped"]).reshape(bp, tp, 3 * ATT_W)
        o_ret, prompt_ret = _retention(z_mix.reshape(bp, tp, MIX_W), zero_state, 0, prompt_ret, i, depth,
                                       chunk=RET_CHUNK, n_chunks=4)
        ogs, lses = zip(*[_band_attention(z_att, g) for g in range(N_GROUPS)])
        tail = min(max(w for w, _ in ATT_GROUPS), tp)
        x_tail = xp.reshape(bp, tp, d)[:, tp - tail:, :].reshape(bp * tail, d)
        z_tail = _norm_proj(x_tail, lw["norm_mix"], lw["w_in_att"], _F32, tm=1024, tn=ATT_W).reshape(bp, tail, 3 * ATT_W)
        for g, (window, _) in enumerate(ATT_GROUPS):
            keep = min(window, tp)
            ck = COL_AK + g * GROUP_W
            cv = COL_AV + g * GROUP_W
            pk[g].append(z_tail[:, tail - keep:, ck:ck + GROUP_W].reshape(bp, keep, HPG, ATT_HEAD_DIM))
            pv[g].append(z_tail[:, tail - keep:, cv:cv + GROUP_W].reshape(bp, keep, HPG, ATT_HEAD_DIM))
        xp = _mix_ffn(xp, o_ret.reshape(bp * tp, RET_V_W), ogs, lses, z_mix, pp, i, lw, g_final,
                      tm=512, tf=1024, final_norm=last)

        z_mix = _norm_proj(xs, lw["norm_mix"], lw["w_in_mix"], _BF16, tm=bs * ts)
        z3 = _norm_proj(xs, lw["norm_mix"], lw["w_in_att"], _F32, tm=bs * ts).reshape(bs, ts, 3 * ATT_W)
        o_ret, sample_ret = _retention(z_mix.reshape(bs, ts, MIX_W), state_ret, i, sample_ret, i, depth,
                                       chunk=sample_chunk, n_chunks=ts // sample_chunk)
        ogs, lses = [], []
        for g in range(N_GROUPS):
            o_g, lse_g, new_k[g], new_v[g] = _step_attention(z3, caches_k[g], caches_v[g], new_k[g], new_v[g], i, g)
            ogs.append(o_g)
            lses.append(lse_g)
        xs = _mix_ffn(xs, o_ret.reshape(bs * ts, RET_V_W), ogs, lses, z_mix, ps, i, lw, g_final,
                      tm=bs * ts, tf=1024, final_norm=last)

    as_heads = lambda a: a.reshape(a.shape[:2] + (a.shape[2] // HPG, HPG, ATT_HEAD_DIM))
    prompt_windows = [jnp.stack(a) for g in range(N_GROUPS) for a in (pk[g], pv[g])]
    sample_windows = [as_heads(a) for g in range(N_GROUPS) for a in (new_k[g], new_v[g])]
    return (xp.reshape(bp, tp, d), xs.reshape(bs, ts, d), *prompt_windows, prompt_ret, *sample_windows, sample_ret)
```

```python
import functools
import math

import jax
import jax.numpy as jnp
import numpy as np
from jax import lax
from jax.experimental import pallas as pl
from jax.experimental.pallas import tpu as pltpu

D_MODEL = 1024
D_PLE = 256
RET_HEADS = 4
RET_DK = 128
RET_DV = 256
RET_CHUNK = 128
ATT_GROUPS = ((128, 1), (512, 4), (2048, 16))
N_GROUPS = 3
HPG = 4
ATT_HEAD_DIM = 128
ATT_HEADS = N_GROUPS * HPG
ATT_TAPS = 128
Q_BLOCK = 128
ATT_TILE = 2048
PROJ_PARTS = 4
D_FF = 4 * D_MODEL
EPS = 1e-6

RET_QK_W = RET_HEADS * RET_DK
RET_V_W = RET_HEADS * RET_DV
GROUP_W = HPG * ATT_HEAD_DIM
ATT_W = ATT_HEADS * ATT_HEAD_DIM
D_IN = 2 * RET_QK_W + 2 * RET_V_W + 3 * ATT_W + 2 * D_MODEL

MIX_W = 2 * D_MODEL + 2 * RET_QK_W + 2 * RET_V_W
COL_GATES = 0
COL_RQ = 2 * D_MODEL
COL_RK = COL_RQ + RET_QK_W
COL_RV = COL_RK + RET_QK_W
COL_RG = COL_RV + RET_V_W
COL_AQ = 0
COL_AK = ATT_W
COL_AV = 2 * ATT_W
COL_BLOCK = 512

LSE_LANES = 128
VMEM_LIMIT = 48 * 1024 * 1024
VMEM_LIMIT_MIX_FFN = 56 * 1024 * 1024

_BF16 = jnp.bfloat16
_F32 = jnp.float32


def _params(*sem, vmem_limit=VMEM_LIMIT):
    return pltpu.CompilerParams(dimension_semantics=sem, vmem_limit_bytes=vmem_limit)


def _rmsnorm(x, g):
    return x * lax.rsqrt(jnp.mean(x * x, axis=-1, keepdims=True) + EPS) * g


def _dot(a, b):
    return jnp.dot(a, b, preferred_element_type=_F32)


def _dot_nt(a, b):
    return lax.dot_general(a, b, (((1,), (1,)), ((), ())), preferred_element_type=_F32)


def _dot_tn(a, b):
    return lax.dot_general(a, b, (((0,), (0,)), ((), ())), preferred_element_type=_F32)


def _norm_proj_kernel(x_ref, g_ref, w_ref, z_ref, h_ref):
    @pl.when(pl.program_id(1) == 0)
    def _():
        h_ref[...] = _rmsnorm(x_ref[...], g_ref[...]).astype(_BF16)

    z_ref[...] = _dot(h_ref[...], w_ref[...]).astype(z_ref.dtype)


def _norm_proj(x, g, w, out_dtype, *, tm, tn=COL_BLOCK):
    n, d = x.shape
    d_out = w.shape[1]
    return pl.pallas_call(
        _norm_proj_kernel,
        grid=(n // tm, d_out // tn),
        in_specs=[
            pl.BlockSpec((tm, d), lambda i, j: (i, 0)),
            pl.BlockSpec((1, d), lambda i, j: (0, 0)),
            pl.BlockSpec((d, tn), lambda i, j: (0, j)),
        ],
        out_specs=pl.BlockSpec((tm, tn), lambda i, j: (i, j)),
        out_shape=jax.ShapeDtypeStruct((n, d_out), out_dtype),
        scratch_shapes=[pltpu.VMEM((tm, d), _BF16)],
        compiler_params=_params("parallel", "arbitrary"),
        name="norm_proj",
    )(x, g, w)


def _norm_proj_att_kernel(x_ref, g_ref, w_ref, z_ref, h_ref, stage_ref):
    j = pl.program_id(1)

    @pl.when(j == 0)
    def _():
        h_ref[...] = _rmsnorm(x_ref[...], g_ref[...]).astype(_BF16)

    part = ATT_TILE // PROJ_PARTS
    for group, (_, dil) in enumerate(ATT_GROUPS):
        @pl.when(j // 3 == group)
        def _(dil=dil):
            for a in range(PROJ_PARTS):
                rows = slice(a * part, (a + 1) * part)
                res = _dot(h_ref[rows, :], w_ref[...])
                if dil == 1:
                    z_ref[rows, :] = res.astype(_BF16)
                    continue
                per, sub = ATT_TILE // dil, part // dil
                for c in range(HPG):
                    stage_ref[c, rows, :] = res[:, c * ATT_HEAD_DIM:(c + 1) * ATT_HEAD_DIM]
                for r in range(dil):
                    for c in range(HPG):
                        z_ref[r * per + a * sub:r * per + (a + 1) * sub, c * ATT_HEAD_DIM:(c + 1) * ATT_HEAD_DIM] = (
                            stage_ref[c, pl.ds(a * part + r, sub, stride=dil), :].astype(_BF16))


def _norm_proj_att(x, g, w):
    n, d = x.shape
    d_out = w.shape[1]
    return pl.pallas_call(
        _norm_proj_att_kernel,
        grid=(n // ATT_TILE, d_out // GROUP_W),
        in_specs=[
            pl.BlockSpec((ATT_TILE, d), lambda i, j: (i, 0)),
            pl.BlockSpec((1, d), lambda i, j: (0, 0)),
            pl.BlockSpec((d, GROUP_W), lambda i, j: (0, j)),
        ],
        out_specs=pl.BlockSpec((ATT_TILE, GROUP_W), lambda i, j: (i, j)),
        out_shape=jax.ShapeDtypeStruct((n, d_out), _BF16),
        scratch_shapes=[pltpu.VMEM((ATT_TILE, d), _BF16), pltpu.VMEM((HPG, ATT_TILE, ATT_HEAD_DIM), _F32)],
        compiler_params=_params("parallel", "arbitrary"),
        name="norm_proj_att",
    )(x, g, w)


def _ret_log_gamma():
    return jnp.log1p(-jnp.exp(jnp.linspace(math.log(1.0 / 32), math.log(1.0 / 512), RET_HEADS))).astype(_F32)


def _retention_tables(chunk):
    lg = _ret_log_gamma()
    pos = jnp.arange(chunk, dtype=_F32)
    diff = pos[:, None] - pos[None, :]
    intra = jnp.where(diff[None] >= 0, jnp.exp(lg[:, None, None] * jnp.maximum(diff, 0.0)[None]), 0.0)
    xi = jnp.exp(lg[:, None] * (pos[None] + 1.0))
    zeta = jnp.exp(lg[:, None] * (chunk - 1.0 - pos)[None])
    decay = jnp.exp(lg * chunk)
    return (intra * (RET_DK ** -0.5),
            jnp.broadcast_to(xi[:, :, None], (RET_HEADS, chunk, RET_DK)),
            jnp.broadcast_to(zeta[:, :, None], (RET_HEADS, chunk, RET_DV)),
            jnp.broadcast_to(decay[:, None, None], (RET_HEADS, 8, RET_DV)))


def _retention_kernel(q_ref, k_ref, v_ref, g_ref, s0_ref, intra_ref, xi_ref, zeta_ref, decay_ref,
                      o_ref, sfin_ref, state, *, chunk, n_chunks):
    j = pl.program_id(1)

    @pl.when(j == 0)
    def _():
        state[...] = s0_ref[...]

    for c in range(n_chunks):
        rows = slice(c * chunk, (c + 1) * chunk)
        for h in range(RET_HEADS):
            qk_cols = slice(h * RET_DK, (h + 1) * RET_DK)
            v_cols = slice(h * RET_DV, (h + 1) * RET_DV)
            q = q_ref[rows, qk_cols]
            kb = k_ref[rows, qk_cols]
            v = v_ref[rows, v_cols]
            g = g_ref[rows, v_cols].astype(_F32)
            r_prev = state[h]
            scores = _dot_nt(q, kb) * intra_ref[h]
            o = _dot(scores.astype(_BF16), v)
            o = o + _dot((q * xi_ref[h]).astype(_BF16), r_prev.astype(_BF16))
            u = _dot_tn(kb, (v * zeta_ref[h]).astype(_BF16)) * (RET_DK ** -0.5)
            state[h] = decay_ref[h, 0:1, :] * r_prev + u
            mu = jnp.mean(o, axis=-1, keepdims=True)
            oc = o - mu
            var = jnp.mean(oc * oc, axis=-1, keepdims=True)
            on = oc * lax.rsqrt(var + EPS)
            o_ref[rows, v_cols] = (on * (g * jax.nn.sigmoid(g))).astype(o_ref.dtype)

    @pl.when(j == pl.num_programs(1) - 1)
    def _():
        sfin_ref[...] = state[...]


def _retention(z3, states0, layer0, prev_states, layer, depth, *, chunk, n_chunks):
    b, t, _ = z3.shape
    tc = chunk * n_chunks
    intra, xi, zeta, decay = _retention_tables(chunk)
    const = lambda shape: pl.BlockSpec(shape, lambda bi, j: (0,) * len(shape))
    state_block = (None, None, RET_HEADS, RET_DK, RET_DV)
    kernel = functools.partial(_retention_kernel, chunk=chunk, n_chunks=n_chunks)
    args = [z3, z3, z3, z3, states0, intra, xi, zeta, decay]
    extra_specs, aliases = [], {}
    if prev_states is not None:
        extra_specs = [pl.BlockSpec(memory_space=pl.ANY)]
        aliases = {len(args): 1}
        kernel = functools.partial(_drop_refs, kernel, len(args), 1)
        args.append(prev_states)
    return pl.pallas_call(
        kernel,
        grid=(b, t // tc),
        in_specs=[
            pl.BlockSpec((None, tc, RET_QK_W), lambda bi, j: (bi, j, COL_RQ // RET_QK_W)),
            pl.BlockSpec((None, tc, RET_QK_W), lambda bi, j: (bi, j, COL_RK // RET_QK_W)),
            pl.BlockSpec((None, tc, RET_V_W), lambda bi, j: (bi, j, COL_RV // RET_V_W)),
            pl.BlockSpec((None, tc, RET_V_W), lambda bi, j: (bi, j, COL_RG // RET_V_W)),
            pl.BlockSpec(state_block, lambda bi, j: (layer0, bi, 0, 0, 0)),
            const((RET_HEADS, chunk, chunk)),
            const((RET_HEADS, chunk, RET_DK)),
            const((RET_HEADS, chunk, RET_DV)),
            const((RET_HEADS, 8, RET_DV)),
        ] + extra_specs,
        out_specs=[
            pl.BlockSpec((None, tc, RET_V_W), lambda bi, j: (bi, j, 0)),
            pl.BlockSpec(state_block, lambda bi, j: (layer, bi, 0, 0, 0)),
        ],
        out_shape=[
            jax.ShapeDtypeStruct((b, t, RET_V_W), _BF16),
            jax.ShapeDtypeStruct((depth, b, RET_HEADS, RET_DK, RET_DV), _F32),
        ],
        scratch_shapes=[pltpu.VMEM((RET_HEADS, RET_DK, RET_DV), _F32)],
        input_output_aliases=aliases,
        compiler_params=_params("parallel", "arbitrary"),
        name="retention",
    )(*args)


def _alibi_slope(head):
    return 2.0 ** (-8.0 * (head + 1.0) / ATT_HEADS)


def _pack_head_stats(cols):
    rows = cols[0].shape[0]
    lane = lax.broadcasted_iota(jnp.int32, (rows, LSE_LANES), 1)
    out = jnp.zeros((rows, LSE_LANES), _F32)
    for h, c in enumerate(cols):
        out = jnp.where(lane == h, c, out)
    return out


def _band_attn_kernel(q_ref, k_ref, v_ref, o_ref, lse_ref, kprev, vprev, *, dil, group):
    i = pl.program_id(1)
    h = pl.program_id(2)
    blk = Q_BLOCK
    per = ATT_TILE // dil
    row = lax.broadcasted_iota(jnp.int32, (blk, 2 * blk), 0)
    col = lax.broadcasted_iota(jnp.int32, (blk, 2 * blk), 1)
    delta = row + blk - col
    slope = jnp.float32(0.0)
    for hh in range(HPG):
        slope = jnp.where(h == hh, jnp.float32(_alibi_slope(group * HPG + hh)), slope)
    in_band = (delta >= 0) & (delta <= ATT_TAPS)
    bias = jnp.where(in_band, -slope * (delta * dil).astype(_F32), -jnp.inf)
    bias_first = jnp.where(col >= jnp.where(i > 0, 0, blk), bias, -jnp.inf)
    lane = lax.broadcasted_iota(jnp.int32, (blk, LSE_LANES), 1)
    scale = ATT_HEAD_DIM ** -0.5

    @pl.when(i == 0)
    def _():
        kprev[h] = jnp.zeros((ATT_TILE, ATT_HEAD_DIM), _BF16)
        vprev[h] = jnp.zeros((ATT_TILE, ATT_HEAD_DIM), _BF16)

    for r in range(dil):
        for sb in range(per // blk):
            lo = r * per + sb * blk
            if sb == 0:
                last = slice((r + 1) * per - blk, (r + 1) * per)
                keys = jnp.concatenate([kprev[h, last, :], k_ref[lo:lo + blk, :]], axis=0)
                values = jnp.concatenate([vprev[h, last, :], v_ref[lo:lo + blk, :]], axis=0)
            else:
                keys = k_ref[lo - blk:lo + blk, :]
                values = v_ref[lo - blk:lo + blk, :]
            s = _dot_nt(q_ref[lo:lo + blk, :], keys) * scale + (bias_first if sb == 0 else bias)
            m = jnp.max(s, axis=-1, keepdims=True)
            e = jnp.exp(s - m)
            den = jnp.sum(e, axis=-1, keepdims=True)
            start = r + sb * blk * dil
            rows = pl.ds(start, blk, stride=dil) if dil > 1 else pl.ds(start, blk)
            o_ref[rows, :] = _dot((e / den).astype(_BF16), values)
            prev = jnp.where(h == 0, 0.0, lse_ref[rows, :])
            lse_ref[rows, :] = jnp.where(lane == h, m + jnp.log(den), prev)

    kprev[h] = k_ref[...]
    vprev[h] = v_ref[...]


def _band_attention(z3, group):
    b, t, _ = z3.shape
    _, dil = ATT_GROUPS[group]
    tiles = t // ATT_TILE
    spec = lambda which: pl.BlockSpec((None, ATT_TILE, ATT_HEAD_DIM),
                                      lambda bi, i, h: (bi, i, (group * 3 + which) * HPG + h))
    carry = pltpu.VMEM((HPG, ATT_TILE, ATT_HEAD_DIM), _BF16)
    return pl.pallas_call(
        functools.partial(_band_attn_kernel, dil=dil, group=group),
        grid=(b, tiles, HPG),
        in_specs=[spec(0), spec(1), spec(2)],
        out_specs=[
            pl.BlockSpec((None, ATT_TILE, ATT_HEAD_DIM), lambda bi, i, h: (h, bi * tiles + i, 0)),
            pl.BlockSpec((ATT_TILE, LSE_LANES), lambda bi, i, h: (bi * tiles + i, 0)),
        ],
        out_shape=[
            jax.ShapeDtypeStruct((HPG, b * t, ATT_HEAD_DIM), _F32),
            jax.ShapeDtypeStruct((b * t, LSE_LANES), _F32),
        ],
        scratch_shapes=[carry, carry],
        compiler_params=_params("parallel", "arbitrary", "arbitrary"),
        name="band_attention_g%d" % group,
    )(z3, z3, z3)


def _step_attn_kernel(q_ref, kn_ref, vn_ref, kc_ref, vc_ref, o_ref, lse_ref, ko_ref, vo_ref, *,
                      t_new, cache_len, dil, group):
    n_res = min(dil, t_new)
    taps = cache_len // dil
    rows_all = HPG * t_new
    head_cols = [slice(h * ATT_HEAD_DIM, (h + 1) * ATT_HEAD_DIM) for h in range(HPG)]
    head_rows = [slice(h * t_new, (h + 1) * t_new) for h in range(HPG)]
    q = [q_ref[:, c].astype(_BF16) for c in head_cols]
    k_new = kn_ref[...]
    v_new = vn_ref[...]

    def cache_taps(ref, r, h):
        return ref[pl.ds(r * HPG + h, taps, stride=HPG * dil), :].astype(_BF16)

    scale = ATT_HEAD_DIM ** -0.5
    log2_dil = dil.bit_length() - 1
    log2_new = t_new.bit_length() - 1

    def row_terms(width):
        row_id = lax.broadcasted_iota(jnp.int32, (rows_all, width), 0)
        slope = jnp.zeros((rows_all, width), _F32)
        for h in range(HPG):
            slope = jnp.where((row_id >> log2_new) == h, _alibi_slope(group * HPG + h), slope)
        return row_id & (t_new - 1), slope

    query, slope = row_terms(taps)
    residue = query & (dil - 1)
    taps_back = taps + (query >> log2_dil) - lax.broadcasted_iota(jnp.int32, (rows_all, taps), 1)
    s_cache = None
    for r in range(n_res):
        s_r = jnp.concatenate([_dot_nt(q[h], cache_taps(kc_ref, r, h)) for h in range(HPG)], axis=0)
        s_cache = s_r if s_cache is None else jnp.where(residue == r, s_r, s_cache)
    s_cache = s_cache * scale - slope * (taps_back << log2_dil).astype(_F32)
    s_cache = jnp.where(taps_back <= ATT_TAPS, s_cache, -jnp.inf)

    query_n, slope_n = row_terms(t_new)
    back = query_n - lax.broadcasted_iota(jnp.int32, (rows_all, t_new), 1)
    s_new = jnp.concatenate([_dot_nt(q[h], k_new[:, head_cols[h]].astype(_BF16)) for h in range(HPG)], axis=0)
    s_new = s_new * scale - slope_n * back.astype(_F32)
    s_new = jnp.where((back >= 0) & ((back & (dil - 1)) == 0), s_new, -jnp.inf)

    m = jnp.maximum(jnp.max(s_cache, axis=-1, keepdims=True), jnp.max(s_new, axis=-1, keepdims=True))
    e_cache = jnp.exp(s_cache - m)
    e_new = jnp.exp(s_new - m)
    den = jnp.sum(e_cache, axis=-1, keepdims=True) + jnp.sum(e_new, axis=-1, keepdims=True)
    p_cache = e_cache / den
    p_new = (e_new / den).astype(_BF16)
    lse = m + jnp.log(den)

    for h in range(HPG):
        out = _dot(p_new[head_rows[h]], v_new[:, head_cols[h]].astype(_BF16))
        for r in range(n_res):
            p_r = p_cache[head_rows[h]]
            if n_res > 1:
                p_r = jnp.where(residue[head_rows[h]] == r, p_r, 0.0)
            out = out + _dot(p_r.astype(_BF16), cache_taps(vc_ref, r, h))
        o_ref[h] = out
    lse_ref[...] = _pack_head_stats([lse[rows] for rows in head_rows])

    keep = (cache_len - t_new) * HPG
    ko_ref[0:keep, :] = kc_ref[t_new * HPG:, :]
    vo_ref[0:keep, :] = vc_ref[t_new * HPG:, :]
    for h in range(HPG):
        ko_ref[pl.ds(keep + h, t_new, stride=HPG), :] = k_new[:, head_cols[h]]
        vo_ref[pl.ds(keep + h, t_new, stride=HPG), :] = v_new[:, head_cols[h]]


def _step_attention(z3, cache_k, cache_v, prev_k, prev_v, layer, group):
    b, t_new, _ = z3.shape
    cache_len = cache_k.shape[2] // HPG
    _, dil = ATT_GROUPS[group]
    cq = (COL_AQ + group * GROUP_W) // COL_BLOCK
    ck = (COL_AK + group * GROUP_W) // COL_BLOCK
    cv = (COL_AV + group * GROUP_W) // COL_BLOCK
    zspec = lambda c: pl.BlockSpec((None, t_new, COL_BLOCK), lambda bi: (bi, 0, c))
    cache_spec = pl.BlockSpec((None, None, cache_len * HPG, ATT_HEAD_DIM), lambda bi: (layer, bi, 0, 0))
    kernel = functools.partial(_step_attn_kernel, t_new=t_new, cache_len=cache_len, dil=dil, group=group)
    in_specs = [zspec(cq), zspec(ck), zspec(cv), cache_spec, cache_spec]
    args = [z3, z3, z3, cache_k, cache_v]
    aliases = {}
    if prev_k is not None:
        in_specs += [pl.BlockSpec(memory_space=pl.ANY)] * 2
        args += [prev_k, prev_v]
        aliases = {5: 2, 6: 3}
        kernel = functools.partial(_drop_refs, kernel, 5, 2)
    cache_shape = jax.ShapeDtypeStruct(cache_k.shape, cache_k.dtype)
    return pl.pallas_call(
        kernel,
        grid=(b,),
        in_specs=in_specs,
        out_specs=[
            pl.BlockSpec((HPG, t_new, ATT_HEAD_DIM), lambda bi: (0, bi, 0)),
            pl.BlockSpec((t_new, LSE_LANES), lambda bi: (bi, 0)),
            cache_spec, cache_spec,
        ],
        out_shape=[
            jax.ShapeDtypeStruct((HPG, b * t_new, ATT_HEAD_DIM), _F32),
            jax.ShapeDtypeStruct((b * t_new, LSE_LANES), _F32),
            cache_shape, cache_shape,
        ],
        input_output_aliases=aliases,
        compiler_params=_params("parallel"),
        name="step_attention_g%d" % group,
    )(*args)


def _drop_refs(kernel, start, count, *refs):
    return kernel(*refs[:start], *refs[start + count:])


def _mix_ffn_kernel(x_ref, oret_ref, og0_ref, og1_ref, og2_ref, l0_ref, l1_ref, l2_ref, gates_ref,
                    wret_ref, watt_ref, wout_ref, g_ref, wup_ref, wdown_ref, p_ref, wple_ref, wgate_ref, gfin_ref,
                    y_ref, xmid_ref, h_ref, acc_ref, *, final_norm):
    j = pl.program_id(1)

    @pl.when(j == 0)
    def _():
        l0, l1, l2 = l0_ref[...], l1_ref[...], l2_ref[...]
        mx = jnp.maximum(jnp.maximum(l0, l1), l2)
        e0, e1, e2 = jnp.exp(l0 - mx), jnp.exp(l1 - mx), jnp.exp(l2 - mx)
        tot = e0 + e1 + e2
        w0, w1, w2 = e0 / tot, e1 / tot, e2 / tot
        parts = []
        for h in range(HPG):
            parts.append(w0[:, h:h + 1] * og0_ref[h] + w1[:, h:h + 1] * og1_ref[h] + w2[:, h:h + 1] * og2_ref[h])
        o_att = jnp.concatenate(parts, axis=-1).astype(_BF16)
        br_ret = _dot(oret_ref[...], wret_ref[...])
        br_att = _dot(o_att, watt_ref[...])
        ga = gates_ref[:, 0:D_MODEL].astype(_F32)
        gb = gates_ref[:, D_MODEL:2 * D_MODEL].astype(_F32)
        mix = jax.nn.sigmoid(ga) * br_ret + jax.nn.sigmoid(gb) * br_att
        x_mid = x_ref[...] + _dot(mix.astype(_BF16), wout_ref[...])
        xmid_ref[...] = x_mid
        h_ref[...] = _rmsnorm(x_mid, g_ref[...]).astype(_BF16)
        acc_ref[...] = jnp.zeros_like(acc_ref)

    u = jnp.maximum(_dot(h_ref[...], wup_ref[...]), 0.0)
    acc_ref[...] += _dot((u * u).astype(_BF16), wdown_ref[...])

    @pl.when(j == pl.num_programs(1) - 1)
    def _():
        x = xmid_ref[...] + acc_ref[...]
        gate = jax.nn.sigmoid(_dot(x.astype(_BF16), wgate_ref[...]))
        x = x + _dot(p_ref[...].astype(_BF16), wple_ref[...]) * gate
        if final_norm:
            x = _rmsnorm(x, gfin_ref[...])
        y_ref[...] = x


def _mix_ffn(x, o_ret, ogs, lses, z_mix, p_all, layer, lw, g_final, *, tm, tf, final_norm):
    n, d = x.shape
    row = lambda w: pl.BlockSpec((tm, w), lambda i, j: (i, 0))
    heads = pl.BlockSpec((HPG, tm, ATT_HEAD_DIM), lambda i, j: (0, i, 0))
    once = lambda a: pl.BlockSpec(a.shape, lambda i, j: (0, 0), pipeline_mode=pl.Buffered(1))
    weights = [lw[k] for k in ("w_ret_br", "w_att_br", "w_out", "norm_ffn")]
    tail = [lw["w_ple"], lw["w_ple_gate"], g_final]
    return pl.pallas_call(
        functools.partial(_mix_ffn_kernel, final_norm=final_norm),
        grid=(n // tm, D_FF // tf),
        in_specs=[row(d), row(RET_V_W), heads, heads, heads,
                  row(LSE_LANES), row(LSE_LANES), row(LSE_LANES), row(2 * D_MODEL)]
        + [once(w) for w in weights]
        + [pl.BlockSpec((d, tf), lambda i, j: (0, j)),
           pl.BlockSpec((tf, d), lambda i, j: (j, 0)),
           pl.BlockSpec((None, tm, D_PLE), lambda i, j: (layer, i, 0))]
        + [once(w) for w in tail],
        out_specs=row(d),
        out_shape=jax.ShapeDtypeStruct((n, d), _F32),
        scratch_shapes=[pltpu.VMEM((tm, d), _F32), pltpu.VMEM((tm, d), _BF16), pltpu.VMEM((tm, d), _F32)],
        compiler_params=_params("parallel", "arbitrary", vmem_limit=VMEM_LIMIT_MIX_FFN),
        name="mix_ffn",
    )(x, o_ret, *ogs, *lses, z_mix, *weights, lw["w_up"], lw["w_down"], p_all, *tail)


def _layer_weights(norm_mix, w_in, w_ret_br, w_att_br, w_out, norm_ffn, w_up, w_down, w_ple, w_ple_gate, i):
    w = w_in[i]
    att_start = 2 * RET_QK_W + 2 * RET_V_W
    w_mix = jnp.concatenate([w[:, D_IN - 2 * D_MODEL:], w[:, :att_start]], axis=1)
    return dict(
        norm_mix=norm_mix[i][None, :], w_in_mix=w_mix.astype(_BF16),
        w_in_att=w[:, att_start:att_start + 3 * ATT_W].astype(_BF16),
        w_in_kv=[jnp.concatenate([w[:, att_start + which * ATT_W + g * GROUP_W:att_start + which * ATT_W + (g + 1) * GROUP_W]
                                  for which in (1, 2)], axis=1).astype(_BF16) for g in range(N_GROUPS)],
        w_in_att_grouped=jnp.concatenate(
            [w[:, att_start + which * ATT_W + g * GROUP_W:att_start + which * ATT_W + (g + 1) * GROUP_W]
             for g in range(N_GROUPS) for which in range(3)], axis=1).astype(_BF16),
        w_ret_br=w_ret_br[i].astype(_BF16), w_att_br=w_att_br[i].astype(_BF16), w_out=w_out[i].astype(_BF16),
        norm_ffn=norm_ffn[i][None, :], w_up=w_up[i].astype(_BF16), w_down=w_down[i].astype(_BF16),
        w_ple=w_ple[i].astype(_BF16), w_ple_gate=w_ple_gate[i].astype(_BF16))


def kernel(x_prompt, x_sample, cache_win_k0, cache_win_v0, cache_win_k1, cache_win_v1, cache_win_k2, cache_win_v2,
           state_ret, p_prompt, p_sample, norm_mix, w_in, w_ret_br, w_att_br, w_out, norm_ffn, w_up, w_down,
           w_ple, w_ple_gate, norm_final):
    depth = w_in.shape[0]
    bp, tp, d = x_prompt.shape
    bs, ts, _ = x_sample.shape
    xp = x_prompt.reshape(bp * tp, d)
    xs = x_sample.reshape(bs * ts, d)
    g_final = norm_final[None, :]
    pp = p_prompt.reshape(depth, bp * tp, D_PLE)
    ps = p_sample.reshape(depth, bs * ts, D_PLE)
    pos_head_rows = lambda c: c.reshape(c.shape[:2] + (c.shape[2] * HPG, ATT_HEAD_DIM))
    caches_k = [pos_head_rows(c) for c in (cache_win_k0, cache_win_k1, cache_win_k2)]
    caches_v = [pos_head_rows(c) for c in (cache_win_v0, cache_win_v1, cache_win_v2)]
    new_k = [None] * N_GROUPS
    new_v = [None] * N_GROUPS
    pk = [[] for _ in range(N_GROUPS)]
    pv = [[] for _ in range(N_GROUPS)]
    prompt_ret = sample_ret = None
    zero_state = jnp.zeros((1, bp, RET_HEADS, RET_DK, RET_DV), _F32)
    sample_chunk = math.gcd(ts, RET_CHUNK)

    for i in range(depth):
        lw = _layer_weights(norm_mix, w_in, w_ret_br, w_att_br, w_out, norm_ffn, w_up, w_down, w_ple, w_ple_gate, i)
        last = i == depth - 1

        z_mix = _norm_proj(xp, lw["norm_mix"], lw["w_in_mix"], _BF16, tm=2048, tn=1024)
        z_att = _norm_proj_att(xp, lw["norm_mix"], lw["w_in_att_grouped"]).reshape(bp, tp, 3 * ATT_W)
        o_ret, prompt_ret = _retention(z_mix.reshape(bp, tp, MIX_W), zero_state, 0, prompt_ret, i, depth,
                                       chunk=RET_CHUNK, n_chunks=4)
        ogs, lses = zip(*[_band_attention(z_att, g) for g in range(N_GROUPS)])
        for g, (window, _) in enumerate(ATT_GROUPS):
            keep = min(window, tp)
            x_tail = xp.reshape(bp, tp, d)[:, tp - keep:, :].reshape(bp * keep, d)
            kv = _norm_proj(x_tail, lw["norm_mix"], lw["w_in_kv"][g], _F32, tm=min(1024, bp * keep), tn=2 * GROUP_W)
            pk[g].append(kv[:, :GROUP_W].reshape(bp, keep, HPG, ATT_HEAD_DIM))
            pv[g].append(kv[:, GROUP_W:].reshape(bp, keep, HPG, ATT_HEAD_DIM))
        xp = _mix_ffn(xp, o_ret.reshape(bp * tp, RET_V_W), ogs, lses, z_mix, pp, i, lw, g_final,
                      tm=512, tf=1024, final_norm=last)

        z_mix = _norm_proj(xs, lw["norm_mix"], lw["w_in_mix"], _BF16, tm=bs * ts)
        z3 = _norm_proj(xs, lw["norm_mix"], lw["w_in_att"], _F32, tm=bs * ts).reshape(bs, ts, 3 * ATT_W)
        o_ret, sample_ret = _retention(z_mix.reshape(bs, ts, MIX_W), state_ret, i, sample_ret, i, depth,
                                       chunk=sample_chunk, n_chunks=ts // sample_chunk)
        ogs, lses = [], []
        for g in range(N_GROUPS):
            o_g, lse_g, new_k[g], new_v[g] = _step_attention(z3, caches_k[g], caches_v[g], new_k[g], new_v[g], i, g)
            ogs.append(o_g)
            lses.append(lse_g)
        xs = _mix_ffn(xs, o_ret.reshape(bs * ts, RET_V_W), ogs, lses, z_mix, ps, i, lw, g_final,
                      tm=bs * ts, tf=1024, final_norm=last)

    as_heads = lambda a: a.reshape(a.shape[:2] + (a.shape[2] // HPG, HPG, ATT_HEAD_DIM))
    prompt_windows = [jnp.stack(a) for g in range(N_GROUPS) for a in (pk[g], pv[g])]
    sample_windows = [as_heads(a) for g in range(N_GROUPS) for a in (new_k[g], new_v[g])]
    return (xp.reshape(bp, tp, d), xs.reshape(bs, ts, d), *prompt_windows, prompt_ret, *sample_windows, sample_ret)
```

```python
import functools
import math

import jax
import jax.numpy as jnp
import numpy as np
from jax import lax
from jax.experimental import pallas as pl
from jax.experimental.pallas import tpu as pltpu

D_MODEL = 1024
D_PLE = 256
RET_HEADS = 4
RET_DK = 128
RET_DV = 256
RET_CHUNK = 128
ATT_GROUPS = ((128, 1), (512, 4), (2048, 16))
N_GROUPS = 3
HPG = 4
ATT_HEAD_DIM = 128
ATT_HEADS = N_GROUPS * HPG
ATT_TAPS = 128
Q_BLOCK = 128
ATT_TILE = 2048
PROJ_PARTS = 4
D_FF = 4 * D_MODEL
EPS = 1e-6

RET_QK_W = RET_HEADS * RET_DK
RET_V_W = RET_HEADS * RET_DV
GROUP_W = HPG * ATT_HEAD_DIM
ATT_W = ATT_HEADS * ATT_HEAD_DIM
D_IN = 2 * RET_QK_W + 2 * RET_V_W + 3 * ATT_W + 2 * D_MODEL

MIX_W = 2 * D_MODEL + 2 * RET_QK_W + 2 * RET_V_W
COL_GATES = 0
COL_RQ = 2 * D_MODEL
COL_RK = COL_RQ + RET_QK_W
COL_RV = COL_RK + RET_QK_W
COL_RG = COL_RV + RET_V_W
ATT_Q, ATT_K, ATT_V = 0, 1, 2
COL_BLOCK = 512

LSE_LANES = 128
VMEM_LIMIT = 48 * 1024 * 1024
VMEM_LIMIT_MIX_FFN = 56 * 1024 * 1024

_BF16 = jnp.bfloat16
_F32 = jnp.float32


def _params(*sem, vmem_limit=VMEM_LIMIT):
    return pltpu.CompilerParams(dimension_semantics=sem, vmem_limit_bytes=vmem_limit)


def _rmsnorm(x, g):
    return x * lax.rsqrt(jnp.mean(x * x, axis=-1, keepdims=True) + EPS) * g


def _dot(a, b):
    return jnp.dot(a, b, preferred_element_type=_F32)


def _dot_nt(a, b):
    return lax.dot_general(a, b, (((1,), (1,)), ((), ())), preferred_element_type=_F32)


def _dot_tn(a, b):
    return lax.dot_general(a, b, (((0,), (0,)), ((), ())), preferred_element_type=_F32)


def _norm_proj_kernel(x_ref, g_ref, w_ref, z_ref, h_ref):
    @pl.when(pl.program_id(1) == 0)
    def _():
        h_ref[...] = _rmsnorm(x_ref[...], g_ref[...]).astype(_BF16)

    z_ref[...] = _dot(h_ref[...], w_ref[...]).astype(z_ref.dtype)


def _norm_proj(x, g, w, out_dtype, *, tm, tn=COL_BLOCK):
    n, d = x.shape
    d_out = w.shape[1]
    return pl.pallas_call(
        _norm_proj_kernel,
        grid=(n // tm, d_out // tn),
        in_specs=[
            pl.BlockSpec((tm, d), lambda i, j: (i, 0)),
            pl.BlockSpec((1, d), lambda i, j: (0, 0)),
            pl.BlockSpec((d, tn), lambda i, j: (0, j)),
        ],
        out_specs=pl.BlockSpec((tm, tn), lambda i, j: (i, j)),
        out_shape=jax.ShapeDtypeStruct((n, d_out), out_dtype),
        scratch_shapes=[pltpu.VMEM((tm, d), _BF16)],
        compiler_params=_params("parallel", "arbitrary"),
        name="norm_proj",
    )(x, g, w)


def _norm_proj_att_kernel(x_ref, g_ref, w_ref, z_ref, h_ref, stage_ref):
    j = pl.program_id(1)

    @pl.when(j == 0)
    def _():
        h_ref[...] = _rmsnorm(x_ref[...], g_ref[...]).astype(_BF16)

    part = ATT_TILE // PROJ_PARTS
    for group, (_, dil) in enumerate(ATT_GROUPS):
        @pl.when(j // 3 == group)
        def _(dil=dil):
            for a in range(PROJ_PARTS):
                rows = slice(a * part, (a + 1) * part)
                res = _dot(h_ref[rows, :], w_ref[...])
                if dil == 1:
                    z_ref[rows, :] = res.astype(_BF16)
                    continue
                per, sub = ATT_TILE // dil, part // dil
                for c in range(HPG):
                    stage_ref[c, rows, :] = res[:, c * ATT_HEAD_DIM:(c + 1) * ATT_HEAD_DIM]
                for r in range(dil):
                    for c in range(HPG):
                        z_ref[r * per + a * sub:r * per + (a + 1) * sub, c * ATT_HEAD_DIM:(c + 1) * ATT_HEAD_DIM] = (
                            stage_ref[c, pl.ds(a * part + r, sub, stride=dil), :].astype(_BF16))


def _norm_proj_att(x, g, w):
    n, d = x.shape
    d_out = w.shape[1]
    return pl.pallas_call(
        _norm_proj_att_kernel,
        grid=(n // ATT_TILE, d_out // GROUP_W),
        in_specs=[
            pl.BlockSpec((ATT_TILE, d), lambda i, j: (i, 0)),
            pl.BlockSpec((1, d), lambda i, j: (0, 0)),
            pl.BlockSpec((d, GROUP_W), lambda i, j: (0, j)),
        ],
        out_specs=pl.BlockSpec((ATT_TILE, GROUP_W), lambda i, j: (i, j)),
        out_shape=jax.ShapeDtypeStruct((n, d_out), _BF16),
        scratch_shapes=[pltpu.VMEM((ATT_TILE, d), _BF16), pltpu.VMEM((HPG, ATT_TILE, ATT_HEAD_DIM), _F32)],
        compiler_params=_params("parallel", "arbitrary"),
        name="norm_proj_att",
    )(x, g, w)


def _kv_window_kernel(x_ref, g_ref, wk_ref, wv_ref, ko_ref, vo_ref):
    tm = x_ref.shape[0]
    h = _rmsnorm(x_ref[...], g_ref[...]).astype(_BF16)
    for w_ref, o_ref in ((wk_ref, ko_ref), (wv_ref, vo_ref)):
        res = _dot(h, w_ref[...])
        for hd in range(HPG):
            o_ref[pl.ds(hd, tm, stride=HPG), :] = res[:, hd * ATT_HEAD_DIM:(hd + 1) * ATT_HEAD_DIM]


def _kv_window(x3, g, w_grouped, group, keep, prev, layer, depth):
    b, t, d = x3.shape
    tm = min(keep, 512)
    first = (t - keep) // tm
    wspec = lambda which: pl.BlockSpec((d, GROUP_W), lambda bi, i: (0, 3 * group + which))
    ospec = pl.BlockSpec((None, None, tm * HPG, ATT_HEAD_DIM), lambda bi, i: (layer, bi, i, 0))
    oshape = jax.ShapeDtypeStruct((depth, b, keep * HPG, ATT_HEAD_DIM), _F32)
    kernel, args, extra_specs, aliases = _kv_window_kernel, [x3, g, w_grouped, w_grouped], [], {}
    if prev is not None:
        extra_specs = [pl.BlockSpec(memory_space=pl.ANY)] * 2
        aliases = {4: 0, 5: 1}
        kernel = functools.partial(_drop_refs, kernel, 4, 2)
        args += list(prev)
    return pl.pallas_call(
        kernel,
        grid=(b, keep // tm),
        in_specs=[pl.BlockSpec((None, tm, d), lambda bi, i: (bi, first + i, 0)),
                  pl.BlockSpec((1, d), lambda bi, i: (0, 0)), wspec(ATT_K), wspec(ATT_V)] + extra_specs,
        out_specs=[ospec, ospec],
        out_shape=[oshape, oshape],
        input_output_aliases=aliases,
        compiler_params=_params("parallel", "parallel"),
        name="kv_window_g%d" % group,
    )(*args)


def _ret_log_gamma():
    return jnp.log1p(-jnp.exp(jnp.linspace(math.log(1.0 / 32), math.log(1.0 / 512), RET_HEADS))).astype(_F32)


def _retention_tables(chunk):
    lg = _ret_log_gamma()
    pos = jnp.arange(chunk, dtype=_F32)
    diff = pos[:, None] - pos[None, :]
    intra = jnp.where(diff[None] >= 0, jnp.exp(lg[:, None, None] * jnp.maximum(diff, 0.0)[None]), 0.0)
    xi = jnp.exp(lg[:, None] * (pos[None] + 1.0))
    zeta = jnp.exp(lg[:, None] * (chunk - 1.0 - pos)[None])
    decay = jnp.exp(lg * chunk)
    return (intra * (RET_DK ** -0.5),
            jnp.broadcast_to(xi[:, :, None], (RET_HEADS, chunk, RET_DK)),
            jnp.broadcast_to(zeta[:, :, None], (RET_HEADS, chunk, RET_DV)),
            jnp.broadcast_to(decay[:, None, None], (RET_HEADS, 8, RET_DV)))


def _retention_kernel(q_ref, k_ref, v_ref, g_ref, s0_ref, intra_ref, xi_ref, zeta_ref, decay_ref,
                      o_ref, sfin_ref, state, *, chunk, n_chunks):
    j = pl.program_id(1)

    @pl.when(j == 0)
    def _():
        state[...] = s0_ref[...]

    for c in range(n_chunks):
        rows = slice(c * chunk, (c + 1) * chunk)
        for h in range(RET_HEADS):
            qk_cols = slice(h * RET_DK, (h + 1) * RET_DK)
            v_cols = slice(h * RET_DV, (h + 1) * RET_DV)
            q = q_ref[rows, qk_cols]
            kb = k_ref[rows, qk_cols]
            v = v_ref[rows, v_cols]
            g = g_ref[rows, v_cols].astype(_F32)
            r_prev = state[h]
            scores = _dot_nt(q, kb) * intra_ref[h]
            o = _dot(scores.astype(_BF16), v)
            o = o + _dot((q * xi_ref[h]).astype(_BF16), r_prev.astype(_BF16))
            u = _dot_tn(kb, (v * zeta_ref[h]).astype(_BF16)) * (RET_DK ** -0.5)
            state[h] = decay_ref[h, 0:1, :] * r_prev + u
            mu = jnp.mean(o, axis=-1, keepdims=True)
            oc = o - mu
            var = jnp.mean(oc * oc, axis=-1, keepdims=True)
            on = oc * lax.rsqrt(var + EPS)
            o_ref[rows, v_cols] = (on * (g * jax.nn.sigmoid(g))).astype(o_ref.dtype)

    @pl.when(j == pl.num_programs(1) - 1)
    def _():
        sfin_ref[...] = state[...]


def _retention(z3, states0, layer0, prev_states, layer, depth, *, chunk, n_chunks):
    b, t, _ = z3.shape
    tc = chunk * n_chunks
    intra, xi, zeta, decay = _retention_tables(chunk)
    const = lambda shape: pl.BlockSpec(shape, lambda bi, j: (0,) * len(shape))
    state_block = (None, None, RET_HEADS, RET_DK, RET_DV)
    kernel = functools.partial(_retention_kernel, chunk=chunk, n_chunks=n_chunks)
    args = [z3, z3, z3, z3, states0, intra, xi, zeta, decay]
    extra_specs, aliases = [], {}
    if prev_states is not None:
        extra_specs = [pl.BlockSpec(memory_space=pl.ANY)]
        aliases = {len(args): 1}
        kernel = functools.partial(_drop_refs, kernel, len(args), 1)
        args.append(prev_states)
    return pl.pallas_call(
        kernel,
        grid=(b, t // tc),
        in_specs=[
            pl.BlockSpec((None, tc, RET_QK_W), lambda bi, j: (bi, j, COL_RQ // RET_QK_W)),
            pl.BlockSpec((None, tc, RET_QK_W), lambda bi, j: (bi, j, COL_RK // RET_QK_W)),
            pl.BlockSpec((None, tc, RET_V_W), lambda bi, j: (bi, j, COL_RV // RET_V_W)),
            pl.BlockSpec((None, tc, RET_V_W), lambda bi, j: (bi, j, COL_RG // RET_V_W)),
            pl.BlockSpec(state_block, lambda bi, j: (layer0, bi, 0, 0, 0)),
            const((RET_HEADS, chunk, chunk)),
            const((RET_HEADS, chunk, RET_DK)),
            const((RET_HEADS, chunk, RET_DV)),
            const((RET_HEADS, 8, RET_DV)),
        ] + extra_specs,
        out_specs=[
            pl.BlockSpec((None, tc, RET_V_W), lambda bi, j: (bi, j, 0)),
            pl.BlockSpec(state_block, lambda bi, j: (layer, bi, 0, 0, 0)),
        ],
        out_shape=[
            jax.ShapeDtypeStruct((b, t, RET_V_W), _BF16),
            jax.ShapeDtypeStruct((depth, b, RET_HEADS, RET_DK, RET_DV), _F32),
        ],
        scratch_shapes=[pltpu.VMEM((RET_HEADS, RET_DK, RET_DV), _F32)],
        input_output_aliases=aliases,
        compiler_params=_params("parallel", "arbitrary"),
        name="retention",
    )(*args)


def _alibi_slope(head):
    return 2.0 ** (-8.0 * (head + 1.0) / ATT_HEADS)


def _pack_head_stats(cols):
    rows = cols[0].shape[0]
    lane = lax.broadcasted_iota(jnp.int32, (rows, LSE_LANES), 1)
    out = jnp.zeros((rows, LSE_LANES), _F32)
    for h, c in enumerate(cols):
        out = jnp.where(lane == h, c, out)
    return out


def _band_attn_kernel(q_ref, k_ref, v_ref, o_ref, lse_ref, kprev, vprev, *, dil, group):
    i = pl.program_id(1)
    h = pl.program_id(2)
    blk = Q_BLOCK
    per = ATT_TILE // dil
    row = lax.broadcasted_iota(jnp.int32, (blk, 2 * blk), 0)
    col = lax.broadcasted_iota(jnp.int32, (blk, 2 * blk), 1)
    delta = row + blk - col
    slope = jnp.float32(0.0)
    for hh in range(HPG):
        slope = jnp.where(h == hh, jnp.float32(_alibi_slope(group * HPG + hh)), slope)
    in_band = (delta >= 0) & (delta <= ATT_TAPS)
    bias = jnp.where(in_band, -slope * (delta * dil).astype(_F32), -jnp.inf)
    bias_first = jnp.where(col >= jnp.where(i > 0, 0, blk), bias, -jnp.inf)
    lane = lax.broadcasted_iota(jnp.int32, (blk, LSE_LANES), 1)
    scale = ATT_HEAD_DIM ** -0.5

    @pl.when(i == 0)
    def _():
        kprev[h] = jnp.zeros((ATT_TILE, ATT_HEAD_DIM), _BF16)
        vprev[h] = jnp.zeros((ATT_TILE, ATT_HEAD_DIM), _BF16)

    for r in range(dil):
        for sb in range(per // blk):
            lo = r * per + sb * blk
            if sb == 0:
                last = slice((r + 1) * per - blk, (r + 1) * per)
                keys = jnp.concatenate([kprev[h, last, :], k_ref[lo:lo + blk, :]], axis=0)
                values = jnp.concatenate([vprev[h, last, :], v_ref[lo:lo + blk, :]], axis=0)
            else:
                keys = k_ref[lo - blk:lo + blk, :]
                values = v_ref[lo - blk:lo + blk, :]
            s = _dot_nt(q_ref[lo:lo + blk, :], keys) * scale + (bias_first if sb == 0 else bias)
            m = jnp.max(s, axis=-1, keepdims=True)
            e = jnp.exp(s - m)
            den = jnp.sum(e, axis=-1, keepdims=True)
            start = r + sb * blk * dil
            rows = pl.ds(start, blk, stride=dil) if dil > 1 else pl.ds(start, blk)
            o_ref[rows, :] = _dot((e / den).astype(_BF16), values)
            prev = jnp.where(h == 0, 0.0, lse_ref[rows, :])
            lse_ref[rows, :] = jnp.where(lane == h, m + jnp.log(den), prev)

    kprev[h] = k_ref[...]
    vprev[h] = v_ref[...]


def _band_attention(z3, group):
    b, t, _ = z3.shape
    _, dil = ATT_GROUPS[group]
    tiles = t // ATT_TILE
    spec = lambda which: pl.BlockSpec((None, ATT_TILE, ATT_HEAD_DIM),
                                      lambda bi, i, h: (bi, i, (3 * group + which) * HPG + h))
    carry = pltpu.VMEM((HPG, ATT_TILE, ATT_HEAD_DIM), _BF16)
    return pl.pallas_call(
        functools.partial(_band_attn_kernel, dil=dil, group=group),
        grid=(b, tiles, HPG),
        in_specs=[spec(ATT_Q), spec(ATT_K), spec(ATT_V)],
        out_specs=[
            pl.BlockSpec((None, ATT_TILE, ATT_HEAD_DIM), lambda bi, i, h: (h, bi * tiles + i, 0)),
            pl.BlockSpec((ATT_TILE, LSE_LANES), lambda bi, i, h: (bi * tiles + i, 0)),
        ],
        out_shape=[
            jax.ShapeDtypeStruct((HPG, b * t, ATT_HEAD_DIM), _F32),
            jax.ShapeDtypeStruct((b * t, LSE_LANES), _F32),
        ],
        scratch_shapes=[carry, carry],
        compiler_params=_params("parallel", "arbitrary", "arbitrary"),
        name="band_attention_g%d" % group,
    )(z3, z3, z3)


def _step_attn_kernel(q_ref, kn_ref, vn_ref, kc_ref, vc_ref, o_ref, lse_ref, ko_ref, vo_ref, *,
                      t_new, cache_len, dil, group):
    n_res = min(dil, t_new)
    taps = cache_len // dil
    rows_all = HPG * t_new
    head_cols = [slice(h * ATT_HEAD_DIM, (h + 1) * ATT_HEAD_DIM) for h in range(HPG)]
    head_rows = [slice(h * t_new, (h + 1) * t_new) for h in range(HPG)]
    q = [q_ref[:, c].astype(_BF16) for c in head_cols]
    k_new = kn_ref[...]
    v_new = vn_ref[...]

    def cache_taps(ref, r, h):
        return ref[pl.ds(r * HPG + h, taps, stride=HPG * dil), :].astype(_BF16)

    scale = ATT_HEAD_DIM ** -0.5
    log2_dil = dil.bit_length() - 1
    log2_new = t_new.bit_length() - 1

    def row_terms(width):
        row_id = lax.broadcasted_iota(jnp.int32, (rows_all, width), 0)
        slope = jnp.zeros((rows_all, width), _F32)
        for h in range(HPG):
            slope = jnp.where((row_id >> log2_new) == h, _alibi_slope(group * HPG + h), slope)
        return row_id & (t_new - 1), slope

    query, slope = row_terms(taps)
    residue = query & (dil - 1)
    taps_back = taps + (query >> log2_dil) - lax.broadcasted_iota(jnp.int32, (rows_all, taps), 1)
    s_cache = None
    for r in range(n_res):
        s_r = jnp.concatenate([_dot_nt(q[h], cache_taps(kc_ref, r, h)) for h in range(HPG)], axis=0)
        s_cache = s_r if s_cache is None else jnp.where(residue == r, s_r, s_cache)
    s_cache = s_cache * scale - slope * (taps_back << log2_dil).astype(_F32)
    s_cache = jnp.where(taps_back <= ATT_TAPS, s_cache, -jnp.inf)

    query_n, slope_n = row_terms(t_new)
    back = query_n - lax.broadcasted_iota(jnp.int32, (rows_all, t_new), 1)
    s_new = jnp.concatenate([_dot_nt(q[h], k_new[:, head_cols[h]].astype(_BF16)) for h in range(HPG)], axis=0)
    s_new = s_new * scale - slope_n * back.astype(_F32)
    s_new = jnp.where((back >= 0) & ((back & (dil - 1)) == 0), s_new, -jnp.inf)

    m = jnp.maximum(jnp.max(s_cache, axis=-1, keepdims=True), jnp.max(s_new, axis=-1, keepdims=True))
    e_cache = jnp.exp(s_cache - m)
    e_new = jnp.exp(s_new - m)
    den = jnp.sum(e_cache, axis=-1, keepdims=True) + jnp.sum(e_new, axis=-1, keepdims=True)
    p_cache = e_cache / den
    p_new = (e_new / den).astype(_BF16)
    lse = m + jnp.log(den)

    for h in range(HPG):
        out = _dot(p_new[head_rows[h]], v_new[:, head_cols[h]].astype(_BF16))
        for r in range(n_res):
            p_r = p_cache[head_rows[h]]
            if n_res > 1:
                p_r = jnp.where(residue[head_rows[h]] == r, p_r, 0.0)
            out = out + _dot(p_r.astype(_BF16), cache_taps(vc_ref, r, h))
        o_ref[h] = out
    lse_ref[...] = _pack_head_stats([lse[rows] for rows in head_rows])

    keep = (cache_len - t_new) * HPG
    ko_ref[0:keep, :] = kc_ref[t_new * HPG:, :]
    vo_ref[0:keep, :] = vc_ref[t_new * HPG:, :]
    for h in range(HPG):
        ko_ref[pl.ds(keep + h, t_new, stride=HPG), :] = k_new[:, head_cols[h]]
        vo_ref[pl.ds(keep + h, t_new, stride=HPG), :] = v_new[:, head_cols[h]]


def _step_attention(z3, cache_k, cache_v, prev_k, prev_v, layer, group):
    b, t_new, _ = z3.shape
    cache_len = cache_k.shape[2] // HPG
    _, dil = ATT_GROUPS[group]
    cq, ck, cv = (3 * group + which for which in (ATT_Q, ATT_K, ATT_V))
    zspec = lambda c: pl.BlockSpec((None, t_new, GROUP_W), lambda bi: (bi, 0, c))
    cache_spec = pl.BlockSpec((None, None, cache_len * HPG, ATT_HEAD_DIM), lambda bi: (layer, bi, 0, 0))
    kernel = functools.partial(_step_attn_kernel, t_new=t_new, cache_len=cache_len, dil=dil, group=group)
    in_specs = [zspec(cq), zspec(ck), zspec(cv), cache_spec, cache_spec]
    args = [z3, z3, z3, cache_k, cache_v]
    aliases = {}
    if prev_k is not None:
        in_specs += [pl.BlockSpec(memory_space=pl.ANY)] * 2
        args += [prev_k, prev_v]
        aliases = {5: 2, 6: 3}
        kernel = functools.partial(_drop_refs, kernel, 5, 2)
    cache_shape = jax.ShapeDtypeStruct(cache_k.shape, cache_k.dtype)
    return pl.pallas_call(
        kernel,
        grid=(b,),
        in_specs=in_specs,
        out_specs=[
            pl.BlockSpec((HPG, t_new, ATT_HEAD_DIM), lambda bi: (0, bi, 0)),
            pl.BlockSpec((t_new, LSE_LANES), lambda bi: (bi, 0)),
            cache_spec, cache_spec,
        ],
        out_shape=[
            jax.ShapeDtypeStruct((HPG, b * t_new, ATT_HEAD_DIM), _F32),
            jax.ShapeDtypeStruct((b * t_new, LSE_LANES), _F32),
            cache_shape, cache_shape,
        ],
        input_output_aliases=aliases,
        compiler_params=_params("parallel"),
        name="step_attention_g%d" % group,
    )(*args)


def _drop_refs(kernel, start, count, *refs):
    return kernel(*refs[:start], *refs[start + count:])


def _mix_ffn_kernel(x_ref, oret_ref, og0_ref, og1_ref, og2_ref, l0_ref, l1_ref, l2_ref, gates_ref,
                    wret_ref, watt_ref, wout_ref, g_ref, wup_ref, wdown_ref, p_ref, wple_ref, wgate_ref, gfin_ref,
                    y_ref, xmid_ref, h_ref, acc_ref, *, final_norm):
    j = pl.program_id(1)

    @pl.when(j == 0)
    def _():
        l0, l1, l2 = l0_ref[...], l1_ref[...], l2_ref[...]
        mx = jnp.maximum(jnp.maximum(l0, l1), l2)
        e0, e1, e2 = jnp.exp(l0 - mx), jnp.exp(l1 - mx), jnp.exp(l2 - mx)
        tot = e0 + e1 + e2
        w0, w1, w2 = e0 / tot, e1 / tot, e2 / tot
        parts = []
        for h in range(HPG):
            parts.append(w0[:, h:h + 1] * og0_ref[h] + w1[:, h:h + 1] * og1_ref[h] + w2[:, h:h + 1] * og2_ref[h])
        o_att = jnp.concatenate(parts, axis=-1).astype(_BF16)
        br_ret = _dot(oret_ref[...], wret_ref[...])
        br_att = _dot(o_att, watt_ref[...])
        ga = gates_ref[:, 0:D_MODEL].astype(_F32)
        gb = gates_ref[:, D_MODEL:2 * D_MODEL].astype(_F32)
        mix = jax.nn.sigmoid(ga) * br_ret + jax.nn.sigmoid(gb) * br_att
        x_mid = x_ref[...] + _dot(mix.astype(_BF16), wout_ref[...])
        xmid_ref[...] = x_mid
        h_ref[...] = _rmsnorm(x_mid, g_ref[...]).astype(_BF16)
        acc_ref[...] = jnp.zeros_like(acc_ref)

    u = jnp.maximum(_dot(h_ref[...], wup_ref[...]), 0.0)
    acc_ref[...] += _dot((u * u).astype(_BF16), wdown_ref[...])

    @pl.when(j == pl.num_programs(1) - 1)
    def _():
        x = xmid_ref[...] + acc_ref[...]
        gate = jax.nn.sigmoid(_dot(x.astype(_BF16), wgate_ref[...]))
        x = x + _dot(p_ref[...].astype(_BF16), wple_ref[...]) * gate
        if final_norm:
            x = _rmsnorm(x, gfin_ref[...])
        y_ref[...] = x


def _mix_ffn(x, o_ret, ogs, lses, z_mix, p_all, layer, lw, g_final, *, tm, tf, final_norm):
    n, d = x.shape
    row = lambda w: pl.BlockSpec((tm, w), lambda i, j: (i, 0))
    heads = pl.BlockSpec((HPG, tm, ATT_HEAD_DIM), lambda i, j: (0, i, 0))
    once = lambda a: pl.BlockSpec(a.shape, lambda i, j: (0, 0), pipeline_mode=pl.Buffered(1))
    weights = [lw[k] for k in ("w_ret_br", "w_att_br", "w_out", "norm_ffn")]
    tail = [lw["w_ple"], lw["w_ple_gate"], g_final]
    return pl.pallas_call(
        functools.partial(_mix_ffn_kernel, final_norm=final_norm),
        grid=(n // tm, D_FF // tf),
        in_specs=[row(d), row(RET_V_W), heads, heads, heads,
                  row(LSE_LANES), row(LSE_LANES), row(LSE_LANES), row(2 * D_MODEL)]
        + [once(w) for w in weights]
        + [pl.BlockSpec((d, tf), lambda i, j: (0, j)),
           pl.BlockSpec((tf, d), lambda i, j: (j, 0)),
           pl.BlockSpec((None, tm, D_PLE), lambda i, j: (layer, i, 0))]
        + [once(w) for w in tail],
        out_specs=row(d),
        out_shape=jax.ShapeDtypeStruct((n, d), _F32),
        scratch_shapes=[pltpu.VMEM((tm, d), _F32), pltpu.VMEM((tm, d), _BF16), pltpu.VMEM((tm, d), _F32)],
        compiler_params=_params("parallel", "arbitrary", vmem_limit=VMEM_LIMIT_MIX_FFN),
        name="mix_ffn",
    )(x, o_ret, *ogs, *lses, z_mix, *weights, lw["w_up"], lw["w_down"], p_all, *tail)


def _layer_weights(norm_mix, w_in, w_ret_br, w_att_br, w_out, norm_ffn, w_up, w_down, w_ple, w_ple_gate, i):
    w = w_in[i]
    att_start = 2 * RET_QK_W + 2 * RET_V_W
    w_mix = jnp.concatenate([w[:, D_IN - 2 * D_MODEL:], w[:, :att_start]], axis=1)
    return dict(
        norm_mix=norm_mix[i][None, :], w_in_mix=w_mix.astype(_BF16),
        w_in_att_grouped=jnp.concatenate(
            [w[:, att_start + which * ATT_W + g * GROUP_W:att_start + which * ATT_W + (g + 1) * GROUP_W]
             for g in range(N_GROUPS) for which in range(3)], axis=1).astype(_BF16),
        w_ret_br=w_ret_br[i].astype(_BF16), w_att_br=w_att_br[i].astype(_BF16), w_out=w_out[i].astype(_BF16),
        norm_ffn=norm_ffn[i][None, :], w_up=w_up[i].astype(_BF16), w_down=w_down[i].astype(_BF16),
        w_ple=w_ple[i].astype(_BF16), w_ple_gate=w_ple_gate[i].astype(_BF16))


def kernel(x_prompt, x_sample, cache_win_k0, cache_win_v0, cache_win_k1, cache_win_v1, cache_win_k2, cache_win_v2,
           state_ret, p_prompt, p_sample, norm_mix, w_in, w_ret_br, w_att_br, w_out, norm_ffn, w_up, w_down,
           w_ple, w_ple_gate, norm_final):
    depth = w_in.shape[0]
    bp, tp, d = x_prompt.shape
    bs, ts, _ = x_sample.shape
    xp = x_prompt.reshape(bp * tp, d)
    xs = x_sample.reshape(bs * ts, d)
    g_final = norm_final[None, :]
    pp = p_prompt.reshape(depth, bp * tp, D_PLE)
    ps = p_sample.reshape(depth, bs * ts, D_PLE)
    pos_head_rows = lambda c: c.reshape(c.shape[:2] + (c.shape[2] * HPG, ATT_HEAD_DIM))
    caches_k = [pos_head_rows(c) for c in (cache_win_k0, cache_win_k1, cache_win_k2)]
    caches_v = [pos_head_rows(c) for c in (cache_win_v0, cache_win_v1, cache_win_v2)]
    new_k = [None] * N_GROUPS
    new_v = [None] * N_GROUPS
    windows = [None] * N_GROUPS
    prompt_ret = sample_ret = None
    zero_state = jnp.zeros((1, bp, RET_HEADS, RET_DK, RET_DV), _F32)
    sample_chunk = math.gcd(ts, RET_CHUNK)

    for i in range(depth):
        lw = _layer_weights(norm_mix, w_in, w_ret_br, w_att_br, w_out, norm_ffn, w_up, w_down, w_ple, w_ple_gate, i)
        last = i == depth - 1

        z_mix = _norm_proj(xp, lw["norm_mix"], lw["w_in_mix"], _BF16, tm=2048, tn=1024)
        z_att = _norm_proj_att(xp, lw["norm_mix"], lw["w_in_att_grouped"]).reshape(bp, tp, 3 * ATT_W)
        o_ret, prompt_ret = _retention(z_mix.reshape(bp, tp, MIX_W), zero_state, 0, prompt_ret, i, depth,
                                       chunk=RET_CHUNK, n_chunks=4)
        ogs, lses = zip(*[_band_attention(z_att, g) for g in range(N_GROUPS)])
        for g, (window, _) in enumerate(ATT_GROUPS):
            windows[g] = _kv_window(xp.reshape(bp, tp, d), lw["norm_mix"], lw["w_in_att_grouped"], g, min(window, tp),
                                    windows[g], i, depth)
        xp = _mix_ffn(xp, o_ret.reshape(bp * tp, RET_V_W), ogs, lses, z_mix, pp, i, lw, g_final,
                      tm=512, tf=1024, final_norm=last)

        z_mix = _norm_proj(xs, lw["norm_mix"], lw["w_in_mix"], _BF16, tm=bs * ts)
        z3 = _norm_proj(xs, lw["norm_mix"], lw["w_in_att_grouped"], _F32, tm=bs * ts).reshape(bs, ts, 3 * ATT_W)
        o_ret, sample_ret = _retention(z_mix.reshape(bs, ts, MIX_W), state_ret, i, sample_ret, i, depth,
                                       chunk=sample_chunk, n_chunks=ts // sample_chunk)
        ogs, lses = [], []
        for g in range(N_GROUPS):
            o_g, lse_g, new_k[g], new_v[g] = _step_attention(z3, caches_k[g], caches_v[g], new_k[g], new_v[g], i, g)
            ogs.append(o_g)
            lses.append(lse_g)
        xs = _mix_ffn(xs, o_ret.reshape(bs * ts, RET_V_W), ogs, lses, z_mix, ps, i, lw, g_final,
                      tm=bs * ts, tf=1024, final_norm=last)

    as_heads = lambda a: a.reshape(a.shape[:2] + (a.shape[2] // HPG, HPG, ATT_HEAD_DIM))
    prompt_windows = [as_heads(a) for kv in windows for a in kv]
    sample_windows = [as_heads(a) for g in range(N_GROUPS) for a in (new_k[g], new_v[g])]
    return (xp.reshape(bp, tp, d), xs.reshape(bs, ts, d), *prompt_windows, prompt_ret, *sample_windows, sample_ret)
```

```python
import functools
import math

import jax
import jax.numpy as jnp
import numpy as np
from jax import lax
from jax.experimental import pallas as pl
from jax.experimental.pallas import tpu as pltpu

D_MODEL = 1024
D_PLE = 256
RET_HEADS = 4
RET_DK = 128
RET_DV = 256
RET_CHUNK = 128
ATT_GROUPS = ((128, 1), (512, 4), (2048, 16))
N_GROUPS = 3
HPG = 4
ATT_HEAD_DIM = 128
ATT_HEADS = N_GROUPS * HPG
ATT_TAPS = 128
Q_BLOCK = 128
ATT_TILE = 2048
PROJ_PARTS = 4
D_FF = 4 * D_MODEL
EPS = 1e-6

RET_QK_W = RET_HEADS * RET_DK
RET_V_W = RET_HEADS * RET_DV
GROUP_W = HPG * ATT_HEAD_DIM
ATT_W = ATT_HEADS * ATT_HEAD_DIM
D_IN = 2 * RET_QK_W + 2 * RET_V_W + 3 * ATT_W + 2 * D_MODEL

MIX_W = 2 * D_MODEL + 2 * RET_QK_W + 2 * RET_V_W
COL_GATES = 0
COL_RQ = 2 * D_MODEL
COL_RK = COL_RQ + RET_QK_W
COL_RV = COL_RK + RET_QK_W
COL_RG = COL_RV + RET_V_W
ATT_Q, ATT_K, ATT_V = 0, 1, 2
COL_BLOCK = 512

LSE_LANES = 128
VMEM_LIMIT = 48 * 1024 * 1024
VMEM_LIMIT_MIX_FFN = 56 * 1024 * 1024

_BF16 = jnp.bfloat16
_F32 = jnp.float32


def _params(*sem, vmem_limit=VMEM_LIMIT):
    return pltpu.CompilerParams(dimension_semantics=sem, vmem_limit_bytes=vmem_limit)


def _rmsnorm(x, g):
    return x * lax.rsqrt(jnp.mean(x * x, axis=-1, keepdims=True) + EPS) * g


def _dot(a, b):
    return jnp.dot(a, b, preferred_element_type=_F32)


def _dot_nt(a, b):
    return lax.dot_general(a, b, (((1,), (1,)), ((), ())), preferred_element_type=_F32)


def _dot_tn(a, b):
    return lax.dot_general(a, b, (((0,), (0,)), ((), ())), preferred_element_type=_F32)


def _norm_proj_kernel(x_ref, g_ref, w_ref, z_ref, h_ref):
    @pl.when(pl.program_id(1) == 0)
    def _():
        h_ref[...] = _rmsnorm(x_ref[...], g_ref[...]).astype(_BF16)

    z_ref[...] = _dot(h_ref[...], w_ref[...]).astype(z_ref.dtype)


def _norm_proj(x, g, w, out_dtype, *, tm, tn=COL_BLOCK):
    n, d = x.shape
    d_out = w.shape[1]
    return pl.pallas_call(
        _norm_proj_kernel,
        grid=(n // tm, d_out // tn),
        in_specs=[
            pl.BlockSpec((tm, d), lambda i, j: (i, 0)),
            pl.BlockSpec((1, d), lambda i, j: (0, 0)),
            pl.BlockSpec((d, tn), lambda i, j: (0, j)),
        ],
        out_specs=pl.BlockSpec((tm, tn), lambda i, j: (i, j)),
        out_shape=jax.ShapeDtypeStruct((n, d_out), out_dtype),
        scratch_shapes=[pltpu.VMEM((tm, d), _BF16)],
        compiler_params=_params("parallel", "arbitrary"),
        name="norm_proj",
    )(x, g, w)


def _norm_proj_att_kernel(x_ref, g_ref, w_ref, z_ref, h_ref, stage_ref):
    j = pl.program_id(1)

    @pl.when(j == 0)
    def _():
        h_ref[...] = _rmsnorm(x_ref[...], g_ref[...]).astype(_BF16)

    part = ATT_TILE // PROJ_PARTS
    for group, (_, dil) in enumerate(ATT_GROUPS):
        @pl.when(j == group)
        def _(dil=dil):
            for which in range(3):
                for a in range(PROJ_PARTS):
                    rows = slice(a * part, (a + 1) * part)
                    col0 = which * GROUP_W
                    res = _dot(h_ref[rows, :], w_ref[:, col0:col0 + GROUP_W])
                    if dil == 1:
                        z_ref[rows, col0:col0 + GROUP_W] = res.astype(_BF16)
                        continue
                    per, sub = ATT_TILE // dil, part // dil
                    for c in range(HPG):
                        stage_ref[c, rows, :] = res[:, c * ATT_HEAD_DIM:(c + 1) * ATT_HEAD_DIM]
                    for r in range(dil):
                        for c in range(HPG):
                            cols = slice(col0 + c * ATT_HEAD_DIM, col0 + (c + 1) * ATT_HEAD_DIM)
                            z_ref[r * per + a * sub:r * per + (a + 1) * sub, cols] = (
                                stage_ref[c, pl.ds(a * part + r, sub, stride=dil), :].astype(_BF16))


def _norm_proj_att(x, g, w):
    n, d = x.shape
    d_out = w.shape[1]
    return pl.pallas_call(
        _norm_proj_att_kernel,
        grid=(n // ATT_TILE, d_out // (3 * GROUP_W)),
        in_specs=[
            pl.BlockSpec((ATT_TILE, d), lambda i, j: (i, 0)),
            pl.BlockSpec((1, d), lambda i, j: (0, 0)),
            pl.BlockSpec((d, 3 * GROUP_W), lambda i, j: (0, j)),
        ],
        out_specs=pl.BlockSpec((ATT_TILE, 3 * GROUP_W), lambda i, j: (i, j)),
        out_shape=jax.ShapeDtypeStruct((n, d_out), _BF16),
        scratch_shapes=[pltpu.VMEM((ATT_TILE, d), _BF16), pltpu.VMEM((HPG, ATT_TILE, ATT_HEAD_DIM), _F32)],
        compiler_params=_params("parallel", "arbitrary"),
        name="norm_proj_att",
    )(x, g, w)


def _kv_window_kernel(x_ref, g_ref, wk_ref, wv_ref, ko_ref, vo_ref):
    tm = x_ref.shape[0]
    h = _rmsnorm(x_ref[...], g_ref[...]).astype(_BF16)
    for w_ref, o_ref in ((wk_ref, ko_ref), (wv_ref, vo_ref)):
        res = _dot(h, w_ref[...])
        for hd in range(HPG):
            o_ref[pl.ds(hd, tm, stride=HPG), :] = res[:, hd * ATT_HEAD_DIM:(hd + 1) * ATT_HEAD_DIM]


def _kv_window(x3, g, w_grouped, group, keep, prev, layer, depth):
    b, t, d = x3.shape
    tm = min(keep, 512)
    first = (t - keep) // tm
    wspec = lambda which: pl.BlockSpec((d, GROUP_W), lambda bi, i: (0, 3 * group + which))
    ospec = pl.BlockSpec((None, None, tm * HPG, ATT_HEAD_DIM), lambda bi, i: (layer, bi, i, 0))
    oshape = jax.ShapeDtypeStruct((depth, b, keep * HPG, ATT_HEAD_DIM), _F32)
    kernel, args, extra_specs, aliases = _kv_window_kernel, [x3, g, w_grouped, w_grouped], [], {}
    if prev is not None:
        extra_specs = [pl.BlockSpec(memory_space=pl.ANY)] * 2
        aliases = {4: 0, 5: 1}
        kernel = functools.partial(_drop_refs, kernel, 4, 2)
        args += list(prev)
    return pl.pallas_call(
        kernel,
        grid=(b, keep // tm),
        in_specs=[pl.BlockSpec((None, tm, d), lambda bi, i: (bi, first + i, 0)),
                  pl.BlockSpec((1, d), lambda bi, i: (0, 0)), wspec(ATT_K), wspec(ATT_V)] + extra_specs,
        out_specs=[ospec, ospec],
        out_shape=[oshape, oshape],
        input_output_aliases=aliases,
        compiler_params=_params("parallel", "parallel"),
        name="kv_window_g%d" % group,
    )(*args)


def _ret_log_gamma():
    return jnp.log1p(-jnp.exp(jnp.linspace(math.log(1.0 / 32), math.log(1.0 / 512), RET_HEADS))).astype(_F32)


def _retention_tables(chunk):
    lg = _ret_log_gamma()
    pos = jnp.arange(chunk, dtype=_F32)
    diff = pos[:, None] - pos[None, :]
    intra = jnp.where(diff[None] >= 0, jnp.exp(lg[:, None, None] * jnp.maximum(diff, 0.0)[None]), 0.0)
    xi = jnp.exp(lg[:, None] * (pos[None] + 1.0))
    zeta = jnp.exp(lg[:, None] * (chunk - 1.0 - pos)[None])
    decay = jnp.exp(lg * chunk)
    return (intra * (RET_DK ** -0.5),
            jnp.broadcast_to(xi[:, :, None], (RET_HEADS, chunk, RET_DK)),
            jnp.broadcast_to(zeta[:, :, None], (RET_HEADS, chunk, RET_DV)),
            jnp.broadcast_to(decay[:, None, None], (RET_HEADS, 8, RET_DV)))


def _retention_kernel(q_ref, k_ref, v_ref, g_ref, s0_ref, intra_ref, xi_ref, zeta_ref, decay_ref,
                      o_ref, sfin_ref, state, *, chunk, n_chunks):
    j = pl.program_id(1)

    @pl.when(j == 0)
    def _():
        state[...] = s0_ref[...]

    for c in range(n_chunks):
        rows = slice(c * chunk, (c + 1) * chunk)
        for h in range(RET_HEADS):
            qk_cols = slice(h * RET_DK, (h + 1) * RET_DK)
            v_cols = slice(h * RET_DV, (h + 1) * RET_DV)
            q = q_ref[rows, qk_cols]
            kb = k_ref[rows, qk_cols]
            v = v_ref[rows, v_cols]
            g = g_ref[rows, v_cols].astype(_F32)
            r_prev = state[h]
            scores = _dot_nt(q, kb) * intra_ref[h]
            o = _dot(scores.astype(_BF16), v)
            o = o + _dot((q * xi_ref[h]).astype(_BF16), r_prev.astype(_BF16))
            u = _dot_tn(kb, (v * zeta_ref[h]).astype(_BF16)) * (RET_DK ** -0.5)
            state[h] = decay_ref[h, 0:1, :] * r_prev + u
            mu = jnp.mean(o, axis=-1, keepdims=True)
            oc = o - mu
            var = jnp.mean(oc * oc, axis=-1, keepdims=True)
            on = oc * lax.rsqrt(var + EPS)
            o_ref[rows, v_cols] = (on * (g * jax.nn.sigmoid(g))).astype(o_ref.dtype)

    @pl.when(j == pl.num_programs(1) - 1)
    def _():
        sfin_ref[...] = state[...]


def _retention(z3, states0, layer0, prev_states, layer, depth, *, chunk, n_chunks):
    b, t, _ = z3.shape
    tc = chunk * n_chunks
    intra, xi, zeta, decay = _retention_tables(chunk)
    const = lambda shape: pl.BlockSpec(shape, lambda bi, j: (0,) * len(shape))
    state_block = (None, None, RET_HEADS, RET_DK, RET_DV)
    kernel = functools.partial(_retention_kernel, chunk=chunk, n_chunks=n_chunks)
    args = [z3, z3, z3, z3, states0, intra, xi, zeta, decay]
    extra_specs, aliases = [], {}
    if prev_states is not None:
        extra_specs = [pl.BlockSpec(memory_space=pl.ANY)]
        aliases = {len(args): 1}
        kernel = functools.partial(_drop_refs, kernel, len(args), 1)
        args.append(prev_states)
    return pl.pallas_call(
        kernel,
        grid=(b, t // tc),
        in_specs=[
            pl.BlockSpec((None, tc, RET_QK_W), lambda bi, j: (bi, j, COL_RQ // RET_QK_W)),
            pl.BlockSpec((None, tc, RET_QK_W), lambda bi, j: (bi, j, COL_RK // RET_QK_W)),
            pl.BlockSpec((None, tc, RET_V_W), lambda bi, j: (bi, j, COL_RV // RET_V_W)),
            pl.BlockSpec((None, tc, RET_V_W), lambda bi, j: (bi, j, COL_RG // RET_V_W)),
            pl.BlockSpec(state_block, lambda bi, j: (layer0, bi, 0, 0, 0)),
            const((RET_HEADS, chunk, chunk)),
            const((RET_HEADS, chunk, RET_DK)),
            const((RET_HEADS, chunk, RET_DV)),
            const((RET_HEADS, 8, RET_DV)),
        ] + extra_specs,
        out_specs=[
            pl.BlockSpec((None, tc, RET_V_W), lambda bi, j: (bi, j, 0)),
            pl.BlockSpec(state_block, lambda bi, j: (layer, bi, 0, 0, 0)),
        ],
        out_shape=[
            jax.ShapeDtypeStruct((b, t, RET_V_W), _BF16),
            jax.ShapeDtypeStruct((depth, b, RET_HEADS, RET_DK, RET_DV), _F32),
        ],
        scratch_shapes=[pltpu.VMEM((RET_HEADS, RET_DK, RET_DV), _F32)],
        input_output_aliases=aliases,
        compiler_params=_params("parallel", "arbitrary"),
        name="retention",
    )(*args)


def _alibi_slope(head):
    return 2.0 ** (-8.0 * (head + 1.0) / ATT_HEADS)


def _pack_head_stats(cols):
    rows = cols[0].shape[0]
    lane = lax.broadcasted_iota(jnp.int32, (rows, LSE_LANES), 1)
    out = jnp.zeros((rows, LSE_LANES), _F32)
    for h, c in enumerate(cols):
        out = jnp.where(lane == h, c, out)
    return out


def _band_attn_kernel(q_ref, k_ref, v_ref, o_ref, lse_ref, kprev, vprev, *, dil, group):
    i = pl.program_id(1)
    h = pl.program_id(2)
    blk = Q_BLOCK
    per = ATT_TILE // dil
    row = lax.broadcasted_iota(jnp.int32, (blk, 2 * blk), 0)
    col = lax.broadcasted_iota(jnp.int32, (blk, 2 * blk), 1)
    delta = row + blk - col
    slope = jnp.float32(0.0)
    for hh in range(HPG):
        slope = jnp.where(h == hh, jnp.float32(_alibi_slope(group * HPG + hh)), slope)
    in_band = (delta >= 0) & (delta <= ATT_TAPS)
    bias = jnp.where(in_band, -slope * (delta * dil).astype(_F32), -jnp.inf)
    bias_first = jnp.where(col >= jnp.where(i > 0, 0, blk), bias, -jnp.inf)
    lane = lax.broadcasted_iota(jnp.int32, (blk, LSE_LANES), 1)
    scale = ATT_HEAD_DIM ** -0.5

    @pl.when(i == 0)
    def _():
        kprev[h] = jnp.zeros((ATT_TILE, ATT_HEAD_DIM), _BF16)
        vprev[h] = jnp.zeros((ATT_TILE, ATT_HEAD_DIM), _BF16)

    for r in range(dil):
        for sb in range(per // blk):
            lo = r * per + sb * blk
            if sb == 0:
                last = slice((r + 1) * per - blk, (r + 1) * per)
                keys = jnp.concatenate([kprev[h, last, :], k_ref[lo:lo + blk, :]], axis=0)
                values = jnp.concatenate([vprev[h, last, :], v_ref[lo:lo + blk, :]], axis=0)
            else:
                keys = k_ref[lo - blk:lo + blk, :]
                values = v_ref[lo - blk:lo + blk, :]
            s = _dot_nt(q_ref[lo:lo + blk, :], keys) * scale + (bias_first if sb == 0 else bias)
            m = jnp.max(s, axis=-1, keepdims=True)
            e = jnp.exp(s - m)
            den = jnp.sum(e, axis=-1, keepdims=True)
            start = r + sb * blk * dil
            rows = pl.ds(start, blk, stride=dil) if dil > 1 else pl.ds(start, blk)
            o_ref[rows, :] = _dot((e / den).astype(_BF16), values)
            prev = jnp.where(h == 0, 0.0, lse_ref[rows, :])
            lse_ref[rows, :] = jnp.where(lane == h, m + jnp.log(den), prev)

    kprev[h] = k_ref[...]
    vprev[h] = v_ref[...]


def _band_attention(z3, group):
    b, t, _ = z3.shape
    _, dil = ATT_GROUPS[group]
    tiles = t // ATT_TILE
    spec = lambda which: pl.BlockSpec((None, ATT_TILE, ATT_HEAD_DIM),
                                      lambda bi, i, h: (bi, i, (3 * group + which) * HPG + h))
    carry = pltpu.VMEM((HPG, ATT_TILE, ATT_HEAD_DIM), _BF16)
    return pl.pallas_call(
        functools.partial(_band_attn_kernel, dil=dil, group=group),
        grid=(b, tiles, HPG),
        in_specs=[spec(ATT_Q), spec(ATT_K), spec(ATT_V)],
        out_specs=[
            pl.BlockSpec((None, ATT_TILE, ATT_HEAD_DIM), lambda bi, i, h: (h, bi * tiles + i, 0)),
            pl.BlockSpec((ATT_TILE, LSE_LANES), lambda bi, i, h: (bi * tiles + i, 0)),
        ],
        out_shape=[
            jax.ShapeDtypeStruct((HPG, b * t, ATT_HEAD_DIM), _F32),
            jax.ShapeDtypeStruct((b * t, LSE_LANES), _F32),
        ],
        scratch_shapes=[carry, carry],
        compiler_params=_params("parallel", "arbitrary", "arbitrary"),
        name="band_attention_g%d" % group,
    )(z3, z3, z3)


def _step_attn_kernel(q_ref, kn_ref, vn_ref, kc_ref, vc_ref, o_ref, lse_ref, ko_ref, vo_ref, *,
                      t_new, cache_len, dil, group):
    n_res = min(dil, t_new)
    taps = cache_len // dil
    rows_all = HPG * t_new
    head_cols = [slice(h * ATT_HEAD_DIM, (h + 1) * ATT_HEAD_DIM) for h in range(HPG)]
    head_rows = [slice(h * t_new, (h + 1) * t_new) for h in range(HPG)]
    q = [q_ref[:, c].astype(_BF16) for c in head_cols]
    k_new = kn_ref[...]
    v_new = vn_ref[...]

    def cache_taps(ref, r, h):
        return ref[pl.ds(r * HPG + h, taps, stride=HPG * dil), :].astype(_BF16)

    scale = ATT_HEAD_DIM ** -0.5
    log2_dil = dil.bit_length() - 1
    log2_new = t_new.bit_length() - 1

    def row_terms(width):
        row_id = lax.broadcasted_iota(jnp.int32, (rows_all, width), 0)
        slope = jnp.zeros((rows_all, width), _F32)
        for h in range(HPG):
            slope = jnp.where((row_id >> log2_new) == h, _alibi_slope(group * HPG + h), slope)
        return row_id & (t_new - 1), slope

    query, slope = row_terms(taps)
    residue = query & (dil - 1)
    taps_back = taps + (query >> log2_dil) - lax.broadcasted_iota(jnp.int32, (rows_all, taps), 1)
    s_cache = None
    for r in range(n_res):
        s_r = jnp.concatenate([_dot_nt(q[h], cache_taps(kc_ref, r, h)) for h in range(HPG)], axis=0)
        s_cache = s_r if s_cache is None else jnp.where(residue == r, s_r, s_cache)
    s_cache = s_cache * scale - slope * (taps_back << log2_dil).astype(_F32)
    s_cache = jnp.where(taps_back <= ATT_TAPS, s_cache, -jnp.inf)

    query_n, slope_n = row_terms(t_new)
    back = query_n - lax.broadcasted_iota(jnp.int32, (rows_all, t_new), 1)
    s_new = jnp.concatenate([_dot_nt(q[h], k_new[:, head_cols[h]].astype(_BF16)) for h in range(HPG)], axis=0)
    s_new = s_new * scale - slope_n * back.astype(_F32)
    s_new = jnp.where((back >= 0) & ((back & (dil - 1)) == 0), s_new, -jnp.inf)

    m = jnp.maximum(jnp.max(s_cache, axis=-1, keepdims=True), jnp.max(s_new, axis=-1, keepdims=True))
    e_cache = jnp.exp(s_cache - m)
    e_new = jnp.exp(s_new - m)
    den = jnp.sum(e_cache, axis=-1, keepdims=True) + jnp.sum(e_new, axis=-1, keepdims=True)
    p_cache = e_cache / den
    p_new = (e_new / den).astype(_BF16)
    lse = m + jnp.log(den)

    for h in range(HPG):
        out = _dot(p_new[head_rows[h]], v_new[:, head_cols[h]].astype(_BF16))
        for r in range(n_res):
            p_r = p_cache[head_rows[h]]
            if n_res > 1:
                p_r = jnp.where(residue[head_rows[h]] == r, p_r, 0.0)
            out = out + _dot(p_r.astype(_BF16), cache_taps(vc_ref, r, h))
        o_ref[h] = out
    lse_ref[...] = _pack_head_stats([lse[rows] for rows in head_rows])

    keep = (cache_len - t_new) * HPG
    ko_ref[0:keep, :] = kc_ref[t_new * HPG:, :]
    vo_ref[0:keep, :] = vc_ref[t_new * HPG:, :]
    for h in range(HPG):
        ko_ref[pl.ds(keep + h, t_new, stride=HPG), :] = k_new[:, head_cols[h]]
        vo_ref[pl.ds(keep + h, t_new, stride=HPG), :] = v_new[:, head_cols[h]]


def _step_attention(z3, cache_k, cache_v, prev_k, prev_v, layer, group):
    b, t_new, _ = z3.shape
    cache_len = cache_k.shape[2] // HPG
    _, dil = ATT_GROUPS[group]
    cq, ck, cv = (3 * group + which for which in (ATT_Q, ATT_K, ATT_V))
    zspec = lambda c: pl.BlockSpec((None, t_new, GROUP_W), lambda bi: (bi, 0, c))
    cache_spec = pl.BlockSpec((None, None, cache_len * HPG, ATT_HEAD_DIM), lambda bi: (layer, bi, 0, 0))
    kernel = functools.partial(_step_attn_kernel, t_new=t_new, cache_len=cache_len, dil=dil, group=group)
    in_specs = [zspec(cq), zspec(ck), zspec(cv), cache_spec, cache_spec]
    args = [z3, z3, z3, cache_k, cache_v]
    aliases = {}
    if prev_k is not None:
        in_specs += [pl.BlockSpec(memory_space=pl.ANY)] * 2
        args += [prev_k, prev_v]
        aliases = {5: 2, 6: 3}
        kernel = functools.partial(_drop_refs, kernel, 5, 2)
    cache_shape = jax.ShapeDtypeStruct(cache_k.shape, cache_k.dtype)
    return pl.pallas_call(
        kernel,
        grid=(b,),
        in_specs=in_specs,
        out_specs=[
            pl.BlockSpec((HPG, t_new, ATT_HEAD_DIM), lambda bi: (0, bi, 0)),
            pl.BlockSpec((t_new, LSE_LANES), lambda bi: (bi, 0)),
            cache_spec, cache_spec,
        ],
        out_shape=[
            jax.ShapeDtypeStruct((HPG, b * t_new, ATT_HEAD_DIM), _F32),
            jax.ShapeDtypeStruct((b * t_new, LSE_LANES), _F32),
            cache_shape, cache_shape,
        ],
        input_output_aliases=aliases,
        compiler_params=_params("parallel"),
        name="step_attention_g%d" % group,
    )(*args)


def _drop_refs(kernel, start, count, *refs):
    return kernel(*refs[:start], *refs[start + count:])


def _mix_ffn_kernel(x_ref, oret_ref, og0_ref, og1_ref, og2_ref, l0_ref, l1_ref, l2_ref, gates_ref,
                    wret_ref, watt_ref, wout_ref, g_ref, wup_ref, wdown_ref, p_ref, wple_ref, wgate_ref, gfin_ref,
                    y_ref, xmid_ref, h_ref, acc_ref, *, final_norm):
    j = pl.program_id(1)

    @pl.when(j == 0)
    def _():
        l0, l1, l2 = l0_ref[...], l1_ref[...], l2_ref[...]
        mx = jnp.maximum(jnp.maximum(l0, l1), l2)
        e0, e1, e2 = jnp.exp(l0 - mx), jnp.exp(l1 - mx), jnp.exp(l2 - mx)
        tot = e0 + e1 + e2
        w0, w1, w2 = e0 / tot, e1 / tot, e2 / tot
        parts = []
        for h in range(HPG):
            parts.append(w0[:, h:h + 1] * og0_ref[h] + w1[:, h:h + 1] * og1_ref[h] + w2[:, h:h + 1] * og2_ref[h])
        o_att = jnp.concatenate(parts, axis=-1).astype(_BF16)
        br_ret = _dot(oret_ref[...], wret_ref[...])
        br_att = _dot(o_att, watt_ref[...])
        ga = gates_ref[:, 0:D_MODEL].astype(_F32)
        gb = gates_ref[:, D_MODEL:2 * D_MODEL].astype(_F32)
        mix = jax.nn.sigmoid(ga) * br_ret + jax.nn.sigmoid(gb) * br_att
        x_mid = x_ref[...] + _dot(mix.astype(_BF16), wout_ref[...])
        xmid_ref[...] = x_mid
        h_ref[...] = _rmsnorm(x_mid, g_ref[...]).astype(_BF16)
        acc_ref[...] = jnp.zeros_like(acc_ref)

    u = jnp.maximum(_dot(h_ref[...], wup_ref[...]), 0.0)
    acc_ref[...] += _dot((u * u).astype(_BF16), wdown_ref[...])

    @pl.when(j == pl.num_programs(1) - 1)
    def _():
        x = xmid_ref[...] + acc_ref[...]
        gate = jax.nn.sigmoid(_dot(x.astype(_BF16), wgate_ref[...]))
        x = x + _dot(p_ref[...].astype(_BF16), wple_ref[...]) * gate
        if final_norm:
            x = _rmsnorm(x, gfin_ref[...])
        y_ref[...] = x


def _mix_ffn(x, o_ret, ogs, lses, z_mix, p_all, layer, lw, g_final, *, tm, tf, final_norm):
    n, d = x.shape
    row = lambda w: pl.BlockSpec((tm, w), lambda i, j: (i, 0))
    heads = pl.BlockSpec((HPG, tm, ATT_HEAD_DIM), lambda i, j: (0, i, 0))
    once = lambda a: pl.BlockSpec(a.shape, lambda i, j: (0, 0), pipeline_mode=pl.Buffered(1))
    weights = [lw[k] for k in ("w_ret_br", "w_att_br", "w_out", "norm_ffn")]
    tail = [lw["w_ple"], lw["w_ple_gate"], g_final]
    return pl.pallas_call(
        functools.partial(_mix_ffn_kernel, final_norm=final_norm),
        grid=(n // tm, D_FF // tf),
        in_specs=[row(d), row(RET_V_W), heads, heads, heads,
                  row(LSE_LANES), row(LSE_LANES), row(LSE_LANES), row(2 * D_MODEL)]
        + [once(w) for w in weights]
        + [pl.BlockSpec((d, tf), lambda i, j: (0, j)),
           pl.BlockSpec((tf, d), lambda i, j: (j, 0)),
           pl.BlockSpec((None, tm, D_PLE), lambda i, j: (layer, i, 0))]
        + [once(w) for w in tail],
        out_specs=row(d),
        out_shape=jax.ShapeDtypeStruct((n, d), _F32),
        scratch_shapes=[pltpu.VMEM((tm, d), _F32), pltpu.VMEM((tm, d), _BF16), pltpu.VMEM((tm, d), _F32)],
        compiler_params=_params("parallel", "arbitrary", vmem_limit=VMEM_LIMIT_MIX_FFN),
        name="mix_ffn",
    )(x, o_ret, *ogs, *lses, z_mix, *weights, lw["w_up"], lw["w_down"], p_all, *tail)


def _layer_weights(norm_mix, w_in, w_ret_br, w_att_br, w_out, norm_ffn, w_up, w_down, w_ple, w_ple_gate, i):
    w = w_in[i]
    att_start = 2 * RET_QK_W + 2 * RET_V_W
    w_mix = jnp.concatenate([w[:, D_IN - 2 * D_MODEL:], w[:, :att_start]], axis=1)
    return dict(
        norm_mix=norm_mix[i][None, :], w_in_mix=w_mix.astype(_BF16),
        w_in_att_grouped=jnp.concatenate(
            [w[:, att_start + which * ATT_W + g * GROUP_W:att_start + which * ATT_W + (g + 1) * GROUP_W]
             for g in range(N_GROUPS) for which in range(3)], axis=1).astype(_BF16),
        w_ret_br=w_ret_br[i].astype(_BF16), w_att_br=w_att_br[i].astype(_BF16), w_out=w_out[i].astype(_BF16),
        norm_ffn=norm_ffn[i][None, :], w_up=w_up[i].astype(_BF16), w_down=w_down[i].astype(_BF16),
        w_ple=w_ple[i].astype(_BF16), w_ple_gate=w_ple_gate[i].astype(_BF16))


def kernel(x_prompt, x_sample, cache_win_k0, cache_win_v0, cache_win_k1, cache_win_v1, cache_win_k2, cache_win_v2,
           state_ret, p_prompt, p_sample, norm_mix, w_in, w_ret_br, w_att_br, w_out, norm_ffn, w_up, w_down,
           w_ple, w_ple_gate, norm_final):
    depth = w_in.shape[0]
    bp, tp, d = x_prompt.shape
    bs, ts, _ = x_sample.shape
    xp = x_prompt.reshape(bp * tp, d)
    xs = x_sample.reshape(bs * ts, d)
    g_final = norm_final[None, :]
    pp = p_prompt.reshape(depth, bp * tp, D_PLE)
    ps = p_sample.reshape(depth, bs * ts, D_PLE)
    pos_head_rows = lambda c: c.reshape(c.shape[:2] + (c.shape[2] * HPG, ATT_HEAD_DIM))
    caches_k = [pos_head_rows(c) for c in (cache_win_k0, cache_win_k1, cache_win_k2)]
    caches_v = [pos_head_rows(c) for c in (cache_win_v0, cache_win_v1, cache_win_v2)]
    new_k = [None] * N_GROUPS
    new_v = [None] * N_GROUPS
    windows = [None] * N_GROUPS
    prompt_ret = sample_ret = None
    zero_state = jnp.zeros((1, bp, RET_HEADS, RET_DK, RET_DV), _F32)
    sample_chunk = math.gcd(ts, RET_CHUNK)

    for i in range(depth):
        lw = _layer_weights(norm_mix, w_in, w_ret_br, w_att_br, w_out, norm_ffn, w_up, w_down, w_ple, w_ple_gate, i)
        last = i == depth - 1

        z_mix = _norm_proj(xp, lw["norm_mix"], lw["w_in_mix"], _BF16, tm=2048, tn=1024)
        z_att = _norm_proj_att(xp, lw["norm_mix"], lw["w_in_att_grouped"]).reshape(bp, tp, 3 * ATT_W)
        o_ret, prompt_ret = _retention(z_mix.reshape(bp, tp, MIX_W), zero_state, 0, prompt_ret, i, depth,
                                       chunk=RET_CHUNK, n_chunks=4)
        ogs, lses = zip(*[_band_attention(z_att, g) for g in range(N_GROUPS)])
        for g, (window, _) in enumerate(ATT_GROUPS):
            windows[g] = _kv_window(xp.reshape(bp, tp, d), lw["norm_mix"], lw["w_in_att_grouped"], g, min(window, tp),
                                    windows[g], i, depth)
        xp = _mix_ffn(xp, o_ret.reshape(bp * tp, RET_V_W), ogs, lses, z_mix, pp, i, lw, g_final,
                      tm=512, tf=1024, final_norm=last)

        z_mix = _norm_proj(xs, lw["norm_mix"], lw["w_in_mix"], _BF16, tm=bs * ts)
        z3 = _norm_proj(xs, lw["norm_mix"], lw["w_in_att_grouped"], _F32, tm=bs * ts).reshape(bs, ts, 3 * ATT_W)
        o_ret, sample_ret = _retention(z_mix.reshape(bs, ts, MIX_W), state_ret, i, sample_ret, i, depth,
                                       chunk=sample_chunk, n_chunks=ts // sample_chunk)
        ogs, lses = [], []
        for g in range(N_GROUPS):
            o_g, lse_g, new_k[g], new_v[g] = _step_attention(z3, caches_k[g], caches_v[g], new_k[g], new_v[g], i, g)
            ogs.append(o_g)
            lses.append(lse_g)
        xs = _mix_ffn(xs, o_ret.reshape(bs * ts, RET_V_W), ogs, lses, z_mix, ps, i, lw, g_final,
                      tm=bs * ts, tf=1024, final_norm=last)

    as_heads = lambda a: a.reshape(a.shape[:2] + (a.shape[2] // HPG, HPG, ATT_HEAD_DIM))
    prompt_windows = [as_heads(a) for kv in windows for a in kv]
    sample_windows = [as_heads(a) for g in range(N_GROUPS) for a in (new_k[g], new_v[g])]
    return (xp.reshape(bp, tp, d), xs.reshape(bs, ts, d), *prompt_windows, prompt_ret, *sample_windows, sample_ret)
```

```python
import functools
import math

import jax
import jax.numpy as jnp
import numpy as np
from jax import lax
from jax.experimental import pallas as pl
from jax.experimental.pallas import tpu as pltpu

D_MODEL = 1024
D_PLE = 256
RET_HEADS = 4
RET_DK = 128
RET_DV = 256
RET_CHUNK = 128
ATT_GROUPS = ((128, 1), (512, 4), (2048, 16))
N_GROUPS = 3
HPG = 4
ATT_HEAD_DIM = 128
ATT_HEADS = N_GROUPS * HPG
ATT_TAPS = 128
Q_BLOCK = 128
ATT_TILE = 2048
PROJ_PARTS = 4
NORM_PARTS = 4
SPLIT_STRIDE = 4
D_FF = 4 * D_MODEL
EPS = 1e-6

RET_QK_W = RET_HEADS * RET_DK
RET_V_W = RET_HEADS * RET_DV
GROUP_W = HPG * ATT_HEAD_DIM
ATT_W = ATT_HEADS * ATT_HEAD_DIM
D_IN = 2 * RET_QK_W + 2 * RET_V_W + 3 * ATT_W + 2 * D_MODEL

MIX_W = 2 * D_MODEL + 2 * RET_QK_W + 2 * RET_V_W
COL_GATES = 0
COL_RQ = 2 * D_MODEL
COL_RK = COL_RQ + RET_QK_W
COL_RV = COL_RK + RET_QK_W
COL_RG = COL_RV + RET_V_W
ATT_Q, ATT_K, ATT_V = 0, 1, 2
COL_BLOCK = 512

LSE_LANES = 128
VMEM_LIMIT = 48 * 1024 * 1024
VMEM_LIMIT_MIX_FFN = 56 * 1024 * 1024

_BF16 = jnp.bfloat16
_F32 = jnp.float32


def _params(*sem, vmem_limit=VMEM_LIMIT):
    return pltpu.CompilerParams(dimension_semantics=sem, vmem_limit_bytes=vmem_limit)


def _rmsnorm(x, g):
    return x * lax.rsqrt(jnp.mean(x * x, axis=-1, keepdims=True) + EPS) * g


def _dot(a, b):
    return jnp.dot(a, b, preferred_element_type=_F32)


def _dot_nt(a, b):
    return lax.dot_general(a, b, (((1,), (1,)), ((), ())), preferred_element_type=_F32)


def _dot_tn(a, b):
    return lax.dot_general(a, b, (((0,), (0,)), ((), ())), preferred_element_type=_F32)


def _norm_proj_kernel(x_ref, g_ref, w_ref, z_ref, h_ref):
    first = pl.program_id(1) == 0

    @pl.when(first)
    def _():
        tm = x_ref.shape[0]
        part = tm // NORM_PARTS if tm % (NORM_PARTS * 16) == 0 else tm
        for a in range(tm // part):
            rows = slice(a * part, (a + 1) * part)
            h = _rmsnorm(x_ref[rows, :], g_ref[...]).astype(_BF16)
            h_ref[rows, :] = h
            z_ref[rows, :] = _dot(h, w_ref[...]).astype(z_ref.dtype)

    @pl.when(jnp.logical_not(first))
    def _():
        z_ref[...] = _dot(h_ref[...], w_ref[...]).astype(z_ref.dtype)


def _norm_proj(x, g, w, out_dtype, *, tm, tn=COL_BLOCK):
    n, d = x.shape
    d_out = w.shape[1]
    return pl.pallas_call(
        _norm_proj_kernel,
        grid=(n // tm, d_out // tn),
        in_specs=[
            pl.BlockSpec((tm, d), lambda i, j: (i, 0)),
            pl.BlockSpec((1, d), lambda i, j: (0, 0)),
            pl.BlockSpec((d, tn), lambda i, j: (0, j)),
        ],
        out_specs=pl.BlockSpec((tm, tn), lambda i, j: (i, j)),
        out_shape=jax.ShapeDtypeStruct((n, d_out), out_dtype),
        scratch_shapes=[pltpu.VMEM((tm, d), _BF16)],
        compiler_params=_params("parallel", "arbitrary"),
        name="norm_proj",
    )(x, g, w)


def _norm_proj_att_kernel(x_ref, g_ref, w_ref, z_ref, h_ref, stage_ref, stage2_ref):
    j = pl.program_id(1)

    @pl.when(j == 0)
    def _():
        h_ref[...] = _rmsnorm(x_ref[...], g_ref[...]).astype(_BF16)

    part = ATT_TILE // PROJ_PARTS
    for group, (_, dil) in enumerate(ATT_GROUPS):
        @pl.when(j == group)
        def _(dil=dil):
            for which in range(3):
                for a in range(PROJ_PARTS):
                    rows = slice(a * part, (a + 1) * part)
                    col0 = which * GROUP_W
                    res = _dot(h_ref[rows, :], w_ref[:, col0:col0 + GROUP_W])
                    if dil == 1:
                        z_ref[rows, col0:col0 + GROUP_W] = res.astype(_BF16)
                        continue
                    per, sub = ATT_TILE // dil, part // dil
                    for c in range(HPG):
                        stage_ref[c, rows, :] = res[:, c * ATT_HEAD_DIM:(c + 1) * ATT_HEAD_DIM]
                    if dil > SPLIT_STRIDE:
                        seg = part // SPLIT_STRIDE
                        for c in range(HPG):
                            for lo in range(SPLIT_STRIDE):
                                stage2_ref[c, lo * seg:(lo + 1) * seg, :] = (
                                    stage_ref[c, pl.ds(a * part + lo, seg, stride=SPLIT_STRIDE), :])
                    for r in range(dil):
                        for c in range(HPG):
                            cols = slice(col0 + c * ATT_HEAD_DIM, col0 + (c + 1) * ATT_HEAD_DIM)
                            if dil > SPLIT_STRIDE:
                                lo, hi = r % SPLIT_STRIDE, r // SPLIT_STRIDE
                                piece = stage2_ref[c, pl.ds(lo * seg + hi, sub, stride=dil // SPLIT_STRIDE), :]
                            else:
                                piece = stage_ref[c, pl.ds(a * part + r, sub, stride=dil), :]
                            z_ref[r * per + a * sub:r * per + (a + 1) * sub, cols] = piece.astype(_BF16)


def _norm_proj_att(x, g, w):
    n, d = x.shape
    d_out = w.shape[1]
    return pl.pallas_call(
        _norm_proj_att_kernel,
        grid=(n // ATT_TILE, d_out // (3 * GROUP_W)),
        in_specs=[
            pl.BlockSpec((ATT_TILE, d), lambda i, j: (i, 0)),
            pl.BlockSpec((1, d), lambda i, j: (0, 0)),
            pl.BlockSpec((d, 3 * GROUP_W), lambda i, j: (0, j)),
        ],
        out_specs=pl.BlockSpec((ATT_TILE, 3 * GROUP_W), lambda i, j: (i, j)),
        out_shape=jax.ShapeDtypeStruct((n, d_out), _BF16),
        scratch_shapes=[pltpu.VMEM((ATT_TILE, d), _BF16), pltpu.VMEM((HPG, ATT_TILE, ATT_HEAD_DIM), _F32),
                        pltpu.VMEM((HPG, ATT_TILE // PROJ_PARTS, ATT_HEAD_DIM), _F32)],
        compiler_params=_params("parallel", "arbitrary"),
        name="norm_proj_att",
    )(x, g, w)


def _kv_window_kernel(x_ref, g_ref, wk_ref, wv_ref, ko_ref, vo_ref):
    tm = x_ref.shape[0]
    h = _rmsnorm(x_ref[...], g_ref[...]).astype(_BF16)
    for w_ref, o_ref in ((wk_ref, ko_ref), (wv_ref, vo_ref)):
        res = _dot(h, w_ref[...])
        for hd in range(HPG):
            o_ref[pl.ds(hd, tm, stride=HPG), :] = res[:, hd * ATT_HEAD_DIM:(hd + 1) * ATT_HEAD_DIM]


def _kv_window(x3, g, w_grouped, group, keep, prev, layer, depth):
    b, t, d = x3.shape
    tm = min(keep, 512)
    first = (t - keep) // tm
    wspec = lambda which: pl.BlockSpec((d, GROUP_W), lambda bi, i: (0, 3 * group + which))
    ospec = pl.BlockSpec((None, None, tm * HPG, ATT_HEAD_DIM), lambda bi, i: (layer, bi, i, 0))
    oshape = jax.ShapeDtypeStruct((depth, b, keep * HPG, ATT_HEAD_DIM), _F32)
    kernel, args, extra_specs, aliases = _kv_window_kernel, [x3, g, w_grouped, w_grouped], [], {}
    if prev is not None:
        extra_specs = [pl.BlockSpec(memory_space=pl.ANY)] * 2
        aliases = {4: 0, 5: 1}
        kernel = functools.partial(_drop_refs, kernel, 4, 2)
        args += list(prev)
    return pl.pallas_call(
        kernel,
        grid=(b, keep // tm),
        in_specs=[pl.BlockSpec((None, tm, d), lambda bi, i: (bi, first + i, 0)),
                  pl.BlockSpec((1, d), lambda bi, i: (0, 0)), wspec(ATT_K), wspec(ATT_V)] + extra_specs,
        out_specs=[ospec, ospec],
        out_shape=[oshape, oshape],
        input_output_aliases=aliases,
        compiler_params=_params("parallel", "parallel"),
        name="kv_window_g%d" % group,
    )(*args)


def _ret_log_gamma():
    return jnp.log1p(-jnp.exp(jnp.linspace(math.log(1.0 / 32), math.log(1.0 / 512), RET_HEADS))).astype(_F32)


def _retention_tables(chunk):
    lg = _ret_log_gamma()
    pos = jnp.arange(chunk, dtype=_F32)
    diff = pos[:, None] - pos[None, :]
    intra = jnp.where(diff[None] >= 0, jnp.exp(lg[:, None, None] * jnp.maximum(diff, 0.0)[None]), 0.0)
    xi = jnp.exp(lg[:, None] * (pos[None] + 1.0))
    zeta = jnp.exp(lg[:, None] * (chunk - 1.0 - pos)[None])
    decay = jnp.exp(lg * chunk)
    return (intra * (RET_DK ** -0.5),
            jnp.broadcast_to(xi[:, :, None], (RET_HEADS, chunk, RET_DK)),
            jnp.broadcast_to(zeta[:, :, None], (RET_HEADS, chunk, RET_DV)),
            jnp.broadcast_to(decay[:, None, None], (RET_HEADS, 8, RET_DV)))


def _retention_kernel(q_ref, k_ref, v_ref, g_ref, s0_ref, intra_ref, xi_ref, zeta_ref, decay_ref,
                      o_ref, sfin_ref, state, *, chunk, n_chunks):
    j = pl.program_id(1)

    @pl.when(j == 0)
    def _():
        state[...] = s0_ref[...]

    for c in range(n_chunks):
        rows = slice(c * chunk, (c + 1) * chunk)
        for h in range(RET_HEADS):
            qk_cols = slice(h * RET_DK, (h + 1) * RET_DK)
            v_cols = slice(h * RET_DV, (h + 1) * RET_DV)
            q = q_ref[rows, qk_cols]
            kb = k_ref[rows, qk_cols]
            v = v_ref[rows, v_cols]
            g = g_ref[rows, v_cols].astype(_F32)
            r_prev = state[h]
            scores = _dot_nt(q, kb) * intra_ref[h]
            o = _dot(scores.astype(_BF16), v)
            o = o + _dot((q * xi_ref[h]).astype(_BF16), r_prev.astype(_BF16))
            u = _dot_tn(kb, (v * zeta_ref[h]).astype(_BF16)) * (RET_DK ** -0.5)
            state[h] = decay_ref[h, 0:1, :] * r_prev + u
            mu = jnp.mean(o, axis=-1, keepdims=True)
            oc = o - mu
            var = jnp.mean(oc * oc, axis=-1, keepdims=True)
            on = oc * lax.rsqrt(var + EPS)
            o_ref[rows, v_cols] = (on * (g * jax.nn.sigmoid(g))).astype(o_ref.dtype)

    @pl.when(j == pl.num_programs(1) - 1)
    def _():
        sfin_ref[...] = state[...]


def _retention(z3, states0, layer0, prev_states, layer, depth, *, chunk, n_chunks):
    b, t, _ = z3.shape
    tc = chunk * n_chunks
    intra, xi, zeta, decay = _retention_tables(chunk)
    const = lambda shape: pl.BlockSpec(shape, lambda bi, j: (0,) * len(shape))
    state_block = (None, None, RET_HEADS, RET_DK, RET_DV)
    kernel = functools.partial(_retention_kernel, chunk=chunk, n_chunks=n_chunks)
    args = [z3, z3, z3, z3, states0, intra, xi, zeta, decay]
    extra_specs, aliases = [], {}
    if prev_states is not None:
        extra_specs = [pl.BlockSpec(memory_space=pl.ANY)]
        aliases = {len(args): 1}
        kernel = functools.partial(_drop_refs, kernel, len(args), 1)
        args.append(prev_states)
    return pl.pallas_call(
        kernel,
        grid=(b, t // tc),
        in_specs=[
            pl.BlockSpec((None, tc, RET_QK_W), lambda bi, j: (bi, j, COL_RQ // RET_QK_W)),
            pl.BlockSpec((None, tc, RET_QK_W), lambda bi, j: (bi, j, COL_RK // RET_QK_W)),
            pl.BlockSpec((None, tc, RET_V_W), lambda bi, j: (bi, j, COL_RV // RET_V_W)),
            pl.BlockSpec((None, tc, RET_V_W), lambda bi, j: (bi, j, COL_RG // RET_V_W)),
            pl.BlockSpec(state_block, lambda bi, j: (layer0, bi, 0, 0, 0)),
            const((RET_HEADS, chunk, chunk)),
            const((RET_HEADS, chunk, RET_DK)),
            const((RET_HEADS, chunk, RET_DV)),
            const((RET_HEADS, 8, RET_DV)),
        ] + extra_specs,
        out_specs=[
            pl.BlockSpec((None, tc, RET_V_W), lambda bi, j: (bi, j, 0)),
            pl.BlockSpec(state_block, lambda bi, j: (layer, bi, 0, 0, 0)),
        ],
        out_shape=[
            jax.ShapeDtypeStruct((b, t, RET_V_W), _BF16),
            jax.ShapeDtypeStruct((depth, b, RET_HEADS, RET_DK, RET_DV), _F32),
        ],
        scratch_shapes=[pltpu.VMEM((RET_HEADS, RET_DK, RET_DV), _F32)],
        input_output_aliases=aliases,
        compiler_params=_params("parallel", "arbitrary"),
        name="retention",
    )(*args)


def _alibi_slope(head):
    return 2.0 ** (-8.0 * (head + 1.0) / ATT_HEADS)


def _pack_head_stats(cols):
    rows = cols[0].shape[0]
    lane = lax.broadcasted_iota(jnp.int32, (rows, LSE_LANES), 1)
    out = jnp.zeros((rows, LSE_LANES), _F32)
    for h, c in enumerate(cols):
        out = jnp.where(lane == h, c, out)
    return out


def _band_attn_kernel(q_ref, k_ref, v_ref, o_ref, lse_ref, kprev, vprev, *, dil, group):
    i = pl.program_id(1)
    h = pl.program_id(2)
    blk = Q_BLOCK
    per = ATT_TILE // dil
    row = lax.broadcasted_iota(jnp.int32, (blk, 2 * blk), 0)
    col = lax.broadcasted_iota(jnp.int32, (blk, 2 * blk), 1)
    delta = row + blk - col
    slope = jnp.float32(0.0)
    for hh in range(HPG):
        slope = jnp.where(h == hh, jnp.float32(_alibi_slope(group * HPG + hh)), slope)
    in_band = (delta >= 0) & (delta <= ATT_TAPS)
    bias = jnp.where(in_band, -slope * (delta * dil).astype(_F32), -jnp.inf)
    bias_first = jnp.where(col >= jnp.where(i > 0, 0, blk), bias, -jnp.inf)
    lane = lax.broadcasted_iota(jnp.int32, (blk, LSE_LANES), 1)
    scale = ATT_HEAD_DIM ** -0.5

    @pl.when(i == 0)
    def _():
        kprev[h] = jnp.zeros((ATT_TILE, ATT_HEAD_DIM), _BF16)
        vprev[h] = jnp.zeros((ATT_TILE, ATT_HEAD_DIM), _BF16)

    for r in range(dil):
        for sb in range(per // blk):
            lo = r * per + sb * blk
            if sb == 0:
                last = slice((r + 1) * per - blk, (r + 1) * per)
                keys = jnp.concatenate([kprev[h, last, :], k_ref[lo:lo + blk, :]], axis=0)
                values = jnp.concatenate([vprev[h, last, :], v_ref[lo:lo + blk, :]], axis=0)
            else:
                keys = k_ref[lo - blk:lo + blk, :]
                values = v_ref[lo - blk:lo + blk, :]
            s = _dot_nt(q_ref[lo:lo + blk, :], keys) * scale + (bias_first if sb == 0 else bias)
            m = jnp.max(s, axis=-1, keepdims=True)
            e = jnp.exp(s - m)
            den = jnp.sum(e, axis=-1, keepdims=True)
            start = r + sb * blk * dil
            rows = pl.ds(start, blk, stride=dil) if dil > 1 else pl.ds(start, blk)
            o_ref[rows, :] = _dot((e / den).astype(_BF16), values)
            prev = jnp.where(h == 0, 0.0, lse_ref[rows, :])
            lse_ref[rows, :] = jnp.where(lane == h, m + jnp.log(den), prev)

    kprev[h] = k_ref[...]
    vprev[h] = v_ref[...]


def _band_attention(z3, group):
    b, t, _ = z3.shape
    _, dil = ATT_GROUPS[group]
    tiles = t // ATT_TILE
    spec = lambda which: pl.BlockSpec((None, ATT_TILE, ATT_HEAD_DIM),
                                      lambda bi, i, h: (bi, i, (3 * group + which) * HPG + h))
    carry = pltpu.VMEM((HPG, ATT_TILE, ATT_HEAD_DIM), _BF16)
    return pl.pallas_call(
        functools.partial(_band_attn_kernel, dil=dil, group=group),
        grid=(b, tiles, HPG),
        in_specs=[spec(ATT_Q), spec(ATT_K), spec(ATT_V)],
        out_specs=[
            pl.BlockSpec((None, ATT_TILE, ATT_HEAD_DIM), lambda bi, i, h: (h, bi * tiles + i, 0)),
            pl.BlockSpec((ATT_TILE, LSE_LANES), lambda bi, i, h: (bi * tiles + i, 0)),
        ],
        out_shape=[
            jax.ShapeDtypeStruct((HPG, b * t, ATT_HEAD_DIM), _F32),
            jax.ShapeDtypeStruct((b * t, LSE_LANES), _F32),
        ],
        scratch_shapes=[carry, carry],
        compiler_params=_params("parallel", "arbitrary", "arbitrary"),
        name="band_attention_g%d" % group,
    )(z3, z3, z3)


def _step_attn_kernel(q_ref, kn_ref, vn_ref, kc_ref, vc_ref, o_ref, lse_ref, ko_ref, vo_ref, *,
                      t_new, cache_len, dil, group):
    n_res = min(dil, t_new)
    taps = cache_len // dil
    rows_all = HPG * t_new
    head_cols = [slice(h * ATT_HEAD_DIM, (h + 1) * ATT_HEAD_DIM) for h in range(HPG)]
    head_rows = [slice(h * t_new, (h + 1) * t_new) for h in range(HPG)]
    q = [q_ref[:, c].astype(_BF16) for c in head_cols]
    k_new = kn_ref[...]
    v_new = vn_ref[...]

    def cache_taps(ref, r, h):
        return ref[pl.ds(r * HPG + h, taps, stride=HPG * dil), :].astype(_BF16)

    scale = ATT_HEAD_DIM ** -0.5
    log2_dil = dil.bit_length() - 1
    log2_new = t_new.bit_length() - 1

    def row_terms(width):
        row_id = lax.broadcasted_iota(jnp.int32, (rows_all, width), 0)
        slope = jnp.zeros((rows_all, width), _F32)
        for h in range(HPG):
            slope = jnp.where((row_id >> log2_new) == h, _alibi_slope(group * HPG + h), slope)
        return row_id & (t_new - 1), slope

    query, slope = row_terms(taps)
    residue = query & (dil - 1)
    taps_back = taps + (query >> log2_dil) - lax.broadcasted_iota(jnp.int32, (rows_all, taps), 1)
    s_cache = None
    for r in range(n_res):
        s_r = jnp.concatenate([_dot_nt(q[h], cache_taps(kc_ref, r, h)) for h in range(HPG)], axis=0)
        s_cache = s_r if s_cache is None else jnp.where(residue == r, s_r, s_cache)
    s_cache = s_cache * scale - slope * (taps_back << log2_dil).astype(_F32)
    s_cache = jnp.where(taps_back <= ATT_TAPS, s_cache, -jnp.inf)

    query_n, slope_n = row_terms(t_new)
    back = query_n - lax.broadcasted_iota(jnp.int32, (rows_all, t_new), 1)
    s_new = jnp.concatenate([_dot_nt(q[h], k_new[:, head_cols[h]].astype(_BF16)) for h in range(HPG)], axis=0)
    s_new = s_new * scale - slope_n * back.astype(_F32)
    s_new = jnp.where((back >= 0) & ((back & (dil - 1)) == 0), s_new, -jnp.inf)

    m = jnp.maximum(jnp.max(s_cache, axis=-1, keepdims=True), jnp.max(s_new, axis=-1, keepdims=True))
    e_cache = jnp.exp(s_cache - m)
    e_new = jnp.exp(s_new - m)
    den = jnp.sum(e_cache, axis=-1, keepdims=True) + jnp.sum(e_new, axis=-1, keepdims=True)
    p_cache = e_cache / den
    p_new = (e_new / den).astype(_BF16)
    lse = m + jnp.log(den)

    for h in range(HPG):
        out = _dot(p_new[head_rows[h]], v_new[:, head_cols[h]].astype(_BF16))
        for r in range(n_res):
            p_r = p_cache[head_rows[h]]
            if n_res > 1:
                p_r = jnp.where(residue[head_rows[h]] == r, p_r, 0.0)
            out = out + _dot(p_r.astype(_BF16), cache_taps(vc_ref, r, h))
        o_ref[h] = out
    lse_ref[...] = _pack_head_stats([lse[rows] for rows in head_rows])

    keep = (cache_len - t_new) * HPG
    ko_ref[0:keep, :] = kc_ref[t_new * HPG:, :]
    vo_ref[0:keep, :] = vc_ref[t_new * HPG:, :]
    for h in range(HPG):
        ko_ref[pl.ds(keep + h, t_new, stride=HPG), :] = k_new[:, head_cols[h]]
        vo_ref[pl.ds(keep + h, t_new, stride=HPG), :] = v_new[:, head_cols[h]]


def _step_attention(z3, cache_k, cache_v, prev_k, prev_v, layer, group):
    b, t_new, _ = z3.shape
    cache_len = cache_k.shape[2] // HPG
    _, dil = ATT_GROUPS[group]
    cq, ck, cv = (3 * group + which for which in (ATT_Q, ATT_K, ATT_V))
    zspec = lambda c: pl.BlockSpec((None, t_new, GROUP_W), lambda bi: (bi, 0, c))
    cache_spec = pl.BlockSpec((None, None, cache_len * HPG, ATT_HEAD_DIM), lambda bi: (layer, bi, 0, 0))
    kernel = functools.partial(_step_attn_kernel, t_new=t_new, cache_len=cache_len, dil=dil, group=group)
    in_specs = [zspec(cq), zspec(ck), zspec(cv), cache_spec, cache_spec]
    args = [z3, z3, z3, cache_k, cache_v]
    aliases = {}
    if prev_k is not None:
        in_specs += [pl.BlockSpec(memory_space=pl.ANY)] * 2
        args += [prev_k, prev_v]
        aliases = {5: 2, 6: 3}
        kernel = functools.partial(_drop_refs, kernel, 5, 2)
    cache_shape = jax.ShapeDtypeStruct(cache_k.shape, cache_k.dtype)
    return pl.pallas_call(
        kernel,
        grid=(b,),
        in_specs=in_specs,
        out_specs=[
            pl.BlockSpec((HPG, t_new, ATT_HEAD_DIM), lambda bi: (0, bi, 0)),
            pl.BlockSpec((t_new, LSE_LANES), lambda bi: (bi, 0)),
            cache_spec, cache_spec,
        ],
        out_shape=[
            jax.ShapeDtypeStruct((HPG, b * t_new, ATT_HEAD_DIM), _F32),
            jax.ShapeDtypeStruct((b * t_new, LSE_LANES), _F32),
            cache_shape, cache_shape,
        ],
        input_output_aliases=aliases,
        compiler_params=_params("parallel"),
        name="step_attention_g%d" % group,
    )(*args)


def _drop_refs(kernel, start, count, *refs):
    return kernel(*refs[:start], *refs[start + count:])


def _mix_ffn_kernel(x_ref, oret_ref, og0_ref, og1_ref, og2_ref, l0_ref, l1_ref, l2_ref, gates_ref,
                    wret_ref, watt_ref, wout_ref, g_ref, wup_ref, wdown_ref, p_ref, wple_ref, wgate_ref, gfin_ref,
                    y_ref, xmid_ref, h_ref, acc_ref, *, final_norm):
    j = pl.program_id(1)

    @pl.when(j == 0)
    def _():
        l0, l1, l2 = l0_ref[...], l1_ref[...], l2_ref[...]
        mx = jnp.maximum(jnp.maximum(l0, l1), l2)
        e0, e1, e2 = jnp.exp(l0 - mx), jnp.exp(l1 - mx), jnp.exp(l2 - mx)
        tot = e0 + e1 + e2
        w0, w1, w2 = e0 / tot, e1 / tot, e2 / tot
        parts = []
        for h in range(HPG):
            parts.append(w0[:, h:h + 1] * og0_ref[h] + w1[:, h:h + 1] * og1_ref[h] + w2[:, h:h + 1] * og2_ref[h])
        o_att = jnp.concatenate(parts, axis=-1).astype(_BF16)
        br_ret = _dot(oret_ref[...], wret_ref[...])
        br_att = _dot(o_att, watt_ref[...])
        ga = gates_ref[:, 0:D_MODEL].astype(_F32)
        gb = gates_ref[:, D_MODEL:2 * D_MODEL].astype(_F32)
        mix = jax.nn.sigmoid(ga) * br_ret + jax.nn.sigmoid(gb) * br_att
        x_mid = x_ref[...] + _dot(mix.astype(_BF16), wout_ref[...])
        xmid_ref[...] = x_mid
        h_ref[...] = _rmsnorm(x_mid, g_ref[...]).astype(_BF16)
        acc_ref[...] = jnp.zeros_like(acc_ref)

    u = jnp.maximum(_dot(h_ref[...], wup_ref[...]), 0.0)
    acc_ref[...] += _dot((u * u).astype(_BF16), wdown_ref[...])

    @pl.when(j == pl.num_programs(1) - 1)
    def _():
        x = xmid_ref[...] + acc_ref[...]
        gate = jax.nn.sigmoid(_dot(x.astype(_BF16), wgate_ref[...]))
        x = x + _dot(p_ref[...].astype(_BF16), wple_ref[...]) * gate
        if final_norm:
            x = _rmsnorm(x, gfin_ref[...])
        y_ref[...] = x


def _mix_ffn(x, o_ret, ogs, lses, z_mix, p_all, layer, lw, g_final, *, tm, tf, final_norm):
    n, d = x.shape
    row = lambda w: pl.BlockSpec((tm, w), lambda i, j: (i, 0))
    heads = pl.BlockSpec((HPG, tm, ATT_HEAD_DIM), lambda i, j: (0, i, 0))
    once = lambda a: pl.BlockSpec(a.shape, lambda i, j: (0, 0), pipeline_mode=pl.Buffered(1))
    weights = [lw[k] for k in ("w_ret_br", "w_att_br", "w_out", "norm_ffn")]
    tail = [lw["w_ple"], lw["w_ple_gate"], g_final]
    return pl.pallas_call(
        functools.partial(_mix_ffn_kernel, final_norm=final_norm),
        grid=(n // tm, D_FF // tf),
        in_specs=[row(d), row(RET_V_W), heads, heads, heads,
                  row(LSE_LANES), row(LSE_LANES), row(LSE_LANES), row(2 * D_MODEL)]
        + [once(w) for w in weights]
        + [pl.BlockSpec((d, tf), lambda i, j: (0, j)),
           pl.BlockSpec((tf, d), lambda i, j: (j, 0)),
           pl.BlockSpec((None, tm, D_PLE), lambda i, j: (layer, i, 0))]
        + [once(w) for w in tail],
        out_specs=row(d),
        out_shape=jax.ShapeDtypeStruct((n, d), _F32),
        scratch_shapes=[pltpu.VMEM((tm, d), _F32), pltpu.VMEM((tm, d), _BF16), pltpu.VMEM((tm, d), _F32)],
        compiler_params=_params("parallel", "arbitrary", vmem_limit=VMEM_LIMIT_MIX_FFN),
        name="mix_ffn",
    )(x, o_ret, *ogs, *lses, z_mix, *weights, lw["w_up"], lw["w_down"], p_all, *tail)


def _layer_weights(norm_mix, w_in, w_ret_br, w_att_br, w_out, norm_ffn, w_up, w_down, w_ple, w_ple_gate, i):
    w = w_in[i]
    att_start = 2 * RET_QK_W + 2 * RET_V_W
    w_mix = jnp.concatenate([w[:, D_IN - 2 * D_MODEL:], w[:, :att_start]], axis=1)
    return dict(
        norm_mix=norm_mix[i][None, :], w_in_mix=w_mix.astype(_BF16),
        w_in_att_grouped=jnp.concatenate(
            [w[:, att_start + which * ATT_W + g * GROUP_W:att_start + which * ATT_W + (g + 1) * GROUP_W]
             for g in range(N_GROUPS) for which in range(3)], axis=1).astype(_BF16),
        w_ret_br=w_ret_br[i].astype(_BF16), w_att_br=w_att_br[i].astype(_BF16), w_out=w_out[i].astype(_BF16),
        norm_ffn=norm_ffn[i][None, :], w_up=w_up[i].astype(_BF16), w_down=w_down[i].astype(_BF16),
        w_ple=w_ple[i].astype(_BF16), w_ple_gate=w_ple_gate[i].astype(_BF16))


def kernel(x_prompt, x_sample, cache_win_k0, cache_win_v0, cache_win_k1, cache_win_v1, cache_win_k2, cache_win_v2,
           state_ret, p_prompt, p_sample, norm_mix, w_in, w_ret_br, w_att_br, w_out, norm_ffn, w_up, w_down,
           w_ple, w_ple_gate, norm_final):
    depth = w_in.shape[0]
    bp, tp, d = x_prompt.shape
    bs, ts, _ = x_sample.shape
    xp = x_prompt.reshape(bp * tp, d)
    xs = x_sample.reshape(bs * ts, d)
    g_final = norm_final[None, :]
    pp = p_prompt.reshape(depth, bp * tp, D_PLE)
    ps = p_sample.reshape(depth, bs * ts, D_PLE)
    pos_head_rows = lambda c: c.reshape(c.shape[:2] + (c.shape[2] * HPG, ATT_HEAD_DIM))
    caches_k = [pos_head_rows(c) for c in (cache_win_k0, cache_win_k1, cache_win_k2)]
    caches_v = [pos_head_rows(c) for c in (cache_win_v0, cache_win_v1, cache_win_v2)]
    new_k = [None] * N_GROUPS
    new_v = [None] * N_GROUPS
    windows = [None] * N_GROUPS
    prompt_ret = sample_ret = None
    zero_state = jnp.zeros((1, bp, RET_HEADS, RET_DK, RET_DV), _F32)
    sample_chunk = math.gcd(ts, RET_CHUNK)

    for i in range(depth):
        lw = _layer_weights(norm_mix, w_in, w_ret_br, w_att_br, w_out, norm_ffn, w_up, w_down, w_ple, w_ple_gate, i)
        last = i == depth - 1

        z_mix = _norm_proj(xp, lw["norm_mix"], lw["w_in_mix"], _BF16, tm=2048, tn=1024)
        z_att = _norm_proj_att(xp, lw["norm_mix"], lw["w_in_att_grouped"]).reshape(bp, tp, 3 * ATT_W)
        o_ret, prompt_ret = _retention(z_mix.reshape(bp, tp, MIX_W), zero_state, 0, prompt_ret, i, depth,
                                       chunk=RET_CHUNK, n_chunks=4)
        ogs, lses = zip(*[_band_attention(z_att, g) for g in range(N_GROUPS)])
        for g, (window, _) in enumerate(ATT_GROUPS):
            windows[g] = _kv_window(xp.reshape(bp, tp, d), lw["norm_mix"], lw["w_in_att_grouped"], g, min(window, tp),
                                    windows[g], i, depth)
        xp = _mix_ffn(xp, o_ret.reshape(bp * tp, RET_V_W), ogs, lses, z_mix, pp, i, lw, g_final,
                      tm=512, tf=1024, final_norm=last)

        z_mix = _norm_proj(xs, lw["norm_mix"], lw["w_in_mix"], _BF16, tm=bs * ts)
        z3 = _norm_proj(xs, lw["norm_mix"], lw["w_in_att_grouped"], _F32, tm=bs * ts).reshape(bs, ts, 3 * ATT_W)
        o_ret, sample_ret = _retention(z_mix.reshape(bs, ts, MIX_W), state_ret, i, sample_ret, i, depth,
                                       chunk=sample_chunk, n_chunks=ts // sample_chunk)
        ogs, lses = [], []
        for g in range(N_GROUPS):
            o_g, lse_g, new_k[g], new_v[g] = _step_attention(z3, caches_k[g], caches_v[g], new_k[g], new_v[g], i, g)
            ogs.append(o_g)
            lses.append(lse_g)
        xs = _mix_ffn(xs, o_ret.reshape(bs * ts, RET_V_W), ogs, lses, z_mix, ps, i, lw, g_final,
                      tm=bs * ts, tf=1024, final_norm=last)

    as_heads = lambda a: a.reshape(a.shape[:2] + (a.shape[2] // HPG, HPG, ATT_HEAD_DIM))
    prompt_windows = [as_heads(a) for kv in windows for a in kv]
    sample_windows = [as_heads(a) for g in range(N_GROUPS) for a in (new_k[g], new_v[g])]
    return (xp.reshape(bp, tp, d), xs.reshape(bs, ts, d), *prompt_windows, prompt_ret, *sample_windows, sample_ret)
```

```python
import functools
import math

import jax
import jax.numpy as jnp
import numpy as np
from jax import lax
from jax.experimental import pallas as pl
from jax.experimental.pallas import tpu as pltpu

D_MODEL = 1024
D_PLE = 256
RET_HEADS = 4
RET_DK = 128
RET_DV = 256
RET_CHUNK = 128
ATT_GROUPS = ((128, 1), (512, 4), (2048, 16))
N_GROUPS = 3
HPG = 4
ATT_HEAD_DIM = 128
ATT_HEADS = N_GROUPS * HPG
ATT_TAPS = 128
Q_BLOCK = 128
ATT_TILE = 2048
PROJ_PARTS = 4
NORM_PARTS = 4
MERGE_PARTS = 1
SPLIT_STRIDE = 4
D_FF = 4 * D_MODEL
EPS = 1e-6

RET_QK_W = RET_HEADS * RET_DK
RET_V_W = RET_HEADS * RET_DV
GROUP_W = HPG * ATT_HEAD_DIM
ATT_W = ATT_HEADS * ATT_HEAD_DIM
D_IN = 2 * RET_QK_W + 2 * RET_V_W + 3 * ATT_W + 2 * D_MODEL

MIX_W = 2 * D_MODEL + 2 * RET_QK_W + 2 * RET_V_W
COL_GATES = 0
COL_RQ = 2 * D_MODEL
COL_RK = COL_RQ + RET_QK_W
COL_RV = COL_RK + RET_QK_W
COL_RG = COL_RV + RET_V_W
ATT_Q, ATT_K, ATT_V = 0, 1, 2
COL_BLOCK = 512

LSE_LANES = 128
VMEM_LIMIT = 48 * 1024 * 1024
VMEM_LIMIT_MIX_FFN = 56 * 1024 * 1024

_BF16 = jnp.bfloat16
_F32 = jnp.float32


def _params(*sem, vmem_limit=VMEM_LIMIT):
    return pltpu.CompilerParams(dimension_semantics=sem, vmem_limit_bytes=vmem_limit)


def _rmsnorm(x, g):
    return x * lax.rsqrt(jnp.mean(x * x, axis=-1, keepdims=True) + EPS) * g


def _dot(a, b):
    return jnp.dot(a, b, preferred_element_type=_F32)


def _dot_nt(a, b):
    return lax.dot_general(a, b, (((1,), (1,)), ((), ())), preferred_element_type=_F32)


def _dot_tn(a, b):
    return lax.dot_general(a, b, (((0,), (0,)), ((), ())), preferred_element_type=_F32)


def _norm_proj_kernel(x_ref, g_ref, w_ref, z_ref, h_ref):
    first = pl.program_id(1) == 0

    @pl.when(first)
    def _():
        tm = x_ref.shape[0]
        part = tm // NORM_PARTS if tm % (NORM_PARTS * 16) == 0 else tm
        for a in range(tm // part):
            rows = slice(a * part, (a + 1) * part)
            h = _rmsnorm(x_ref[rows, :], g_ref[...]).astype(_BF16)
            h_ref[rows, :] = h
            z_ref[rows, :] = _dot(h, w_ref[...]).astype(z_ref.dtype)

    @pl.when(jnp.logical_not(first))
    def _():
        z_ref[...] = _dot(h_ref[...], w_ref[...]).astype(z_ref.dtype)


def _norm_proj(x, g, w, out_dtype, *, tm, tn=COL_BLOCK):
    n, d = x.shape
    d_out = w.shape[1]
    return pl.pallas_call(
        _norm_proj_kernel,
        grid=(n // tm, d_out // tn),
        in_specs=[
            pl.BlockSpec((tm, d), lambda i, j: (i, 0)),
            pl.BlockSpec((1, d), lambda i, j: (0, 0)),
            pl.BlockSpec((d, tn), lambda i, j: (0, j)),
        ],
        out_specs=pl.BlockSpec((tm, tn), lambda i, j: (i, j)),
        out_shape=jax.ShapeDtypeStruct((n, d_out), out_dtype),
        scratch_shapes=[pltpu.VMEM((tm, d), _BF16)],
        compiler_params=_params("parallel", "arbitrary"),
        name="norm_proj",
    )(x, g, w)


def _norm_proj_att_kernel(x_ref, g_ref, w_ref, z_ref, h_ref, stage_ref, stage2_ref):
    j = pl.program_id(1)
    part = ATT_TILE // PROJ_PARTS
    for group, (_, dil) in enumerate(ATT_GROUPS):
        @pl.when(j == group)
        def _(group=group, dil=dil):
            if group == 0:
                units = [(a, which) for a in range(PROJ_PARTS) for which in range(3)]
            else:
                units = [(a, which) for which in range(3) for a in range(PROJ_PARTS)]
            for a, which in units:
                rows = slice(a * part, (a + 1) * part)
                col0 = which * GROUP_W
                if group == 0 and which == 0:
                    h_ref[rows, :] = _rmsnorm(x_ref[rows, :], g_ref[...]).astype(_BF16)
                res = _dot(h_ref[rows, :], w_ref[:, col0:col0 + GROUP_W])
                if dil == 1:
                    z_ref[rows, col0:col0 + GROUP_W] = res.astype(_BF16)
                    continue
                per, sub = ATT_TILE // dil, part // dil
                for c in range(HPG):
                    stage_ref[c, rows, :] = res[:, c * ATT_HEAD_DIM:(c + 1) * ATT_HEAD_DIM]
                if dil > SPLIT_STRIDE:
                    seg = part // SPLIT_STRIDE
                    for c in range(HPG):
                        for lo in range(SPLIT_STRIDE):
                            stage2_ref[c, lo * seg:(lo + 1) * seg, :] = (
                                stage_ref[c, pl.ds(a * part + lo, seg, stride=SPLIT_STRIDE), :])
                for r in range(dil):
                    for c in range(HPG):
                        cols = slice(col0 + c * ATT_HEAD_DIM, col0 + (c + 1) * ATT_HEAD_DIM)
                        if dil > SPLIT_STRIDE:
                            lo, hi = r % SPLIT_STRIDE, r // SPLIT_STRIDE
                            piece = stage2_ref[c, pl.ds(lo * seg + hi, sub, stride=dil // SPLIT_STRIDE), :]
                        else:
                            piece = stage_ref[c, pl.ds(a * part + r, sub, stride=dil), :]
                        z_ref[r * per + a * sub:r * per + (a + 1) * sub, cols] = piece.astype(_BF16)


def _norm_proj_att(x, g, w):
    n, d = x.shape
    d_out = w.shape[1]
    return pl.pallas_call(
        _norm_proj_att_kernel,
        grid=(n // ATT_TILE, d_out // (3 * GROUP_W)),
        in_specs=[
            pl.BlockSpec((ATT_TILE, d), lambda i, j: (i, 0)),
            pl.BlockSpec((1, d), lambda i, j: (0, 0)),
            pl.BlockSpec((d, 3 * GROUP_W), lambda i, j: (0, j)),
        ],
        out_specs=pl.BlockSpec((ATT_TILE, 3 * GROUP_W), lambda i, j: (i, j)),
        out_shape=jax.ShapeDtypeStruct((n, d_out), _BF16),
        scratch_shapes=[pltpu.VMEM((ATT_TILE, d), _BF16), pltpu.VMEM((HPG, ATT_TILE, ATT_HEAD_DIM), _F32),
                        pltpu.VMEM((HPG, ATT_TILE // PROJ_PARTS, ATT_HEAD_DIM), _F32)],
        compiler_params=_params("parallel", "arbitrary"),
        name="norm_proj_att",
    )(x, g, w)


def _kv_window_kernel(x_ref, g_ref, wk_ref, wv_ref, ko_ref, vo_ref):
    tm = x_ref.shape[0]
    h = _rmsnorm(x_ref[...], g_ref[...]).astype(_BF16)
    for w_ref, o_ref in ((wk_ref, ko_ref), (wv_ref, vo_ref)):
        res = _dot(h, w_ref[...])
        for hd in range(HPG):
            o_ref[pl.ds(hd, tm, stride=HPG), :] = res[:, hd * ATT_HEAD_DIM:(hd + 1) * ATT_HEAD_DIM]


def _kv_window(x3, g, w_grouped, group, keep, prev, layer, depth):
    b, t, d = x3.shape
    tm = min(keep, 512)
    first = (t - keep) // tm
    wspec = lambda which: pl.BlockSpec((d, GROUP_W), lambda bi, i: (0, 3 * group + which))
    ospec = pl.BlockSpec((None, None, tm * HPG, ATT_HEAD_DIM), lambda bi, i: (layer, bi, i, 0))
    oshape = jax.ShapeDtypeStruct((depth, b, keep * HPG, ATT_HEAD_DIM), _F32)
    kernel, args, extra_specs, aliases = _kv_window_kernel, [x3, g, w_grouped, w_grouped], [], {}
    if prev is not None:
        extra_specs = [pl.BlockSpec(memory_space=pl.ANY)] * 2
        aliases = {4: 0, 5: 1}
        kernel = functools.partial(_drop_refs, kernel, 4, 2)
        args += list(prev)
    return pl.pallas_call(
        kernel,
        grid=(b, keep // tm),
        in_specs=[pl.BlockSpec((None, tm, d), lambda bi, i: (bi, first + i, 0)),
                  pl.BlockSpec((1, d), lambda bi, i: (0, 0)), wspec(ATT_K), wspec(ATT_V)] + extra_specs,
        out_specs=[ospec, ospec],
        out_shape=[oshape, oshape],
        input_output_aliases=aliases,
        compiler_params=_params("parallel", "parallel"),
        name="kv_window_g%d" % group,
    )(*args)


def _ret_log_gamma():
    return jnp.log1p(-jnp.exp(jnp.linspace(math.log(1.0 / 32), math.log(1.0 / 512), RET_HEADS))).astype(_F32)


def _retention_tables(chunk):
    lg = _ret_log_gamma()
    pos = jnp.arange(chunk, dtype=_F32)
    diff = pos[:, None] - pos[None, :]
    intra = jnp.where(diff[None] >= 0, jnp.exp(lg[:, None, None] * jnp.maximum(diff, 0.0)[None]), 0.0)
    xi = jnp.exp(lg[:, None] * (pos[None] + 1.0))
    zeta = jnp.exp(lg[:, None] * (chunk - 1.0 - pos)[None])
    decay = jnp.exp(lg * chunk)
    return (intra * (RET_DK ** -0.5),
            jnp.broadcast_to(xi[:, :, None], (RET_HEADS, chunk, RET_DK)),
            jnp.broadcast_to(zeta[:, :, None], (RET_HEADS, chunk, RET_DV)),
            jnp.broadcast_to(decay[:, None, None], (RET_HEADS, 8, RET_DV)))


def _retention_kernel(q_ref, k_ref, v_ref, g_ref, s0_ref, intra_ref, xi_ref, zeta_ref, decay_ref,
                      o_ref, sfin_ref, state, *, chunk, n_chunks):
    j = pl.program_id(1)

    @pl.when(j == 0)
    def _():
        state[...] = s0_ref[...]

    for c in range(n_chunks):
        rows = slice(c * chunk, (c + 1) * chunk)
        for h in range(RET_HEADS):
            qk_cols = slice(h * RET_DK, (h + 1) * RET_DK)
            v_cols = slice(h * RET_DV, (h + 1) * RET_DV)
            q = q_ref[rows, qk_cols]
            kb = k_ref[rows, qk_cols]
            v = v_ref[rows, v_cols]
            g = g_ref[rows, v_cols].astype(_F32)
            r_prev = state[h]
            scores = _dot_nt(q, kb) * intra_ref[h]
            o = _dot(scores.astype(_BF16), v)
            o = o + _dot((q * xi_ref[h]).astype(_BF16), r_prev.astype(_BF16))
            u = _dot_tn(kb, (v * zeta_ref[h]).astype(_BF16)) * (RET_DK ** -0.5)
            state[h] = decay_ref[h, 0:1, :] * r_prev + u
            mu = jnp.mean(o, axis=-1, keepdims=True)
            oc = o - mu
            var = jnp.mean(oc * oc, axis=-1, keepdims=True)
            on = oc * lax.rsqrt(var + EPS)
            o_ref[rows, v_cols] = (on * (g * jax.nn.sigmoid(g))).astype(o_ref.dtype)

    @pl.when(j == pl.num_programs(1) - 1)
    def _():
        sfin_ref[...] = state[...]


def _retention(z3, states0, layer0, prev_states, layer, depth, *, chunk, n_chunks):
    b, t, _ = z3.shape
    tc = chunk * n_chunks
    intra, xi, zeta, decay = _retention_tables(chunk)
    const = lambda shape: pl.BlockSpec(shape, lambda bi, j: (0,) * len(shape))
    state_block = (None, None, RET_HEADS, RET_DK, RET_DV)
    kernel = functools.partial(_retention_kernel, chunk=chunk, n_chunks=n_chunks)
    args = [z3, z3, z3, z3, states0, intra, xi, zeta, decay]
    extra_specs, aliases = [], {}
    if prev_states is not None:
        extra_specs = [pl.BlockSpec(memory_space=pl.ANY)]
        aliases = {len(args): 1}
        kernel = functools.partial(_drop_refs, kernel, len(args), 1)
        args.append(prev_states)
    return pl.pallas_call(
        kernel,
        grid=(b, t // tc),
        in_specs=[
            pl.BlockSpec((None, tc, RET_QK_W), lambda bi, j: (bi, j, COL_RQ // RET_QK_W)),
            pl.BlockSpec((None, tc, RET_QK_W), lambda bi, j: (bi, j, COL_RK // RET_QK_W)),
            pl.BlockSpec((None, tc, RET_V_W), lambda bi, j: (bi, j, COL_RV // RET_V_W)),
            pl.BlockSpec((None, tc, RET_V_W), lambda bi, j: (bi, j, COL_RG // RET_V_W)),
            pl.BlockSpec(state_block, lambda bi, j: (layer0, bi, 0, 0, 0)),
            const((RET_HEADS, chunk, chunk)),
            const((RET_HEADS, chunk, RET_DK)),
            const((RET_HEADS, chunk, RET_DV)),
            const((RET_HEADS, 8, RET_DV)),
        ] + extra_specs,
        out_specs=[
            pl.BlockSpec((None, tc, RET_V_W), lambda bi, j: (bi, j, 0)),
            pl.BlockSpec(state_block, lambda bi, j: (layer, bi, 0, 0, 0)),
        ],
        out_shape=[
            jax.ShapeDtypeStruct((b, t, RET_V_W), _BF16),
            jax.ShapeDtypeStruct((depth, b, RET_HEADS, RET_DK, RET_DV), _F32),
        ],
        scratch_shapes=[pltpu.VMEM((RET_HEADS, RET_DK, RET_DV), _F32)],
        input_output_aliases=aliases,
        compiler_params=_params("parallel", "arbitrary"),
        name="retention",
    )(*args)


def _alibi_slope(head):
    return 2.0 ** (-8.0 * (head + 1.0) / ATT_HEADS)


def _pack_head_stats(cols):
    rows = cols[0].shape[0]
    lane = lax.broadcasted_iota(jnp.int32, (rows, LSE_LANES), 1)
    out = jnp.zeros((rows, LSE_LANES), _F32)
    for h, c in enumerate(cols):
        out = jnp.where(lane == h, c, out)
    return out


def _band_attn_kernel(q_ref, k_ref, v_ref, o_ref, lse_ref, kprev, vprev, stats, *, dil, group):
    i = pl.program_id(1)
    h = pl.program_id(2)
    blk = Q_BLOCK
    per = ATT_TILE // dil
    row = lax.broadcasted_iota(jnp.int32, (blk, 2 * blk), 0)
    col = lax.broadcasted_iota(jnp.int32, (blk, 2 * blk), 1)
    delta = row + blk - col
    slope = jnp.float32(0.0)
    for hh in range(HPG):
        slope = jnp.where(h == hh, jnp.float32(_alibi_slope(group * HPG + hh)), slope)
    in_band = (delta >= 0) & (delta <= ATT_TAPS)
    bias = jnp.where(in_band, -slope * (delta * dil).astype(_F32), -jnp.inf)
    bias_first = jnp.where(col >= jnp.where(i > 0, 0, blk), bias, -jnp.inf)
    lane = lax.broadcasted_iota(jnp.int32, (blk, LSE_LANES), 1)
    scale = ATT_HEAD_DIM ** -0.5

    def token_rows(r, sb):
        start = r + sb * blk * dil
        return pl.ds(start, blk, stride=dil) if dil > 1 else pl.ds(start, blk)

    @pl.when(i == 0)
    def _():
        kprev[h] = jnp.zeros((ATT_TILE, ATT_HEAD_DIM), _BF16)
        vprev[h] = jnp.zeros((ATT_TILE, ATT_HEAD_DIM), _BF16)

    for r in range(dil):
        for sb in range(per // blk):
            lo = r * per + sb * blk
            if sb == 0:
                last = slice((r + 1) * per - blk, (r + 1) * per)
                keys = jnp.concatenate([kprev[h, last, :], k_ref[lo:lo + blk, :]], axis=0)
                values = jnp.concatenate([vprev[h, last, :], v_ref[lo:lo + blk, :]], axis=0)
            else:
                keys = k_ref[lo - blk:lo + blk, :]
                values = v_ref[lo - blk:lo + blk, :]
            s = _dot_nt(q_ref[lo:lo + blk, :], keys) * scale + (bias_first if sb == 0 else bias)
            m = jnp.max(s, axis=-1, keepdims=True)
            e = jnp.exp(s - m)
            den = jnp.sum(e, axis=-1, keepdims=True)
            o_ref[token_rows(r, sb), :] = _dot((e / den).astype(_BF16), values)
            prev = jnp.where(h == 0, 0.0, stats[lo:lo + blk, :])
            stats[lo:lo + blk, :] = jnp.where(lane == h, m + jnp.log(den), prev)

    kprev[h] = k_ref[...]
    vprev[h] = v_ref[...]

    @pl.when(h == HPG - 1)
    def _():
        for r in range(dil):
            for sb in range(per // blk):
                lo = r * per + sb * blk
                lse_ref[token_rows(r, sb), :] = stats[lo:lo + blk, :]


def _band_attention(z3, group):
    b, t, _ = z3.shape
    _, dil = ATT_GROUPS[group]
    tiles = t // ATT_TILE
    spec = lambda which: pl.BlockSpec((None, ATT_TILE, ATT_HEAD_DIM),
                                      lambda bi, i, h: (bi, i, (3 * group + which) * HPG + h))
    carry = pltpu.VMEM((HPG, ATT_TILE, ATT_HEAD_DIM), _BF16)
    return pl.pallas_call(
        functools.partial(_band_attn_kernel, dil=dil, group=group),
        grid=(b, tiles, HPG),
        in_specs=[spec(ATT_Q), spec(ATT_K), spec(ATT_V)],
        out_specs=[
            pl.BlockSpec((None, ATT_TILE, ATT_HEAD_DIM), lambda bi, i, h: (h, bi * tiles + i, 0)),
            pl.BlockSpec((ATT_TILE, LSE_LANES), lambda bi, i, h: (bi * tiles + i, 0)),
        ],
        out_shape=[
            jax.ShapeDtypeStruct((HPG, b * t, ATT_HEAD_DIM), _F32),
            jax.ShapeDtypeStruct((b * t, LSE_LANES), _F32),
        ],
        scratch_shapes=[carry, carry, pltpu.VMEM((ATT_TILE, LSE_LANES), _F32)],
        compiler_params=_params("parallel", "arbitrary", "arbitrary"),
        name="band_attention_g%d" % group,
    )(z3, z3, z3)


def _step_attn_kernel(q_ref, kn_ref, vn_ref, kc_ref, vc_ref, o_ref, lse_ref, ko_ref, vo_ref, *,
                      t_new, cache_len, dil, group):
    n_res = min(dil, t_new)
    taps = cache_len // dil
    rows_all = HPG * t_new
    head_cols = [slice(h * ATT_HEAD_DIM, (h + 1) * ATT_HEAD_DIM) for h in range(HPG)]
    head_rows = [slice(h * t_new, (h + 1) * t_new) for h in range(HPG)]
    q = [q_ref[:, c].astype(_BF16) for c in head_cols]
    k_new = kn_ref[...]
    v_new = vn_ref[...]

    def cache_taps(ref, r, h):
        return ref[pl.ds(r * HPG + h, taps, stride=HPG * dil), :].astype(_BF16)

    scale = ATT_HEAD_DIM ** -0.5
    log2_dil = dil.bit_length() - 1
    log2_new = t_new.bit_length() - 1

    def row_terms(width):
        row_id = lax.broadcasted_iota(jnp.int32, (rows_all, width), 0)
        slope = jnp.zeros((rows_all, width), _F32)
        for h in range(HPG):
            slope = jnp.where((row_id >> log2_new) == h, _alibi_slope(group * HPG + h), slope)
        return row_id & (t_new - 1), slope

    query, slope = row_terms(taps)
    residue = query & (dil - 1)
    taps_back = taps + (query >> log2_dil) - lax.broadcasted_iota(jnp.int32, (rows_all, taps), 1)
    s_cache = None
    for r in range(n_res):
        s_r = jnp.concatenate([_dot_nt(q[h], cache_taps(kc_ref, r, h)) for h in range(HPG)], axis=0)
        s_cache = s_r if s_cache is None else jnp.where(residue == r, s_r, s_cache)
    s_cache = s_cache * scale - slope * (taps_back << log2_dil).astype(_F32)
    s_cache = jnp.where(taps_back <= ATT_TAPS, s_cache, -jnp.inf)

    query_n, slope_n = row_terms(t_new)
    back = query_n - lax.broadcasted_iota(jnp.int32, (rows_all, t_new), 1)
    s_new = jnp.concatenate([_dot_nt(q[h], k_new[:, head_cols[h]].astype(_BF16)) for h in range(HPG)], axis=0)
    s_new = s_new * scale - slope_n * back.astype(_F32)
    s_new = jnp.where((back >= 0) & ((back & (dil - 1)) == 0), s_new, -jnp.inf)

    m = jnp.maximum(jnp.max(s_cache, axis=-1, keepdims=True), jnp.max(s_new, axis=-1, keepdims=True))
    e_cache = jnp.exp(s_cache - m)
    e_new = jnp.exp(s_new - m)
    den = jnp.sum(e_cache, axis=-1, keepdims=True) + jnp.sum(e_new, axis=-1, keepdims=True)
    p_cache = e_cache / den
    p_new = (e_new / den).astype(_BF16)
    lse = m + jnp.log(den)

    for h in range(HPG):
        out = _dot(p_new[head_rows[h]], v_new[:, head_cols[h]].astype(_BF16))
        for r in range(n_res):
            p_r = p_cache[head_rows[h]]
            if n_res > 1:
                p_r = jnp.where(residue[head_rows[h]] == r, p_r, 0.0)
            out = out + _dot(p_r.astype(_BF16), cache_taps(vc_ref, r, h))
        o_ref[h] = out
    lse_ref[...] = _pack_head_stats([lse[rows] for rows in head_rows])

    keep = (cache_len - t_new) * HPG
    ko_ref[0:keep, :] = kc_ref[t_new * HPG:, :]
    vo_ref[0:keep, :] = vc_ref[t_new * HPG:, :]
    for h in range(HPG):
        ko_ref[pl.ds(keep + h, t_new, stride=HPG), :] = k_new[:, head_cols[h]]
        vo_ref[pl.ds(keep + h, t_new, stride=HPG), :] = v_new[:, head_cols[h]]


def _step_attention(z3, cache_k, cache_v, prev_k, prev_v, layer, group):
    b, t_new, _ = z3.shape
    cache_len = cache_k.shape[2] // HPG
    _, dil = ATT_GROUPS[group]
    cq, ck, cv = (3 * group + which for which in (ATT_Q, ATT_K, ATT_V))
    zspec = lambda c: pl.BlockSpec((None, t_new, GROUP_W), lambda bi: (bi, 0, c))
    cache_spec = pl.BlockSpec((None, None, cache_len * HPG, ATT_HEAD_DIM), lambda bi: (layer, bi, 0, 0))
    kernel = functools.partial(_step_attn_kernel, t_new=t_new, cache_len=cache_len, dil=dil, group=group)
    in_specs = [zspec(cq), zspec(ck), zspec(cv), cache_spec, cache_spec]
    args = [z3, z3, z3, cache_k, cache_v]
    aliases = {}
    if prev_k is not None:
        in_specs += [pl.BlockSpec(memory_space=pl.ANY)] * 2
        args += [prev_k, prev_v]
        aliases = {5: 2, 6: 3}
        kernel = functools.partial(_drop_refs, kernel, 5, 2)
    cache_shape = jax.ShapeDtypeStruct(cache_k.shape, cache_k.dtype)
    return pl.pallas_call(
        kernel,
        grid=(b,),
        in_specs=in_specs,
        out_specs=[
            pl.BlockSpec((HPG, t_new, ATT_HEAD_DIM), lambda bi: (0, bi, 0)),
            pl.BlockSpec((t_new, LSE_LANES), lambda bi: (bi, 0)),
            cache_spec, cache_spec,
        ],
        out_shape=[
            jax.ShapeDtypeStruct((HPG, b * t_new, ATT_HEAD_DIM), _F32),
            jax.ShapeDtypeStruct((b * t_new, LSE_LANES), _F32),
            cache_shape, cache_shape,
        ],
        input_output_aliases=aliases,
        compiler_params=_params("parallel"),
        name="step_attention_g%d" % group,
    )(*args)


def _drop_refs(kernel, start, count, *refs):
    return kernel(*refs[:start], *refs[start + count:])


def _mix_ffn_kernel(x_ref, oret_ref, og0_ref, og1_ref, og2_ref, l0_ref, l1_ref, l2_ref, gates_ref,
                    wret_ref, watt_ref, wout_ref, g_ref, wup_ref, wdown_ref, p_ref, wple_ref, wgate_ref, gfin_ref,
                    y_ref, xmid_ref, h_ref, acc_ref, *, final_norm):
    j = pl.program_id(1)

    @pl.when(j == 0)
    def _():
        tm = x_ref.shape[0]
        part = tm // MERGE_PARTS if tm % (MERGE_PARTS * 16) == 0 else tm
        for a in range(tm // part):
            rows = slice(a * part, (a + 1) * part)
            l0, l1, l2 = l0_ref[rows, :], l1_ref[rows, :], l2_ref[rows, :]
            mx = jnp.maximum(jnp.maximum(l0, l1), l2)
            e0, e1, e2 = jnp.exp(l0 - mx), jnp.exp(l1 - mx), jnp.exp(l2 - mx)
            tot = e0 + e1 + e2
            w0, w1, w2 = e0 / tot, e1 / tot, e2 / tot
            heads = []
            for h in range(HPG):
                heads.append(w0[:, h:h + 1] * og0_ref[h, rows, :] + w1[:, h:h + 1] * og1_ref[h, rows, :]
                             + w2[:, h:h + 1] * og2_ref[h, rows, :])
            o_att = jnp.concatenate(heads, axis=-1).astype(_BF16)
            br_ret = _dot(oret_ref[rows, :], wret_ref[...])
            br_att = _dot(o_att, watt_ref[...])
            ga = gates_ref[rows, 0:D_MODEL].astype(_F32)
            gb = gates_ref[rows, D_MODEL:2 * D_MODEL].astype(_F32)
            mix = jax.nn.sigmoid(ga) * br_ret + jax.nn.sigmoid(gb) * br_att
            x_mid = x_ref[rows, :] + _dot(mix.astype(_BF16), wout_ref[...])
            xmid_ref[rows, :] = x_mid
            h_ref[rows, :] = _rmsnorm(x_mid, g_ref[...]).astype(_BF16)
        acc_ref[...] = jnp.zeros_like(acc_ref)

    u = jnp.maximum(_dot(h_ref[...], wup_ref[...]), 0.0)
    acc_ref[...] += _dot((u * u).astype(_BF16), wdown_ref[...])

    @pl.when(j == pl.num_programs(1) - 1)
    def _():
        x = xmid_ref[...] + acc_ref[...]
        gate = jax.nn.sigmoid(_dot(x.astype(_BF16), wgate_ref[...]))
        x = x + _dot(p_ref[...].astype(_BF16), wple_ref[...]) * gate
        if final_norm:
            x = _rmsnorm(x, gfin_ref[...])
        y_ref[...] = x


def _mix_ffn(x, o_ret, ogs, lses, z_mix, p_all, layer, lw, g_final, *, tm, tf, final_norm):
    n, d = x.shape
    row = lambda w: pl.BlockSpec((tm, w), lambda i, j: (i, 0))
    heads = pl.BlockSpec((HPG, tm, ATT_HEAD_DIM), lambda i, j: (0, i, 0))
    once = lambda a: pl.BlockSpec(a.shape, lambda i, j: (0, 0), pipeline_mode=pl.Buffered(1))
    weights = [lw[k] for k in ("w_ret_br", "w_att_br", "w_out", "norm_ffn")]
    tail = [lw["w_ple"], lw["w_ple_gate"], g_final]
    return pl.pallas_call(
        functools.partial(_mix_ffn_kernel, final_norm=final_norm),
        grid=(n // tm, D_FF // tf),
        in_specs=[row(d), row(RET_V_W), heads, heads, heads,
                  row(LSE_LANES), row(LSE_LANES), row(LSE_LANES), row(2 * D_MODEL)]
        + [once(w) for w in weights]
        + [pl.BlockSpec((d, tf), lambda i, j: (0, j)),
           pl.BlockSpec((tf, d), lambda i, j: (j, 0)),
           pl.BlockSpec((None, tm, D_PLE), lambda i, j: (layer, i, 0))]
        + [once(w) for w in tail],
        out_specs=row(d),
        out_shape=jax.ShapeDtypeStruct((n, d), _F32),
        scratch_shapes=[pltpu.VMEM((tm, d), _F32), pltpu.VMEM((tm, d), _BF16), pltpu.VMEM((tm, d), _F32)],
        compiler_params=_params("parallel", "arbitrary", vmem_limit=VMEM_LIMIT_MIX_FFN),
        name="mix_ffn",
    )(x, o_ret, *ogs, *lses, z_mix, *weights, lw["w_up"], lw["w_down"], p_all, *tail)


def _layer_weights(norm_mix, w_in, w_ret_br, w_att_br, w_out, norm_ffn, w_up, w_down, w_ple, w_ple_gate, i):
    w = w_in[i]
    att_start = 2 * RET_QK_W + 2 * RET_V_W
    w_mix = jnp.concatenate([w[:, D_IN - 2 * D_MODEL:], w[:, :att_start]], axis=1)
    return dict(
        norm_mix=norm_mix[i][None, :], w_in_mix=w_mix.astype(_BF16),
        w_in_att_grouped=jnp.concatenate(
            [w[:, att_start + which * ATT_W + g * GROUP_W:att_start + which * ATT_W + (g + 1) * GROUP_W]
             for g in range(N_GROUPS) for which in range(3)], axis=1).astype(_BF16),
        w_ret_br=w_ret_br[i].astype(_BF16), w_att_br=w_att_br[i].astype(_BF16), w_out=w_out[i].astype(_BF16),
        norm_ffn=norm_ffn[i][None, :], w_up=w_up[i].astype(_BF16), w_down=w_down[i].astype(_BF16),
        w_ple=w_ple[i].astype(_BF16), w_ple_gate=w_ple_gate[i].astype(_BF16))


def kernel(x_prompt, x_sample, cache_win_k0, cache_win_v0, cache_win_k1, cache_win_v1, cache_win_k2, cache_win_v2,
           state_ret, p_prompt, p_sample, norm_mix, w_in, w_ret_br, w_att_br, w_out, norm_ffn, w_up, w_down,
           w_ple, w_ple_gate, norm_final):
    depth = w_in.shape[0]
    bp, tp, d = x_prompt.shape
    bs, ts, _ = x_sample.shape
    xp = x_prompt.reshape(bp * tp, d)
    xs = x_sample.reshape(bs * ts, d)
    g_final = norm_final[None, :]
    pp = p_prompt.reshape(depth, bp * tp, D_PLE)
    ps = p_sample.reshape(depth, bs * ts, D_PLE)
    pos_head_rows = lambda c: c.reshape(c.shape[:2] + (c.shape[2] * HPG, ATT_HEAD_DIM))
    caches_k = [pos_head_rows(c) for c in (cache_win_k0, cache_win_k1, cache_win_k2)]
    caches_v = [pos_head_rows(c) for c in (cache_win_v0, cache_win_v1, cache_win_v2)]
    new_k = [None] * N_GROUPS
    new_v = [None] * N_GROUPS
    windows = [None] * N_GROUPS
    prompt_ret = sample_ret = None
    zero_state = jnp.zeros((1, bp, RET_HEADS, RET_DK, RET_DV), _F32)
    sample_chunk = math.gcd(ts, RET_CHUNK)

    for i in range(depth):
        lw = _layer_weights(norm_mix, w_in, w_ret_br, w_att_br, w_out, norm_ffn, w_up, w_down, w_ple, w_ple_gate, i)
        last = i == depth - 1

        z_mix = _norm_proj(xp, lw["norm_mix"], lw["w_in_mix"], _BF16, tm=2048, tn=1024)
        z_att = _norm_proj_att(xp, lw["norm_mix"], lw["w_in_att_grouped"]).reshape(bp, tp, 3 * ATT_W)
        o_ret, prompt_ret = _retention(z_mix.reshape(bp, tp, MIX_W), zero_state, 0, prompt_ret, i, depth,
                                       chunk=RET_CHUNK, n_chunks=4)
        ogs, lses = zip(*[_band_attention(z_att, g) for g in range(N_GROUPS)])
        for g, (window, _) in enumerate(ATT_GROUPS):
            windows[g] = _kv_window(xp.reshape(bp, tp, d), lw["norm_mix"], lw["w_in_att_grouped"], g, min(window, tp),
                                    windows[g], i, depth)
        xp = _mix_ffn(xp, o_ret.reshape(bp * tp, RET_V_W), ogs, lses, z_mix, pp, i, lw, g_final,
                      tm=512, tf=1024, final_norm=last)

        z_mix = _norm_proj(xs, lw["norm_mix"], lw["w_in_mix"], _BF16, tm=bs * ts)
        z3 = _norm_proj(xs, lw["norm_mix"], lw["w_in_att_grouped"], _F32, tm=bs * ts).reshape(bs, ts, 3 * ATT_W)
        o_ret, sample_ret = _retention(z_mix.reshape(bs, ts, MIX_W), state_ret, i, sample_ret, i, depth,
                                       chunk=sample_chunk, n_chunks=ts // sample_chunk)
        ogs, lses = [], []
        for g in range(N_GROUPS):
            o_g, lse_g, new_k[g], new_v[g] = _step_attention(z3, caches_k[g], caches_v[g], new_k[g], new_v[g], i, g)
            ogs.append(o_g)
            lses.append(lse_g)
        xs = _mix_ffn(xs, o_ret.reshape(bs * ts, RET_V_W), ogs, lses, z_mix, ps, i, lw, g_final,
                      tm=bs * ts, tf=1024, final_norm=last)

    as_heads = lambda a: a.reshape(a.shape[:2] + (a.shape[2] // HPG, HPG, ATT_HEAD_DIM))
    prompt_windows = [as_heads(a) for kv in windows for a in kv]
    sample_windows = [as_heads(a) for g in range(N_GROUPS) for a in (new_k[g], new_v[g])]
    return (xp.reshape(bp, tp, d), xs.reshape(bs, ts, d), *prompt_windows, prompt_ret, *sample_windows, sample_ret)
```

```python
import functools
import math

import jax
import jax.numpy as jnp
import numpy as np
from jax import lax
from jax.experimental import pallas as pl
from jax.experimental.pallas import tpu as pltpu

D_MODEL = 1024
D_PLE = 256
RET_HEADS = 4
RET_DK = 128
RET_DV = 256
RET_CHUNK = 128
ATT_GROUPS = ((128, 1), (512, 4), (2048, 16))
N_GROUPS = 3
HPG = 4
ATT_HEAD_DIM = 128
ATT_HEADS = N_GROUPS * HPG
ATT_TAPS = 128
Q_BLOCK = 128
ATT_TILE = 2048
PROJ_PARTS = 4
NORM_PARTS = 4
MERGE_PARTS = 1
SPLIT_STRIDE = 4
D_FF = 4 * D_MODEL
EPS = 1e-6

RET_QK_W = RET_HEADS * RET_DK
RET_V_W = RET_HEADS * RET_DV
GROUP_W = HPG * ATT_HEAD_DIM
ATT_W = ATT_HEADS * ATT_HEAD_DIM
D_IN = 2 * RET_QK_W + 2 * RET_V_W + 3 * ATT_W + 2 * D_MODEL

MIX_W = 2 * D_MODEL + 2 * RET_QK_W + 2 * RET_V_W
COL_GATES = 0
COL_RQ = 2 * D_MODEL
COL_RK = COL_RQ + RET_QK_W
COL_RV = COL_RK + RET_QK_W
COL_RG = COL_RV + RET_V_W
ATT_Q, ATT_K, ATT_V = 0, 1, 2
COL_BLOCK = 512

LSE_LANES = 128
VMEM_LIMIT = 48 * 1024 * 1024
VMEM_LIMIT_MIX_FFN = 56 * 1024 * 1024

_BF16 = jnp.bfloat16
_F32 = jnp.float32


def _params(*sem, vmem_limit=VMEM_LIMIT):
    return pltpu.CompilerParams(dimension_semantics=sem, vmem_limit_bytes=vmem_limit)


def _rmsnorm(x, g):
    return x * lax.rsqrt(jnp.mean(x * x, axis=-1, keepdims=True) + EPS) * g


def _dot(a, b):
    return jnp.dot(a, b, preferred_element_type=_F32)


def _dot_nt(a, b):
    return lax.dot_general(a, b, (((1,), (1,)), ((), ())), preferred_element_type=_F32)


def _dot_tn(a, b):
    return lax.dot_general(a, b, (((0,), (0,)), ((), ())), preferred_element_type=_F32)


def _norm_proj_kernel(x_ref, g_ref, w_ref, z_ref, h_ref):
    first = pl.program_id(1) == 0

    @pl.when(first)
    def _():
        tm = x_ref.shape[0]
        part = tm // NORM_PARTS if tm % (NORM_PARTS * 16) == 0 else tm
        for a in range(tm // part):
            rows = slice(a * part, (a + 1) * part)
            h = _rmsnorm(x_ref[rows, :], g_ref[...]).astype(_BF16)
            h_ref[rows, :] = h
            z_ref[rows, :] = _dot(h, w_ref[...]).astype(z_ref.dtype)

    @pl.when(jnp.logical_not(first))
    def _():
        z_ref[...] = _dot(h_ref[...], w_ref[...]).astype(z_ref.dtype)


def _norm_proj(x, g, w, out_dtype, *, tm, tn=COL_BLOCK):
    n, d = x.shape
    d_out = w.shape[1]
    return pl.pallas_call(
        _norm_proj_kernel,
        grid=(n // tm, d_out // tn),
        in_specs=[
            pl.BlockSpec((tm, d), lambda i, j: (i, 0)),
            pl.BlockSpec((1, d), lambda i, j: (0, 0)),
            pl.BlockSpec((d, tn), lambda i, j: (0, j)),
        ],
        out_specs=pl.BlockSpec((tm, tn), lambda i, j: (i, j)),
        out_shape=jax.ShapeDtypeStruct((n, d_out), out_dtype),
        scratch_shapes=[pltpu.VMEM((tm, d), _BF16)],
        compiler_params=_params("parallel", "arbitrary"),
        name="norm_proj",
    )(x, g, w)


def _norm_proj_att_kernel(x_ref, g_ref, w_ref, z_ref, h_ref, stage_ref, stage2_ref):
    j = pl.program_id(1)
    part = ATT_TILE // PROJ_PARTS
    for group, (_, dil) in enumerate(ATT_GROUPS):
        @pl.when(j == group)
        def _(group=group, dil=dil):
            if group == 0:
                units = [(a, which) for a in range(PROJ_PARTS) for which in range(3)]
            else:
                units = [(a, which) for which in range(3) for a in range(PROJ_PARTS)]
            for a, which in units:
                rows = slice(a * part, (a + 1) * part)
                col0 = which * GROUP_W
                if group == 0 and which == 0:
                    h_ref[rows, :] = _rmsnorm(x_ref[rows, :], g_ref[...]).astype(_BF16)
                res = _dot(h_ref[rows, :], w_ref[:, col0:col0 + GROUP_W])
                if dil == 1:
                    z_ref[rows, col0:col0 + GROUP_W] = res.astype(_BF16)
                    continue
                per, sub = ATT_TILE // dil, part // dil
                for c in range(HPG):
                    stage_ref[c, rows, :] = res[:, c * ATT_HEAD_DIM:(c + 1) * ATT_HEAD_DIM]
                if dil > SPLIT_STRIDE:
                    seg = part // SPLIT_STRIDE
                    for c in range(HPG):
                        for lo in range(SPLIT_STRIDE):
                            stage2_ref[c, lo * seg:(lo + 1) * seg, :] = (
                                stage_ref[c, pl.ds(a * part + lo, seg, stride=SPLIT_STRIDE), :])
                for r in range(dil):
                    for c in range(HPG):
                        cols = slice(col0 + c * ATT_HEAD_DIM, col0 + (c + 1) * ATT_HEAD_DIM)
                        if dil > SPLIT_STRIDE:
                            lo, hi = r % SPLIT_STRIDE, r // SPLIT_STRIDE
                            piece = stage2_ref[c, pl.ds(lo * seg + hi, sub, stride=dil // SPLIT_STRIDE), :]
                        else:
                            piece = stage_ref[c, pl.ds(a * part + r, sub, stride=dil), :]
                        z_ref[r * per + a * sub:r * per + (a + 1) * sub, cols] = piece.astype(_BF16)


def _norm_proj_att(x, g, w):
    n, d = x.shape
    d_out = w.shape[1]
    return pl.pallas_call(
        _norm_proj_att_kernel,
        grid=(n // ATT_TILE, d_out // (3 * GROUP_W)),
        in_specs=[
            pl.BlockSpec((ATT_TILE, d), lambda i, j: (i, 0)),
            pl.BlockSpec((1, d), lambda i, j: (0, 0)),
            pl.BlockSpec((d, 3 * GROUP_W), lambda i, j: (0, j)),
        ],
        out_specs=pl.BlockSpec((ATT_TILE, 3 * GROUP_W), lambda i, j: (i, j)),
        out_shape=jax.ShapeDtypeStruct((n, d_out), _BF16),
        scratch_shapes=[pltpu.VMEM((ATT_TILE, d), _BF16), pltpu.VMEM((HPG, ATT_TILE, ATT_HEAD_DIM), _F32),
                        pltpu.VMEM((HPG, ATT_TILE // PROJ_PARTS, ATT_HEAD_DIM), _F32)],
        compiler_params=_params("parallel", "arbitrary"),
        name="norm_proj_att",
    )(x, g, w)


def _kv_window_kernel(x_ref, g_ref, wk_ref, wv_ref, ko_ref, vo_ref):
    tm = x_ref.shape[0]
    h = _rmsnorm(x_ref[...], g_ref[...]).astype(_BF16)
    for w_ref, o_ref in ((wk_ref, ko_ref), (wv_ref, vo_ref)):
        res = _dot(h, w_ref[...])
        for hd in range(HPG):
            o_ref[pl.ds(hd, tm, stride=HPG), :] = res[:, hd * ATT_HEAD_DIM:(hd + 1) * ATT_HEAD_DIM]


def _kv_window(x3, g, w_grouped, group, keep, prev, layer, depth):
    b, t, d = x3.shape
    tm = min(keep, 512)
    first = (t - keep) // tm
    wspec = lambda which: pl.BlockSpec((d, GROUP_W), lambda bi, i: (0, 3 * group + which))
    ospec = pl.BlockSpec((None, None, tm * HPG, ATT_HEAD_DIM), lambda bi, i: (layer, bi, i, 0))
    oshape = jax.ShapeDtypeStruct((depth, b, keep * HPG, ATT_HEAD_DIM), _F32)
    kernel, args, extra_specs, aliases = _kv_window_kernel, [x3, g, w_grouped, w_grouped], [], {}
    if prev is not None:
        extra_specs = [pl.BlockSpec(memory_space=pl.ANY)] * 2
        aliases = {4: 0, 5: 1}
        kernel = functools.partial(_drop_refs, kernel, 4, 2)
        args += list(prev)
    return pl.pallas_call(
        kernel,
        grid=(b, keep // tm),
        in_specs=[pl.BlockSpec((None, tm, d), lambda bi, i: (bi, first + i, 0)),
                  pl.BlockSpec((1, d), lambda bi, i: (0, 0)), wspec(ATT_K), wspec(ATT_V)] + extra_specs,
        out_specs=[ospec, ospec],
        out_shape=[oshape, oshape],
        input_output_aliases=aliases,
        compiler_params=_params("parallel", "parallel"),
        name="kv_window_g%d" % group,
    )(*args)


def _ret_log_gamma():
    return jnp.log1p(-jnp.exp(jnp.linspace(math.log(1.0 / 32), math.log(1.0 / 512), RET_HEADS))).astype(_F32)


def _retention_tables(chunk):
    lg = _ret_log_gamma()
    pos = jnp.arange(chunk, dtype=_F32)
    diff = pos[:, None] - pos[None, :]
    intra = jnp.where(diff[None] >= 0, jnp.exp(lg[:, None, None] * jnp.maximum(diff, 0.0)[None]), 0.0)
    xi = jnp.exp(lg[:, None] * (pos[None] + 1.0))
    zeta = jnp.exp(lg[:, None] * (chunk - 1.0 - pos)[None])
    decay = jnp.exp(lg * chunk)
    return (intra * (RET_DK ** -0.5),
            jnp.broadcast_to(xi[:, :, None], (RET_HEADS, chunk, RET_DK)),
            jnp.broadcast_to(zeta[:, :, None], (RET_HEADS, chunk, RET_DV)),
            jnp.broadcast_to(decay[:, None, None], (RET_HEADS, 8, RET_DV)))


def _retention_kernel(q_ref, k_ref, v_ref, g_ref, s0_ref, intra_ref, xi_ref, zeta_ref, decay_ref,
                      o_ref, sfin_ref, state, *, chunk, n_chunks):
    j = pl.program_id(1)

    @pl.when(j == 0)
    def _():
        state[...] = s0_ref[...]

    for c in range(n_chunks):
        rows = slice(c * chunk, (c + 1) * chunk)
        for h in range(RET_HEADS):
            qk_cols = slice(h * RET_DK, (h + 1) * RET_DK)
            v_cols = slice(h * RET_DV, (h + 1) * RET_DV)
            q = q_ref[rows, qk_cols]
            kb = k_ref[rows, qk_cols]
            v = v_ref[rows, v_cols]
            g = g_ref[rows, v_cols].astype(_F32)
            r_prev = state[h]
            scores = _dot_nt(q, kb) * intra_ref[h]
            o = _dot(scores.astype(_BF16), v)
            o = o + _dot((q * xi_ref[h]).astype(_BF16), r_prev.astype(_BF16))
            u = _dot_tn(kb, (v * zeta_ref[h]).astype(_BF16)) * (RET_DK ** -0.5)
            state[h] = decay_ref[h, 0:1, :] * r_prev + u
            mu = jnp.mean(o, axis=-1, keepdims=True)
            oc = o - mu
            var = jnp.mean(oc * oc, axis=-1, keepdims=True)
            on = oc * lax.rsqrt(var + EPS)
            o_ref[rows, v_cols] = (on * (g * jax.nn.sigmoid(g))).astype(o_ref.dtype)

    @pl.when(j == pl.num_programs(1) - 1)
    def _():
        sfin_ref[...] = state[...]


def _retention(z3, states0, layer0, prev_states, layer, depth, *, chunk, n_chunks):
    b, t, _ = z3.shape
    tc = chunk * n_chunks
    intra, xi, zeta, decay = _retention_tables(chunk)
    const = lambda shape: pl.BlockSpec(shape, lambda bi, j: (0,) * len(shape))
    state_block = (None, None, RET_HEADS, RET_DK, RET_DV)
    kernel = functools.partial(_retention_kernel, chunk=chunk, n_chunks=n_chunks)
    args = [z3, z3, z3, z3, states0, intra, xi, zeta, decay]
    extra_specs, aliases = [], {}
    if prev_states is not None:
        extra_specs = [pl.BlockSpec(memory_space=pl.ANY)]
        aliases = {len(args): 1}
        kernel = functools.partial(_drop_refs, kernel, len(args), 1)
        args.append(prev_states)
    return pl.pallas_call(
        kernel,
        grid=(b, t // tc),
        in_specs=[
            pl.BlockSpec((None, tc, RET_QK_W), lambda bi, j: (bi, j, COL_RQ // RET_QK_W)),
            pl.BlockSpec((None, tc, RET_QK_W), lambda bi, j: (bi, j, COL_RK // RET_QK_W)),
            pl.BlockSpec((None, tc, RET_V_W), lambda bi, j: (bi, j, COL_RV // RET_V_W)),
            pl.BlockSpec((None, tc, RET_V_W), lambda bi, j: (bi, j, COL_RG // RET_V_W)),
            pl.BlockSpec(state_block, lambda bi, j: (layer0, bi, 0, 0, 0)),
            const((RET_HEADS, chunk, chunk)),
            const((RET_HEADS, chunk, RET_DK)),
            const((RET_HEADS, chunk, RET_DV)),
            const((RET_HEADS, 8, RET_DV)),
        ] + extra_specs,
        out_specs=[
            pl.BlockSpec((None, tc, RET_V_W), lambda bi, j: (bi, j, 0)),
            pl.BlockSpec(state_block, lambda bi, j: (layer, bi, 0, 0, 0)),
        ],
        out_shape=[
            jax.ShapeDtypeStruct((b, t, RET_V_W), _BF16),
            jax.ShapeDtypeStruct((depth, b, RET_HEADS, RET_DK, RET_DV), _F32),
        ],
        scratch_shapes=[pltpu.VMEM((RET_HEADS, RET_DK, RET_DV), _F32)],
        input_output_aliases=aliases,
        compiler_params=_params("parallel", "arbitrary"),
        name="retention",
    )(*args)


def _alibi_slope(head):
    return 2.0 ** (-8.0 * (head + 1.0) / ATT_HEADS)


def _pack_head_stats(cols):
    rows = cols[0].shape[0]
    lane = lax.broadcasted_iota(jnp.int32, (rows, LSE_LANES), 1)
    out = jnp.zeros((rows, LSE_LANES), _F32)
    for h, c in enumerate(cols):
        out = jnp.where(lane == h, c, out)
    return out


def _band_attn_kernel(q_ref, k_ref, v_ref, o_ref, lse_ref, kprev, vprev, stats, *, dil, group):
    i = pl.program_id(1)
    h = pl.program_id(2)
    blk = Q_BLOCK
    per = ATT_TILE // dil
    row = lax.broadcasted_iota(jnp.int32, (blk, 2 * blk), 0)
    col = lax.broadcasted_iota(jnp.int32, (blk, 2 * blk), 1)
    delta = row + blk - col
    slope = jnp.float32(0.0)
    for hh in range(HPG):
        slope = jnp.where(h == hh, jnp.float32(_alibi_slope(group * HPG + hh)), slope)
    in_band = (delta >= 0) & (delta <= ATT_TAPS)
    bias = jnp.where(in_band, -slope * (delta * dil).astype(_F32), -jnp.inf)
    bias_first = jnp.where(col >= jnp.where(i > 0, 0, blk), bias, -jnp.inf)
    lane = lax.broadcasted_iota(jnp.int32, (blk, LSE_LANES), 1)
    scale = ATT_HEAD_DIM ** -0.5

    def token_rows(r, sb):
        start = r + sb * blk * dil
        return pl.ds(start, blk, stride=dil) if dil > 1 else pl.ds(start, blk)

    @pl.when(i == 0)
    def _():
        kprev[h] = jnp.zeros((ATT_TILE, ATT_HEAD_DIM), _BF16)
        vprev[h] = jnp.zeros((ATT_TILE, ATT_HEAD_DIM), _BF16)

    for r in range(dil):
        for sb in range(per // blk):
            lo = r * per + sb * blk
            if sb == 0:
                last = slice((r + 1) * per - blk, (r + 1) * per)
                keys = jnp.concatenate([kprev[h, last, :], k_ref[lo:lo + blk, :]], axis=0)
                values = jnp.concatenate([vprev[h, last, :], v_ref[lo:lo + blk, :]], axis=0)
            else:
                keys = k_ref[lo - blk:lo + blk, :]
                values = v_ref[lo - blk:lo + blk, :]
            s = _dot_nt(q_ref[lo:lo + blk, :], keys) * scale + (bias_first if sb == 0 else bias)
            m = jnp.max(s, axis=-1, keepdims=True)
            e = jnp.exp(s - m)
            den = jnp.sum(e, axis=-1, keepdims=True)
            o_ref[token_rows(r, sb), :] = _dot((e / den).astype(_BF16), values)
            prev = jnp.where(h == 0, 0.0, stats[lo:lo + blk, :])
            stats[lo:lo + blk, :] = jnp.where(lane == h, m + jnp.log(den), prev)

    kprev[h] = k_ref[...]
    vprev[h] = v_ref[...]

    @pl.when(h == HPG - 1)
    def _():
        for r in range(dil):
            for sb in range(per // blk):
                lo = r * per + sb * blk
                lse_ref[token_rows(r, sb), :] = stats[lo:lo + blk, :]


def _band_attention(z3, group):
    b, t, _ = z3.shape
    _, dil = ATT_GROUPS[group]
    tiles = t // ATT_TILE
    spec = lambda which: pl.BlockSpec((None, ATT_TILE, ATT_HEAD_DIM),
                                      lambda bi, i, h: (bi, i, (3 * group + which) * HPG + h))
    carry = pltpu.VMEM((HPG, ATT_TILE, ATT_HEAD_DIM), _BF16)
    return pl.pallas_call(
        functools.partial(_band_attn_kernel, dil=dil, group=group),
        grid=(b, tiles, HPG),
        in_specs=[spec(ATT_Q), spec(ATT_K), spec(ATT_V)],
        out_specs=[
            pl.BlockSpec((None, ATT_TILE, ATT_HEAD_DIM), lambda bi, i, h: (h, bi * tiles + i, 0)),
            pl.BlockSpec((ATT_TILE, LSE_LANES), lambda bi, i, h: (bi * tiles + i, 0)),
        ],
        out_shape=[
            jax.ShapeDtypeStruct((HPG, b * t, ATT_HEAD_DIM), _F32),
            jax.ShapeDtypeStruct((b * t, LSE_LANES), _F32),
        ],
        scratch_shapes=[carry, carry, pltpu.VMEM((ATT_TILE, LSE_LANES), _F32)],
        compiler_params=_params("parallel", "arbitrary", "arbitrary"),
        name="band_attention_g%d" % group,
    )(z3, z3, z3)


def _step_attn_kernel(q_ref, kn_ref, vn_ref, kc_ref, vc_ref, o_ref, lse_ref, ko_ref, vo_ref, *,
                      t_new, cache_len, dil, group):
    n_res = min(dil, t_new)
    taps = cache_len // dil
    rows_all = HPG * t_new
    head_cols = [slice(h * ATT_HEAD_DIM, (h + 1) * ATT_HEAD_DIM) for h in range(HPG)]
    head_rows = [slice(h * t_new, (h + 1) * t_new) for h in range(HPG)]
    q = [q_ref[:, c].astype(_BF16) for c in head_cols]
    k_new = kn_ref[...]
    v_new = vn_ref[...]

    def cache_taps(ref, r, h):
        return ref[pl.ds(r * HPG + h, taps, stride=HPG * dil), :].astype(_BF16)

    scale = ATT_HEAD_DIM ** -0.5
    log2_dil = dil.bit_length() - 1
    log2_new = t_new.bit_length() - 1

    def row_terms(width):
        row_id = lax.broadcasted_iota(jnp.int32, (rows_all, width), 0)
        slope = jnp.zeros((rows_all, width), _F32)
        for h in range(HPG):
            slope = jnp.where((row_id >> log2_new) == h, _alibi_slope(group * HPG + h), slope)
        return row_id & (t_new - 1), slope

    query, slope = row_terms(taps)
    residue = query & (dil - 1)
    taps_back = taps + (query >> log2_dil) - lax.broadcasted_iota(jnp.int32, (rows_all, taps), 1)
    s_cache = None
    for r in range(n_res):
        s_r = jnp.concatenate([_dot_nt(q[h], cache_taps(kc_ref, r, h)) for h in range(HPG)], axis=0)
        s_cache = s_r if s_cache is None else jnp.where(residue == r, s_r, s_cache)
    s_cache = s_cache * scale - slope * (taps_back << log2_dil).astype(_F32)
    s_cache = jnp.where(taps_back <= ATT_TAPS, s_cache, -jnp.inf)

    query_n, slope_n = row_terms(t_new)
    back = query_n - lax.broadcasted_iota(jnp.int32, (rows_all, t_new), 1)
    s_new = jnp.concatenate([_dot_nt(q[h], k_new[:, head_cols[h]].astype(_BF16)) for h in range(HPG)], axis=0)
    s_new = s_new * scale - slope_n * back.astype(_F32)
    s_new = jnp.where((back >= 0) & ((back & (dil - 1)) == 0), s_new, -jnp.inf)

    m = jnp.maximum(jnp.max(s_cache, axis=-1, keepdims=True), jnp.max(s_new, axis=-1, keepdims=True))
    e_cache = jnp.exp(s_cache - m)
    e_new = jnp.exp(s_new - m)
    den = jnp.sum(e_cache, axis=-1, keepdims=True) + jnp.sum(e_new, axis=-1, keepdims=True)
    p_cache = e_cache / den
    p_new = (e_new / den).astype(_BF16)
    lse = m + jnp.log(den)

    for h in range(HPG):
        out = _dot(p_new[head_rows[h]], v_new[:, head_cols[h]].astype(_BF16))
        for r in range(n_res):
            p_r = p_cache[head_rows[h]]
            if n_res > 1:
                p_r = jnp.where(residue[head_rows[h]] == r, p_r, 0.0)
            out = out + _dot(p_r.astype(_BF16), cache_taps(vc_ref, r, h))
        o_ref[h] = out
    lse_ref[...] = _pack_head_stats([lse[rows] for rows in head_rows])

    keep = (cache_len - t_new) * HPG
    ko_ref[0:keep, :] = kc_ref[t_new * HPG:, :]
    vo_ref[0:keep, :] = vc_ref[t_new * HPG:, :]
    for h in range(HPG):
        ko_ref[pl.ds(keep + h, t_new, stride=HPG), :] = k_new[:, head_cols[h]]
        vo_ref[pl.ds(keep + h, t_new, stride=HPG), :] = v_new[:, head_cols[h]]


def _step_attention(z3, cache_k, cache_v, prev_k, prev_v, layer, group):
    b, t_new, _ = z3.shape
    cache_len = cache_k.shape[2] // HPG
    _, dil = ATT_GROUPS[group]
    cq, ck, cv = (3 * group + which for which in (ATT_Q, ATT_K, ATT_V))
    zspec = lambda c: pl.BlockSpec((None, t_new, GROUP_W), lambda bi: (bi, 0, c))
    cache_spec = pl.BlockSpec((None, None, cache_len * HPG, ATT_HEAD_DIM), lambda bi: (layer, bi, 0, 0))
    kernel = functools.partial(_step_attn_kernel, t_new=t_new, cache_len=cache_len, dil=dil, group=group)
    in_specs = [zspec(cq), zspec(ck), zspec(cv), cache_spec, cache_spec]
    args = [z3, z3, z3, cache_k, cache_v]
    aliases = {}
    if prev_k is not None:
        in_specs += [pl.BlockSpec(memory_space=pl.ANY)] * 2
        args += [prev_k, prev_v]
        aliases = {5: 2, 6: 3}
        kernel = functools.partial(_drop_refs, kernel, 5, 2)
    cache_shape = jax.ShapeDtypeStruct(cache_k.shape, cache_k.dtype)
    return pl.pallas_call(
        kernel,
        grid=(b,),
        in_specs=in_specs,
        out_specs=[
            pl.BlockSpec((HPG, t_new, ATT_HEAD_DIM), lambda bi: (0, bi, 0)),
            pl.BlockSpec((t_new, LSE_LANES), lambda bi: (bi, 0)),
            cache_spec, cache_spec,
        ],
        out_shape=[
            jax.ShapeDtypeStruct((HPG, b * t_new, ATT_HEAD_DIM), _F32),
            jax.ShapeDtypeStruct((b * t_new, LSE_LANES), _F32),
            cache_shape, cache_shape,
        ],
        input_output_aliases=aliases,
        compiler_params=_params("parallel"),
        name="step_attention_g%d" % group,
    )(*args)


def _drop_refs(kernel, start, count, *refs):
    return kernel(*refs[:start], *refs[start + count:])


def _mix_ffn_kernel(x_ref, oret_ref, og0_ref, og1_ref, og2_ref, l0_ref, l1_ref, l2_ref, gates_ref,
                    wret_ref, watt_ref, wout_ref, g_ref, wup_ref, wdown_ref, p_ref, wple_ref, wgate_ref, gfin_ref,
                    y_ref, h_ref, acc_ref, *, final_norm):
    j = pl.program_id(1)

    @pl.when(j == 0)
    def _():
        tm = x_ref.shape[0]
        part = tm // MERGE_PARTS if tm % (MERGE_PARTS * 16) == 0 else tm
        for a in range(tm // part):
            rows = slice(a * part, (a + 1) * part)
            l0, l1, l2 = l0_ref[rows, :], l1_ref[rows, :], l2_ref[rows, :]
            mx = jnp.maximum(jnp.maximum(l0, l1), l2)
            e0, e1, e2 = jnp.exp(l0 - mx), jnp.exp(l1 - mx), jnp.exp(l2 - mx)
            tot = e0 + e1 + e2
            w0, w1, w2 = e0 / tot, e1 / tot, e2 / tot
            heads = []
            for h in range(HPG):
                heads.append(w0[:, h:h + 1] * og0_ref[h, rows, :] + w1[:, h:h + 1] * og1_ref[h, rows, :]
                             + w2[:, h:h + 1] * og2_ref[h, rows, :])
            o_att = jnp.concatenate(heads, axis=-1).astype(_BF16)
            br_ret = _dot(oret_ref[rows, :], wret_ref[...])
            br_att = _dot(o_att, watt_ref[...])
            ga = gates_ref[rows, 0:D_MODEL].astype(_F32)
            gb = gates_ref[rows, D_MODEL:2 * D_MODEL].astype(_F32)
            mix = jax.nn.sigmoid(ga) * br_ret + jax.nn.sigmoid(gb) * br_att
            x_mid = x_ref[rows, :] + _dot(mix.astype(_BF16), wout_ref[...])
            y_ref[rows, :] = x_mid
            h_ref[rows, :] = _rmsnorm(x_mid, g_ref[...]).astype(_BF16)
        acc_ref[...] = jnp.zeros_like(acc_ref)

    u = jnp.maximum(_dot(h_ref[...], wup_ref[...]), 0.0)
    acc_ref[...] += _dot((u * u).astype(_BF16), wdown_ref[...])

    @pl.when(j == pl.num_programs(1) - 1)
    def _():
        x = y_ref[...] + acc_ref[...]
        gate = jax.nn.sigmoid(_dot(x.astype(_BF16), wgate_ref[...]))
        x = x + _dot(p_ref[...].astype(_BF16), wple_ref[...]) * gate
        if final_norm:
            x = _rmsnorm(x, gfin_ref[...])
        y_ref[...] = x


def _mix_ffn(x, o_ret, ogs, lses, z_mix, p_all, layer, lw, g_final, *, tm, tf, final_norm):
    n, d = x.shape
    row = lambda w: pl.BlockSpec((tm, w), lambda i, j: (i, 0))
    heads = pl.BlockSpec((HPG, tm, ATT_HEAD_DIM), lambda i, j: (0, i, 0))
    once = lambda a: pl.BlockSpec(a.shape, lambda i, j: (0, 0), pipeline_mode=pl.Buffered(1))
    weights = [lw[k] for k in ("w_ret_br", "w_att_br", "w_out", "norm_ffn")]
    tail = [lw["w_ple"], lw["w_ple_gate"], g_final]
    return pl.pallas_call(
        functools.partial(_mix_ffn_kernel, final_norm=final_norm),
        grid=(n // tm, D_FF // tf),
        in_specs=[row(d), row(RET_V_W), heads, heads, heads,
                  row(LSE_LANES), row(LSE_LANES), row(LSE_LANES), row(2 * D_MODEL)]
        + [once(w) for w in weights]
        + [pl.BlockSpec((d, tf), lambda i, j: (0, j)),
           pl.BlockSpec((tf, d), lambda i, j: (j, 0)),
           pl.BlockSpec((None, tm, D_PLE), lambda i, j: (layer, i, 0))]
        + [once(w) for w in tail],
        out_specs=row(d),
        out_shape=jax.ShapeDtypeStruct((n, d), _F32),
        scratch_shapes=[pltpu.VMEM((tm, d), _BF16), pltpu.VMEM((tm, d), _F32)],
        compiler_params=_params("parallel", "arbitrary", vmem_limit=VMEM_LIMIT_MIX_FFN),
        name="mix_ffn",
    )(x, o_ret, *ogs, *lses, z_mix, *weights, lw["w_up"], lw["w_down"], p_all, *tail)


def _layer_weights(norm_mix, w_in, w_ret_br, w_att_br, w_out, norm_ffn, w_up, w_down, w_ple, w_ple_gate, i):
    w = w_in[i]
    att_start = 2 * RET_QK_W + 2 * RET_V_W
    w_mix = jnp.concatenate([w[:, D_IN - 2 * D_MODEL:], w[:, :att_start]], axis=1)
    return dict(
        norm_mix=norm_mix[i][None, :], w_in_mix=w_mix.astype(_BF16),
        w_in_att_grouped=jnp.concatenate(
            [w[:, att_start + which * ATT_W + g * GROUP_W:att_start + which * ATT_W + (g + 1) * GROUP_W]
             for g in range(N_GROUPS) for which in range(3)], axis=1).astype(_BF16),
        w_ret_br=w_ret_br[i].astype(_BF16), w_att_br=w_att_br[i].astype(_BF16), w_out=w_out[i].astype(_BF16),
        norm_ffn=norm_ffn[i][None, :], w_up=w_up[i].astype(_BF16), w_down=w_down[i].astype(_BF16),
        w_ple=w_ple[i].astype(_BF16), w_ple_gate=w_ple_gate[i].astype(_BF16))


def kernel(x_prompt, x_sample, cache_win_k0, cache_win_v0, cache_win_k1, cache_win_v1, cache_win_k2, cache_win_v2,
           state_ret, p_prompt, p_sample, norm_mix, w_in, w_ret_br, w_att_br, w_out, norm_ffn, w_up, w_down,
           w_ple, w_ple_gate, norm_final):
    depth = w_in.shape[0]
    bp, tp, d = x_prompt.shape
    bs, ts, _ = x_sample.shape
    xp = x_prompt.reshape(bp * tp, d)
    xs = x_sample.reshape(bs * ts, d)
    g_final = norm_final[None, :]
    pp = p_prompt.reshape(depth, bp * tp, D_PLE)
    ps = p_sample.reshape(depth, bs * ts, D_PLE)
    pos_head_rows = lambda c: c.reshape(c.shape[:2] + (c.shape[2] * HPG, ATT_HEAD_DIM))
    caches_k = [pos_head_rows(c) for c in (cache_win_k0, cache_win_k1, cache_win_k2)]
    caches_v = [pos_head_rows(c) for c in (cache_win_v0, cache_win_v1, cache_win_v2)]
    new_k = [None] * N_GROUPS
    new_v = [None] * N_GROUPS
    windows = [None] * N_GROUPS
    prompt_ret = sample_ret = None
    zero_state = jnp.zeros((1, bp, RET_HEADS, RET_DK, RET_DV), _F32)
    sample_chunk = math.gcd(ts, RET_CHUNK)

    for i in range(depth):
        lw = _layer_weights(norm_mix, w_in, w_ret_br, w_att_br, w_out, norm_ffn, w_up, w_down, w_ple, w_ple_gate, i)
        last = i == depth - 1

        z_mix = _norm_proj(xp, lw["norm_mix"], lw["w_in_mix"], _BF16, tm=2048, tn=1024)
        z_att = _norm_proj_att(xp, lw["norm_mix"], lw["w_in_att_grouped"]).reshape(bp, tp, 3 * ATT_W)
        o_ret, prompt_ret = _retention(z_mix.reshape(bp, tp, MIX_W), zero_state, 0, prompt_ret, i, depth,
                                       chunk=RET_CHUNK, n_chunks=4)
        ogs, lses = zip(*[_band_attention(z_att, g) for g in range(N_GROUPS)])
        for g, (window, _) in enumerate(ATT_GROUPS):
            windows[g] = _kv_window(xp.reshape(bp, tp, d), lw["norm_mix"], lw["w_in_att_grouped"], g, min(window, tp),
                                    windows[g], i, depth)
        xp = _mix_ffn(xp, o_ret.reshape(bp * tp, RET_V_W), ogs, lses, z_mix, pp, i, lw, g_final,
                      tm=512, tf=2048, final_norm=last)

        z_mix = _norm_proj(xs, lw["norm_mix"], lw["w_in_mix"], _BF16, tm=bs * ts, tn=MIX_W // 2)
        z3 = _norm_proj(xs, lw["norm_mix"], lw["w_in_att_grouped"], _F32, tm=bs * ts, tn=ATT_W).reshape(bs, ts, 3 * ATT_W)
        o_ret, sample_ret = _retention(z_mix.reshape(bs, ts, MIX_W), state_ret, i, sample_ret, i, depth,
                                       chunk=sample_chunk, n_chunks=ts // sample_chunk)
        ogs, lses = [], []
        for g in range(N_GROUPS):
            o_g, lse_g, new_k[g], new_v[g] = _step_attention(z3, caches_k[g], caches_v[g], new_k[g], new_v[g], i, g)
            ogs.append(o_g)
            lses.append(lse_g)
        xs = _mix_ffn(xs, o_ret.reshape(bs * ts, RET_V_W), ogs, lses, z_mix, ps, i, lw, g_final,
                      tm=bs * ts, tf=2048, final_norm=last)

    as_heads = lambda a: a.reshape(a.shape[:2] + (a.shape[2] // HPG, HPG, ATT_HEAD_DIM))
    prompt_windows = [as_heads(a) for kv in windows for a in kv]
    sample_windows = [as_heads(a) for g in range(N_GROUPS) for a in (new_k[g], new_v[g])]
    return (xp.reshape(bp, tp, d), xs.reshape(bs, ts, d), *prompt_windows, prompt_ret, *sample_windows, sample_ret)
```

```python
import functools
import math

import jax
import jax.numpy as jnp
import numpy as np
from jax import lax
from jax.experimental import pallas as pl
from jax.experimental.pallas import tpu as pltpu

D_MODEL = 1024
D_PLE = 256
RET_HEADS = 4
RET_DK = 128
RET_DV = 256
RET_CHUNK = 128
ATT_GROUPS = ((128, 1), (512, 4), (2048, 16))
N_GROUPS = 3
HPG = 4
ATT_HEAD_DIM = 128
ATT_HEADS = N_GROUPS * HPG
ATT_TAPS = 128
Q_BLOCK = 128
ATT_TILE = 2048
PROJ_PARTS = 4
NORM_PARTS = 4
MERGE_PARTS = 1
SPLIT_STRIDE = 4
D_FF = 4 * D_MODEL
EPS = 1e-6

RET_QK_W = RET_HEADS * RET_DK
RET_V_W = RET_HEADS * RET_DV
GROUP_W = HPG * ATT_HEAD_DIM
ATT_W = ATT_HEADS * ATT_HEAD_DIM
D_IN = 2 * RET_QK_W + 2 * RET_V_W + 3 * ATT_W + 2 * D_MODEL

MIX_W = 2 * D_MODEL + 2 * RET_QK_W + 2 * RET_V_W
COL_GATES = 0
COL_RQ = 2 * D_MODEL
COL_RK = COL_RQ + RET_QK_W
COL_RV = COL_RK + RET_QK_W
COL_RG = COL_RV + RET_V_W
ATT_Q, ATT_K, ATT_V = 0, 1, 2
COL_BLOCK = 512

LSE_LANES = 128
STEP_CACHE_ROWS = 8192
RET_ROWS_PER_STEP = 32
VMEM_LIMIT = 48 * 1024 * 1024
VMEM_LIMIT_MIX_FFN = 56 * 1024 * 1024

_BF16 = jnp.bfloat16
_F32 = jnp.float32


def _params(*sem, vmem_limit=VMEM_LIMIT):
    return pltpu.CompilerParams(dimension_semantics=sem, vmem_limit_bytes=vmem_limit)


def _rmsnorm(x, g):
    return x * lax.rsqrt(jnp.mean(x * x, axis=-1, keepdims=True) + EPS) * g


def _dot(a, b):
    return jnp.dot(a, b, preferred_element_type=_F32)


def _dot_nt(a, b):
    return lax.dot_general(a, b, (((1,), (1,)), ((), ())), preferred_element_type=_F32)


def _dot_tn(a, b):
    return lax.dot_general(a, b, (((0,), (0,)), ((), ())), preferred_element_type=_F32)


def _norm_proj_kernel(x_ref, g_ref, w_ref, z_ref, h_ref):
    first = pl.program_id(1) == 0

    @pl.when(first)
    def _():
        tm = x_ref.shape[0]
        part = tm // NORM_PARTS if tm % (NORM_PARTS * 16) == 0 else tm
        for a in range(tm // part):
            rows = slice(a * part, (a + 1) * part)
            h = _rmsnorm(x_ref[rows, :], g_ref[...]).astype(_BF16)
            h_ref[rows, :] = h
            z_ref[rows, :] = _dot(h, w_ref[...]).astype(z_ref.dtype)

    @pl.when(jnp.logical_not(first))
    def _():
        z_ref[...] = _dot(h_ref[...], w_ref[...]).astype(z_ref.dtype)


def _norm_proj(x, g, w, out_dtype, *, tm, tn=COL_BLOCK):
    n, d = x.shape
    d_out = w.shape[1]
    return pl.pallas_call(
        _norm_proj_kernel,
        grid=(n // tm, d_out // tn),
        in_specs=[
            pl.BlockSpec((tm, d), lambda i, j: (i, 0)),
            pl.BlockSpec((1, d), lambda i, j: (0, 0)),
            pl.BlockSpec((d, tn), lambda i, j: (0, j)),
        ],
        out_specs=pl.BlockSpec((tm, tn), lambda i, j: (i, j)),
        out_shape=jax.ShapeDtypeStruct((n, d_out), out_dtype),
        scratch_shapes=[pltpu.VMEM((tm, d), _BF16)],
        compiler_params=_params("parallel", "arbitrary"),
        name="norm_proj",
    )(x, g, w)


def _norm_proj_att_kernel(x_ref, g_ref, w_ref, z_ref, h_ref, stage_ref, stage2_ref):
    j = pl.program_id(1)
    part = ATT_TILE // PROJ_PARTS
    for group, (_, dil) in enumerate(ATT_GROUPS):
        @pl.when(j == group)
        def _(group=group, dil=dil):
            if group == 0:
                units = [(a, which) for a in range(PROJ_PARTS) for which in range(3)]
            else:
                units = [(a, which) for which in range(3) for a in range(PROJ_PARTS)]
            for a, which in units:
                rows = slice(a * part, (a + 1) * part)
                col0 = which * GROUP_W
                if group == 0 and which == 0:
                    h_ref[rows, :] = _rmsnorm(x_ref[rows, :], g_ref[...]).astype(_BF16)
                res = _dot(h_ref[rows, :], w_ref[:, col0:col0 + GROUP_W])
                if dil == 1:
                    z_ref[rows, col0:col0 + GROUP_W] = res.astype(_BF16)
                    continue
                per, sub = ATT_TILE // dil, part // dil
                for c in range(HPG):
                    stage_ref[c, rows, :] = res[:, c * ATT_HEAD_DIM:(c + 1) * ATT_HEAD_DIM]
                if dil > SPLIT_STRIDE:
                    seg = part // SPLIT_STRIDE
                    for c in range(HPG):
                        for lo in range(SPLIT_STRIDE):
                            stage2_ref[c, lo * seg:(lo + 1) * seg, :] = (
                                stage_ref[c, pl.ds(a * part + lo, seg, stride=SPLIT_STRIDE), :])
                for r in range(dil):
                    for c in range(HPG):
                        cols = slice(col0 + c * ATT_HEAD_DIM, col0 + (c + 1) * ATT_HEAD_DIM)
                        if dil > SPLIT_STRIDE:
                            lo, hi = r % SPLIT_STRIDE, r // SPLIT_STRIDE
                            piece = stage2_ref[c, pl.ds(lo * seg + hi, sub, stride=dil // SPLIT_STRIDE), :]
                        else:
                            piece = stage_ref[c, pl.ds(a * part + r, sub, stride=dil), :]
                        z_ref[r * per + a * sub:r * per + (a + 1) * sub, cols] = piece.astype(_BF16)


def _norm_proj_att(x, g, w):
    n, d = x.shape
    d_out = w.shape[1]
    return pl.pallas_call(
        _norm_proj_att_kernel,
        grid=(n // ATT_TILE, d_out // (3 * GROUP_W)),
        in_specs=[
            pl.BlockSpec((ATT_TILE, d), lambda i, j: (i, 0)),
            pl.BlockSpec((1, d), lambda i, j: (0, 0)),
            pl.BlockSpec((d, 3 * GROUP_W), lambda i, j: (0, j)),
        ],
        out_specs=pl.BlockSpec((ATT_TILE, 3 * GROUP_W), lambda i, j: (i, j)),
        out_shape=jax.ShapeDtypeStruct((n, d_out), _BF16),
        scratch_shapes=[pltpu.VMEM((ATT_TILE, d), _BF16), pltpu.VMEM((HPG, ATT_TILE, ATT_HEAD_DIM), _F32),
                        pltpu.VMEM((HPG, ATT_TILE // PROJ_PARTS, ATT_HEAD_DIM), _F32)],
        compiler_params=_params("parallel", "arbitrary"),
        name="norm_proj_att",
    )(x, g, w)


def _kv_window_kernel(x_ref, g_ref, wk_ref, wv_ref, ko_ref, vo_ref):
    tm = x_ref.shape[0]
    h = _rmsnorm(x_ref[...], g_ref[...]).astype(_BF16)
    for w_ref, o_ref in ((wk_ref, ko_ref), (wv_ref, vo_ref)):
        res = _dot(h, w_ref[...])
        for hd in range(HPG):
            o_ref[pl.ds(hd, tm, stride=HPG), :] = res[:, hd * ATT_HEAD_DIM:(hd + 1) * ATT_HEAD_DIM]


def _kv_window(x3, g, w_grouped, group, keep, prev, layer, depth):
    b, t, d = x3.shape
    tm = min(keep, 512)
    first = (t - keep) // tm
    wspec = lambda which: pl.BlockSpec((d, GROUP_W), lambda bi, i: (0, 3 * group + which))
    ospec = pl.BlockSpec((None, None, tm * HPG, ATT_HEAD_DIM), lambda bi, i: (layer, bi, i, 0))
    oshape = jax.ShapeDtypeStruct((depth, b, keep * HPG, ATT_HEAD_DIM), _F32)
    kernel, args, extra_specs, aliases = _kv_window_kernel, [x3, g, w_grouped, w_grouped], [], {}
    if prev is not None:
        extra_specs = [pl.BlockSpec(memory_space=pl.ANY)] * 2
        aliases = {4: 0, 5: 1}
        kernel = functools.partial(_drop_refs, kernel, 4, 2)
        args += list(prev)
    return pl.pallas_call(
        kernel,
        grid=(b, keep // tm),
        in_specs=[pl.BlockSpec((None, tm, d), lambda bi, i: (bi, first + i, 0)),
                  pl.BlockSpec((1, d), lambda bi, i: (0, 0)), wspec(ATT_K), wspec(ATT_V)] + extra_specs,
        out_specs=[ospec, ospec],
        out_shape=[oshape, oshape],
        input_output_aliases=aliases,
        compiler_params=_params("parallel", "parallel"),
        name="kv_window_g%d" % group,
    )(*args)


def _ret_log_gamma():
    return jnp.log1p(-jnp.exp(jnp.linspace(math.log(1.0 / 32), math.log(1.0 / 512), RET_HEADS))).astype(_F32)


def _retention_tables(chunk):
    lg = _ret_log_gamma()
    pos = jnp.arange(chunk, dtype=_F32)
    diff = pos[:, None] - pos[None, :]
    intra = jnp.where(diff[None] >= 0, jnp.exp(lg[:, None, None] * jnp.maximum(diff, 0.0)[None]), 0.0)
    xi = jnp.exp(lg[:, None] * (pos[None] + 1.0))
    zeta = jnp.exp(lg[:, None] * (chunk - 1.0 - pos)[None])
    decay = jnp.exp(lg * chunk)
    return (intra * (RET_DK ** -0.5),
            jnp.broadcast_to(xi[:, :, None], (RET_HEADS, chunk, RET_DK)),
            jnp.broadcast_to(zeta[:, :, None], (RET_HEADS, chunk, RET_DV)),
            jnp.broadcast_to(decay[:, None, None], (RET_HEADS, 8, RET_DV)))


def _retention_kernel(q_ref, k_ref, v_ref, g_ref, s0_ref, intra_ref, xi_ref, zeta_ref, decay_ref,
                      o_ref, sfin_ref, state, *, batches, **kw):
    for bb in range(batches):
        _retention_one(q_ref.at[bb], k_ref.at[bb], v_ref.at[bb], g_ref.at[bb], s0_ref.at[bb], intra_ref, xi_ref,
                       zeta_ref, decay_ref, o_ref.at[bb], sfin_ref.at[bb], state.at[bb], **kw)


def _retention_one(q_ref, k_ref, v_ref, g_ref, s0_ref, intra_ref, xi_ref, zeta_ref, decay_ref,
                   o_ref, sfin_ref, state, *, chunk, n_chunks):
    j = pl.program_id(1)

    @pl.when(j == 0)
    def _():
        state[...] = s0_ref[...]

    for c in range(n_chunks):
        rows = slice(c * chunk, (c + 1) * chunk)
        for h in range(RET_HEADS):
            qk_cols = slice(h * RET_DK, (h + 1) * RET_DK)
            v_cols = slice(h * RET_DV, (h + 1) * RET_DV)
            q = q_ref[rows, qk_cols]
            kb = k_ref[rows, qk_cols]
            v = v_ref[rows, v_cols]
            g = g_ref[rows, v_cols].astype(_F32)
            r_prev = state[h]
            scores = _dot_nt(q, kb) * intra_ref[h]
            o = _dot(scores.astype(_BF16), v)
            o = o + _dot((q * xi_ref[h]).astype(_BF16), r_prev.astype(_BF16))
            u = _dot_tn(kb, (v * zeta_ref[h]).astype(_BF16)) * (RET_DK ** -0.5)
            state[h] = decay_ref[h, 0:1, :] * r_prev + u
            mu = jnp.mean(o, axis=-1, keepdims=True)
            oc = o - mu
            var = jnp.mean(oc * oc, axis=-1, keepdims=True)
            on = oc * lax.rsqrt(var + EPS)
            o_ref[rows, v_cols] = (on * (g * jax.nn.sigmoid(g))).astype(o_ref.dtype)

    @pl.when(j == pl.num_programs(1) - 1)
    def _():
        sfin_ref[...] = state[...]


def _retention(z3, states0, layer0, prev_states, layer, depth, *, chunk, n_chunks):
    b, t, _ = z3.shape
    tc = chunk * n_chunks
    batches = max(1, min(b, RET_ROWS_PER_STEP // tc))
    intra, xi, zeta, decay = _retention_tables(chunk)
    const = lambda shape: pl.BlockSpec(shape, lambda bi, j: (0,) * len(shape))
    state_block = (None, batches, RET_HEADS, RET_DK, RET_DV)
    kernel = functools.partial(_retention_kernel, batches=batches, chunk=chunk, n_chunks=n_chunks)
    args = [z3, z3, z3, z3, states0, intra, xi, zeta, decay]
    extra_specs, aliases = [], {}
    if prev_states is not None:
        extra_specs = [pl.BlockSpec(memory_space=pl.ANY)]
        aliases = {len(args): 1}
        kernel = functools.partial(_drop_refs, kernel, len(args), 1)
        args.append(prev_states)
    return pl.pallas_call(
        kernel,
        grid=(b // batches, t // tc),
        in_specs=[
            pl.BlockSpec((batches, tc, RET_QK_W), lambda bi, j: (bi, j, COL_RQ // RET_QK_W)),
            pl.BlockSpec((batches, tc, RET_QK_W), lambda bi, j: (bi, j, COL_RK // RET_QK_W)),
            pl.BlockSpec((batches, tc, RET_V_W), lambda bi, j: (bi, j, COL_RV // RET_V_W)),
            pl.BlockSpec((batches, tc, RET_V_W), lambda bi, j: (bi, j, COL_RG // RET_V_W)),
            pl.BlockSpec(state_block, lambda bi, j: (layer0, bi, 0, 0, 0)),
            const((RET_HEADS, chunk, chunk)),
            const((RET_HEADS, chunk, RET_DK)),
            const((RET_HEADS, chunk, RET_DV)),
            const((RET_HEADS, 8, RET_DV)),
        ] + extra_specs,
        out_specs=[
            pl.BlockSpec((batches, tc, RET_V_W), lambda bi, j: (bi, j, 0)),
            pl.BlockSpec(state_block, lambda bi, j: (layer, bi, 0, 0, 0)),
        ],
        out_shape=[
            jax.ShapeDtypeStruct((b, t, RET_V_W), _BF16),
            jax.ShapeDtypeStruct((depth, b, RET_HEADS, RET_DK, RET_DV), _F32),
        ],
        scratch_shapes=[pltpu.VMEM((batches, RET_HEADS, RET_DK, RET_DV), _F32)],
        input_output_aliases=aliases,
        compiler_params=_params("parallel", "arbitrary"),
        name="retention",
    )(*args)


def _alibi_slope(head):
    return 2.0 ** (-8.0 * (head + 1.0) / ATT_HEADS)


def _pack_head_stats(cols):
    rows = cols[0].shape[0]
    lane = lax.broadcasted_iota(jnp.int32, (rows, LSE_LANES), 1)
    out = jnp.zeros((rows, LSE_LANES), _F32)
    for h, c in enumerate(cols):
        out = jnp.where(lane == h, c, out)
    return out


def _band_attn_kernel(q_ref, k_ref, v_ref, o_ref, lse_ref, kprev, vprev, stats, *, dil, group):
    i = pl.program_id(1)
    h = pl.program_id(2)
    blk = Q_BLOCK
    per = ATT_TILE // dil
    row = lax.broadcasted_iota(jnp.int32, (blk, 2 * blk), 0)
    col = lax.broadcasted_iota(jnp.int32, (blk, 2 * blk), 1)
    delta = row + blk - col
    slope = jnp.float32(0.0)
    for hh in range(HPG):
        slope = jnp.where(h == hh, jnp.float32(_alibi_slope(group * HPG + hh)), slope)
    in_band = (delta >= 0) & (delta <= ATT_TAPS)
    bias = jnp.where(in_band, -slope * (delta * dil).astype(_F32), -jnp.inf)
    bias_first = jnp.where(col >= jnp.where(i > 0, 0, blk), bias, -jnp.inf)
    lane = lax.broadcasted_iota(jnp.int32, (blk, LSE_LANES), 1)
    scale = ATT_HEAD_DIM ** -0.5

    def token_rows(r, sb):
        start = r + sb * blk * dil
        return pl.ds(start, blk, stride=dil) if dil > 1 else pl.ds(start, blk)

    @pl.when(i == 0)
    def _():
        kprev[h] = jnp.zeros((ATT_TILE, ATT_HEAD_DIM), _BF16)
        vprev[h] = jnp.zeros((ATT_TILE, ATT_HEAD_DIM), _BF16)

    for r in range(dil):
        for sb in range(per // blk):
            lo = r * per + sb * blk
            if sb == 0:
                last = slice((r + 1) * per - blk, (r + 1) * per)
                keys = jnp.concatenate([kprev[h, last, :], k_ref[lo:lo + blk, :]], axis=0)
                values = jnp.concatenate([vprev[h, last, :], v_ref[lo:lo + blk, :]], axis=0)
            else:
                keys = k_ref[lo - blk:lo + blk, :]
                values = v_ref[lo - blk:lo + blk, :]
            s = _dot_nt(q_ref[lo:lo + blk, :], keys) * scale + (bias_first if sb == 0 else bias)
            m = jnp.max(s, axis=-1, keepdims=True)
            e = jnp.exp(s - m)
            den = jnp.sum(e, axis=-1, keepdims=True)
            o_ref[token_rows(r, sb), :] = _dot((e / den).astype(_BF16), values)
            prev = jnp.where(h == 0, 0.0, stats[lo:lo + blk, :])
            stats[lo:lo + blk, :] = jnp.where(lane == h, m + jnp.log(den), prev)

    kprev[h] = k_ref[...]
    vprev[h] = v_ref[...]

    @pl.when(h == HPG - 1)
    def _():
        for r in range(dil):
            for sb in range(per // blk):
                lo = r * per + sb * blk
                lse_ref[token_rows(r, sb), :] = stats[lo:lo + blk, :]


def _band_attention(z3, group):
    b, t, _ = z3.shape
    _, dil = ATT_GROUPS[group]
    tiles = t // ATT_TILE
    spec = lambda which: pl.BlockSpec((None, ATT_TILE, ATT_HEAD_DIM),
                                      lambda bi, i, h: (bi, i, (3 * group + which) * HPG + h))
    carry = pltpu.VMEM((HPG, ATT_TILE, ATT_HEAD_DIM), _BF16)
    return pl.pallas_call(
        functools.partial(_band_attn_kernel, dil=dil, group=group),
        grid=(b, tiles, HPG),
        in_specs=[spec(ATT_Q), spec(ATT_K), spec(ATT_V)],
        out_specs=[
            pl.BlockSpec((None, ATT_TILE, ATT_HEAD_DIM), lambda bi, i, h: (h, bi * tiles + i, 0)),
            pl.BlockSpec((ATT_TILE, LSE_LANES), lambda bi, i, h: (bi * tiles + i, 0)),
        ],
        out_shape=[
            jax.ShapeDtypeStruct((HPG, b * t, ATT_HEAD_DIM), _F32),
            jax.ShapeDtypeStruct((b * t, LSE_LANES), _F32),
        ],
        scratch_shapes=[carry, carry, pltpu.VMEM((ATT_TILE, LSE_LANES), _F32)],
        compiler_params=_params("parallel", "arbitrary", "arbitrary"),
        name="band_attention_g%d" % group,
    )(z3, z3, z3)


def _step_attn_kernel(q_ref, kn_ref, vn_ref, kc_ref, vc_ref, o_ref, lse_ref, ko_ref, vo_ref, *, batches, t_new, **kw):
    for bb in range(batches):
        tokens = pl.ds(bb * t_new, t_new)
        _step_attn_one(q_ref.at[bb], kn_ref.at[bb], vn_ref.at[bb], kc_ref.at[bb], vc_ref.at[bb],
                       o_ref.at[:, tokens, :], lse_ref.at[tokens, :], ko_ref.at[bb], vo_ref.at[bb], t_new=t_new, **kw)


def _step_attn_one(q_ref, kn_ref, vn_ref, kc_ref, vc_ref, o_ref, lse_ref, ko_ref, vo_ref, *,
                   t_new, cache_len, dil, group):
    n_res = min(dil, t_new)
    taps = cache_len // dil
    rows_all = HPG * t_new
    head_cols = [slice(h * ATT_HEAD_DIM, (h + 1) * ATT_HEAD_DIM) for h in range(HPG)]
    head_rows = [slice(h * t_new, (h + 1) * t_new) for h in range(HPG)]
    q = [q_ref[:, c].astype(_BF16) for c in head_cols]
    k_new = kn_ref[...]
    v_new = vn_ref[...]

    def cache_taps(ref, r, h):
        return ref[pl.ds(r * HPG + h, taps, stride=HPG * dil), :].astype(_BF16)

    scale = ATT_HEAD_DIM ** -0.5
    log2_dil = dil.bit_length() - 1
    log2_new = t_new.bit_length() - 1

    def row_terms(width):
        row_id = lax.broadcasted_iota(jnp.int32, (rows_all, width), 0)
        slope = jnp.zeros((rows_all, width), _F32)
        for h in range(HPG):
            slope = jnp.where((row_id >> log2_new) == h, _alibi_slope(group * HPG + h), slope)
        return row_id & (t_new - 1), slope

    query, slope = row_terms(taps)
    residue = query & (dil - 1)
    taps_back = taps + (query >> log2_dil) - lax.broadcasted_iota(jnp.int32, (rows_all, taps), 1)
    s_cache = None
    for r in range(n_res):
        s_r = jnp.concatenate([_dot_nt(q[h], cache_taps(kc_ref, r, h)) for h in range(HPG)], axis=0)
        s_cache = s_r if s_cache is None else jnp.where(residue == r, s_r, s_cache)
    s_cache = s_cache * scale - slope * (taps_back << log2_dil).astype(_F32)
    s_cache = jnp.where(taps_back <= ATT_TAPS, s_cache, -jnp.inf)

    query_n, slope_n = row_terms(t_new)
    back = query_n - lax.broadcasted_iota(jnp.int32, (rows_all, t_new), 1)
    s_new = jnp.concatenate([_dot_nt(q[h], k_new[:, head_cols[h]].astype(_BF16)) for h in range(HPG)], axis=0)
    s_new = s_new * scale - slope_n * back.astype(_F32)
    s_new = jnp.where((back >= 0) & ((back & (dil - 1)) == 0), s_new, -jnp.inf)

    m = jnp.maximum(jnp.max(s_cache, axis=-1, keepdims=True), jnp.max(s_new, axis=-1, keepdims=True))
    e_cache = jnp.exp(s_cache - m)
    e_new = jnp.exp(s_new - m)
    den = jnp.sum(e_cache, axis=-1, keepdims=True) + jnp.sum(e_new, axis=-1, keepdims=True)
    p_cache = e_cache / den
    p_new = (e_new / den).astype(_BF16)
    lse = m + jnp.log(den)

    for h in range(HPG):
        out = _dot(p_new[head_rows[h]], v_new[:, head_cols[h]].astype(_BF16))
        for r in range(n_res):
            p_r = p_cache[head_rows[h]]
            if n_res > 1:
                p_r = jnp.where(residue[head_rows[h]] == r, p_r, 0.0)
            out = out + _dot(p_r.astype(_BF16), cache_taps(vc_ref, r, h))
        o_ref[h] = out
    lse_ref[...] = _pack_head_stats([lse[rows] for rows in head_rows])

    keep = (cache_len - t_new) * HPG
    ko_ref[0:keep, :] = kc_ref[t_new * HPG:, :]
    vo_ref[0:keep, :] = vc_ref[t_new * HPG:, :]
    for h in range(HPG):
        ko_ref[pl.ds(keep + h, t_new, stride=HPG), :] = k_new[:, head_cols[h]]
        vo_ref[pl.ds(keep + h, t_new, stride=HPG), :] = v_new[:, head_cols[h]]


def _step_attention(z3, cache_k, cache_v, prev_k, prev_v, layer, group):
    b, t_new, _ = z3.shape
    cache_len = cache_k.shape[2] // HPG
    _, dil = ATT_GROUPS[group]
    cq, ck, cv = (3 * group + which for which in (ATT_Q, ATT_K, ATT_V))
    batches = max(1, min(b, STEP_CACHE_ROWS // (cache_len * HPG)))
    zspec = lambda c: pl.BlockSpec((batches, t_new, GROUP_W), lambda bi: (bi, 0, c))
    cache_spec = pl.BlockSpec((None, batches, cache_len * HPG, ATT_HEAD_DIM), lambda bi: (layer, bi, 0, 0))
    kernel = functools.partial(_step_attn_kernel, batches=batches, t_new=t_new, cache_len=cache_len, dil=dil,
                               group=group)
    in_specs = [zspec(cq), zspec(ck), zspec(cv), cache_spec, cache_spec]
    args = [z3, z3, z3, cache_k, cache_v]
    aliases = {}
    if prev_k is not None:
        in_specs += [pl.BlockSpec(memory_space=pl.ANY)] * 2
        args += [prev_k, prev_v]
        aliases = {5: 2, 6: 3}
        kernel = functools.partial(_drop_refs, kernel, 5, 2)
    cache_shape = jax.ShapeDtypeStruct(cache_k.shape, cache_k.dtype)
    return pl.pallas_call(
        kernel,
        grid=(b // batches,),
        in_specs=in_specs,
        out_specs=[
            pl.BlockSpec((HPG, batches * t_new, ATT_HEAD_DIM), lambda bi: (0, bi, 0)),
            pl.BlockSpec((batches * t_new, LSE_LANES), lambda bi: (bi, 0)),
            cache_spec, cache_spec,
        ],
        out_shape=[
            jax.ShapeDtypeStruct((HPG, b * t_new, ATT_HEAD_DIM), _F32),
            jax.ShapeDtypeStruct((b * t_new, LSE_LANES), _F32),
            cache_shape, cache_shape,
        ],
        input_output_aliases=aliases,
        compiler_params=_params("parallel"),
        name="step_attention_g%d" % group,
    )(*args)


def _drop_refs(kernel, start, count, *refs):
    return kernel(*refs[:start], *refs[start + count:])


def _mix_ffn_kernel(x_ref, oret_ref, og0_ref, og1_ref, og2_ref, l0_ref, l1_ref, l2_ref, gates_ref,
                    wret_ref, watt_ref, wout_ref, g_ref, wup_ref, wdown_ref, p_ref, wple_ref, wgate_ref, gfin_ref,
                    y_ref, h_ref, acc_ref, *, final_norm):
    j = pl.program_id(1)

    @pl.when(j == 0)
    def _():
        tm = x_ref.shape[0]
        part = tm // MERGE_PARTS if tm % (MERGE_PARTS * 16) == 0 else tm
        for a in range(tm // part):
            rows = slice(a * part, (a + 1) * part)
            l0, l1, l2 = l0_ref[rows, :], l1_ref[rows, :], l2_ref[rows, :]
            mx = jnp.maximum(jnp.maximum(l0, l1), l2)
            e0, e1, e2 = jnp.exp(l0 - mx), jnp.exp(l1 - mx), jnp.exp(l2 - mx)
            tot = e0 + e1 + e2
            w0, w1, w2 = e0 / tot, e1 / tot, e2 / tot
            heads = []
            for h in range(HPG):
                heads.append(w0[:, h:h + 1] * og0_ref[h, rows, :] + w1[:, h:h + 1] * og1_ref[h, rows, :]
                             + w2[:, h:h + 1] * og2_ref[h, rows, :])
            o_att = jnp.concatenate(heads, axis=-1).astype(_BF16)
            br_ret = _dot(oret_ref[rows, :], wret_ref[...])
            br_att = _dot(o_att, watt_ref[...])
            ga = gates_ref[rows, 0:D_MODEL].astype(_F32)
            gb = gates_ref[rows, D_MODEL:2 * D_MODEL].astype(_F32)
            mix = jax.nn.sigmoid(ga) * br_ret + jax.nn.sigmoid(gb) * br_att
            x_mid = x_ref[rows, :] + _dot(mix.astype(_BF16), wout_ref[...])
            y_ref[rows, :] = x_mid
            h_ref[rows, :] = _rmsnorm(x_mid, g_ref[...]).astype(_BF16)
        acc_ref[...] = jnp.zeros_like(acc_ref)

    u = jnp.maximum(_dot(h_ref[...], wup_ref[...]), 0.0)
    acc_ref[...] += _dot((u * u).astype(_BF16), wdown_ref[...])

    @pl.when(j == pl.num_programs(1) - 1)
    def _():
        x = y_ref[...] + acc_ref[...]
        gate = jax.nn.sigmoid(_dot(x.astype(_BF16), wgate_ref[...]))
        x = x + _dot(p_ref[...].astype(_BF16), wple_ref[...]) * gate
        if final_norm:
            x = _rmsnorm(x, gfin_ref[...])
        y_ref[...] = x


def _mix_ffn(x, o_ret, ogs, lses, z_mix, p_all, layer, lw, g_final, *, tm, tf, final_norm):
    n, d = x.shape
    row = lambda w: pl.BlockSpec((tm, w), lambda i, j: (i, 0))
    heads = pl.BlockSpec((HPG, tm, ATT_HEAD_DIM), lambda i, j: (0, i, 0))
    once = lambda a: pl.BlockSpec(a.shape, lambda i, j: (0, 0), pipeline_mode=pl.Buffered(1))
    weights = [lw[k] for k in ("w_ret_br", "w_att_br", "w_out", "norm_ffn")]
    tail = [lw["w_ple"], lw["w_ple_gate"], g_final]
    return pl.pallas_call(
        functools.partial(_mix_ffn_kernel, final_norm=final_norm),
        grid=(n // tm, D_FF // tf),
        in_specs=[row(d), row(RET_V_W), heads, heads, heads,
                  row(LSE_LANES), row(LSE_LANES), row(LSE_LANES), row(2 * D_MODEL)]
        + [once(w) for w in weights]
        + [pl.BlockSpec((d, tf), lambda i, j: (0, j)),
           pl.BlockSpec((tf, d), lambda i, j: (j, 0)),
           pl.BlockSpec((None, tm, D_PLE), lambda i, j: (layer, i, 0))]
        + [once(w) for w in tail],
        out_specs=row(d),
        out_shape=jax.ShapeDtypeStruct((n, d), _F32),
        scratch_shapes=[pltpu.VMEM((tm, d), _BF16), pltpu.VMEM((tm, d), _F32)],
        compiler_params=_params("parallel", "arbitrary", vmem_limit=VMEM_LIMIT_MIX_FFN),
        name="mix_ffn",
    )(x, o_ret, *ogs, *lses, z_mix, *weights, lw["w_up"], lw["w_down"], p_all, *tail)


def _layer_weights(norm_mix, w_in, w_ret_br, w_att_br, w_out, norm_ffn, w_up, w_down, w_ple, w_ple_gate, i):
    w = w_in[i]
    att_start = 2 * RET_QK_W + 2 * RET_V_W
    w_mix = jnp.concatenate([w[:, D_IN - 2 * D_MODEL:], w[:, :att_start]], axis=1)
    return dict(
        norm_mix=norm_mix[i][None, :], w_in_mix=w_mix.astype(_BF16),
        w_in_att_grouped=jnp.concatenate(
            [w[:, att_start + which * ATT_W + g * GROUP_W:att_start + which * ATT_W + (g + 1) * GROUP_W]
             for g in range(N_GROUPS) for which in range(3)], axis=1).astype(_BF16),
        w_ret_br=w_ret_br[i].astype(_BF16), w_att_br=w_att_br[i].astype(_BF16), w_out=w_out[i].astype(_BF16),
        norm_ffn=norm_ffn[i][None, :], w_up=w_up[i].astype(_BF16), w_down=w_down[i].astype(_BF16),
        w_ple=w_ple[i].astype(_BF16), w_ple_gate=w_ple_gate[i].astype(_BF16))


def kernel(x_prompt, x_sample, cache_win_k0, cache_win_v0, cache_win_k1, cache_win_v1, cache_win_k2, cache_win_v2,
           state_ret, p_prompt, p_sample, norm_mix, w_in, w_ret_br, w_att_br, w_out, norm_ffn, w_up, w_down,
           w_ple, w_ple_gate, norm_final):
    depth = w_in.shape[0]
    bp, tp, d = x_prompt.shape
    bs, ts, _ = x_sample.shape
    xp = x_prompt.reshape(bp * tp, d)
    xs = x_sample.reshape(bs * ts, d)
    g_final = norm_final[None, :]
    pp = p_prompt.reshape(depth, bp * tp, D_PLE)
    ps = p_sample.reshape(depth, bs * ts, D_PLE)
    pos_head_rows = lambda c: c.reshape(c.shape[:2] + (c.shape[2] * HPG, ATT_HEAD_DIM))
    caches_k = [pos_head_rows(c) for c in (cache_win_k0, cache_win_k1, cache_win_k2)]
    caches_v = [pos_head_rows(c) for c in (cache_win_v0, cache_win_v1, cache_win_v2)]
    new_k = [None] * N_GROUPS
    new_v = [None] * N_GROUPS
    windows = [None] * N_GROUPS
    prompt_ret = sample_ret = None
    zero_state = jnp.zeros((1, bp, RET_HEADS, RET_DK, RET_DV), _F32)
    sample_chunk = math.gcd(ts, RET_CHUNK)

    for i in range(depth):
        lw = _layer_weights(norm_mix, w_in, w_ret_br, w_att_br, w_out, norm_ffn, w_up, w_down, w_ple, w_ple_gate, i)
        last = i == depth - 1

        z_mix = _norm_proj(xp, lw["norm_mix"], lw["w_in_mix"], _BF16, tm=2048, tn=1024)
        z_att = _norm_proj_att(xp, lw["norm_mix"], lw["w_in_att_grouped"]).reshape(bp, tp, 3 * ATT_W)
        o_ret, prompt_ret = _retention(z_mix.reshape(bp, tp, MIX_W), zero_state, 0, prompt_ret, i, depth,
                                       chunk=RET_CHUNK, n_chunks=4)
        ogs, lses = zip(*[_band_attention(z_att, g) for g in range(N_GROUPS)])
        for g, (window, _) in enumerate(ATT_GROUPS):
            windows[g] = _kv_window(xp.reshape(bp, tp, d), lw["norm_mix"], lw["w_in_att_grouped"], g, min(window, tp),
                                    windows[g], i, depth)
        xp = _mix_ffn(xp, o_ret.reshape(bp * tp, RET_V_W), ogs, lses, z_mix, pp, i, lw, g_final,
                      tm=512, tf=2048, final_norm=last)

        z_mix = _norm_proj(xs, lw["norm_mix"], lw["w_in_mix"], _BF16, tm=bs * ts, tn=MIX_W // 2)
        z3 = _norm_proj(xs, lw["norm_mix"], lw["w_in_att_grouped"], _F32, tm=bs * ts, tn=ATT_W).reshape(bs, ts, 3 * ATT_W)
        o_ret, sample_ret = _retention(z_mix.reshape(bs, ts, MIX_W), state_ret, i, sample_ret, i, depth,
                                       chunk=sample_chunk, n_chunks=ts // sample_chunk)
        ogs, lses = [], []
        for g in range(N_GROUPS):
            o_g, lse_g, new_k[g], new_v[g] = _step_attention(z3, caches_k[g], caches_v[g], new_k[g], new_v[g], i, g)
            ogs.append(o_g)
            lses.append(lse_g)
        xs = _mix_ffn(xs, o_ret.reshape(bs * ts, RET_V_W), ogs, lses, z_mix, ps, i, lw, g_final,
                      tm=bs * ts, tf=2048, final_norm=last)

    as_heads = lambda a: a.reshape(a.shape[:2] + (a.shape[2] // HPG, HPG, ATT_HEAD_DIM))
    prompt_windows = [as_heads(a) for kv in windows for a in kv]
    sample_windows = [as_heads(a) for g in range(N_GROUPS) for a in (new_k[g], new_v[g])]
    return (xp.reshape(bp, tp, d), xs.reshape(bs, ts, d), *prompt_windows, prompt_ret, *sample_windows, sample_ret)
```

```python
import functools
import math

import jax
import jax.numpy as jnp
import numpy as np
from jax import lax
from jax.experimental import pallas as pl
from jax.experimental.pallas import tpu as pltpu

D_MODEL = 1024
D_PLE = 256
RET_HEADS = 4
RET_DK = 128
RET_DV = 256
RET_CHUNK = 128
ATT_GROUPS = ((128, 1), (512, 4), (2048, 16))
N_GROUPS = 3
HPG = 4
ATT_HEAD_DIM = 128
ATT_HEADS = N_GROUPS * HPG
ATT_TAPS = 128
Q_BLOCK = 128
ATT_TILE = 2048
PROJ_PARTS = 4
NORM_PARTS = 4
MERGE_PARTS = 1
SPLIT_STRIDE = 4
D_FF = 4 * D_MODEL
EPS = 1e-6

RET_QK_W = RET_HEADS * RET_DK
RET_V_W = RET_HEADS * RET_DV
GROUP_W = HPG * ATT_HEAD_DIM
ATT_W = ATT_HEADS * ATT_HEAD_DIM
D_IN = 2 * RET_QK_W + 2 * RET_V_W + 3 * ATT_W + 2 * D_MODEL

MIX_W = 2 * D_MODEL + 2 * RET_QK_W + 2 * RET_V_W
COL_GATES = 0
COL_RQ = 2 * D_MODEL
COL_RK = COL_RQ + RET_QK_W
COL_RV = COL_RK + RET_QK_W
COL_RG = COL_RV + RET_V_W
ATT_Q, ATT_K, ATT_V = 0, 1, 2
COL_BLOCK = 512

LSE_LANES = 128
STEP_CACHE_ROWS = 8192
RET_ROWS_PER_STEP = 32
VMEM_LIMIT = 48 * 1024 * 1024
VMEM_LIMIT_MIX_FFN = 56 * 1024 * 1024

_BF16 = jnp.bfloat16
_F32 = jnp.float32


def _params(*sem, vmem_limit=VMEM_LIMIT):
    return pltpu.CompilerParams(dimension_semantics=sem, vmem_limit_bytes=vmem_limit)


def _rmsnorm(x, g):
    return x * lax.rsqrt(jnp.mean(x * x, axis=-1, keepdims=True) + EPS) * g


def _dot(a, b):
    return jnp.dot(a, b, preferred_element_type=_F32)


def _dot_nt(a, b):
    return lax.dot_general(a, b, (((1,), (1,)), ((), ())), preferred_element_type=_F32)


def _dot_tn(a, b):
    return lax.dot_general(a, b, (((0,), (0,)), ((), ())), preferred_element_type=_F32)


def _norm_proj_kernel(x_ref, g_ref, w_ref, z_ref, h_ref, *, gate_blocks):
    j = pl.program_id(1)
    tm = x_ref.shape[0]
    part = tm // NORM_PARTS if tm % (NORM_PARTS * 16) == 0 else tm

    def by_parts(with_norm, with_sigmoid):
        for a in range(tm // part):
            rows = slice(a * part, (a + 1) * part)
            if with_norm:
                h_ref[rows, :] = _rmsnorm(x_ref[rows, :], g_ref[...]).astype(_BF16)
            z = _dot(h_ref[rows, :], w_ref[...])
            z_ref[rows, :] = (jax.nn.sigmoid(z) if with_sigmoid else z).astype(z_ref.dtype)

    @pl.when(j == 0)
    def _():
        by_parts(True, gate_blocks > 0)

    if gate_blocks > 1:
        @pl.when((j > 0) & (j < gate_blocks))
        def _():
            by_parts(False, True)

    @pl.when(j >= max(gate_blocks, 1))
    def _():
        z_ref[...] = _dot(h_ref[...], w_ref[...]).astype(z_ref.dtype)


def _norm_proj(x, g, w, out_dtype, *, tm, tn=COL_BLOCK, gate_cols=0):
    n, d = x.shape
    d_out = w.shape[1]
    return pl.pallas_call(
        functools.partial(_norm_proj_kernel, gate_blocks=gate_cols // tn),
        grid=(n // tm, d_out // tn),
        in_specs=[
            pl.BlockSpec((tm, d), lambda i, j: (i, 0)),
            pl.BlockSpec((1, d), lambda i, j: (0, 0)),
            pl.BlockSpec((d, tn), lambda i, j: (0, j)),
        ],
        out_specs=pl.BlockSpec((tm, tn), lambda i, j: (i, j)),
        out_shape=jax.ShapeDtypeStruct((n, d_out), out_dtype),
        scratch_shapes=[pltpu.VMEM((tm, d), _BF16)],
        compiler_params=_params("parallel", "arbitrary"),
        name="norm_proj",
    )(x, g, w)


def _norm_proj_att_kernel(x_ref, g_ref, w_ref, z_ref, h_ref, stage_ref, stage2_ref):
    j = pl.program_id(1)
    part = ATT_TILE // PROJ_PARTS
    for group, (_, dil) in enumerate(ATT_GROUPS):
        @pl.when(j == group)
        def _(group=group, dil=dil):
            if group == 0:
                units = [(a, which) for a in range(PROJ_PARTS) for which in range(3)]
            else:
                units = [(a, which) for which in range(3) for a in range(PROJ_PARTS)]
            for a, which in units:
                rows = slice(a * part, (a + 1) * part)
                col0 = which * GROUP_W
                if group == 0 and which == 0:
                    h_ref[rows, :] = _rmsnorm(x_ref[rows, :], g_ref[...]).astype(_BF16)
                res = _dot(h_ref[rows, :], w_ref[:, col0:col0 + GROUP_W])
                if dil == 1:
                    z_ref[rows, col0:col0 + GROUP_W] = res.astype(_BF16)
                    continue
                per, sub = ATT_TILE // dil, part // dil
                for c in range(HPG):
                    stage_ref[c, rows, :] = res[:, c * ATT_HEAD_DIM:(c + 1) * ATT_HEAD_DIM]
                if dil > SPLIT_STRIDE:
                    seg = part // SPLIT_STRIDE
                    for c in range(HPG):
                        for lo in range(SPLIT_STRIDE):
                            stage2_ref[c, lo * seg:(lo + 1) * seg, :] = (
                                stage_ref[c, pl.ds(a * part + lo, seg, stride=SPLIT_STRIDE), :])
                for r in range(dil):
                    for c in range(HPG):
                        cols = slice(col0 + c * ATT_HEAD_DIM, col0 + (c + 1) * ATT_HEAD_DIM)
                        if dil > SPLIT_STRIDE:
                            lo, hi = r % SPLIT_STRIDE, r // SPLIT_STRIDE
                            piece = stage2_ref[c, pl.ds(lo * seg + hi, sub, stride=dil // SPLIT_STRIDE), :]
                        else:
                            piece = stage_ref[c, pl.ds(a * part + r, sub, stride=dil), :]
                        z_ref[r * per + a * sub:r * per + (a + 1) * sub, cols] = piece.astype(_BF16)


def _norm_proj_att(x, g, w):
    n, d = x.shape
    d_out = w.shape[1]
    return pl.pallas_call(
        _norm_proj_att_kernel,
        grid=(n // ATT_TILE, d_out // (3 * GROUP_W)),
        in_specs=[
            pl.BlockSpec((ATT_TILE, d), lambda i, j: (i, 0)),
            pl.BlockSpec((1, d), lambda i, j: (0, 0)),
            pl.BlockSpec((d, 3 * GROUP_W), lambda i, j: (0, j)),
        ],
        out_specs=pl.BlockSpec((ATT_TILE, 3 * GROUP_W), lambda i, j: (i, j)),
        out_shape=jax.ShapeDtypeStruct((n, d_out), _BF16),
        scratch_shapes=[pltpu.VMEM((ATT_TILE, d), _BF16), pltpu.VMEM((HPG, ATT_TILE, ATT_HEAD_DIM), _F32),
                        pltpu.VMEM((HPG, ATT_TILE // PROJ_PARTS, ATT_HEAD_DIM), _F32)],
        compiler_params=_params("parallel", "arbitrary"),
        name="norm_proj_att",
    )(x, g, w)


def _kv_window_kernel(x_ref, g_ref, wk_ref, wv_ref, ko_ref, vo_ref):
    tm = x_ref.shape[0]
    h = _rmsnorm(x_ref[...], g_ref[...]).astype(_BF16)
    for w_ref, o_ref in ((wk_ref, ko_ref), (wv_ref, vo_ref)):
        res = _dot(h, w_ref[...])
        for hd in range(HPG):
            o_ref[pl.ds(hd, tm, stride=HPG), :] = res[:, hd * ATT_HEAD_DIM:(hd + 1) * ATT_HEAD_DIM]


def _kv_window(x3, g, w_grouped, group, keep, prev, layer, depth):
    b, t, d = x3.shape
    tm = min(keep, 512)
    first = (t - keep) // tm
    wspec = lambda which: pl.BlockSpec((d, GROUP_W), lambda bi, i: (0, 3 * group + which))
    ospec = pl.BlockSpec((None, None, tm * HPG, ATT_HEAD_DIM), lambda bi, i: (layer, bi, i, 0))
    oshape = jax.ShapeDtypeStruct((depth, b, keep * HPG, ATT_HEAD_DIM), _F32)
    kernel, args, extra_specs, aliases = _kv_window_kernel, [x3, g, w_grouped, w_grouped], [], {}
    if prev is not None:
        extra_specs = [pl.BlockSpec(memory_space=pl.ANY)] * 2
        aliases = {4: 0, 5: 1}
        kernel = functools.partial(_drop_refs, kernel, 4, 2)
        args += list(prev)
    return pl.pallas_call(
        kernel,
        grid=(b, keep // tm),
        in_specs=[pl.BlockSpec((None, tm, d), lambda bi, i: (bi, first + i, 0)),
                  pl.BlockSpec((1, d), lambda bi, i: (0, 0)), wspec(ATT_K), wspec(ATT_V)] + extra_specs,
        out_specs=[ospec, ospec],
        out_shape=[oshape, oshape],
        input_output_aliases=aliases,
        compiler_params=_params("parallel", "parallel"),
        name="kv_window_g%d" % group,
    )(*args)


def _ret_log_gamma():
    return jnp.log1p(-jnp.exp(jnp.linspace(math.log(1.0 / 32), math.log(1.0 / 512), RET_HEADS))).astype(_F32)


def _retention_tables(chunk):
    lg = _ret_log_gamma()
    pos = jnp.arange(chunk, dtype=_F32)
    diff = pos[:, None] - pos[None, :]
    intra = jnp.where(diff[None] >= 0, jnp.exp(lg[:, None, None] * jnp.maximum(diff, 0.0)[None]), 0.0)
    xi = jnp.exp(lg[:, None] * (pos[None] + 1.0))
    zeta = jnp.exp(lg[:, None] * (chunk - 1.0 - pos)[None])
    decay = jnp.exp(lg * chunk)
    return (intra * (RET_DK ** -0.5),
            jnp.broadcast_to(xi[:, :, None], (RET_HEADS, chunk, RET_DK)),
            jnp.broadcast_to(zeta[:, :, None], (RET_HEADS, chunk, RET_DV)),
            jnp.broadcast_to(decay[:, None, None], (RET_HEADS, 8, RET_DV)))


def _retention_kernel(q_ref, k_ref, v_ref, g_ref, s0_ref, intra_ref, xi_ref, zeta_ref, decay_ref,
                      o_ref, sfin_ref, state, *, batches, **kw):
    for bb in range(batches):
        _retention_one(q_ref.at[bb], k_ref.at[bb], v_ref.at[bb], g_ref.at[bb], s0_ref.at[bb], intra_ref, xi_ref,
                       zeta_ref, decay_ref, o_ref.at[bb], sfin_ref.at[bb], state.at[bb], **kw)


def _retention_one(q_ref, k_ref, v_ref, g_ref, s0_ref, intra_ref, xi_ref, zeta_ref, decay_ref,
                   o_ref, sfin_ref, state, *, chunk, n_chunks):
    j = pl.program_id(1)

    @pl.when(j == 0)
    def _():
        state[...] = s0_ref[...]

    for c in range(n_chunks):
        rows = slice(c * chunk, (c + 1) * chunk)
        for h in range(RET_HEADS):
            qk_cols = slice(h * RET_DK, (h + 1) * RET_DK)
            v_cols = slice(h * RET_DV, (h + 1) * RET_DV)
            q = q_ref[rows, qk_cols]
            kb = k_ref[rows, qk_cols]
            v = v_ref[rows, v_cols]
            g = g_ref[rows, v_cols].astype(_F32)
            r_prev = state[h]
            scores = _dot_nt(q, kb) * intra_ref[h]
            o = _dot(scores.astype(_BF16), v)
            o = o + _dot((q * xi_ref[h]).astype(_BF16), r_prev.astype(_BF16))
            u = _dot_tn(kb, (v * zeta_ref[h]).astype(_BF16)) * (RET_DK ** -0.5)
            state[h] = decay_ref[h, 0:1, :] * r_prev + u
            mu = jnp.mean(o, axis=-1, keepdims=True)
            oc = o - mu
            var = jnp.mean(oc * oc, axis=-1, keepdims=True)
            on = oc * lax.rsqrt(var + EPS)
            o_ref[rows, v_cols] = (on * (g * jax.nn.sigmoid(g))).astype(o_ref.dtype)

    @pl.when(j == pl.num_programs(1) - 1)
    def _():
        sfin_ref[...] = state[...]


def _retention(z3, states0, layer0, prev_states, layer, depth, *, chunk, n_chunks):
    b, t, _ = z3.shape
    tc = chunk * n_chunks
    batches = max(1, min(b, RET_ROWS_PER_STEP // tc))
    intra, xi, zeta, decay = _retention_tables(chunk)
    const = lambda shape: pl.BlockSpec(shape, lambda bi, j: (0,) * len(shape))
    state_block = (None, batches, RET_HEADS, RET_DK, RET_DV)
    kernel = functools.partial(_retention_kernel, batches=batches, chunk=chunk, n_chunks=n_chunks)
    args = [z3, z3, z3, z3, states0, intra, xi, zeta, decay]
    extra_specs, aliases = [], {}
    if prev_states is not None:
        extra_specs = [pl.BlockSpec(memory_space=pl.ANY)]
        aliases = {len(args): 1}
        kernel = functools.partial(_drop_refs, kernel, len(args), 1)
        args.append(prev_states)
    return pl.pallas_call(
        kernel,
        grid=(b // batches, t // tc),
        in_specs=[
            pl.BlockSpec((batches, tc, RET_QK_W), lambda bi, j: (bi, j, COL_RQ // RET_QK_W)),
            pl.BlockSpec((batches, tc, RET_QK_W), lambda bi, j: (bi, j, COL_RK // RET_QK_W)),
            pl.BlockSpec((batches, tc, RET_V_W), lambda bi, j: (bi, j, COL_RV // RET_V_W)),
            pl.BlockSpec((batches, tc, RET_V_W), lambda bi, j: (bi, j, COL_RG // RET_V_W)),
            pl.BlockSpec(state_block, lambda bi, j: (layer0, bi, 0, 0, 0)),
            const((RET_HEADS, chunk, chunk)),
            const((RET_HEADS, chunk, RET_DK)),
            const((RET_HEADS, chunk, RET_DV)),
            const((RET_HEADS, 8, RET_DV)),
        ] + extra_specs,
        out_specs=[
            pl.BlockSpec((batches, tc, RET_V_W), lambda bi, j: (bi, j, 0)),
            pl.BlockSpec(state_block, lambda bi, j: (layer, bi, 0, 0, 0)),
        ],
        out_shape=[
            jax.ShapeDtypeStruct((b, t, RET_V_W), _BF16),
            jax.ShapeDtypeStruct((depth, b, RET_HEADS, RET_DK, RET_DV), _F32),
        ],
        scratch_shapes=[pltpu.VMEM((batches, RET_HEADS, RET_DK, RET_DV), _F32)],
        input_output_aliases=aliases,
        compiler_params=_params("parallel", "arbitrary"),
        name="retention",
    )(*args)


def _alibi_slope(head):
    return 2.0 ** (-8.0 * (head + 1.0) / ATT_HEADS)


def _pack_head_stats(cols):
    rows = cols[0].shape[0]
    lane = lax.broadcasted_iota(jnp.int32, (rows, LSE_LANES), 1)
    out = jnp.zeros((rows, LSE_LANES), _F32)
    for h, c in enumerate(cols):
        out = jnp.where(lane == h, c, out)
    return out


def _band_attn_kernel(q_ref, k_ref, v_ref, o_ref, lse_ref, kprev, vprev, stats, *, dil, group):
    i = pl.program_id(1)
    h = pl.program_id(2)
    blk = Q_BLOCK
    per = ATT_TILE // dil
    row = lax.broadcasted_iota(jnp.int32, (blk, 2 * blk), 0)
    col = lax.broadcasted_iota(jnp.int32, (blk, 2 * blk), 1)
    delta = row + blk - col
    slope = jnp.float32(0.0)
    for hh in range(HPG):
        slope = jnp.where(h == hh, jnp.float32(_alibi_slope(group * HPG + hh)), slope)
    in_band = (delta >= 0) & (delta <= ATT_TAPS)
    bias = jnp.where(in_band, -slope * (delta * dil).astype(_F32), -jnp.inf)
    bias_first = jnp.where(col >= jnp.where(i > 0, 0, blk), bias, -jnp.inf)
    lane = lax.broadcasted_iota(jnp.int32, (blk, LSE_LANES), 1)
    scale = ATT_HEAD_DIM ** -0.5

    def token_rows(r, sb):
        start = r + sb * blk * dil
        return pl.ds(start, blk, stride=dil) if dil > 1 else pl.ds(start, blk)

    @pl.when(i == 0)
    def _():
        kprev[h] = jnp.zeros((ATT_TILE, ATT_HEAD_DIM), _BF16)
        vprev[h] = jnp.zeros((ATT_TILE, ATT_HEAD_DIM), _BF16)

    for r in range(dil):
        for sb in range(per // blk):
            lo = r * per + sb * blk
            if sb == 0:
                last = slice((r + 1) * per - blk, (r + 1) * per)
                keys = jnp.concatenate([kprev[h, last, :], k_ref[lo:lo + blk, :]], axis=0)
                values = jnp.concatenate([vprev[h, last, :], v_ref[lo:lo + blk, :]], axis=0)
            else:
                keys = k_ref[lo - blk:lo + blk, :]
                values = v_ref[lo - blk:lo + blk, :]
            s = _dot_nt(q_ref[lo:lo + blk, :], keys) * scale + (bias_first if sb == 0 else bias)
            m = jnp.max(s, axis=-1, keepdims=True)
            e = jnp.exp(s - m)
            den = jnp.sum(e, axis=-1, keepdims=True)
            o_ref[token_rows(r, sb), :] = _dot((e / den).astype(_BF16), values)
            prev = jnp.where(h == 0, 0.0, stats[lo:lo + blk, :])
            stats[lo:lo + blk, :] = jnp.where(lane == h, m + jnp.log(den), prev)

    kprev[h] = k_ref[...]
    vprev[h] = v_ref[...]

    @pl.when(h == HPG - 1)
    def _():
        for r in range(dil):
            for sb in range(per // blk):
                lo = r * per + sb * blk
                lse_ref[token_rows(r, sb), :] = stats[lo:lo + blk, :]


def _band_attention(z3, group):
    b, t, _ = z3.shape
    _, dil = ATT_GROUPS[group]
    tiles = t // ATT_TILE
    spec = lambda which: pl.BlockSpec((None, ATT_TILE, ATT_HEAD_DIM),
                                      lambda bi, i, h: (bi, i, (3 * group + which) * HPG + h))
    carry = pltpu.VMEM((HPG, ATT_TILE, ATT_HEAD_DIM), _BF16)
    return pl.pallas_call(
        functools.partial(_band_attn_kernel, dil=dil, group=group),
        grid=(b, tiles, HPG),
        in_specs=[spec(ATT_Q), spec(ATT_K), spec(ATT_V)],
        out_specs=[
            pl.BlockSpec((None, ATT_TILE, ATT_HEAD_DIM), lambda bi, i, h: (h, bi * tiles + i, 0)),
            pl.BlockSpec((ATT_TILE, LSE_LANES), lambda bi, i, h: (bi * tiles + i, 0)),
        ],
        out_shape=[
            jax.ShapeDtypeStruct((HPG, b * t, ATT_HEAD_DIM), _F32),
            jax.ShapeDtypeStruct((b * t, LSE_LANES), _F32),
        ],
        scratch_shapes=[carry, carry, pltpu.VMEM((ATT_TILE, LSE_LANES), _F32)],
        compiler_params=_params("parallel", "arbitrary", "arbitrary"),
        name="band_attention_g%d" % group,
    )(z3, z3, z3)


def _step_attn_kernel(q_ref, kn_ref, vn_ref, kc_ref, vc_ref, o_ref, lse_ref, ko_ref, vo_ref, *, batches, t_new, **kw):
    for bb in range(batches):
        tokens = pl.ds(bb * t_new, t_new)
        _step_attn_one(q_ref.at[bb], kn_ref.at[bb], vn_ref.at[bb], kc_ref.at[bb], vc_ref.at[bb],
                       o_ref.at[:, tokens, :], lse_ref.at[tokens, :], ko_ref.at[bb], vo_ref.at[bb], t_new=t_new, **kw)


def _step_attn_one(q_ref, kn_ref, vn_ref, kc_ref, vc_ref, o_ref, lse_ref, ko_ref, vo_ref, *,
                   t_new, cache_len, dil, group):
    n_res = min(dil, t_new)
    taps = cache_len // dil
    rows_all = HPG * t_new
    head_cols = [slice(h * ATT_HEAD_DIM, (h + 1) * ATT_HEAD_DIM) for h in range(HPG)]
    head_rows = [slice(h * t_new, (h + 1) * t_new) for h in range(HPG)]
    q = [q_ref[:, c].astype(_BF16) for c in head_cols]
    k_new = kn_ref[...]
    v_new = vn_ref[...]

    def cache_taps(ref, r, h):
        return ref[pl.ds(r * HPG + h, taps, stride=HPG * dil), :].astype(_BF16)

    scale = ATT_HEAD_DIM ** -0.5
    log2_dil = dil.bit_length() - 1
    log2_new = t_new.bit_length() - 1

    def row_terms(width):
        row_id = lax.broadcasted_iota(jnp.int32, (rows_all, width), 0)
        slope = jnp.zeros((rows_all, width), _F32)
        for h in range(HPG):
            slope = jnp.where((row_id >> log2_new) == h, _alibi_slope(group * HPG + h), slope)
        return row_id & (t_new - 1), slope

    query, slope = row_terms(taps)
    residue = query & (dil - 1)
    taps_back = taps + (query >> log2_dil) - lax.broadcasted_iota(jnp.int32, (rows_all, taps), 1)
    s_cache = None
    for r in range(n_res):
        s_r = jnp.concatenate([_dot_nt(q[h], cache_taps(kc_ref, r, h)) for h in range(HPG)], axis=0)
        s_cache = s_r if s_cache is None else jnp.where(residue == r, s_r, s_cache)
    s_cache = s_cache * scale - slope * (taps_back << log2_dil).astype(_F32)
    s_cache = jnp.where(taps_back <= ATT_TAPS, s_cache, -jnp.inf)

    query_n, slope_n = row_terms(t_new)
    back = query_n - lax.broadcasted_iota(jnp.int32, (rows_all, t_new), 1)
    s_new = jnp.concatenate([_dot_nt(q[h], k_new[:, head_cols[h]].astype(_BF16)) for h in range(HPG)], axis=0)
    s_new = s_new * scale - slope_n * back.astype(_F32)
    s_new = jnp.where((back >= 0) & ((back & (dil - 1)) == 0), s_new, -jnp.inf)

    m = jnp.maximum(jnp.max(s_cache, axis=-1, keepdims=True), jnp.max(s_new, axis=-1, keepdims=True))
    e_cache = jnp.exp(s_cache - m)
    e_new = jnp.exp(s_new - m)
    den = jnp.sum(e_cache, axis=-1, keepdims=True) + jnp.sum(e_new, axis=-1, keepdims=True)
    p_cache = e_cache / den
    p_new = (e_new / den).astype(_BF16)
    lse = m + jnp.log(den)

    for h in range(HPG):
        out = _dot(p_new[head_rows[h]], v_new[:, head_cols[h]].astype(_BF16))
        for r in range(n_res):
            p_r = p_cache[head_rows[h]]
            if n_res > 1:
                p_r = jnp.where(residue[head_rows[h]] == r, p_r, 0.0)
            out = out + _dot(p_r.astype(_BF16), cache_taps(vc_ref, r, h))
        o_ref[h] = out
    lse_ref[...] = _pack_head_stats([lse[rows] for rows in head_rows])

    keep = (cache_len - t_new) * HPG
    ko_ref[0:keep, :] = kc_ref[t_new * HPG:, :]
    vo_ref[0:keep, :] = vc_ref[t_new * HPG:, :]
    for h in range(HPG):
        ko_ref[pl.ds(keep + h, t_new, stride=HPG), :] = k_new[:, head_cols[h]]
        vo_ref[pl.ds(keep + h, t_new, stride=HPG), :] = v_new[:, head_cols[h]]


def _step_attention(z3, cache_k, cache_v, prev_k, prev_v, layer, group):
    b, t_new, _ = z3.shape
    cache_len = cache_k.shape[2] // HPG
    _, dil = ATT_GROUPS[group]
    cq, ck, cv = (3 * group + which for which in (ATT_Q, ATT_K, ATT_V))
    batches = max(1, min(b, STEP_CACHE_ROWS // (cache_len * HPG)))
    zspec = lambda c: pl.BlockSpec((batches, t_new, GROUP_W), lambda bi: (bi, 0, c))
    cache_spec = pl.BlockSpec((None, batches, cache_len * HPG, ATT_HEAD_DIM), lambda bi: (layer, bi, 0, 0))
    kernel = functools.partial(_step_attn_kernel, batches=batches, t_new=t_new, cache_len=cache_len, dil=dil,
                               group=group)
    in_specs = [zspec(cq), zspec(ck), zspec(cv), cache_spec, cache_spec]
    args = [z3, z3, z3, cache_k, cache_v]
    aliases = {}
    if prev_k is not None:
        in_specs += [pl.BlockSpec(memory_space=pl.ANY)] * 2
        args += [prev_k, prev_v]
        aliases = {5: 2, 6: 3}
        kernel = functools.partial(_drop_refs, kernel, 5, 2)
    cache_shape = jax.ShapeDtypeStruct(cache_k.shape, cache_k.dtype)
    return pl.pallas_call(
        kernel,
        grid=(b // batches,),
        in_specs=in_specs,
        out_specs=[
            pl.BlockSpec((HPG, batches * t_new, ATT_HEAD_DIM), lambda bi: (0, bi, 0)),
            pl.BlockSpec((batches * t_new, LSE_LANES), lambda bi: (bi, 0)),
            cache_spec, cache_spec,
        ],
        out_shape=[
            jax.ShapeDtypeStruct((HPG, b * t_new, ATT_HEAD_DIM), _F32),
            jax.ShapeDtypeStruct((b * t_new, LSE_LANES), _F32),
            cache_shape, cache_shape,
        ],
        input_output_aliases=aliases,
        compiler_params=_params("parallel"),
        name="step_attention_g%d" % group,
    )(*args)


def _drop_refs(kernel, start, count, *refs):
    return kernel(*refs[:start], *refs[start + count:])


def _mix_ffn_kernel(x_ref, oret_ref, og0_ref, og1_ref, og2_ref, l0_ref, l1_ref, l2_ref, gates_ref,
                    wret_ref, watt_ref, wout_ref, g_ref, wup_ref, wdown_ref, p_ref, wple_ref, wgate_ref, gfin_ref,
                    y_ref, h_ref, acc_ref, *, final_norm):
    j = pl.program_id(1)

    @pl.when(j == 0)
    def _():
        tm = x_ref.shape[0]
        part = tm // MERGE_PARTS if tm % (MERGE_PARTS * 16) == 0 else tm
        for a in range(tm // part):
            rows = slice(a * part, (a + 1) * part)
            l0, l1, l2 = l0_ref[rows, :], l1_ref[rows, :], l2_ref[rows, :]
            mx = jnp.maximum(jnp.maximum(l0, l1), l2)
            e0, e1, e2 = jnp.exp(l0 - mx), jnp.exp(l1 - mx), jnp.exp(l2 - mx)
            tot = e0 + e1 + e2
            w0, w1, w2 = e0 / tot, e1 / tot, e2 / tot
            heads = []
            for h in range(HPG):
                heads.append(w0[:, h:h + 1] * og0_ref[h, rows, :] + w1[:, h:h + 1] * og1_ref[h, rows, :]
                             + w2[:, h:h + 1] * og2_ref[h, rows, :])
            o_att = jnp.concatenate(heads, axis=-1).astype(_BF16)
            br_ret = _dot(oret_ref[rows, :], wret_ref[...])
            br_att = _dot(o_att, watt_ref[...])
            gate_ret = gates_ref[rows, 0:D_MODEL].astype(_F32)
            gate_att = gates_ref[rows, D_MODEL:2 * D_MODEL].astype(_F32)
            mix = gate_ret * br_ret + gate_att * br_att
            x_mid = x_ref[rows, :] + _dot(mix.astype(_BF16), wout_ref[...])
            y_ref[rows, :] = x_mid
            h_ref[rows, :] = _rmsnorm(x_mid, g_ref[...]).astype(_BF16)
        acc_ref[...] = jnp.zeros_like(acc_ref)

    u = jnp.maximum(_dot(h_ref[...], wup_ref[...]), 0.0)
    acc_ref[...] += _dot((u * u).astype(_BF16), wdown_ref[...])

    @pl.when(j == pl.num_programs(1) - 1)
    def _():
        x = y_ref[...] + acc_ref[...]
        gate = jax.nn.sigmoid(_dot(x.astype(_BF16), wgate_ref[...]))
        x = x + _dot(p_ref[...].astype(_BF16), wple_ref[...]) * gate
        if final_norm:
            x = _rmsnorm(x, gfin_ref[...])
        y_ref[...] = x


def _mix_ffn(x, o_ret, ogs, lses, z_mix, p_all, layer, lw, g_final, *, tm, tf, final_norm):
    n, d = x.shape
    row = lambda w: pl.BlockSpec((tm, w), lambda i, j: (i, 0))
    heads = pl.BlockSpec((HPG, tm, ATT_HEAD_DIM), lambda i, j: (0, i, 0))
    once = lambda a: pl.BlockSpec(a.shape, lambda i, j: (0, 0), pipeline_mode=pl.Buffered(1))
    weights = [lw[k] for k in ("w_ret_br", "w_att_br", "w_out", "norm_ffn")]
    tail = [lw["w_ple"], lw["w_ple_gate"], g_final]
    return pl.pallas_call(
        functools.partial(_mix_ffn_kernel, final_norm=final_norm),
        grid=(n // tm, D_FF // tf),
        in_specs=[row(d), row(RET_V_W), heads, heads, heads,
                  row(LSE_LANES), row(LSE_LANES), row(LSE_LANES), row(2 * D_MODEL)]
        + [once(w) for w in weights]
        + [pl.BlockSpec((d, tf), lambda i, j: (0, j)),
           pl.BlockSpec((tf, d), lambda i, j: (j, 0)),
           pl.BlockSpec((None, tm, D_PLE), lambda i, j: (layer, i, 0))]
        + [once(w) for w in tail],
        out_specs=row(d),
        out_shape=jax.ShapeDtypeStruct((n, d), _F32),
        scratch_shapes=[pltpu.VMEM((tm, d), _BF16), pltpu.VMEM((tm, d), _F32)],
        compiler_params=_params("parallel", "arbitrary", vmem_limit=VMEM_LIMIT_MIX_FFN),
        name="mix_ffn",
    )(x, o_ret, *ogs, *lses, z_mix, *weights, lw["w_up"], lw["w_down"], p_all, *tail)


def _layer_weights(norm_mix, w_in, w_ret_br, w_att_br, w_out, norm_ffn, w_up, w_down, w_ple, w_ple_gate, i):
    w = w_in[i]
    att_start = 2 * RET_QK_W + 2 * RET_V_W
    w_mix = jnp.concatenate([w[:, D_IN - 2 * D_MODEL:], w[:, :att_start]], axis=1)
    return dict(
        norm_mix=norm_mix[i][None, :], w_in_mix=w_mix.astype(_BF16),
        w_in_att_grouped=jnp.concatenate(
            [w[:, att_start + which * ATT_W + g * GROUP_W:att_start + which * ATT_W + (g + 1) * GROUP_W]
             for g in range(N_GROUPS) for which in range(3)], axis=1).astype(_BF16),
        w_ret_br=w_ret_br[i].astype(_BF16), w_att_br=w_att_br[i].astype(_BF16), w_out=w_out[i].astype(_BF16),
        norm_ffn=norm_ffn[i][None, :], w_up=w_up[i].astype(_BF16), w_down=w_down[i].astype(_BF16),
        w_ple=w_ple[i].astype(_BF16), w_ple_gate=w_ple_gate[i].astype(_BF16))


def kernel(x_prompt, x_sample, cache_win_k0, cache_win_v0, cache_win_k1, cache_win_v1, cache_win_k2, cache_win_v2,
           state_ret, p_prompt, p_sample, norm_mix, w_in, w_ret_br, w_att_br, w_out, norm_ffn, w_up, w_down,
           w_ple, w_ple_gate, norm_final):
    depth = w_in.shape[0]
    bp, tp, d = x_prompt.shape
    bs, ts, _ = x_sample.shape
    xp = x_prompt.reshape(bp * tp, d)
    xs = x_sample.reshape(bs * ts, d)
    g_final = norm_final[None, :]
    pp = p_prompt.reshape(depth, bp * tp, D_PLE)
    ps = p_sample.reshape(depth, bs * ts, D_PLE)
    pos_head_rows = lambda c: c.reshape(c.shape[:2] + (c.shape[2] * HPG, ATT_HEAD_DIM))
    caches_k = [pos_head_rows(c) for c in (cache_win_k0, cache_win_k1, cache_win_k2)]
    caches_v = [pos_head_rows(c) for c in (cache_win_v0, cache_win_v1, cache_win_v2)]
    new_k = [None] * N_GROUPS
    new_v = [None] * N_GROUPS
    windows = [None] * N_GROUPS
    prompt_ret = sample_ret = None
    zero_state = jnp.zeros((1, bp, RET_HEADS, RET_DK, RET_DV), _F32)
    sample_chunk = math.gcd(ts, RET_CHUNK)

    for i in range(depth):
        lw = _layer_weights(norm_mix, w_in, w_ret_br, w_att_br, w_out, norm_ffn, w_up, w_down, w_ple, w_ple_gate, i)
        last = i == depth - 1

        z_mix = _norm_proj(xp, lw["norm_mix"], lw["w_in_mix"], _BF16, tm=2048, tn=1024, gate_cols=2 * D_MODEL)
        z_att = _norm_proj_att(xp, lw["norm_mix"], lw["w_in_att_grouped"]).reshape(bp, tp, 3 * ATT_W)
        o_ret, prompt_ret = _retention(z_mix.reshape(bp, tp, MIX_W), zero_state, 0, prompt_ret, i, depth,
                                       chunk=RET_CHUNK, n_chunks=4)
        ogs, lses = zip(*[_band_attention(z_att, g) for g in range(N_GROUPS)])
        for g, (window, _) in enumerate(ATT_GROUPS):
            windows[g] = _kv_window(xp.reshape(bp, tp, d), lw["norm_mix"], lw["w_in_att_grouped"], g, min(window, tp),
                                    windows[g], i, depth)
        xp = _mix_ffn(xp, o_ret.reshape(bp * tp, RET_V_W), ogs, lses, z_mix, pp, i, lw, g_final,
                      tm=512, tf=2048, final_norm=last)

        z_mix = _norm_proj(xs, lw["norm_mix"], lw["w_in_mix"], _BF16, tm=bs * ts, tn=1024, gate_cols=2 * D_MODEL)
        z3 = _norm_proj(xs, lw["norm_mix"], lw["w_in_att_grouped"], _F32, tm=bs * ts, tn=ATT_W).reshape(bs, ts, 3 * ATT_W)
        o_ret, sample_ret = _retention(z_mix.reshape(bs, ts, MIX_W), state_ret, i, sample_ret, i, depth,
                                       chunk=sample_chunk, n_chunks=ts // sample_chunk)
        ogs, lses = [], []
        for g in range(N_GROUPS):
            o_g, lse_g, new_k[g], new_v[g] = _step_attention(z3, caches_k[g], caches_v[g], new_k[g], new_v[g], i, g)
            ogs.append(o_g)
            lses.append(lse_g)
        xs = _mix_ffn(xs, o_ret.reshape(bs * ts, RET_V_W), ogs, lses, z_mix, ps, i, lw, g_final,
                      tm=bs * ts, tf=2048, final_norm=last)

    as_heads = lambda a: a.reshape(a.shape[:2] + (a.shape[2] // HPG, HPG, ATT_HEAD_DIM))
    prompt_windows = [as_heads(a) for kv in windows for a in kv]
    sample_windows = [as_heads(a) for g in range(N_GROUPS) for a in (new_k[g], new_v[g])]
    return (xp.reshape(bp, tp, d), xs.reshape(bs, ts, d), *prompt_windows, prompt_ret, *sample_windows, sample_ret)
```

```python
import functools
import math

import jax
import jax.numpy as jnp
import numpy as np
from jax import lax
from jax.experimental import pallas as pl
from jax.experimental.pallas import tpu as pltpu

D_MODEL = 1024
D_PLE = 256
RET_HEADS = 4
RET_DK = 128
RET_DV = 256
RET_CHUNK = 128
ATT_GROUPS = ((128, 1), (512, 4), (2048, 16))
N_GROUPS = 3
HPG = 4
ATT_HEAD_DIM = 128
ATT_HEADS = N_GROUPS * HPG
ATT_TAPS = 128
Q_BLOCK = 128
ATT_TILE = 2048
PROJ_PARTS = 4
NORM_PARTS = 4
FF_SLICE = 2048
SPLIT_STRIDE = 4
D_FF = 4 * D_MODEL
EPS = 1e-6

RET_QK_W = RET_HEADS * RET_DK
RET_V_W = RET_HEADS * RET_DV
GROUP_W = HPG * ATT_HEAD_DIM
ATT_W = ATT_HEADS * ATT_HEAD_DIM
D_IN = 2 * RET_QK_W + 2 * RET_V_W + 3 * ATT_W + 2 * D_MODEL

MIX_W = 2 * D_MODEL + 2 * RET_QK_W + 2 * RET_V_W
COL_GATES = 0
COL_RQ = 2 * D_MODEL
COL_RK = COL_RQ + RET_QK_W
COL_RV = COL_RK + RET_QK_W
COL_RG = COL_RV + RET_V_W
ATT_Q, ATT_K, ATT_V = 0, 1, 2
COL_BLOCK = 512

LSE_LANES = 128
STEP_CACHE_ROWS = 8192
RET_ROWS_PER_STEP = 32
VMEM_LIMIT = 48 * 1024 * 1024
VMEM_LIMIT_MIX_FFN = 56 * 1024 * 1024

_BF16 = jnp.bfloat16
_F32 = jnp.float32


def _params(*sem, vmem_limit=VMEM_LIMIT):
    return pltpu.CompilerParams(dimension_semantics=sem, vmem_limit_bytes=vmem_limit)


def _rmsnorm(x, g):
    return x * lax.rsqrt(jnp.mean(x * x, axis=-1, keepdims=True) + EPS) * g


def _dot(a, b):
    return jnp.dot(a, b, preferred_element_type=_F32)


def _dot_nt(a, b):
    return lax.dot_general(a, b, (((1,), (1,)), ((), ())), preferred_element_type=_F32)


def _dot_tn(a, b):
    return lax.dot_general(a, b, (((0,), (0,)), ((), ())), preferred_element_type=_F32)


def _norm_proj_kernel(x_ref, g_ref, w_ref, z_ref, h_ref, *, gate_blocks):
    j = pl.program_id(1)
    tm = x_ref.shape[0]
    part = tm // NORM_PARTS if tm % (NORM_PARTS * 16) == 0 else tm

    def by_parts(with_norm, with_sigmoid):
        for a in range(tm // part):
            rows = slice(a * part, (a + 1) * part)
            if with_norm:
                h_ref[rows, :] = _rmsnorm(x_ref[rows, :], g_ref[...]).astype(_BF16)
            z = _dot(h_ref[rows, :], w_ref[...])
            z_ref[rows, :] = (jax.nn.sigmoid(z) if with_sigmoid else z).astype(z_ref.dtype)

    @pl.when(j == 0)
    def _():
        by_parts(True, gate_blocks > 0)

    if gate_blocks > 1:
        @pl.when((j > 0) & (j < gate_blocks))
        def _():
            by_parts(False, True)

    @pl.when(j >= max(gate_blocks, 1))
    def _():
        z_ref[...] = _dot(h_ref[...], w_ref[...]).astype(z_ref.dtype)


def _norm_proj(x, g, w, out_dtype, *, tm, tn=COL_BLOCK, gate_cols=0):
    n, d = x.shape
    d_out = w.shape[1]
    return pl.pallas_call(
        functools.partial(_norm_proj_kernel, gate_blocks=gate_cols // tn),
        grid=(n // tm, d_out // tn),
        in_specs=[
            pl.BlockSpec((tm, d), lambda i, j: (i, 0)),
            pl.BlockSpec((1, d), lambda i, j: (0, 0)),
            pl.BlockSpec((d, tn), lambda i, j: (0, j)),
        ],
        out_specs=pl.BlockSpec((tm, tn), lambda i, j: (i, j)),
        out_shape=jax.ShapeDtypeStruct((n, d_out), out_dtype),
        scratch_shapes=[pltpu.VMEM((tm, d), _BF16)],
        compiler_params=_params("parallel", "arbitrary"),
        name="norm_proj",
    )(x, g, w)


def _norm_proj_att_kernel(x_ref, g_ref, w_ref, z_ref, h_ref, stage_ref, stage2_ref):
    j = pl.program_id(1)
    part = ATT_TILE // PROJ_PARTS
    for group, (_, dil) in enumerate(ATT_GROUPS):
        @pl.when(j == group)
        def _(group=group, dil=dil):
            if group == 0:
                units = [(a, which) for a in range(PROJ_PARTS) for which in range(3)]
            else:
                units = [(a, which) for which in range(3) for a in range(PROJ_PARTS)]
            for a, which in units:
                rows = slice(a * part, (a + 1) * part)
                col0 = which * GROUP_W
                if group == 0 and which == 0:
                    h_ref[rows, :] = _rmsnorm(x_ref[rows, :], g_ref[...]).astype(_BF16)
                res = _dot(h_ref[rows, :], w_ref[:, col0:col0 + GROUP_W])
                if dil == 1:
                    z_ref[rows, col0:col0 + GROUP_W] = res.astype(_BF16)
                    continue
                per, sub = ATT_TILE // dil, part // dil
                for c in range(HPG):
                    stage_ref[c, rows, :] = res[:, c * ATT_HEAD_DIM:(c + 1) * ATT_HEAD_DIM]
                if dil > SPLIT_STRIDE:
                    seg = part // SPLIT_STRIDE
                    for c in range(HPG):
                        for lo in range(SPLIT_STRIDE):
                            stage2_ref[c, lo * seg:(lo + 1) * seg, :] = (
                                stage_ref[c, pl.ds(a * part + lo, seg, stride=SPLIT_STRIDE), :])
                for r in range(dil):
                    for c in range(HPG):
                        cols = slice(col0 + c * ATT_HEAD_DIM, col0 + (c + 1) * ATT_HEAD_DIM)
                        if dil > SPLIT_STRIDE:
                            lo, hi = r % SPLIT_STRIDE, r // SPLIT_STRIDE
                            piece = stage2_ref[c, pl.ds(lo * seg + hi, sub, stride=dil // SPLIT_STRIDE), :]
                        else:
                            piece = stage_ref[c, pl.ds(a * part + r, sub, stride=dil), :]
                        z_ref[r * per + a * sub:r * per + (a + 1) * sub, cols] = piece.astype(_BF16)


def _norm_proj_att(x, g, w):
    n, d = x.shape
    d_out = w.shape[1]
    return pl.pallas_call(
        _norm_proj_att_kernel,
        grid=(n // ATT_TILE, d_out // (3 * GROUP_W)),
        in_specs=[
            pl.BlockSpec((ATT_TILE, d), lambda i, j: (i, 0)),
            pl.BlockSpec((1, d), lambda i, j: (0, 0)),
            pl.BlockSpec((d, 3 * GROUP_W), lambda i, j: (0, j)),
        ],
        out_specs=pl.BlockSpec((ATT_TILE, 3 * GROUP_W), lambda i, j: (i, j)),
        out_shape=jax.ShapeDtypeStruct((n, d_out), _BF16),
        scratch_shapes=[pltpu.VMEM((ATT_TILE, d), _BF16), pltpu.VMEM((HPG, ATT_TILE, ATT_HEAD_DIM), _F32),
                        pltpu.VMEM((HPG, ATT_TILE // PROJ_PARTS, ATT_HEAD_DIM), _F32)],
        compiler_params=_params("parallel", "arbitrary"),
        name="norm_proj_att",
    )(x, g, w)


def _kv_window_kernel(x_ref, g_ref, wk_ref, wv_ref, ko_ref, vo_ref):
    tm = x_ref.shape[0]
    h = _rmsnorm(x_ref[...], g_ref[...]).astype(_BF16)
    for w_ref, o_ref in ((wk_ref, ko_ref), (wv_ref, vo_ref)):
        res = _dot(h, w_ref[...])
        for hd in range(HPG):
            o_ref[pl.ds(hd, tm, stride=HPG), :] = res[:, hd * ATT_HEAD_DIM:(hd + 1) * ATT_HEAD_DIM]


def _kv_window(x3, g, w_grouped, group, keep, prev, layer, depth):
    b, t, d = x3.shape
    tm = min(keep, 512)
    first = (t - keep) // tm
    wspec = lambda which: pl.BlockSpec((d, GROUP_W), lambda bi, i: (0, 3 * group + which))
    ospec = pl.BlockSpec((None, None, tm * HPG, ATT_HEAD_DIM), lambda bi, i: (layer, bi, i, 0))
    oshape = jax.ShapeDtypeStruct((depth, b, keep * HPG, ATT_HEAD_DIM), _F32)
    kernel, args, extra_specs, aliases = _kv_window_kernel, [x3, g, w_grouped, w_grouped], [], {}
    if prev is not None:
        extra_specs = [pl.BlockSpec(memory_space=pl.ANY)] * 2
        aliases = {4: 0, 5: 1}
        kernel = functools.partial(_drop_refs, kernel, 4, 2)
        args += list(prev)
    return pl.pallas_call(
        kernel,
        grid=(b, keep // tm),
        in_specs=[pl.BlockSpec((None, tm, d), lambda bi, i: (bi, first + i, 0)),
                  pl.BlockSpec((1, d), lambda bi, i: (0, 0)), wspec(ATT_K), wspec(ATT_V)] + extra_specs,
        out_specs=[ospec, ospec],
        out_shape=[oshape, oshape],
        input_output_aliases=aliases,
        compiler_params=_params("parallel", "parallel"),
        name="kv_window_g%d" % group,
    )(*args)


def _ret_log_gamma():
    return jnp.log1p(-jnp.exp(jnp.linspace(math.log(1.0 / 32), math.log(1.0 / 512), RET_HEADS))).astype(_F32)


def _retention_tables(chunk):
    lg = _ret_log_gamma()
    pos = jnp.arange(chunk, dtype=_F32)
    diff = pos[:, None] - pos[None, :]
    intra = jnp.where(diff[None] >= 0, jnp.exp(lg[:, None, None] * jnp.maximum(diff, 0.0)[None]), 0.0)
    xi = jnp.exp(lg[:, None] * (pos[None] + 1.0))
    zeta = jnp.exp(lg[:, None] * (chunk - 1.0 - pos)[None])
    decay = jnp.exp(lg * chunk)
    return (intra * (RET_DK ** -0.5),
            jnp.broadcast_to(xi[:, :, None], (RET_HEADS, chunk, RET_DK)),
            jnp.broadcast_to(zeta[:, :, None], (RET_HEADS, chunk, RET_DV)),
            jnp.broadcast_to(decay[:, None, None], (RET_HEADS, 8, RET_DV)))


def _retention_kernel(q_ref, k_ref, v_ref, g_ref, s0_ref, intra_ref, xi_ref, zeta_ref, decay_ref,
                      o_ref, sfin_ref, state, *, batches, **kw):
    for bb in range(batches):
        _retention_one(q_ref.at[bb], k_ref.at[bb], v_ref.at[bb], g_ref.at[bb], s0_ref.at[bb], intra_ref, xi_ref,
                       zeta_ref, decay_ref, o_ref.at[bb], sfin_ref.at[bb], state.at[bb], **kw)


def _retention_one(q_ref, k_ref, v_ref, g_ref, s0_ref, intra_ref, xi_ref, zeta_ref, decay_ref,
                   o_ref, sfin_ref, state, *, chunk, n_chunks):
    j = pl.program_id(1)

    @pl.when(j == 0)
    def _():
        state[...] = s0_ref[...]

    for c in range(n_chunks):
        rows = slice(c * chunk, (c + 1) * chunk)
        for h in range(RET_HEADS):
            qk_cols = slice(h * RET_DK, (h + 1) * RET_DK)
            v_cols = slice(h * RET_DV, (h + 1) * RET_DV)
            q = q_ref[rows, qk_cols]
            kb = k_ref[rows, qk_cols]
            v = v_ref[rows, v_cols]
            g = g_ref[rows, v_cols].astype(_F32)
            r_prev = state[h]
            scores = _dot_nt(q, kb) * intra_ref[h]
            o = _dot(scores.astype(_BF16), v)
            o = o + _dot((q * xi_ref[h]).astype(_BF16), r_prev.astype(_BF16))
            u = _dot_tn(kb, (v * zeta_ref[h]).astype(_BF16)) * (RET_DK ** -0.5)
            state[h] = decay_ref[h, 0:1, :] * r_prev + u
            mu = jnp.mean(o, axis=-1, keepdims=True)
            oc = o - mu
            var = jnp.mean(oc * oc, axis=-1, keepdims=True)
            on = oc * lax.rsqrt(var + EPS)
            o_ref[rows, v_cols] = (on * (g * jax.nn.sigmoid(g))).astype(o_ref.dtype)

    @pl.when(j == pl.num_programs(1) - 1)
    def _():
        sfin_ref[...] = state[...]


def _retention(z3, states0, layer0, prev_states, layer, depth, *, chunk, n_chunks):
    b, t, _ = z3.shape
    tc = chunk * n_chunks
    batches = max(1, min(b, RET_ROWS_PER_STEP // tc))
    intra, xi, zeta, decay = _retention_tables(chunk)
    const = lambda shape: pl.BlockSpec(shape, lambda bi, j: (0,) * len(shape))
    state_block = (None, batches, RET_HEADS, RET_DK, RET_DV)
    kernel = functools.partial(_retention_kernel, batches=batches, chunk=chunk, n_chunks=n_chunks)
    args = [z3, z3, z3, z3, states0, intra, xi, zeta, decay]
    extra_specs, aliases = [], {}
    if prev_states is not None:
        extra_specs = [pl.BlockSpec(memory_space=pl.ANY)]
        aliases = {len(args): 1}
        kernel = functools.partial(_drop_refs, kernel, len(args), 1)
        args.append(prev_states)
    return pl.pallas_call(
        kernel,
        grid=(b // batches, t // tc),
        in_specs=[
            pl.BlockSpec((batches, tc, RET_QK_W), lambda bi, j: (bi, j, COL_RQ // RET_QK_W)),
            pl.BlockSpec((batches, tc, RET_QK_W), lambda bi, j: (bi, j, COL_RK // RET_QK_W)),
            pl.BlockSpec((batches, tc, RET_V_W), lambda bi, j: (bi, j, COL_RV // RET_V_W)),
            pl.BlockSpec((batches, tc, RET_V_W), lambda bi, j: (bi, j, COL_RG // RET_V_W)),
            pl.BlockSpec(state_block, lambda bi, j: (layer0, bi, 0, 0, 0)),
            const((RET_HEADS, chunk, chunk)),
            const((RET_HEADS, chunk, RET_DK)),
            const((RET_HEADS, chunk, RET_DV)),
            const((RET_HEADS, 8, RET_DV)),
        ] + extra_specs,
        out_specs=[
            pl.BlockSpec((batches, tc, RET_V_W), lambda bi, j: (bi, j, 0)),
            pl.BlockSpec(state_block, lambda bi, j: (layer, bi, 0, 0, 0)),
        ],
        out_shape=[
            jax.ShapeDtypeStruct((b, t, RET_V_W), _BF16),
            jax.ShapeDtypeStruct((depth, b, RET_HEADS, RET_DK, RET_DV), _F32),
        ],
        scratch_shapes=[pltpu.VMEM((batches, RET_HEADS, RET_DK, RET_DV), _F32)],
        input_output_aliases=aliases,
        compiler_params=_params("parallel", "arbitrary"),
        name="retention",
    )(*args)


def _alibi_slope(head):
    return 2.0 ** (-8.0 * (head + 1.0) / ATT_HEADS)


def _pack_head_stats(cols):
    rows = cols[0].shape[0]
    lane = lax.broadcasted_iota(jnp.int32, (rows, LSE_LANES), 1)
    out = jnp.zeros((rows, LSE_LANES), _F32)
    for h, c in enumerate(cols):
        out = jnp.where(lane == h, c, out)
    return out


def _band_attn_kernel(q_ref, k_ref, v_ref, o_ref, lse_ref, kprev, vprev, stats, *, dil, group):
    i = pl.program_id(1)
    h = pl.program_id(2)
    blk = Q_BLOCK
    per = ATT_TILE // dil
    row = lax.broadcasted_iota(jnp.int32, (blk, 2 * blk), 0)
    col = lax.broadcasted_iota(jnp.int32, (blk, 2 * blk), 1)
    delta = row + blk - col
    slope = jnp.float32(0.0)
    for hh in range(HPG):
        slope = jnp.where(h == hh, jnp.float32(_alibi_slope(group * HPG + hh)), slope)
    in_band = (delta >= 0) & (delta <= ATT_TAPS)
    bias = jnp.where(in_band, -slope * (delta * dil).astype(_F32), -jnp.inf)
    bias_first = jnp.where(col >= jnp.where(i > 0, 0, blk), bias, -jnp.inf)
    lane = lax.broadcasted_iota(jnp.int32, (blk, LSE_LANES), 1)
    scale = ATT_HEAD_DIM ** -0.5

    def token_rows(r, sb):
        start = r + sb * blk * dil
        return pl.ds(start, blk, stride=dil) if dil > 1 else pl.ds(start, blk)

    @pl.when(i == 0)
    def _():
        kprev[h] = jnp.zeros((ATT_TILE, ATT_HEAD_DIM), _BF16)
        vprev[h] = jnp.zeros((ATT_TILE, ATT_HEAD_DIM), _BF16)

    for r in range(dil):
        for sb in range(per // blk):
            lo = r * per + sb * blk
            if sb == 0:
                last = slice((r + 1) * per - blk, (r + 1) * per)
                keys = jnp.concatenate([kprev[h, last, :], k_ref[lo:lo + blk, :]], axis=0)
                values = jnp.concatenate([vprev[h, last, :], v_ref[lo:lo + blk, :]], axis=0)
            else:
                keys = k_ref[lo - blk:lo + blk, :]
                values = v_ref[lo - blk:lo + blk, :]
            s = _dot_nt(q_ref[lo:lo + blk, :], keys) * scale + (bias_first if sb == 0 else bias)
            m = jnp.max(s, axis=-1, keepdims=True)
            e = jnp.exp(s - m)
            den = jnp.sum(e, axis=-1, keepdims=True)
            o_ref[token_rows(r, sb), :] = _dot((e / den).astype(_BF16), values)
            prev = jnp.where(h == 0, 0.0, stats[lo:lo + blk, :])
            stats[lo:lo + blk, :] = jnp.where(lane == h, m + jnp.log(den), prev)

    kprev[h] = k_ref[...]
    vprev[h] = v_ref[...]

    @pl.when(h == HPG - 1)
    def _():
        for r in range(dil):
            for sb in range(per // blk):
                lo = r * per + sb * blk
                lse_ref[token_rows(r, sb), :] = stats[lo:lo + blk, :]


def _band_attention(z3, group):
    b, t, _ = z3.shape
    _, dil = ATT_GROUPS[group]
    tiles = t // ATT_TILE
    spec = lambda which: pl.BlockSpec((None, ATT_TILE, ATT_HEAD_DIM),
                                      lambda bi, i, h: (bi, i, (3 * group + which) * HPG + h))
    carry = pltpu.VMEM((HPG, ATT_TILE, ATT_HEAD_DIM), _BF16)
    return pl.pallas_call(
        functools.partial(_band_attn_kernel, dil=dil, group=group),
        grid=(b, tiles, HPG),
        in_specs=[spec(ATT_Q), spec(ATT_K), spec(ATT_V)],
        out_specs=[
            pl.BlockSpec((None, ATT_TILE, ATT_HEAD_DIM), lambda bi, i, h: (h, bi * tiles + i, 0)),
            pl.BlockSpec((ATT_TILE, LSE_LANES), lambda bi, i, h: (bi * tiles + i, 0)),
        ],
        out_shape=[
            jax.ShapeDtypeStruct((HPG, b * t, ATT_HEAD_DIM), _F32),
            jax.ShapeDtypeStruct((b * t, LSE_LANES), _F32),
        ],
        scratch_shapes=[carry, carry, pltpu.VMEM((ATT_TILE, LSE_LANES), _F32)],
        compiler_params=_params("parallel", "arbitrary", "arbitrary"),
        name="band_attention_g%d" % group,
    )(z3, z3, z3)


def _step_attn_kernel(q_ref, kn_ref, vn_ref, kc_ref, vc_ref, o_ref, lse_ref, ko_ref, vo_ref, *, batches, t_new, **kw):
    for bb in range(batches):
        tokens = pl.ds(bb * t_new, t_new)
        _step_attn_one(q_ref.at[bb], kn_ref.at[bb], vn_ref.at[bb], kc_ref.at[bb], vc_ref.at[bb],
                       o_ref.at[:, tokens, :], lse_ref.at[tokens, :], ko_ref.at[bb], vo_ref.at[bb], t_new=t_new, **kw)


def _step_attn_one(q_ref, kn_ref, vn_ref, kc_ref, vc_ref, o_ref, lse_ref, ko_ref, vo_ref, *,
                   t_new, cache_len, dil, group):
    n_res = min(dil, t_new)
    taps = cache_len // dil
    rows_all = HPG * t_new
    head_cols = [slice(h * ATT_HEAD_DIM, (h + 1) * ATT_HEAD_DIM) for h in range(HPG)]
    head_rows = [slice(h * t_new, (h + 1) * t_new) for h in range(HPG)]
    q = [q_ref[:, c].astype(_BF16) for c in head_cols]
    k_new = kn_ref[...]
    v_new = vn_ref[...]

    def cache_taps(ref, r, h):
        return ref[pl.ds(r * HPG + h, taps, stride=HPG * dil), :].astype(_BF16)

    scale = ATT_HEAD_DIM ** -0.5
    log2_dil = dil.bit_length() - 1
    log2_new = t_new.bit_length() - 1

    def row_terms(width):
        row_id = lax.broadcasted_iota(jnp.int32, (rows_all, width), 0)
        slope = jnp.zeros((rows_all, width), _F32)
        for h in range(HPG):
            slope = jnp.where((row_id >> log2_new) == h, _alibi_slope(group * HPG + h), slope)
        return row_id & (t_new - 1), slope

    query, slope = row_terms(taps)
    residue = query & (dil - 1)
    taps_back = taps + (query >> log2_dil) - lax.broadcasted_iota(jnp.int32, (rows_all, taps), 1)
    s_cache = None
    for r in range(n_res):
        s_r = jnp.concatenate([_dot_nt(q[h], cache_taps(kc_ref, r, h)) for h in range(HPG)], axis=0)
        s_cache = s_r if s_cache is None else jnp.where(residue == r, s_r, s_cache)
    s_cache = s_cache * scale - slope * (taps_back << log2_dil).astype(_F32)
    s_cache = jnp.where(taps_back <= ATT_TAPS, s_cache, -jnp.inf)

    query_n, slope_n = row_terms(t_new)
    back = query_n - lax.broadcasted_iota(jnp.int32, (rows_all, t_new), 1)
    s_new = jnp.concatenate([_dot_nt(q[h], k_new[:, head_cols[h]].astype(_BF16)) for h in range(HPG)], axis=0)
    s_new = s_new * scale - slope_n * back.astype(_F32)
    s_new = jnp.where((back >= 0) & ((back & (dil - 1)) == 0), s_new, -jnp.inf)

    m = jnp.maximum(jnp.max(s_cache, axis=-1, keepdims=True), jnp.max(s_new, axis=-1, keepdims=True))
    e_cache = jnp.exp(s_cache - m)
    e_new = jnp.exp(s_new - m)
    den = jnp.sum(e_cache, axis=-1, keepdims=True) + jnp.sum(e_new, axis=-1, keepdims=True)
    p_cache = e_cache / den
    p_new = (e_new / den).astype(_BF16)
    lse = m + jnp.log(den)

    for h in range(HPG):
        out = _dot(p_new[head_rows[h]], v_new[:, head_cols[h]].astype(_BF16))
        for r in range(n_res):
            p_r = p_cache[head_rows[h]]
            if n_res > 1:
                p_r = jnp.where(residue[head_rows[h]] == r, p_r, 0.0)
            out = out + _dot(p_r.astype(_BF16), cache_taps(vc_ref, r, h))
        o_ref[h] = out
    lse_ref[...] = _pack_head_stats([lse[rows] for rows in head_rows])

    keep = (cache_len - t_new) * HPG
    ko_ref[0:keep, :] = kc_ref[t_new * HPG:, :]
    vo_ref[0:keep, :] = vc_ref[t_new * HPG:, :]
    for h in range(HPG):
        ko_ref[pl.ds(keep + h, t_new, stride=HPG), :] = k_new[:, head_cols[h]]
        vo_ref[pl.ds(keep + h, t_new, stride=HPG), :] = v_new[:, head_cols[h]]


def _step_attention(z3, cache_k, cache_v, prev_k, prev_v, layer, group):
    b, t_new, _ = z3.shape
    cache_len = cache_k.shape[2] // HPG
    _, dil = ATT_GROUPS[group]
    cq, ck, cv = (3 * group + which for which in (ATT_Q, ATT_K, ATT_V))
    batches = max(1, min(b, STEP_CACHE_ROWS // (cache_len * HPG)))
    zspec = lambda c: pl.BlockSpec((batches, t_new, GROUP_W), lambda bi: (bi, 0, c))
    cache_spec = pl.BlockSpec((None, batches, cache_len * HPG, ATT_HEAD_DIM), lambda bi: (layer, bi, 0, 0))
    kernel = functools.partial(_step_attn_kernel, batches=batches, t_new=t_new, cache_len=cache_len, dil=dil,
                               group=group)
    in_specs = [zspec(cq), zspec(ck), zspec(cv), cache_spec, cache_spec]
    args = [z3, z3, z3, cache_k, cache_v]
    aliases = {}
    if prev_k is not None:
        in_specs += [pl.BlockSpec(memory_space=pl.ANY)] * 2
        args += [prev_k, prev_v]
        aliases = {5: 2, 6: 3}
        kernel = functools.partial(_drop_refs, kernel, 5, 2)
    cache_shape = jax.ShapeDtypeStruct(cache_k.shape, cache_k.dtype)
    return pl.pallas_call(
        kernel,
        grid=(b // batches,),
        in_specs=in_specs,
        out_specs=[
            pl.BlockSpec((HPG, batches * t_new, ATT_HEAD_DIM), lambda bi: (0, bi, 0)),
            pl.BlockSpec((batches * t_new, LSE_LANES), lambda bi: (bi, 0)),
            cache_spec, cache_spec,
        ],
        out_shape=[
            jax.ShapeDtypeStruct((HPG, b * t_new, ATT_HEAD_DIM), _F32),
            jax.ShapeDtypeStruct((b * t_new, LSE_LANES), _F32),
            cache_shape, cache_shape,
        ],
        input_output_aliases=aliases,
        compiler_params=_params("parallel"),
        name="step_attention_g%d" % group,
    )(*args)


def _drop_refs(kernel, start, count, *refs):
    return kernel(*refs[:start], *refs[start + count:])


def _mix_ffn_kernel(x_ref, oret_ref, og0_ref, og1_ref, og2_ref, l0_ref, l1_ref, l2_ref, gates_ref,
                    wret_ref, watt_ref, wout_ref, g_ref, wup_ref, wdown_ref, p_ref, wple_ref, wgate_ref, gfin_ref,
                    y_ref, *, final_norm):
    l0, l1, l2 = l0_ref[...], l1_ref[...], l2_ref[...]
    mx = jnp.maximum(jnp.maximum(l0, l1), l2)
    e0, e1, e2 = jnp.exp(l0 - mx), jnp.exp(l1 - mx), jnp.exp(l2 - mx)
    tot = e0 + e1 + e2
    w0, w1, w2 = e0 / tot, e1 / tot, e2 / tot
    heads = []
    for h in range(HPG):
        heads.append(w0[:, h:h + 1] * og0_ref[h] + w1[:, h:h + 1] * og1_ref[h] + w2[:, h:h + 1] * og2_ref[h])
    o_att = jnp.concatenate(heads, axis=-1).astype(_BF16)
    br_ret = _dot(oret_ref[...], wret_ref[...])
    br_att = _dot(o_att, watt_ref[...])
    gate_ret = gates_ref[:, 0:D_MODEL].astype(_F32)
    gate_att = gates_ref[:, D_MODEL:2 * D_MODEL].astype(_F32)
    mix = gate_ret * br_ret + gate_att * br_att
    x = x_ref[...] + _dot(mix.astype(_BF16), wout_ref[...])

    h = _rmsnorm(x, g_ref[...]).astype(_BF16)
    for c in range(D_FF // FF_SLICE):
        cols = slice(c * FF_SLICE, (c + 1) * FF_SLICE)
        u = jnp.maximum(_dot(h, wup_ref[:, cols]), 0.0)
        x = x + _dot((u * u).astype(_BF16), wdown_ref[cols, :])

    gate = jax.nn.sigmoid(_dot(x.astype(_BF16), wgate_ref[...]))
    x = x + _dot(p_ref[...].astype(_BF16), wple_ref[...]) * gate
    if final_norm:
        x = _rmsnorm(x, gfin_ref[...])
    y_ref[...] = x


def _mix_ffn(x, o_ret, ogs, lses, z_mix, p_all, layer, lw, g_final, *, tm, final_norm):
    n, d = x.shape
    row = lambda w: pl.BlockSpec((tm, w), lambda i: (i, 0))
    heads = pl.BlockSpec((HPG, tm, ATT_HEAD_DIM), lambda i: (0, i, 0))
    once = lambda a: pl.BlockSpec(a.shape, lambda i: (0, 0), pipeline_mode=pl.Buffered(1))
    weights = [lw[k] for k in ("w_ret_br", "w_att_br", "w_out", "norm_ffn", "w_up", "w_down")]
    tail = [lw["w_ple"], lw["w_ple_gate"], g_final]
    return pl.pallas_call(
        functools.partial(_mix_ffn_kernel, final_norm=final_norm),
        grid=(n // tm,),
        in_specs=[row(d), row(RET_V_W), heads, heads, heads,
                  row(LSE_LANES), row(LSE_LANES), row(LSE_LANES), row(2 * D_MODEL)]
        + [once(w) for w in weights]
        + [pl.BlockSpec((None, tm, D_PLE), lambda i: (layer, i, 0))]
        + [once(w) for w in tail],
        out_specs=row(d),
        out_shape=jax.ShapeDtypeStruct((n, d), _F32),
        compiler_params=_params("parallel", vmem_limit=VMEM_LIMIT_MIX_FFN),
        name="mix_ffn",
    )(x, o_ret, *ogs, *lses, z_mix, *weights, p_all, *tail)


def _layer_weights(norm_mix, w_in, w_ret_br, w_att_br, w_out, norm_ffn, w_up, w_down, w_ple, w_ple_gate, i):
    w = w_in[i]
    att_start = 2 * RET_QK_W + 2 * RET_V_W
    w_mix = jnp.concatenate([w[:, D_IN - 2 * D_MODEL:], w[:, :att_start]], axis=1)
    return dict(
        norm_mix=norm_mix[i][None, :], w_in_mix=w_mix.astype(_BF16),
        w_in_att_grouped=jnp.concatenate(
            [w[:, att_start + which * ATT_W + g * GROUP_W:att_start + which * ATT_W + (g + 1) * GROUP_W]
             for g in range(N_GROUPS) for which in range(3)], axis=1).astype(_BF16),
        w_ret_br=w_ret_br[i].astype(_BF16), w_att_br=w_att_br[i].astype(_BF16), w_out=w_out[i].astype(_BF16),
        norm_ffn=norm_ffn[i][None, :], w_up=w_up[i].astype(_BF16), w_down=w_down[i].astype(_BF16),
        w_ple=w_ple[i].astype(_BF16), w_ple_gate=w_ple_gate[i].astype(_BF16))


def kernel(x_prompt, x_sample, cache_win_k0, cache_win_v0, cache_win_k1, cache_win_v1, cache_win_k2, cache_win_v2,
           state_ret, p_prompt, p_sample, norm_mix, w_in, w_ret_br, w_att_br, w_out, norm_ffn, w_up, w_down,
           w_ple, w_ple_gate, norm_final):
    depth = w_in.shape[0]
    bp, tp, d = x_prompt.shape
    bs, ts, _ = x_sample.shape
    xp = x_prompt.reshape(bp * tp, d)
    xs = x_sample.reshape(bs * ts, d)
    g_final = norm_final[None, :]
    pp = p_prompt.reshape(depth, bp * tp, D_PLE)
    ps = p_sample.reshape(depth, bs * ts, D_PLE)
    pos_head_rows = lambda c: c.reshape(c.shape[:2] + (c.shape[2] * HPG, ATT_HEAD_DIM))
    caches_k = [pos_head_rows(c) for c in (cache_win_k0, cache_win_k1, cache_win_k2)]
    caches_v = [pos_head_rows(c) for c in (cache_win_v0, cache_win_v1, cache_win_v2)]
    new_k = [None] * N_GROUPS
    new_v = [None] * N_GROUPS
    windows = [None] * N_GROUPS
    prompt_ret = sample_ret = None
    zero_state = jnp.zeros((1, bp, RET_HEADS, RET_DK, RET_DV), _F32)
    sample_chunk = math.gcd(ts, RET_CHUNK)

    for i in range(depth):
        lw = _layer_weights(norm_mix, w_in, w_ret_br, w_att_br, w_out, norm_ffn, w_up, w_down, w_ple, w_ple_gate, i)
        last = i == depth - 1

        z_mix = _norm_proj(xp, lw["norm_mix"], lw["w_in_mix"], _BF16, tm=2048, tn=1024, gate_cols=2 * D_MODEL)
        z_att = _norm_proj_att(xp, lw["norm_mix"], lw["w_in_att_grouped"]).reshape(bp, tp, 3 * ATT_W)
        o_ret, prompt_ret = _retention(z_mix.reshape(bp, tp, MIX_W), zero_state, 0, prompt_ret, i, depth,
                                       chunk=RET_CHUNK, n_chunks=4)
        ogs, lses = zip(*[_band_attention(z_att, g) for g in range(N_GROUPS)])
        for g, (window, _) in enumerate(ATT_GROUPS):
            windows[g] = _kv_window(xp.reshape(bp, tp, d), lw["norm_mix"], lw["w_in_att_grouped"], g, min(window, tp),
                                    windows[g], i, depth)
        xp = _mix_ffn(xp, o_ret.reshape(bp * tp, RET_V_W), ogs, lses, z_mix, pp, i, lw, g_final,
                      tm=512, final_norm=last)

        z_mix = _norm_proj(xs, lw["norm_mix"], lw["w_in_mix"], _BF16, tm=bs * ts, tn=1024, gate_cols=2 * D_MODEL)
        z3 = _norm_proj(xs, lw["norm_mix"], lw["w_in_att_grouped"], _F32, tm=bs * ts, tn=ATT_W).reshape(bs, ts, 3 * ATT_W)
        o_ret, sample_ret = _retention(z_mix.reshape(bs, ts, MIX_W), state_ret, i, sample_ret, i, depth,
                                       chunk=sample_chunk, n_chunks=ts // sample_chunk)
        ogs, lses = [], []
        for g in range(N_GROUPS):
            o_g, lse_g, new_k[g], new_v[g] = _step_attention(z3, caches_k[g], caches_v[g], new_k[g], new_v[g], i, g)
            ogs.append(o_g)
            lses.append(lse_g)
        xs = _mix_ffn(xs, o_ret.reshape(bs * ts, RET_V_W), ogs, lses, z_mix, ps, i, lw, g_final,
                      tm=bs * ts, final_norm=last)

    as_heads = lambda a: a.reshape(a.shape[:2] + (a.shape[2] // HPG, HPG, ATT_HEAD_DIM))
    prompt_windows = [as_heads(a) for kv in windows for a in kv]
    sample_windows = [as_heads(a) for g in range(N_GROUPS) for a in (new_k[g], new_v[g])]
    return (xp.reshape(bp, tp, d), xs.reshape(bs, ts, d), *prompt_windows, prompt_ret, *sample_windows, sample_ret)
```

```python
import functools
import math

import jax
import jax.numpy as jnp
import numpy as np
from jax import lax
from jax.experimental import pallas as pl
from jax.experimental.pallas import tpu as pltpu

D_MODEL = 1024
D_PLE = 256
RET_HEADS = 4
RET_DK = 128
RET_DV = 256
RET_CHUNK = 128
ATT_GROUPS = ((128, 1), (512, 4), (2048, 16))
N_GROUPS = 3
HPG = 4
ATT_HEAD_DIM = 128
ATT_HEADS = N_GROUPS * HPG
ATT_TAPS = 128
Q_BLOCK = 128
ATT_TILE = 2048
BAND_HEADS = 2
PROJ_PARTS = 4
NORM_PARTS = 4
FF_SLICE = 2048
SPLIT_STRIDE = 4
D_FF = 4 * D_MODEL
EPS = 1e-6

RET_QK_W = RET_HEADS * RET_DK
RET_V_W = RET_HEADS * RET_DV
GROUP_W = HPG * ATT_HEAD_DIM
ATT_W = ATT_HEADS * ATT_HEAD_DIM
D_IN = 2 * RET_QK_W + 2 * RET_V_W + 3 * ATT_W + 2 * D_MODEL

MIX_W = 2 * D_MODEL + 2 * RET_QK_W + 2 * RET_V_W
COL_GATES = 0
COL_RQ = 2 * D_MODEL
COL_RK = COL_RQ + RET_QK_W
COL_RV = COL_RK + RET_QK_W
COL_RG = COL_RV + RET_V_W
ATT_Q, ATT_K, ATT_V = 0, 1, 2
COL_BLOCK = 512

LSE_LANES = 128
STEP_CACHE_ROWS = 8192
RET_ROWS_PER_STEP = 32
VMEM_LIMIT = 48 * 1024 * 1024
VMEM_LIMIT_MIX_FFN = 56 * 1024 * 1024

_BF16 = jnp.bfloat16
_F32 = jnp.float32


def _params(*sem, vmem_limit=VMEM_LIMIT):
    return pltpu.CompilerParams(dimension_semantics=sem, vmem_limit_bytes=vmem_limit)


def _rmsnorm(x, g):
    return x * lax.rsqrt(jnp.mean(x * x, axis=-1, keepdims=True) + EPS) * g


def _dot(a, b):
    return jnp.dot(a, b, preferred_element_type=_F32)


def _dot_nt(a, b):
    return lax.dot_general(a, b, (((1,), (1,)), ((), ())), preferred_element_type=_F32)


def _dot_tn(a, b):
    return lax.dot_general(a, b, (((0,), (0,)), ((), ())), preferred_element_type=_F32)


def _norm_proj_kernel(x_ref, g_ref, w_ref, z_ref, h_ref, *, gate_blocks):
    j = pl.program_id(1)
    tm = x_ref.shape[0]
    part = tm // NORM_PARTS if tm % (NORM_PARTS * 16) == 0 else tm

    def by_parts(with_norm, with_sigmoid):
        for a in range(tm // part):
            rows = slice(a * part, (a + 1) * part)
            if with_norm:
                h_ref[rows, :] = _rmsnorm(x_ref[rows, :], g_ref[...]).astype(_BF16)
            z = _dot(h_ref[rows, :], w_ref[...])
            z_ref[rows, :] = (jax.nn.sigmoid(z) if with_sigmoid else z).astype(z_ref.dtype)

    @pl.when(j == 0)
    def _():
        by_parts(True, gate_blocks > 0)

    if gate_blocks > 1:
        @pl.when((j > 0) & (j < gate_blocks))
        def _():
            by_parts(False, True)

    @pl.when(j >= max(gate_blocks, 1))
    def _():
        z_ref[...] = _dot(h_ref[...], w_ref[...]).astype(z_ref.dtype)


def _norm_proj(x, g, w, out_dtype, *, tm, tn=COL_BLOCK, gate_cols=0):
    n, d = x.shape
    d_out = w.shape[1]
    return pl.pallas_call(
        functools.partial(_norm_proj_kernel, gate_blocks=gate_cols // tn),
        grid=(n // tm, d_out // tn),
        in_specs=[
            pl.BlockSpec((tm, d), lambda i, j: (i, 0)),
            pl.BlockSpec((1, d), lambda i, j: (0, 0)),
            pl.BlockSpec((d, tn), lambda i, j: (0, j)),
        ],
        out_specs=pl.BlockSpec((tm, tn), lambda i, j: (i, j)),
        out_shape=jax.ShapeDtypeStruct((n, d_out), out_dtype),
        scratch_shapes=[pltpu.VMEM((tm, d), _BF16)],
        compiler_params=_params("parallel", "arbitrary"),
        name="norm_proj",
    )(x, g, w)


def _norm_proj_att_kernel(x_ref, g_ref, w_ref, z_ref, h_ref, stage_ref, stage2_ref):
    j = pl.program_id(1)
    part = ATT_TILE // PROJ_PARTS
    for group, (_, dil) in enumerate(ATT_GROUPS):
        @pl.when(j == group)
        def _(group=group, dil=dil):
            if group == 0:
                units = [(a, which) for a in range(PROJ_PARTS) for which in range(3)]
            else:
                units = [(a, which) for which in range(3) for a in range(PROJ_PARTS)]
            for a, which in units:
                rows = slice(a * part, (a + 1) * part)
                col0 = which * GROUP_W
                if group == 0 and which == 0:
                    h_ref[rows, :] = _rmsnorm(x_ref[rows, :], g_ref[...]).astype(_BF16)
                res = _dot(h_ref[rows, :], w_ref[:, col0:col0 + GROUP_W])
                if dil == 1:
                    z_ref[rows, col0:col0 + GROUP_W] = res.astype(_BF16)
                    continue
                per, sub = ATT_TILE // dil, part // dil
                for c in range(HPG):
                    stage_ref[c, rows, :] = res[:, c * ATT_HEAD_DIM:(c + 1) * ATT_HEAD_DIM]
                if dil > SPLIT_STRIDE:
                    seg = part // SPLIT_STRIDE
                    for c in range(HPG):
                        for lo in range(SPLIT_STRIDE):
                            stage2_ref[c, lo * seg:(lo + 1) * seg, :] = (
                                stage_ref[c, pl.ds(a * part + lo, seg, stride=SPLIT_STRIDE), :])
                for r in range(dil):
                    for c in range(HPG):
                        cols = slice(col0 + c * ATT_HEAD_DIM, col0 + (c + 1) * ATT_HEAD_DIM)
                        if dil > SPLIT_STRIDE:
                            lo, hi = r % SPLIT_STRIDE, r // SPLIT_STRIDE
                            piece = stage2_ref[c, pl.ds(lo * seg + hi, sub, stride=dil // SPLIT_STRIDE), :]
                        else:
                            piece = stage_ref[c, pl.ds(a * part + r, sub, stride=dil), :]
                        z_ref[r * per + a * sub:r * per + (a + 1) * sub, cols] = piece.astype(_BF16)


def _norm_proj_att(x, g, w):
    n, d = x.shape
    d_out = w.shape[1]
    return pl.pallas_call(
        _norm_proj_att_kernel,
        grid=(n // ATT_TILE, d_out // (3 * GROUP_W)),
        in_specs=[
            pl.BlockSpec((ATT_TILE, d), lambda i, j: (i, 0)),
            pl.BlockSpec((1, d), lambda i, j: (0, 0)),
            pl.BlockSpec((d, 3 * GROUP_W), lambda i, j: (0, j)),
        ],
        out_specs=pl.BlockSpec((ATT_TILE, 3 * GROUP_W), lambda i, j: (i, j)),
        out_shape=jax.ShapeDtypeStruct((n, d_out), _BF16),
        scratch_shapes=[pltpu.VMEM((ATT_TILE, d), _BF16), pltpu.VMEM((HPG, ATT_TILE, ATT_HEAD_DIM), _F32),
                        pltpu.VMEM((HPG, ATT_TILE // PROJ_PARTS, ATT_HEAD_DIM), _F32)],
        compiler_params=_params("parallel", "arbitrary"),
        name="norm_proj_att",
    )(x, g, w)


def _kv_window_kernel(x_ref, g_ref, wk_ref, wv_ref, ko_ref, vo_ref):
    tm = x_ref.shape[0]
    h = _rmsnorm(x_ref[...], g_ref[...]).astype(_BF16)
    for w_ref, o_ref in ((wk_ref, ko_ref), (wv_ref, vo_ref)):
        res = _dot(h, w_ref[...])
        for hd in range(HPG):
            o_ref[pl.ds(hd, tm, stride=HPG), :] = res[:, hd * ATT_HEAD_DIM:(hd + 1) * ATT_HEAD_DIM]


def _kv_window(x3, g, w_grouped, group, keep, prev, layer, depth):
    b, t, d = x3.shape
    tm = min(keep, 512)
    first = (t - keep) // tm
    wspec = lambda which: pl.BlockSpec((d, GROUP_W), lambda bi, i: (0, 3 * group + which))
    ospec = pl.BlockSpec((None, None, tm * HPG, ATT_HEAD_DIM), lambda bi, i: (layer, bi, i, 0))
    oshape = jax.ShapeDtypeStruct((depth, b, keep * HPG, ATT_HEAD_DIM), _F32)
    kernel, args, extra_specs, aliases = _kv_window_kernel, [x3, g, w_grouped, w_grouped], [], {}
    if prev is not None:
        extra_specs = [pl.BlockSpec(memory_space=pl.ANY)] * 2
        aliases = {4: 0, 5: 1}
        kernel = functools.partial(_drop_refs, kernel, 4, 2)
        args += list(prev)
    return pl.pallas_call(
        kernel,
        grid=(b, keep // tm),
        in_specs=[pl.BlockSpec((None, tm, d), lambda bi, i: (bi, first + i, 0)),
                  pl.BlockSpec((1, d), lambda bi, i: (0, 0)), wspec(ATT_K), wspec(ATT_V)] + extra_specs,
        out_specs=[ospec, ospec],
        out_shape=[oshape, oshape],
        input_output_aliases=aliases,
        compiler_params=_params("parallel", "parallel"),
        name="kv_window_g%d" % group,
    )(*args)


def _ret_log_gamma():
    return jnp.log1p(-jnp.exp(jnp.linspace(math.log(1.0 / 32), math.log(1.0 / 512), RET_HEADS))).astype(_F32)


def _retention_tables(chunk):
    lg = _ret_log_gamma()
    pos = jnp.arange(chunk, dtype=_F32)
    diff = pos[:, None] - pos[None, :]
    intra = jnp.where(diff[None] >= 0, jnp.exp(lg[:, None, None] * jnp.maximum(diff, 0.0)[None]), 0.0)
    xi = jnp.exp(lg[:, None] * (pos[None] + 1.0))
    zeta = jnp.exp(lg[:, None] * (chunk - 1.0 - pos)[None])
    decay = jnp.exp(lg * chunk)
    return (intra * (RET_DK ** -0.5),
            jnp.broadcast_to(xi[:, :, None], (RET_HEADS, chunk, RET_DK)),
            jnp.broadcast_to(zeta[:, :, None], (RET_HEADS, chunk, RET_DV)),
            jnp.broadcast_to(decay[:, None, None], (RET_HEADS, 8, RET_DV)))


def _retention_kernel(q_ref, k_ref, v_ref, g_ref, s0_ref, intra_ref, xi_ref, zeta_ref, decay_ref,
                      o_ref, sfin_ref, state, *, batches, **kw):
    for bb in range(batches):
        _retention_one(q_ref.at[bb], k_ref.at[bb], v_ref.at[bb], g_ref.at[bb], s0_ref.at[bb], intra_ref, xi_ref,
                       zeta_ref, decay_ref, o_ref.at[bb], sfin_ref.at[bb], state.at[bb], **kw)


def _retention_one(q_ref, k_ref, v_ref, g_ref, s0_ref, intra_ref, xi_ref, zeta_ref, decay_ref,
                   o_ref, sfin_ref, state, *, chunk, n_chunks):
    j = pl.program_id(1)

    @pl.when(j == 0)
    def _():
        state[...] = s0_ref[...]

    for c in range(n_chunks):
        rows = slice(c * chunk, (c + 1) * chunk)
        for h in range(RET_HEADS):
            qk_cols = slice(h * RET_DK, (h + 1) * RET_DK)
            v_cols = slice(h * RET_DV, (h + 1) * RET_DV)
            q = q_ref[rows, qk_cols]
            kb = k_ref[rows, qk_cols]
            v = v_ref[rows, v_cols]
            g = g_ref[rows, v_cols].astype(_F32)
            r_prev = state[h]
            scores = _dot_nt(q, kb) * intra_ref[h]
            o = _dot(scores.astype(_BF16), v)
            o = o + _dot((q * xi_ref[h]).astype(_BF16), r_prev.astype(_BF16))
            u = _dot_tn(kb, (v * zeta_ref[h]).astype(_BF16)) * (RET_DK ** -0.5)
            state[h] = decay_ref[h, 0:1, :] * r_prev + u
            mu = jnp.mean(o, axis=-1, keepdims=True)
            oc = o - mu
            var = jnp.mean(oc * oc, axis=-1, keepdims=True)
            on = oc * lax.rsqrt(var + EPS)
            o_ref[rows, v_cols] = (on * (g * jax.nn.sigmoid(g))).astype(o_ref.dtype)

    @pl.when(j == pl.num_programs(1) - 1)
    def _():
        sfin_ref[...] = state[...]


def _retention(z3, states0, layer0, prev_states, layer, depth, *, chunk, n_chunks):
    b, t, _ = z3.shape
    tc = chunk * n_chunks
    batches = max(1, min(b, RET_ROWS_PER_STEP // tc))
    intra, xi, zeta, decay = _retention_tables(chunk)
    const = lambda shape: pl.BlockSpec(shape, lambda bi, j: (0,) * len(shape))
    state_block = (None, batches, RET_HEADS, RET_DK, RET_DV)
    kernel = functools.partial(_retention_kernel, batches=batches, chunk=chunk, n_chunks=n_chunks)
    args = [z3, z3, z3, z3, states0, intra, xi, zeta, decay]
    extra_specs, aliases = [], {}
    if prev_states is not None:
        extra_specs = [pl.BlockSpec(memory_space=pl.ANY)]
        aliases = {len(args): 1}
        kernel = functools.partial(_drop_refs, kernel, len(args), 1)
        args.append(prev_states)
    return pl.pallas_call(
        kernel,
        grid=(b // batches, t // tc),
        in_specs=[
            pl.BlockSpec((batches, tc, RET_QK_W), lambda bi, j: (bi, j, COL_RQ // RET_QK_W)),
            pl.BlockSpec((batches, tc, RET_QK_W), lambda bi, j: (bi, j, COL_RK // RET_QK_W)),
            pl.BlockSpec((batches, tc, RET_V_W), lambda bi, j: (bi, j, COL_RV // RET_V_W)),
            pl.BlockSpec((batches, tc, RET_V_W), lambda bi, j: (bi, j, COL_RG // RET_V_W)),
            pl.BlockSpec(state_block, lambda bi, j: (layer0, bi, 0, 0, 0)),
            const((RET_HEADS, chunk, chunk)),
            const((RET_HEADS, chunk, RET_DK)),
            const((RET_HEADS, chunk, RET_DV)),
            const((RET_HEADS, 8, RET_DV)),
        ] + extra_specs,
        out_specs=[
            pl.BlockSpec((batches, tc, RET_V_W), lambda bi, j: (bi, j, 0)),
            pl.BlockSpec(state_block, lambda bi, j: (layer, bi, 0, 0, 0)),
        ],
        out_shape=[
            jax.ShapeDtypeStruct((b, t, RET_V_W), _BF16),
            jax.ShapeDtypeStruct((depth, b, RET_HEADS, RET_DK, RET_DV), _F32),
        ],
        scratch_shapes=[pltpu.VMEM((batches, RET_HEADS, RET_DK, RET_DV), _F32)],
        input_output_aliases=aliases,
        compiler_params=_params("parallel", "arbitrary"),
        name="retention",
    )(*args)


def _alibi_slope(head):
    return 2.0 ** (-8.0 * (head + 1.0) / ATT_HEADS)


def _pack_head_stats(cols):
    rows = cols[0].shape[0]
    lane = lax.broadcasted_iota(jnp.int32, (rows, LSE_LANES), 1)
    out = jnp.zeros((rows, LSE_LANES), _F32)
    for h, c in enumerate(cols):
        out = jnp.where(lane == h, c, out)
    return out


def _band_attn_kernel(q_ref, k_ref, v_ref, o_ref, lse_ref, kprev, vprev, stats, *, dil, group):
    i = pl.program_id(1)
    hp = pl.program_id(2)
    blk = Q_BLOCK
    per = ATT_TILE // dil
    row = lax.broadcasted_iota(jnp.int32, (blk, 2 * blk), 0)
    col = lax.broadcasted_iota(jnp.int32, (blk, 2 * blk), 1)
    delta = row + blk - col
    in_band = (delta >= 0) & (delta <= ATT_TAPS)
    first_cols = col >= jnp.where(i > 0, 0, blk)
    lane = lax.broadcasted_iota(jnp.int32, (blk, LSE_LANES), 1)
    scale = ATT_HEAD_DIM ** -0.5

    def token_rows(r, sb):
        start = r + sb * blk * dil
        return pl.ds(start, blk, stride=dil) if dil > 1 else pl.ds(start, blk)

    @pl.when(i == 0)
    def _():
        for hh in range(BAND_HEADS):
            kprev[hp * BAND_HEADS + hh] = jnp.zeros((ATT_TILE, ATT_HEAD_DIM), _BF16)
            vprev[hp * BAND_HEADS + hh] = jnp.zeros((ATT_TILE, ATT_HEAD_DIM), _BF16)

    for hh in range(BAND_HEADS):
        head = hp * BAND_HEADS + hh
        cols = slice(hh * ATT_HEAD_DIM, (hh + 1) * ATT_HEAD_DIM)
        slope = jnp.float32(0.0)
        for p in range(HPG // BAND_HEADS):
            slope = jnp.where(hp == p, jnp.float32(_alibi_slope(group * HPG + p * BAND_HEADS + hh)), slope)
        bias = jnp.where(in_band, -slope * (delta * dil).astype(_F32), -jnp.inf)
        bias_first = jnp.where(first_cols, bias, -jnp.inf)

        for r in range(dil):
            for sb in range(per // blk):
                lo = r * per + sb * blk
                if sb == 0:
                    last = slice((r + 1) * per - blk, (r + 1) * per)
                    keys = jnp.concatenate([kprev[head, last, :], k_ref[lo:lo + blk, cols]], axis=0)
                    values = jnp.concatenate([vprev[head, last, :], v_ref[lo:lo + blk, cols]], axis=0)
                else:
                    keys = k_ref[lo - blk:lo + blk, cols]
                    values = v_ref[lo - blk:lo + blk, cols]
                s = _dot_nt(q_ref[lo:lo + blk, cols], keys) * scale + (bias_first if sb == 0 else bias)
                m = jnp.max(s, axis=-1, keepdims=True)
                e = jnp.exp(s - m)
                den = jnp.sum(e, axis=-1, keepdims=True)
                o_ref[hh, token_rows(r, sb), :] = _dot((e / den).astype(_BF16), values)
                prev = jnp.where(head == 0, 0.0, stats[lo:lo + blk, :])
                stats[lo:lo + blk, :] = jnp.where(lane == head, m + jnp.log(den), prev)

        kprev[head] = k_ref[:, cols]
        vprev[head] = v_ref[:, cols]

    @pl.when(hp == HPG // BAND_HEADS - 1)
    def _():
        for r in range(dil):
            for sb in range(per // blk):
                lo = r * per + sb * blk
                lse_ref[token_rows(r, sb), :] = stats[lo:lo + blk, :]


def _band_attention(z3, group):
    b, t, _ = z3.shape
    _, dil = ATT_GROUPS[group]
    tiles = t // ATT_TILE
    steps = HPG // BAND_HEADS
    spec = lambda which: pl.BlockSpec((None, ATT_TILE, BAND_HEADS * ATT_HEAD_DIM),
                                      lambda bi, i, hp: (bi, i, (3 * group + which) * steps + hp))
    carry = pltpu.VMEM((HPG, ATT_TILE, ATT_HEAD_DIM), _BF16)
    return pl.pallas_call(
        functools.partial(_band_attn_kernel, dil=dil, group=group),
        grid=(b, tiles, steps),
        in_specs=[spec(ATT_Q), spec(ATT_K), spec(ATT_V)],
        out_specs=[
            pl.BlockSpec((BAND_HEADS, ATT_TILE, ATT_HEAD_DIM), lambda bi, i, hp: (hp, bi * tiles + i, 0)),
            pl.BlockSpec((ATT_TILE, LSE_LANES), lambda bi, i, hp: (bi * tiles + i, 0)),
        ],
        out_shape=[
            jax.ShapeDtypeStruct((HPG, b * t, ATT_HEAD_DIM), _F32),
            jax.ShapeDtypeStruct((b * t, LSE_LANES), _F32),
        ],
        scratch_shapes=[carry, carry, pltpu.VMEM((ATT_TILE, LSE_LANES), _F32)],
        compiler_params=_params("parallel", "arbitrary", "arbitrary"),
        name="band_attention_g%d" % group,
    )(z3, z3, z3)


def _step_attn_kernel(q_ref, kn_ref, vn_ref, kc_ref, vc_ref, o_ref, lse_ref, ko_ref, vo_ref, *, batches, t_new, **kw):
    for bb in range(batches):
        tokens = pl.ds(bb * t_new, t_new)
        _step_attn_one(q_ref.at[bb], kn_ref.at[bb], vn_ref.at[bb], kc_ref.at[bb], vc_ref.at[bb],
                       o_ref.at[:, tokens, :], lse_ref.at[tokens, :], ko_ref.at[bb], vo_ref.at[bb], t_new=t_new, **kw)


def _step_attn_one(q_ref, kn_ref, vn_ref, kc_ref, vc_ref, o_ref, lse_ref, ko_ref, vo_ref, *,
                   t_new, cache_len, dil, group):
    n_res = min(dil, t_new)
    taps = cache_len // dil
    rows_all = HPG * t_new
    head_cols = [slice(h * ATT_HEAD_DIM, (h + 1) * ATT_HEAD_DIM) for h in range(HPG)]
    head_rows = [slice(h * t_new, (h + 1) * t_new) for h in range(HPG)]
    q = [q_ref[:, c].astype(_BF16) for c in head_cols]
    k_new = kn_ref[...]
    v_new = vn_ref[...]

    def cache_taps(ref, r, h):
        return ref[pl.ds(r * HPG + h, taps, stride=HPG * dil), :].astype(_BF16)

    scale = ATT_HEAD_DIM ** -0.5
    log2_dil = dil.bit_length() - 1
    log2_new = t_new.bit_length() - 1

    def row_terms(width):
        row_id = lax.broadcasted_iota(jnp.int32, (rows_all, width), 0)
        slope = jnp.zeros((rows_all, width), _F32)
        for h in range(HPG):
            slope = jnp.where((row_id >> log2_new) == h, _alibi_slope(group * HPG + h), slope)
        return row_id & (t_new - 1), slope

    query, slope = row_terms(taps)
    residue = query & (dil - 1)
    taps_back = taps + (query >> log2_dil) - lax.broadcasted_iota(jnp.int32, (rows_all, taps), 1)
    s_cache = None
    for r in range(n_res):
        s_r = jnp.concatenate([_dot_nt(q[h], cache_taps(kc_ref, r, h)) for h in range(HPG)], axis=0)
        s_cache = s_r if s_cache is None else jnp.where(residue == r, s_r, s_cache)
    s_cache = s_cache * scale - slope * (taps_back << log2_dil).astype(_F32)
    s_cache = jnp.where(taps_back <= ATT_TAPS, s_cache, -jnp.inf)

    query_n, slope_n = row_terms(t_new)
    back = query_n - lax.broadcasted_iota(jnp.int32, (rows_all, t_new), 1)
    s_new = jnp.concatenate([_dot_nt(q[h], k_new[:, head_cols[h]].astype(_BF16)) for h in range(HPG)], axis=0)
    s_new = s_new * scale - slope_n * back.astype(_F32)
    s_new = jnp.where((back >= 0) & ((back & (dil - 1)) == 0), s_new, -jnp.inf)

    m = jnp.maximum(jnp.max(s_cache, axis=-1, keepdims=True), jnp.max(s_new, axis=-1, keepdims=True))
    e_cache = jnp.exp(s_cache - m)
    e_new = jnp.exp(s_new - m)
    den = jnp.sum(e_cache, axis=-1, keepdims=True) + jnp.sum(e_new, axis=-1, keepdims=True)
    p_cache = e_cache / den
    p_new = (e_new / den).astype(_BF16)
    lse = m + jnp.log(den)

    for h in range(HPG):
        out = _dot(p_new[head_rows[h]], v_new[:, head_cols[h]].astype(_BF16))
        for r in range(n_res):
            p_r = p_cache[head_rows[h]]
            if n_res > 1:
                p_r = jnp.where(residue[head_rows[h]] == r, p_r, 0.0)
            out = out + _dot(p_r.astype(_BF16), cache_taps(vc_ref, r, h))
        o_ref[h] = out
    lse_ref[...] = _pack_head_stats([lse[rows] for rows in head_rows])

    keep = (cache_len - t_new) * HPG
    ko_ref[0:keep, :] = kc_ref[t_new * HPG:, :]
    vo_ref[0:keep, :] = vc_ref[t_new * HPG:, :]
    for h in range(HPG):
        ko_ref[pl.ds(keep + h, t_new, stride=HPG), :] = k_new[:, head_cols[h]]
        vo_ref[pl.ds(keep + h, t_new, stride=HPG), :] = v_new[:, head_cols[h]]


def _step_attention(z3, cache_k, cache_v, prev_k, prev_v, layer, group):
    b, t_new, _ = z3.shape
    cache_len = cache_k.shape[2] // HPG
    _, dil = ATT_GROUPS[group]
    cq, ck, cv = (3 * group + which for which in (ATT_Q, ATT_K, ATT_V))
    batches = max(1, min(b, STEP_CACHE_ROWS // (cache_len * HPG)))
    zspec = lambda c: pl.BlockSpec((batches, t_new, GROUP_W), lambda bi: (bi, 0, c))
    cache_spec = pl.BlockSpec((None, batches, cache_len * HPG, ATT_HEAD_DIM), lambda bi: (layer, bi, 0, 0))
    kernel = functools.partial(_step_attn_kernel, batches=batches, t_new=t_new, cache_len=cache_len, dil=dil,
                               group=group)
    in_specs = [zspec(cq), zspec(ck), zspec(cv), cache_spec, cache_spec]
    args = [z3, z3, z3, cache_k, cache_v]
    aliases = {}
    if prev_k is not None:
        in_specs += [pl.BlockSpec(memory_space=pl.ANY)] * 2
        args += [prev_k, prev_v]
        aliases = {5: 2, 6: 3}
        kernel = functools.partial(_drop_refs, kernel, 5, 2)
    cache_shape = jax.ShapeDtypeStruct(cache_k.shape, cache_k.dtype)
    return pl.pallas_call(
        kernel,
        grid=(b // batches,),
        in_specs=in_specs,
        out_specs=[
            pl.BlockSpec((HPG, batches * t_new, ATT_HEAD_DIM), lambda bi: (0, bi, 0)),
            pl.BlockSpec((batches * t_new, LSE_LANES), lambda bi: (bi, 0)),
            cache_spec, cache_spec,
        ],
        out_shape=[
            jax.ShapeDtypeStruct((HPG, b * t_new, ATT_HEAD_DIM), _F32),
            jax.ShapeDtypeStruct((b * t_new, LSE_LANES), _F32),
            cache_shape, cache_shape,
        ],
        input_output_aliases=aliases,
        compiler_params=_params("parallel"),
        name="step_attention_g%d" % group,
    )(*args)


def _drop_refs(kernel, start, count, *refs):
    return kernel(*refs[:start], *refs[start + count:])


def _mix_ffn_kernel(x_ref, oret_ref, og0_ref, og1_ref, og2_ref, l0_ref, l1_ref, l2_ref, gates_ref,
                    wret_ref, watt_ref, wout_ref, g_ref, wup_ref, wdown_ref, p_ref, wple_ref, wgate_ref, gfin_ref,
                    y_ref, *, final_norm):
    l0, l1, l2 = l0_ref[...], l1_ref[...], l2_ref[...]
    mx = jnp.maximum(jnp.maximum(l0, l1), l2)
    e0, e1, e2 = jnp.exp(l0 - mx), jnp.exp(l1 - mx), jnp.exp(l2 - mx)
    tot = e0 + e1 + e2
    w0, w1, w2 = e0 / tot, e1 / tot, e2 / tot
    heads = []
    for h in range(HPG):
        heads.append(w0[:, h:h + 1] * og0_ref[h] + w1[:, h:h + 1] * og1_ref[h] + w2[:, h:h + 1] * og2_ref[h])
    o_att = jnp.concatenate(heads, axis=-1).astype(_BF16)
    br_ret = _dot(oret_ref[...], wret_ref[...])
    br_att = _dot(o_att, watt_ref[...])
    gate_ret = gates_ref[:, 0:D_MODEL].astype(_F32)
    gate_att = gates_ref[:, D_MODEL:2 * D_MODEL].astype(_F32)
    mix = gate_ret * br_ret + gate_att * br_att
    x = x_ref[...] + _dot(mix.astype(_BF16), wout_ref[...])

    h = _rmsnorm(x, g_ref[...]).astype(_BF16)
    for c in range(D_FF // FF_SLICE):
        cols = slice(c * FF_SLICE, (c + 1) * FF_SLICE)
        u = jnp.maximum(_dot(h, wup_ref[:, cols]), 0.0)
        x = x + _dot((u * u).astype(_BF16), wdown_ref[cols, :])

    gate = jax.nn.sigmoid(_dot(x.astype(_BF16), wgate_ref[...]))
    x = x + _dot(p_ref[...].astype(_BF16), wple_ref[...]) * gate
    if final_norm:
        x = _rmsnorm(x, gfin_ref[...])
    y_ref[...] = x


def _mix_ffn(x, o_ret, ogs, lses, z_mix, p_all, layer, lw, g_final, *, tm, final_norm):
    n, d = x.shape
    row = lambda w: pl.BlockSpec((tm, w), lambda i: (i, 0))
    heads = pl.BlockSpec((HPG, tm, ATT_HEAD_DIM), lambda i: (0, i, 0))
    once = lambda a: pl.BlockSpec(a.shape, lambda i: (0, 0), pipeline_mode=pl.Buffered(1))
    weights = [lw[k] for k in ("w_ret_br", "w_att_br", "w_out", "norm_ffn", "w_up", "w_down")]
    tail = [lw["w_ple"], lw["w_ple_gate"], g_final]
    return pl.pallas_call(
        functools.partial(_mix_ffn_kernel, final_norm=final_norm),
        grid=(n // tm,),
        in_specs=[row(d), row(RET_V_W), heads, heads, heads,
                  row(LSE_LANES), row(LSE_LANES), row(LSE_LANES), row(2 * D_MODEL)]
        + [once(w) for w in weights]
        + [pl.BlockSpec((None, tm, D_PLE), lambda i: (layer, i, 0))]
        + [once(w) for w in tail],
        out_specs=row(d),
        out_shape=jax.ShapeDtypeStruct((n, d), _F32),
        compiler_params=_params("parallel", vmem_limit=VMEM_LIMIT_MIX_FFN),
        name="mix_ffn",
    )(x, o_ret, *ogs, *lses, z_mix, *weights, p_all, *tail)


def _layer_weights(norm_mix, w_in, w_ret_br, w_att_br, w_out, norm_ffn, w_up, w_down, w_ple, w_ple_gate, i):
    w = w_in[i]
    att_start = 2 * RET_QK_W + 2 * RET_V_W
    w_mix = jnp.concatenate([w[:, D_IN - 2 * D_MODEL:], w[:, :att_start]], axis=1)
    return dict(
        norm_mix=norm_mix[i][None, :], w_in_mix=w_mix.astype(_BF16),
        w_in_att_grouped=jnp.concatenate(
            [w[:, att_start + which * ATT_W + g * GROUP_W:att_start + which * ATT_W + (g + 1) * GROUP_W]
             for g in range(N_GROUPS) for which in range(3)], axis=1).astype(_BF16),
        w_ret_br=w_ret_br[i].astype(_BF16), w_att_br=w_att_br[i].astype(_BF16), w_out=w_out[i].astype(_BF16),
        norm_ffn=norm_ffn[i][None, :], w_up=w_up[i].astype(_BF16), w_down=w_down[i].astype(_BF16),
        w_ple=w_ple[i].astype(_BF16), w_ple_gate=w_ple_gate[i].astype(_BF16))


def kernel(x_prompt, x_sample, cache_win_k0, cache_win_v0, cache_win_k1, cache_win_v1, cache_win_k2, cache_win_v2,
           state_ret, p_prompt, p_sample, norm_mix, w_in, w_ret_br, w_att_br, w_out, norm_ffn, w_up, w_down,
           w_ple, w_ple_gate, norm_final):
    depth = w_in.shape[0]
    bp, tp, d = x_prompt.shape
    bs, ts, _ = x_sample.shape
    xp = x_prompt.reshape(bp * tp, d)
    xs = x_sample.reshape(bs * ts, d)
    g_final = norm_final[None, :]
    pp = p_prompt.reshape(depth, bp * tp, D_PLE)
    ps = p_sample.reshape(depth, bs * ts, D_PLE)
    pos_head_rows = lambda c: c.reshape(c.shape[:2] + (c.shape[2] * HPG, ATT_HEAD_DIM))
    caches_k = [pos_head_rows(c) for c in (cache_win_k0, cache_win_k1, cache_win_k2)]
    caches_v = [pos_head_rows(c) for c in (cache_win_v0, cache_win_v1, cache_win_v2)]
    new_k = [None] * N_GROUPS
    new_v = [None] * N_GROUPS
    windows = [None] * N_GROUPS
    prompt_ret = sample_ret = None
    zero_state = jnp.zeros((1, bp, RET_HEADS, RET_DK, RET_DV), _F32)
    sample_chunk = math.gcd(ts, RET_CHUNK)

    for i in range(depth):
        lw = _layer_weights(norm_mix, w_in, w_ret_br, w_att_br, w_out, norm_ffn, w_up, w_down, w_ple, w_ple_gate, i)
        last = i == depth - 1

        z_mix = _norm_proj(xp, lw["norm_mix"], lw["w_in_mix"], _BF16, tm=2048, tn=1024, gate_cols=2 * D_MODEL)
        z_att = _norm_proj_att(xp, lw["norm_mix"], lw["w_in_att_grouped"]).reshape(bp, tp, 3 * ATT_W)
        o_ret, prompt_ret = _retention(z_mix.reshape(bp, tp, MIX_W), zero_state, 0, prompt_ret, i, depth,
                                       chunk=RET_CHUNK, n_chunks=4)
        ogs, lses = zip(*[_band_attention(z_att, g) for g in range(N_GROUPS)])
        for g, (window, _) in enumerate(ATT_GROUPS):
            windows[g] = _kv_window(xp.reshape(bp, tp, d), lw["norm_mix"], lw["w_in_att_grouped"], g, min(window, tp),
                                    windows[g], i, depth)
        xp = _mix_ffn(xp, o_ret.reshape(bp * tp, RET_V_W), ogs, lses, z_mix, pp, i, lw, g_final,
                      tm=512, final_norm=last)

        z_mix = _norm_proj(xs, lw["norm_mix"], lw["w_in_mix"], _BF16, tm=bs * ts, tn=1024, gate_cols=2 * D_MODEL)
        z3 = _norm_proj(xs, lw["norm_mix"], lw["w_in_att_grouped"], _F32, tm=bs * ts, tn=ATT_W).reshape(bs, ts, 3 * ATT_W)
        o_ret, sample_ret = _retention(z_mix.reshape(bs, ts, MIX_W), state_ret, i, sample_ret, i, depth,
                                       chunk=sample_chunk, n_chunks=ts // sample_chunk)
        ogs, lses = [], []
        for g in range(N_GROUPS):
            o_g, lse_g, new_k[g], new_v[g] = _step_attention(z3, caches_k[g], caches_v[g], new_k[g], new_v[g], i, g)
            ogs.append(o_g)
            lses.append(lse_g)
        xs = _mix_ffn(xs, o_ret.reshape(bs * ts, RET_V_W), ogs, lses, z_mix, ps, i, lw, g_final,
                      tm=bs * ts, final_norm=last)

    as_heads = lambda a: a.reshape(a.shape[:2] + (a.shape[2] // HPG, HPG, ATT_HEAD_DIM))
    prompt_windows = [as_heads(a) for kv in windows for a in kv]
    sample_windows = [as_heads(a) for g in range(N_GROUPS) for a in (new_k[g], new_v[g])]
    return (xp.reshape(bp, tp, d), xs.reshape(bs, ts, d), *prompt_windows, prompt_ret, *sample_windows, sample_ret)
```

```python
import functools
import math

import jax
import jax.numpy as jnp
import numpy as np
from jax import lax
from jax.experimental import pallas as pl
from jax.experimental.pallas import tpu as pltpu

D_MODEL = 1024
D_PLE = 256
RET_HEADS = 4
RET_DK = 128
RET_DV = 256
RET_CHUNK = 128
ATT_GROUPS = ((128, 1), (512, 4), (2048, 16))
N_GROUPS = 3
HPG = 4
ATT_HEAD_DIM = 128
ATT_HEADS = N_GROUPS * HPG
ATT_TAPS = 128
Q_BLOCK = 128
ATT_TILE = 2048
BAND_HEADS = 4
PROJ_PARTS = 4
NORM_PARTS = 4
FF_SLICE = 2048
SPLIT_STRIDE = 4
D_FF = 4 * D_MODEL
EPS = 1e-6

RET_QK_W = RET_HEADS * RET_DK
RET_V_W = RET_HEADS * RET_DV
GROUP_W = HPG * ATT_HEAD_DIM
ATT_W = ATT_HEADS * ATT_HEAD_DIM
D_IN = 2 * RET_QK_W + 2 * RET_V_W + 3 * ATT_W + 2 * D_MODEL

MIX_W = 2 * D_MODEL + 2 * RET_QK_W + 2 * RET_V_W
COL_GATES = 0
COL_RQ = 2 * D_MODEL
COL_RK = COL_RQ + RET_QK_W
COL_RV = COL_RK + RET_QK_W
COL_RG = COL_RV + RET_V_W
ATT_Q, ATT_K, ATT_V = 0, 1, 2
COL_BLOCK = 512

LSE_LANES = 128
STEP_CACHE_ROWS = 8192
RET_ROWS_PER_STEP = 32
VMEM_LIMIT = 48 * 1024 * 1024
VMEM_LIMIT_MIX_FFN = 56 * 1024 * 1024

_BF16 = jnp.bfloat16
_F32 = jnp.float32


def _params(*sem, vmem_limit=VMEM_LIMIT):
    return pltpu.CompilerParams(dimension_semantics=sem, vmem_limit_bytes=vmem_limit)


def _rmsnorm(x, g):
    return x * lax.rsqrt(jnp.mean(x * x, axis=-1, keepdims=True) + EPS) * g


def _dot(a, b):
    return jnp.dot(a, b, preferred_element_type=_F32)


def _dot_nt(a, b):
    return lax.dot_general(a, b, (((1,), (1,)), ((), ())), preferred_element_type=_F32)


def _dot_tn(a, b):
    return lax.dot_general(a, b, (((0,), (0,)), ((), ())), preferred_element_type=_F32)


def _norm_proj_kernel(x_ref, g_ref, w_ref, z_ref, h_ref, *, gate_blocks):
    j = pl.program_id(1)
    tm = x_ref.shape[0]
    part = tm // NORM_PARTS if tm % (NORM_PARTS * 16) == 0 else tm

    def by_parts(with_norm, with_sigmoid):
        for a in range(tm // part):
            rows = slice(a * part, (a + 1) * part)
            if with_norm:
                h_ref[rows, :] = _rmsnorm(x_ref[rows, :], g_ref[...]).astype(_BF16)
            z = _dot(h_ref[rows, :], w_ref[...])
            z_ref[rows, :] = (jax.nn.sigmoid(z) if with_sigmoid else z).astype(z_ref.dtype)

    @pl.when(j == 0)
    def _():
        by_parts(True, gate_blocks > 0)

    if gate_blocks > 1:
        @pl.when((j > 0) & (j < gate_blocks))
        def _():
            by_parts(False, True)

    @pl.when(j >= max(gate_blocks, 1))
    def _():
        z_ref[...] = _dot(h_ref[...], w_ref[...]).astype(z_ref.dtype)


def _norm_proj(x, g, w, out_dtype, *, tm, tn=COL_BLOCK, gate_cols=0):
    n, d = x.shape
    d_out = w.shape[1]
    return pl.pallas_call(
        functools.partial(_norm_proj_kernel, gate_blocks=gate_cols // tn),
        grid=(n // tm, d_out // tn),
        in_specs=[
            pl.BlockSpec((tm, d), lambda i, j: (i, 0)),
            pl.BlockSpec((1, d), lambda i, j: (0, 0)),
            pl.BlockSpec((d, tn), lambda i, j: (0, j)),
        ],
        out_specs=pl.BlockSpec((tm, tn), lambda i, j: (i, j)),
        out_shape=jax.ShapeDtypeStruct((n, d_out), out_dtype),
        scratch_shapes=[pltpu.VMEM((tm, d), _BF16)],
        compiler_params=_params("parallel", "arbitrary"),
        name="norm_proj",
    )(x, g, w)


def _norm_proj_att_kernel(x_ref, g_ref, w_ref, z_ref, h_ref, stage_ref, stage2_ref):
    j = pl.program_id(1)
    part = ATT_TILE // PROJ_PARTS
    for group, (_, dil) in enumerate(ATT_GROUPS):
        @pl.when(j == group)
        def _(group=group, dil=dil):
            if group == 0:
                units = [(a, which) for a in range(PROJ_PARTS) for which in range(3)]
            else:
                units = [(a, which) for which in range(3) for a in range(PROJ_PARTS)]
            for a, which in units:
                rows = slice(a * part, (a + 1) * part)
                col0 = which * GROUP_W
                if group == 0 and which == 0:
                    h_ref[rows, :] = _rmsnorm(x_ref[rows, :], g_ref[...]).astype(_BF16)
                res = _dot(h_ref[rows, :], w_ref[:, col0:col0 + GROUP_W])
                if dil == 1:
                    z_ref[rows, col0:col0 + GROUP_W] = res.astype(_BF16)
                    continue
                per, sub = ATT_TILE // dil, part // dil
                for c in range(HPG):
                    stage_ref[c, rows, :] = res[:, c * ATT_HEAD_DIM:(c + 1) * ATT_HEAD_DIM]
                if dil > SPLIT_STRIDE:
                    seg = part // SPLIT_STRIDE
                    for c in range(HPG):
                        for lo in range(SPLIT_STRIDE):
                            stage2_ref[c, lo * seg:(lo + 1) * seg, :] = (
                                stage_ref[c, pl.ds(a * part + lo, seg, stride=SPLIT_STRIDE), :])
                for r in range(dil):
                    for c in range(HPG):
                        cols = slice(col0 + c * ATT_HEAD_DIM, col0 + (c + 1) * ATT_HEAD_DIM)
                        if dil > SPLIT_STRIDE:
                            lo, hi = r % SPLIT_STRIDE, r // SPLIT_STRIDE
                            piece = stage2_ref[c, pl.ds(lo * seg + hi, sub, stride=dil // SPLIT_STRIDE), :]
                        else:
                            piece = stage_ref[c, pl.ds(a * part + r, sub, stride=dil), :]
                        z_ref[r * per + a * sub:r * per + (a + 1) * sub, cols] = piece.astype(_BF16)


def _norm_proj_att(x, g, w):
    n, d = x.shape
    d_out = w.shape[1]
    return pl.pallas_call(
        _norm_proj_att_kernel,
        grid=(n // ATT_TILE, d_out // (3 * GROUP_W)),
        in_specs=[
            pl.BlockSpec((ATT_TILE, d), lambda i, j: (i, 0)),
            pl.BlockSpec((1, d), lambda i, j: (0, 0)),
            pl.BlockSpec((d, 3 * GROUP_W), lambda i, j: (0, j)),
        ],
        out_specs=pl.BlockSpec((ATT_TILE, 3 * GROUP_W), lambda i, j: (i, j)),
        out_shape=jax.ShapeDtypeStruct((n, d_out), _BF16),
        scratch_shapes=[pltpu.VMEM((ATT_TILE, d), _BF16), pltpu.VMEM((HPG, ATT_TILE, ATT_HEAD_DIM), _F32),
                        pltpu.VMEM((HPG, ATT_TILE // PROJ_PARTS, ATT_HEAD_DIM), _F32)],
        compiler_params=_params("parallel", "arbitrary"),
        name="norm_proj_att",
    )(x, g, w)


def _kv_window_kernel(x_ref, g_ref, wk_ref, wv_ref, ko_ref, vo_ref):
    tm = x_ref.shape[0]
    h = _rmsnorm(x_ref[...], g_ref[...]).astype(_BF16)
    for w_ref, o_ref in ((wk_ref, ko_ref), (wv_ref, vo_ref)):
        res = _dot(h, w_ref[...])
        for hd in range(HPG):
            o_ref[pl.ds(hd, tm, stride=HPG), :] = res[:, hd * ATT_HEAD_DIM:(hd + 1) * ATT_HEAD_DIM]


def _kv_window(x3, g, w_grouped, group, keep, prev, layer, depth):
    b, t, d = x3.shape
    tm = min(keep, 512)
    first = (t - keep) // tm
    wspec = lambda which: pl.BlockSpec((d, GROUP_W), lambda bi, i: (0, 3 * group + which))
    ospec = pl.BlockSpec((None, None, tm * HPG, ATT_HEAD_DIM), lambda bi, i: (layer, bi, i, 0))
    oshape = jax.ShapeDtypeStruct((depth, b, keep * HPG, ATT_HEAD_DIM), _F32)
    kernel, args, extra_specs, aliases = _kv_window_kernel, [x3, g, w_grouped, w_grouped], [], {}
    if prev is not None:
        extra_specs = [pl.BlockSpec(memory_space=pl.ANY)] * 2
        aliases = {4: 0, 5: 1}
        kernel = functools.partial(_drop_refs, kernel, 4, 2)
        args += list(prev)
    return pl.pallas_call(
        kernel,
        grid=(b, keep // tm),
        in_specs=[pl.BlockSpec((None, tm, d), lambda bi, i: (bi, first + i, 0)),
                  pl.BlockSpec((1, d), lambda bi, i: (0, 0)), wspec(ATT_K), wspec(ATT_V)] + extra_specs,
        out_specs=[ospec, ospec],
        out_shape=[oshape, oshape],
        input_output_aliases=aliases,
        compiler_params=_params("parallel", "parallel"),
        name="kv_window_g%d" % group,
    )(*args)


def _ret_log_gamma():
    return jnp.log1p(-jnp.exp(jnp.linspace(math.log(1.0 / 32), math.log(1.0 / 512), RET_HEADS))).astype(_F32)


def _retention_tables(chunk):
    lg = _ret_log_gamma()
    pos = jnp.arange(chunk, dtype=_F32)
    diff = pos[:, None] - pos[None, :]
    intra = jnp.where(diff[None] >= 0, jnp.exp(lg[:, None, None] * jnp.maximum(diff, 0.0)[None]), 0.0)
    xi = jnp.exp(lg[:, None] * (pos[None] + 1.0))
    zeta = jnp.exp(lg[:, None] * (chunk - 1.0 - pos)[None])
    decay = jnp.exp(lg * chunk)
    return (intra * (RET_DK ** -0.5),
            jnp.broadcast_to(xi[:, :, None], (RET_HEADS, chunk, RET_DK)),
            jnp.broadcast_to(zeta[:, :, None], (RET_HEADS, chunk, RET_DV)),
            jnp.broadcast_to(decay[:, None, None], (RET_HEADS, 8, RET_DV)))


def _retention_kernel(q_ref, k_ref, v_ref, g_ref, s0_ref, intra_ref, xi_ref, zeta_ref, decay_ref,
                      o_ref, sfin_ref, state, *, batches, **kw):
    for bb in range(batches):
        _retention_one(q_ref.at[bb], k_ref.at[bb], v_ref.at[bb], g_ref.at[bb], s0_ref.at[bb], intra_ref, xi_ref,
                       zeta_ref, decay_ref, o_ref.at[bb], sfin_ref.at[bb], state.at[bb], **kw)


def _retention_one(q_ref, k_ref, v_ref, g_ref, s0_ref, intra_ref, xi_ref, zeta_ref, decay_ref,
                   o_ref, sfin_ref, state, *, chunk, n_chunks):
    j = pl.program_id(1)

    @pl.when(j == 0)
    def _():
        state[...] = s0_ref[...]

    for c in range(n_chunks):
        rows = slice(c * chunk, (c + 1) * chunk)
        for h in range(RET_HEADS):
            qk_cols = slice(h * RET_DK, (h + 1) * RET_DK)
            v_cols = slice(h * RET_DV, (h + 1) * RET_DV)
            q = q_ref[rows, qk_cols]
            kb = k_ref[rows, qk_cols]
            v = v_ref[rows, v_cols]
            g = g_ref[rows, v_cols].astype(_F32)
            r_prev = state[h]
            scores = _dot_nt(q, kb) * intra_ref[h]
            o = _dot(scores.astype(_BF16), v)
            o = o + _dot((q * xi_ref[h]).astype(_BF16), r_prev.astype(_BF16))
            u = _dot_tn(kb, (v * zeta_ref[h]).astype(_BF16)) * (RET_DK ** -0.5)
            state[h] = decay_ref[h, 0:1, :] * r_prev + u
            mu = jnp.mean(o, axis=-1, keepdims=True)
            oc = o - mu
            var = jnp.mean(oc * oc, axis=-1, keepdims=True)
            on = oc * lax.rsqrt(var + EPS)
            o_ref[rows, v_cols] = (on * (g * jax.nn.sigmoid(g))).astype(o_ref.dtype)

    @pl.when(j == pl.num_programs(1) - 1)
    def _():
        sfin_ref[...] = state[...]


def _retention(z3, states0, layer0, prev_states, layer, depth, *, chunk, n_chunks):
    b, t, _ = z3.shape
    tc = chunk * n_chunks
    batches = max(1, min(b, RET_ROWS_PER_STEP // tc))
    intra, xi, zeta, decay = _retention_tables(chunk)
    const = lambda shape: pl.BlockSpec(shape, lambda bi, j: (0,) * len(shape))
    state_block = (None, batches, RET_HEADS, RET_DK, RET_DV)
    kernel = functools.partial(_retention_kernel, batches=batches, chunk=chunk, n_chunks=n_chunks)
    args = [z3, z3, z3, z3, states0, intra, xi, zeta, decay]
    extra_specs, aliases = [], {}
    if prev_states is not None:
        extra_specs = [pl.BlockSpec(memory_space=pl.ANY)]
        aliases = {len(args): 1}
        kernel = functools.partial(_drop_refs, kernel, len(args), 1)
        args.append(prev_states)
    return pl.pallas_call(
        kernel,
        grid=(b // batches, t // tc),
        in_specs=[
            pl.BlockSpec((batches, tc, RET_QK_W), lambda bi, j: (bi, j, COL_RQ // RET_QK_W)),
            pl.BlockSpec((batches, tc, RET_QK_W), lambda bi, j: (bi, j, COL_RK // RET_QK_W)),
            pl.BlockSpec((batches, tc, RET_V_W), lambda bi, j: (bi, j, COL_RV // RET_V_W)),
            pl.BlockSpec((batches, tc, RET_V_W), lambda bi, j: (bi, j, COL_RG // RET_V_W)),
            pl.BlockSpec(state_block, lambda bi, j: (layer0, bi, 0, 0, 0)),
            const((RET_HEADS, chunk, chunk)),
            const((RET_HEADS, chunk, RET_DK)),
            const((RET_HEADS, chunk, RET_DV)),
            const((RET_HEADS, 8, RET_DV)),
        ] + extra_specs,
        out_specs=[
            pl.BlockSpec((batches, tc, RET_V_W), lambda bi, j: (bi, j, 0)),
            pl.BlockSpec(state_block, lambda bi, j: (layer, bi, 0, 0, 0)),
        ],
        out_shape=[
            jax.ShapeDtypeStruct((b, t, RET_V_W), _BF16),
            jax.ShapeDtypeStruct((depth, b, RET_HEADS, RET_DK, RET_DV), _F32),
        ],
        scratch_shapes=[pltpu.VMEM((batches, RET_HEADS, RET_DK, RET_DV), _F32)],
        input_output_aliases=aliases,
        compiler_params=_params("parallel", "arbitrary"),
        name="retention",
    )(*args)


def _alibi_slope(head):
    return 2.0 ** (-8.0 * (head + 1.0) / ATT_HEADS)


def _pack_head_stats(cols):
    rows = cols[0].shape[0]
    lane = lax.broadcasted_iota(jnp.int32, (rows, LSE_LANES), 1)
    out = jnp.zeros((rows, LSE_LANES), _F32)
    for h, c in enumerate(cols):
        out = jnp.where(lane == h, c, out)
    return out


def _band_attn_kernel(q_ref, k_ref, v_ref, o_ref, lse_ref, kprev, vprev, stats, *, dil, group):
    i = pl.program_id(1)
    hp = pl.program_id(2)
    blk = Q_BLOCK
    per = ATT_TILE // dil
    row = lax.broadcasted_iota(jnp.int32, (blk, 2 * blk), 0)
    col = lax.broadcasted_iota(jnp.int32, (blk, 2 * blk), 1)
    delta = row + blk - col
    in_band = (delta >= 0) & (delta <= ATT_TAPS)
    first_cols = col >= jnp.where(i > 0, 0, blk)
    lane = lax.broadcasted_iota(jnp.int32, (blk, LSE_LANES), 1)
    scale = ATT_HEAD_DIM ** -0.5

    def token_rows(r, sb):
        start = r + sb * blk * dil
        return pl.ds(start, blk, stride=dil) if dil > 1 else pl.ds(start, blk)

    @pl.when(i == 0)
    def _():
        for hh in range(BAND_HEADS):
            kprev[hp * BAND_HEADS + hh] = jnp.zeros((ATT_TILE, ATT_HEAD_DIM), _BF16)
            vprev[hp * BAND_HEADS + hh] = jnp.zeros((ATT_TILE, ATT_HEAD_DIM), _BF16)

    for hh in range(BAND_HEADS):
        head = hp * BAND_HEADS + hh
        cols = slice(hh * ATT_HEAD_DIM, (hh + 1) * ATT_HEAD_DIM)
        slope = jnp.float32(0.0)
        for p in range(HPG // BAND_HEADS):
            slope = jnp.where(hp == p, jnp.float32(_alibi_slope(group * HPG + p * BAND_HEADS + hh)), slope)
        bias = jnp.where(in_band, -slope * (delta * dil).astype(_F32), -jnp.inf)
        bias_first = jnp.where(first_cols, bias, -jnp.inf)

        for r in range(dil):
            for sb in range(per // blk):
                lo = r * per + sb * blk
                if sb == 0:
                    last = slice((r + 1) * per - blk, (r + 1) * per)
                    keys = jnp.concatenate([kprev[head, last, :], k_ref[lo:lo + blk, cols]], axis=0)
                    values = jnp.concatenate([vprev[head, last, :], v_ref[lo:lo + blk, cols]], axis=0)
                else:
                    keys = k_ref[lo - blk:lo + blk, cols]
                    values = v_ref[lo - blk:lo + blk, cols]
                s = _dot_nt(q_ref[lo:lo + blk, cols], keys) * scale + (bias_first if sb == 0 else bias)
                m = jnp.max(s, axis=-1, keepdims=True)
                e = jnp.exp(s - m)
                den = jnp.sum(e, axis=-1, keepdims=True)
                o_ref[hh, token_rows(r, sb), :] = _dot((e / den).astype(_BF16), values)
                prev = jnp.where(head == 0, 0.0, stats[lo:lo + blk, :])
                stats[lo:lo + blk, :] = jnp.where(lane == head, m + jnp.log(den), prev)

        kprev[head] = k_ref[:, cols]
        vprev[head] = v_ref[:, cols]

    @pl.when(hp == HPG // BAND_HEADS - 1)
    def _():
        for r in range(dil):
            for sb in range(per // blk):
                lo = r * per + sb * blk
                lse_ref[token_rows(r, sb), :] = stats[lo:lo + blk, :]


def _band_attention(z3, group):
    b, t, _ = z3.shape
    _, dil = ATT_GROUPS[group]
    tiles = t // ATT_TILE
    steps = HPG // BAND_HEADS
    spec = lambda which: pl.BlockSpec((None, ATT_TILE, BAND_HEADS * ATT_HEAD_DIM),
                                      lambda bi, i, hp: (bi, i, (3 * group + which) * steps + hp))
    carry = pltpu.VMEM((HPG, ATT_TILE, ATT_HEAD_DIM), _BF16)
    return pl.pallas_call(
        functools.partial(_band_attn_kernel, dil=dil, group=group),
        grid=(b, tiles, steps),
        in_specs=[spec(ATT_Q), spec(ATT_K), spec(ATT_V)],
        out_specs=[
            pl.BlockSpec((BAND_HEADS, ATT_TILE, ATT_HEAD_DIM), lambda bi, i, hp: (hp, bi * tiles + i, 0)),
            pl.BlockSpec((ATT_TILE, LSE_LANES), lambda bi, i, hp: (bi * tiles + i, 0)),
        ],
        out_shape=[
            jax.ShapeDtypeStruct((HPG, b * t, ATT_HEAD_DIM), _F32),
            jax.ShapeDtypeStruct((b * t, LSE_LANES), _F32),
        ],
        scratch_shapes=[carry, carry, pltpu.VMEM((ATT_TILE, LSE_LANES), _F32)],
        compiler_params=_params("parallel", "arbitrary", "arbitrary"),
        name="band_attention_g%d" % group,
    )(z3, z3, z3)


def _step_attn_kernel(q_ref, kn_ref, vn_ref, kc_ref, vc_ref, o_ref, lse_ref, ko_ref, vo_ref, *, batches, t_new, **kw):
    for bb in range(batches):
        tokens = pl.ds(bb * t_new, t_new)
        _step_attn_one(q_ref.at[bb], kn_ref.at[bb], vn_ref.at[bb], kc_ref.at[bb], vc_ref.at[bb],
                       o_ref.at[:, tokens, :], lse_ref.at[tokens, :], ko_ref.at[bb], vo_ref.at[bb], t_new=t_new, **kw)


def _step_attn_one(q_ref, kn_ref, vn_ref, kc_ref, vc_ref, o_ref, lse_ref, ko_ref, vo_ref, *,
                   t_new, cache_len, dil, group):
    n_res = min(dil, t_new)
    taps = cache_len // dil
    rows_all = HPG * t_new
    head_cols = [slice(h * ATT_HEAD_DIM, (h + 1) * ATT_HEAD_DIM) for h in range(HPG)]
    head_rows = [slice(h * t_new, (h + 1) * t_new) for h in range(HPG)]
    q = [q_ref[:, c].astype(_BF16) for c in head_cols]
    k_new = kn_ref[...]
    v_new = vn_ref[...]

    def cache_taps(ref, r, h):
        return ref[pl.ds(r * HPG + h, taps, stride=HPG * dil), :].astype(_BF16)

    scale = ATT_HEAD_DIM ** -0.5
    log2_dil = dil.bit_length() - 1
    log2_new = t_new.bit_length() - 1

    def row_terms(width):
        row_id = lax.broadcasted_iota(jnp.int32, (rows_all, width), 0)
        slope = jnp.zeros((rows_all, width), _F32)
        for h in range(HPG):
            slope = jnp.where((row_id >> log2_new) == h, _alibi_slope(group * HPG + h), slope)
        return row_id & (t_new - 1), slope

    query, slope = row_terms(taps)
    residue = query & (dil - 1)
    taps_back = taps + (query >> log2_dil) - lax.broadcasted_iota(jnp.int32, (rows_all, taps), 1)
    s_cache = None
    for r in range(n_res):
        s_r = jnp.concatenate([_dot_nt(q[h], cache_taps(kc_ref, r, h)) for h in range(HPG)], axis=0)
        s_cache = s_r if s_cache is None else jnp.where(residue == r, s_r, s_cache)
    s_cache = s_cache * scale - slope * (taps_back << log2_dil).astype(_F32)
    s_cache = jnp.where(taps_back <= ATT_TAPS, s_cache, -jnp.inf)

    query_n, slope_n = row_terms(t_new)
    back = query_n - lax.broadcasted_iota(jnp.int32, (rows_all, t_new), 1)
    s_new = jnp.concatenate([_dot_nt(q[h], k_new[:, head_cols[h]].astype(_BF16)) for h in range(HPG)], axis=0)
    s_new = s_new * scale - slope_n * back.astype(_F32)
    s_new = jnp.where((back >= 0) & ((back & (dil - 1)) == 0), s_new, -jnp.inf)

    m = jnp.maximum(jnp.max(s_cache, axis=-1, keepdims=True), jnp.max(s_new, axis=-1, keepdims=True))
    e_cache = jnp.exp(s_cache - m)
    e_new = jnp.exp(s_new - m)
    den = jnp.sum(e_cache, axis=-1, keepdims=True) + jnp.sum(e_new, axis=-1, keepdims=True)
    p_cache = e_cache / den
    p_new = (e_new / den).astype(_BF16)
    lse = m + jnp.log(den)

    for h in range(HPG):
        out = _dot(p_new[head_rows[h]], v_new[:, head_cols[h]].astype(_BF16))
        for r in range(n_res):
            p_r = p_cache[head_rows[h]]
            if n_res > 1:
                p_r = jnp.where(residue[head_rows[h]] == r, p_r, 0.0)
            out = out + _dot(p_r.astype(_BF16), cache_taps(vc_ref, r, h))
        o_ref[h] = out
    lse_ref[...] = _pack_head_stats([lse[rows] for rows in head_rows])

    keep = (cache_len - t_new) * HPG
    ko_ref[0:keep, :] = kc_ref[t_new * HPG:, :]
    vo_ref[0:keep, :] = vc_ref[t_new * HPG:, :]
    for h in range(HPG):
        ko_ref[pl.ds(keep + h, t_new, stride=HPG), :] = k_new[:, head_cols[h]]
        vo_ref[pl.ds(keep + h, t_new, stride=HPG), :] = v_new[:, head_cols[h]]


def _step_attention(z3, cache_k, cache_v, prev_k, prev_v, layer, group):
    b, t_new, _ = z3.shape
    cache_len = cache_k.shape[2] // HPG
    _, dil = ATT_GROUPS[group]
    cq, ck, cv = (3 * group + which for which in (ATT_Q, ATT_K, ATT_V))
    batches = max(1, min(b, STEP_CACHE_ROWS // (cache_len * HPG)))
    zspec = lambda c: pl.BlockSpec((batches, t_new, GROUP_W), lambda bi: (bi, 0, c))
    cache_spec = pl.BlockSpec((None, batches, cache_len * HPG, ATT_HEAD_DIM), lambda bi: (layer, bi, 0, 0))
    kernel = functools.partial(_step_attn_kernel, batches=batches, t_new=t_new, cache_len=cache_len, dil=dil,
                               group=group)
    in_specs = [zspec(cq), zspec(ck), zspec(cv), cache_spec, cache_spec]
    args = [z3, z3, z3, cache_k, cache_v]
    aliases = {}
    if prev_k is not None:
        in_specs += [pl.BlockSpec(memory_space=pl.ANY)] * 2
        args += [prev_k, prev_v]
        aliases = {5: 2, 6: 3}
        kernel = functools.partial(_drop_refs, kernel, 5, 2)
    cache_shape = jax.ShapeDtypeStruct(cache_k.shape, cache_k.dtype)
    return pl.pallas_call(
        kernel,
        grid=(b // batches,),
        in_specs=in_specs,
        out_specs=[
            pl.BlockSpec((HPG, batches * t_new, ATT_HEAD_DIM), lambda bi: (0, bi, 0)),
            pl.BlockSpec((batches * t_new, LSE_LANES), lambda bi: (bi, 0)),
            cache_spec, cache_spec,
        ],
        out_shape=[
            jax.ShapeDtypeStruct((HPG, b * t_new, ATT_HEAD_DIM), _F32),
            jax.ShapeDtypeStruct((b * t_new, LSE_LANES), _F32),
            cache_shape, cache_shape,
        ],
        input_output_aliases=aliases,
        compiler_params=_params("parallel"),
        name="step_attention_g%d" % group,
    )(*args)


def _drop_refs(kernel, start, count, *refs):
    return kernel(*refs[:start], *refs[start + count:])


def _mix_ffn_kernel(x_ref, oret_ref, og0_ref, og1_ref, og2_ref, l0_ref, l1_ref, l2_ref, gates_ref,
                    wret_ref, watt_ref, wout_ref, g_ref, wup_ref, wdown_ref, p_ref, wple_ref, wgate_ref, gfin_ref,
                    y_ref, *, final_norm):
    l0, l1, l2 = l0_ref[...], l1_ref[...], l2_ref[...]
    mx = jnp.maximum(jnp.maximum(l0, l1), l2)
    e0, e1, e2 = jnp.exp(l0 - mx), jnp.exp(l1 - mx), jnp.exp(l2 - mx)
    tot = e0 + e1 + e2
    w0, w1, w2 = e0 / tot, e1 / tot, e2 / tot
    heads = []
    for h in range(HPG):
        heads.append(w0[:, h:h + 1] * og0_ref[h] + w1[:, h:h + 1] * og1_ref[h] + w2[:, h:h + 1] * og2_ref[h])
    o_att = jnp.concatenate(heads, axis=-1).astype(_BF16)
    br_ret = _dot(oret_ref[...], wret_ref[...])
    br_att = _dot(o_att, watt_ref[...])
    gate_ret = gates_ref[:, 0:D_MODEL].astype(_F32)
    gate_att = gates_ref[:, D_MODEL:2 * D_MODEL].astype(_F32)
    mix = gate_ret * br_ret + gate_att * br_att
    x = x_ref[...] + _dot(mix.astype(_BF16), wout_ref[...])

    h = _rmsnorm(x, g_ref[...]).astype(_BF16)
    for c in range(D_FF // FF_SLICE):
        cols = slice(c * FF_SLICE, (c + 1) * FF_SLICE)
        u = jnp.maximum(_dot(h, wup_ref[:, cols]), 0.0)
        x = x + _dot((u * u).astype(_BF16), wdown_ref[cols, :])

    gate = jax.nn.sigmoid(_dot(x.astype(_BF16), wgate_ref[...]))
    x = x + _dot(p_ref[...].astype(_BF16), wple_ref[...]) * gate
    if final_norm:
        x = _rmsnorm(x, gfin_ref[...])
    y_ref[...] = x


def _mix_ffn(x, o_ret, ogs, lses, z_mix, p_all, layer, lw, g_final, *, tm, final_norm):
    n, d = x.shape
    row = lambda w: pl.BlockSpec((tm, w), lambda i: (i, 0))
    heads = pl.BlockSpec((HPG, tm, ATT_HEAD_DIM), lambda i: (0, i, 0))
    once = lambda a: pl.BlockSpec(a.shape, lambda i: (0, 0), pipeline_mode=pl.Buffered(1))
    weights = [lw[k] for k in ("w_ret_br", "w_att_br", "w_out", "norm_ffn", "w_up", "w_down")]
    tail = [lw["w_ple"], lw["w_ple_gate"], g_final]
    return pl.pallas_call(
        functools.partial(_mix_ffn_kernel, final_norm=final_norm),
        grid=(n // tm,),
        in_specs=[row(d), row(RET_V_W), heads, heads, heads,
                  row(LSE_LANES), row(LSE_LANES), row(LSE_LANES), row(2 * D_MODEL)]
        + [once(w) for w in weights]
        + [pl.BlockSpec((None, tm, D_PLE), lambda i: (layer, i, 0))]
        + [once(w) for w in tail],
        out_specs=row(d),
        out_shape=jax.ShapeDtypeStruct((n, d), _F32),
        compiler_params=_params("parallel", vmem_limit=VMEM_LIMIT_MIX_FFN),
        name="mix_ffn",
    )(x, o_ret, *ogs, *lses, z_mix, *weights, p_all, *tail)


def _layer_weights(norm_mix, w_in, w_ret_br, w_att_br, w_out, norm_ffn, w_up, w_down, w_ple, w_ple_gate, i):
    w = w_in[i]
    att_start = 2 * RET_QK_W + 2 * RET_V_W
    w_mix = jnp.concatenate([w[:, D_IN - 2 * D_MODEL:], w[:, :att_start]], axis=1)
    return dict(
        norm_mix=norm_mix[i][None, :], w_in_mix=w_mix.astype(_BF16),
        w_in_att_grouped=jnp.concatenate(
            [w[:, att_start + which * ATT_W + g * GROUP_W:att_start + which * ATT_W + (g + 1) * GROUP_W]
             for g in range(N_GROUPS) for which in range(3)], axis=1).astype(_BF16),
        w_ret_br=w_ret_br[i].astype(_BF16), w_att_br=w_att_br[i].astype(_BF16), w_out=w_out[i].astype(_BF16),
        norm_ffn=norm_ffn[i][None, :], w_up=w_up[i].astype(_BF16), w_down=w_down[i].astype(_BF16),
        w_ple=w_ple[i].astype(_BF16), w_ple_gate=w_ple_gate[i].astype(_BF16))


def kernel(x_prompt, x_sample, cache_win_k0, cache_win_v0, cache_win_k1, cache_win_v1, cache_win_k2, cache_win_v2,
           state_ret, p_prompt, p_sample, norm_mix, w_in, w_ret_br, w_att_br, w_out, norm_ffn, w_up, w_down,
           w_ple, w_ple_gate, norm_final):
    depth = w_in.shape[0]
    bp, tp, d = x_prompt.shape
    bs, ts, _ = x_sample.shape
    xp = x_prompt.reshape(bp * tp, d)
    xs = x_sample.reshape(bs * ts, d)
    g_final = norm_final[None, :]
    pp = p_prompt.reshape(depth, bp * tp, D_PLE)
    ps = p_sample.reshape(depth, bs * ts, D_PLE)
    pos_head_rows = lambda c: c.reshape(c.shape[:2] + (c.shape[2] * HPG, ATT_HEAD_DIM))
    caches_k = [pos_head_rows(c) for c in (cache_win_k0, cache_win_k1, cache_win_k2)]
    caches_v = [pos_head_rows(c) for c in (cache_win_v0, cache_win_v1, cache_win_v2)]
    new_k = [None] * N_GROUPS
    new_v = [None] * N_GROUPS
    windows = [None] * N_GROUPS
    prompt_ret = sample_ret = None
    zero_state = jnp.zeros((1, bp, RET_HEADS, RET_DK, RET_DV), _F32)
    sample_chunk = math.gcd(ts, RET_CHUNK)

    for i in range(depth):
        lw = _layer_weights(norm_mix, w_in, w_ret_br, w_att_br, w_out, norm_ffn, w_up, w_down, w_ple, w_ple_gate, i)
        last = i == depth - 1

        z_mix = _norm_proj(xp, lw["norm_mix"], lw["w_in_mix"], _BF16, tm=2048, tn=1024, gate_cols=2 * D_MODEL)
        z_att = _norm_proj_att(xp, lw["norm_mix"], lw["w_in_att_grouped"]).reshape(bp, tp, 3 * ATT_W)
        o_ret, prompt_ret = _retention(z_mix.reshape(bp, tp, MIX_W), zero_state, 0, prompt_ret, i, depth,
                                       chunk=RET_CHUNK, n_chunks=4)
        ogs, lses = zip(*[_band_attention(z_att, g) for g in range(N_GROUPS)])
        for g, (window, _) in enumerate(ATT_GROUPS):
            windows[g] = _kv_window(xp.reshape(bp, tp, d), lw["norm_mix"], lw["w_in_att_grouped"], g, min(window, tp),
                                    windows[g], i, depth)
        xp = _mix_ffn(xp, o_ret.reshape(bp * tp, RET_V_W), ogs, lses, z_mix, pp, i, lw, g_final,
                      tm=512, final_norm=last)

        z_mix = _norm_proj(xs, lw["norm_mix"], lw["w_in_mix"], _BF16, tm=bs * ts, tn=1024, gate_cols=2 * D_MODEL)
        z3 = _norm_proj(xs, lw["norm_mix"], lw["w_in_att_grouped"], _F32, tm=bs * ts, tn=ATT_W).reshape(bs, ts, 3 * ATT_W)
        o_ret, sample_ret = _retention(z_mix.reshape(bs, ts, MIX_W), state_ret, i, sample_ret, i, depth,
                                       chunk=sample_chunk, n_chunks=ts // sample_chunk)
        ogs, lses = [], []
        for g in range(N_GROUPS):
            o_g, lse_g, new_k[g], new_v[g] = _step_attention(z3, caches_k[g], caches_v[g], new_k[g], new_v[g], i, g)
            ogs.append(o_g)
            lses.append(lse_g)
        xs = _mix_ffn(xs, o_ret.reshape(bs * ts, RET_V_W), ogs, lses, z_mix, ps, i, lw, g_final,
                      tm=bs * ts, final_norm=last)

    as_heads = lambda a: a.reshape(a.shape[:2] + (a.shape[2] // HPG, HPG, ATT_HEAD_DIM))
    prompt_windows = [as_heads(a) for kv in windows for a in kv]
    sample_windows = [as_heads(a) for g in range(N_GROUPS) for a in (new_k[g], new_v[g])]
    return (xp.reshape(bp, tp, d), xs.reshape(bs, ts, d), *prompt_windows, prompt_ret, *sample_windows, sample_ret)
```

```python
import functools
import math

import jax
import jax.numpy as jnp
from jax import lax
from jax.experimental import pallas as pl
from jax.experimental.pallas import tpu as pltpu

D_MODEL = 1024
D_PLE = 256
RET_HEADS = 4
RET_DK = 128
RET_DV = 256
RET_CHUNK = 128
ATT_GROUPS = ((128, 1), (512, 4), (2048, 16))
N_GROUPS = 3
HPG = 4
ATT_HEAD_DIM = 128
ATT_HEADS = N_GROUPS * HPG
ATT_TAPS = 128
Q_BLOCK = 128
ATT_TILE = 2048
BAND_HEADS = 4
PROJ_PARTS = 4
NORM_PARTS = 4
FF_SLICE = 2048
SPLIT_STRIDE = 4
D_FF = 4 * D_MODEL
EPS = 1e-6

RET_QK_W = RET_HEADS * RET_DK
RET_V_W = RET_HEADS * RET_DV
GROUP_W = HPG * ATT_HEAD_DIM
ATT_W = ATT_HEADS * ATT_HEAD_DIM
D_IN = 2 * RET_QK_W + 2 * RET_V_W + 3 * ATT_W + 2 * D_MODEL

MIX_W = 2 * D_MODEL + 2 * RET_QK_W + 2 * RET_V_W
COL_RQ = 2 * D_MODEL
COL_RK = COL_RQ + RET_QK_W
COL_RV = COL_RK + RET_QK_W
COL_RG = COL_RV + RET_V_W
ATT_Q, ATT_K, ATT_V = 0, 1, 2
COL_BLOCK = 512

LSE_LANES = 128

PROJ_ROWS = 2048
PROJ_COLS = 1024
MIX_FFN_ROWS = 512
RET_CHUNKS_PER_STEP = 4
STEP_CACHE_ROWS = 8192
RET_ROWS_PER_STEP = 32
VMEM_LIMIT = 48 * 1024 * 1024
VMEM_LIMIT_MIX_FFN = 56 * 1024 * 1024

_BF16 = jnp.bfloat16
_F32 = jnp.float32


def _params(*sem, vmem_limit=VMEM_LIMIT):
    return pltpu.CompilerParams(dimension_semantics=sem, vmem_limit_bytes=vmem_limit)


def _rmsnorm(x, g):
    return x * lax.rsqrt(jnp.mean(x * x, axis=-1, keepdims=True) + EPS) * g


def _dot(a, b):
    return jnp.dot(a, b, preferred_element_type=_F32)


def _dot_nt(a, b):
    return lax.dot_general(a, b, (((1,), (1,)), ((), ())), preferred_element_type=_F32)


def _dot_tn(a, b):
    return lax.dot_general(a, b, (((0,), (0,)), ((), ())), preferred_element_type=_F32)


def _norm_proj_kernel(x_ref, g_ref, w_ref, z_ref, h_ref, *, gate_blocks):
    j = pl.program_id(1)
    tm = x_ref.shape[0]
    part = tm // NORM_PARTS if tm % (NORM_PARTS * 16) == 0 else tm

    def by_parts(with_norm, with_sigmoid):
        for a in range(tm // part):
            rows = slice(a * part, (a + 1) * part)
            if with_norm:
                h_ref[rows, :] = _rmsnorm(x_ref[rows, :], g_ref[...]).astype(_BF16)
            z = _dot(h_ref[rows, :], w_ref[...])
            z_ref[rows, :] = (jax.nn.sigmoid(z) if with_sigmoid else z).astype(z_ref.dtype)

    @pl.when(j == 0)
    def _():
        by_parts(True, gate_blocks > 0)

    if gate_blocks > 1:
        @pl.when((j > 0) & (j < gate_blocks))
        def _():
            by_parts(False, True)

    @pl.when(j >= max(gate_blocks, 1))
    def _():
        z_ref[...] = _dot(h_ref[...], w_ref[...]).astype(z_ref.dtype)


def _norm_proj(x, g, w, out_dtype, *, tm, tn=COL_BLOCK, gate_cols=0):
    n, d = x.shape
    d_out = w.shape[1]
    return pl.pallas_call(
        functools.partial(_norm_proj_kernel, gate_blocks=gate_cols // tn),
        grid=(n // tm, d_out // tn),
        in_specs=[
            pl.BlockSpec((tm, d), lambda i, j: (i, 0)),
            pl.BlockSpec((1, d), lambda i, j: (0, 0)),
            pl.BlockSpec((d, tn), lambda i, j: (0, j)),
        ],
        out_specs=pl.BlockSpec((tm, tn), lambda i, j: (i, j)),
        out_shape=jax.ShapeDtypeStruct((n, d_out), out_dtype),
        scratch_shapes=[pltpu.VMEM((tm, d), _BF16)],
        compiler_params=_params("parallel", "arbitrary"),
        name="norm_proj",
    )(x, g, w)


def _norm_proj_att_kernel(x_ref, g_ref, w_ref, z_ref, h_ref, stage_ref, stage2_ref):
    j = pl.program_id(1)
    part = ATT_TILE // PROJ_PARTS
    for group, (_, dil) in enumerate(ATT_GROUPS):
        @pl.when(j == group)
        def _(group=group, dil=dil):
            if group == 0:
                units = [(a, which) for a in range(PROJ_PARTS) for which in range(3)]
            else:
                units = [(a, which) for which in range(3) for a in range(PROJ_PARTS)]
            for a, which in units:
                rows = slice(a * part, (a + 1) * part)
                col0 = which * GROUP_W
                if group == 0 and which == 0:
                    h_ref[rows, :] = _rmsnorm(x_ref[rows, :], g_ref[...]).astype(_BF16)
                res = _dot(h_ref[rows, :], w_ref[:, col0:col0 + GROUP_W])
                if dil == 1:
                    z_ref[rows, col0:col0 + GROUP_W] = res.astype(_BF16)
                    continue
                per, sub = ATT_TILE // dil, part // dil
                for c in range(HPG):
                    stage_ref[c, rows, :] = res[:, c * ATT_HEAD_DIM:(c + 1) * ATT_HEAD_DIM]
                if dil > SPLIT_STRIDE:
                    seg = part // SPLIT_STRIDE
                    for c in range(HPG):
                        for lo in range(SPLIT_STRIDE):
                            stage2_ref[c, lo * seg:(lo + 1) * seg, :] = (
                                stage_ref[c, pl.ds(a * part + lo, seg, stride=SPLIT_STRIDE), :])
                for r in range(dil):
                    for c in range(HPG):
                        cols = slice(col0 + c * ATT_HEAD_DIM, col0 + (c + 1) * ATT_HEAD_DIM)
                        if dil > SPLIT_STRIDE:
                            lo, hi = r % SPLIT_STRIDE, r // SPLIT_STRIDE
                            piece = stage2_ref[c, pl.ds(lo * seg + hi, sub, stride=dil // SPLIT_STRIDE), :]
                        else:
                            piece = stage_ref[c, pl.ds(a * part + r, sub, stride=dil), :]
                        z_ref[r * per + a * sub:r * per + (a + 1) * sub, cols] = piece.astype(_BF16)


def _norm_proj_att(x, g, w):
    n, d = x.shape
    d_out = w.shape[1]
    return pl.pallas_call(
        _norm_proj_att_kernel,
        grid=(n // ATT_TILE, d_out // (3 * GROUP_W)),
        in_specs=[
            pl.BlockSpec((ATT_TILE, d), lambda i, j: (i, 0)),
            pl.BlockSpec((1, d), lambda i, j: (0, 0)),
            pl.BlockSpec((d, 3 * GROUP_W), lambda i, j: (0, j)),
        ],
        out_specs=pl.BlockSpec((ATT_TILE, 3 * GROUP_W), lambda i, j: (i, j)),
        out_shape=jax.ShapeDtypeStruct((n, d_out), _BF16),
        scratch_shapes=[pltpu.VMEM((ATT_TILE, d), _BF16), pltpu.VMEM((HPG, ATT_TILE, ATT_HEAD_DIM), _F32),
                        pltpu.VMEM((HPG, ATT_TILE // PROJ_PARTS, ATT_HEAD_DIM), _F32)],
        compiler_params=_params("parallel", "arbitrary"),
        name="norm_proj_att",
    )(x, g, w)


def _kv_window_kernel(x_ref, g_ref, wk_ref, wv_ref, ko_ref, vo_ref):
    tm = x_ref.shape[0]
    h = _rmsnorm(x_ref[...], g_ref[...]).astype(_BF16)
    for w_ref, o_ref in ((wk_ref, ko_ref), (wv_ref, vo_ref)):
        res = _dot(h, w_ref[...])
        for hd in range(HPG):
            o_ref[pl.ds(hd, tm, stride=HPG), :] = res[:, hd * ATT_HEAD_DIM:(hd + 1) * ATT_HEAD_DIM]


def _kv_window(x3, g, w_grouped, group, keep, prev, layer, depth):
    b, t, d = x3.shape
    tm = min(keep, 512)
    first = (t - keep) // tm
    wspec = lambda which: pl.BlockSpec((d, GROUP_W), lambda bi, i: (0, 3 * group + which))
    ospec = pl.BlockSpec((None, None, tm * HPG, ATT_HEAD_DIM), lambda bi, i: (layer, bi, i, 0))
    oshape = jax.ShapeDtypeStruct((depth, b, keep * HPG, ATT_HEAD_DIM), _F32)
    kernel, args, extra_specs, aliases = _kv_window_kernel, [x3, g, w_grouped, w_grouped], [], {}
    if prev is not None:
        extra_specs = [pl.BlockSpec(memory_space=pl.ANY)] * 2
        aliases = {4: 0, 5: 1}
        kernel = functools.partial(_drop_refs, kernel, 4, 2)
        args += list(prev)
    return pl.pallas_call(
        kernel,
        grid=(b, keep // tm),
        in_specs=[pl.BlockSpec((None, tm, d), lambda bi, i: (bi, first + i, 0)),
                  pl.BlockSpec((1, d), lambda bi, i: (0, 0)), wspec(ATT_K), wspec(ATT_V)] + extra_specs,
        out_specs=[ospec, ospec],
        out_shape=[oshape, oshape],
        input_output_aliases=aliases,
        compiler_params=_params("parallel", "parallel"),
        name="kv_window_g%d" % group,
    )(*args)


def _ret_log_gamma():
    return jnp.log1p(-jnp.exp(jnp.linspace(math.log(1.0 / 32), math.log(1.0 / 512), RET_HEADS))).astype(_F32)


def _retention_tables(chunk):
    lg = _ret_log_gamma()
    pos = jnp.arange(chunk, dtype=_F32)
    diff = pos[:, None] - pos[None, :]
    intra = jnp.where(diff[None] >= 0, jnp.exp(lg[:, None, None] * jnp.maximum(diff, 0.0)[None]), 0.0)
    xi = jnp.exp(lg[:, None] * (pos[None] + 1.0))
    zeta = jnp.exp(lg[:, None] * (chunk - 1.0 - pos)[None])
    decay = jnp.exp(lg * chunk)
    return (intra * (RET_DK ** -0.5),
            jnp.broadcast_to(xi[:, :, None], (RET_HEADS, chunk, RET_DK)),
            jnp.broadcast_to(zeta[:, :, None], (RET_HEADS, chunk, RET_DV)),
            jnp.broadcast_to(decay[:, None, None], (RET_HEADS, 8, RET_DV)))


def _retention_kernel(q_ref, k_ref, v_ref, g_ref, s0_ref, intra_ref, xi_ref, zeta_ref, decay_ref,
                      o_ref, sfin_ref, state, *, batches, **kw):
    for bb in range(batches):
        _retention_one(q_ref.at[bb], k_ref.at[bb], v_ref.at[bb], g_ref.at[bb], s0_ref.at[bb], intra_ref, xi_ref,
                       zeta_ref, decay_ref, o_ref.at[bb], sfin_ref.at[bb], state.at[bb], **kw)


def _retention_one(q_ref, k_ref, v_ref, g_ref, s0_ref, intra_ref, xi_ref, zeta_ref, decay_ref,
                   o_ref, sfin_ref, state, *, chunk, n_chunks):
    j = pl.program_id(1)

    @pl.when(j == 0)
    def _():
        state[...] = s0_ref[...]

    for c in range(n_chunks):
        rows = slice(c * chunk, (c + 1) * chunk)
        for h in range(RET_HEADS):
            qk_cols = slice(h * RET_DK, (h + 1) * RET_DK)
            v_cols = slice(h * RET_DV, (h + 1) * RET_DV)
            q = q_ref[rows, qk_cols]
            kb = k_ref[rows, qk_cols]
            v = v_ref[rows, v_cols]
            g = g_ref[rows, v_cols].astype(_F32)
            r_prev = state[h]
            scores = _dot_nt(q, kb) * intra_ref[h]
            o = _dot(scores.astype(_BF16), v)
            o = o + _dot((q * xi_ref[h]).astype(_BF16), r_prev.astype(_BF16))
            u = _dot_tn(kb, (v * zeta_ref[h]).astype(_BF16)) * (RET_DK ** -0.5)
            state[h] = decay_ref[h, 0:1, :] * r_prev + u
            mu = jnp.mean(o, axis=-1, keepdims=True)
            oc = o - mu
            var = jnp.mean(oc * oc, axis=-1, keepdims=True)
            on = oc * lax.rsqrt(var + EPS)
            o_ref[rows, v_cols] = (on * (g * jax.nn.sigmoid(g))).astype(o_ref.dtype)

    @pl.when(j == pl.num_programs(1) - 1)
    def _():
        sfin_ref[...] = state[...]


def _retention(z3, states0, layer0, prev_states, layer, depth, *, chunk, n_chunks):
    b, t, _ = z3.shape
    tc = chunk * n_chunks
    batches = max(1, min(b, RET_ROWS_PER_STEP // tc))
    intra, xi, zeta, decay = _retention_tables(chunk)
    const = lambda shape: pl.BlockSpec(shape, lambda bi, j: (0,) * len(shape))
    state_block = (None, batches, RET_HEADS, RET_DK, RET_DV)
    kernel = functools.partial(_retention_kernel, batches=batches, chunk=chunk, n_chunks=n_chunks)
    args = [z3, z3, z3, z3, states0, intra, xi, zeta, decay]
    extra_specs, aliases = [], {}
    if prev_states is not None:
        extra_specs = [pl.BlockSpec(memory_space=pl.ANY)]
        aliases = {len(args): 1}
        kernel = functools.partial(_drop_refs, kernel, len(args), 1)
        args.append(prev_states)
    return pl.pallas_call(
        kernel,
        grid=(b // batches, t // tc),
        in_specs=[
            pl.BlockSpec((batches, tc, RET_QK_W), lambda bi, j: (bi, j, COL_RQ // RET_QK_W)),
            pl.BlockSpec((batches, tc, RET_QK_W), lambda bi, j: (bi, j, COL_RK // RET_QK_W)),
            pl.BlockSpec((batches, tc, RET_V_W), lambda bi, j: (bi, j, COL_RV // RET_V_W)),
            pl.BlockSpec((batches, tc, RET_V_W), lambda bi, j: (bi, j, COL_RG // RET_V_W)),
            pl.BlockSpec(state_block, lambda bi, j: (layer0, bi, 0, 0, 0)),
            const((RET_HEADS, chunk, chunk)),
            const((RET_HEADS, chunk, RET_DK)),
            const((RET_HEADS, chunk, RET_DV)),
            const((RET_HEADS, 8, RET_DV)),
        ] + extra_specs,
        out_specs=[
            pl.BlockSpec((batches, tc, RET_V_W), lambda bi, j: (bi, j, 0)),
            pl.BlockSpec(state_block, lambda bi, j: (layer, bi, 0, 0, 0)),
        ],
        out_shape=[
            jax.ShapeDtypeStruct((b, t, RET_V_W), _BF16),
            jax.ShapeDtypeStruct((depth, b, RET_HEADS, RET_DK, RET_DV), _F32),
        ],
        scratch_shapes=[pltpu.VMEM((batches, RET_HEADS, RET_DK, RET_DV), _F32)],
        input_output_aliases=aliases,
        compiler_params=_params("parallel", "arbitrary"),
        name="retention",
    )(*args)


def _alibi_slope(head):
    return 2.0 ** (-8.0 * (head + 1.0) / ATT_HEADS)


def _pack_head_stats(cols):
    rows = cols[0].shape[0]
    lane = lax.broadcasted_iota(jnp.int32, (rows, LSE_LANES), 1)
    out = jnp.zeros((rows, LSE_LANES), _F32)
    for h, c in enumerate(cols):
        out = jnp.where(lane == h, c, out)
    return out


def _band_attn_kernel(q_ref, k_ref, v_ref, o_ref, lse_ref, kprev, vprev, stats, *, dil, group):
    i = pl.program_id(1)
    hp = pl.program_id(2)
    blk = Q_BLOCK
    per = ATT_TILE // dil
    row = lax.broadcasted_iota(jnp.int32, (blk, 2 * blk), 0)
    col = lax.broadcasted_iota(jnp.int32, (blk, 2 * blk), 1)
    delta = row + blk - col
    in_band = (delta >= 0) & (delta <= ATT_TAPS)
    first_cols = col >= jnp.where(i > 0, 0, blk)
    lane = lax.broadcasted_iota(jnp.int32, (blk, LSE_LANES), 1)
    scale = ATT_HEAD_DIM ** -0.5

    def token_rows(r, sb):
        start = r + sb * blk * dil
        return pl.ds(start, blk, stride=dil) if dil > 1 else pl.ds(start, blk)

    @pl.when(i == 0)
    def _():
        for hh in range(BAND_HEADS):
            kprev[hp * BAND_HEADS + hh] = jnp.zeros((ATT_TILE, ATT_HEAD_DIM), _BF16)
            vprev[hp * BAND_HEADS + hh] = jnp.zeros((ATT_TILE, ATT_HEAD_DIM), _BF16)

    for hh in range(BAND_HEADS):
        head = hp * BAND_HEADS + hh
        cols = slice(hh * ATT_HEAD_DIM, (hh + 1) * ATT_HEAD_DIM)
        slope = jnp.float32(0.0)
        for p in range(HPG // BAND_HEADS):
            slope = jnp.where(hp == p, jnp.float32(_alibi_slope(group * HPG + p * BAND_HEADS + hh)), slope)
        bias = jnp.where(in_band, -slope * (delta * dil).astype(_F32), -jnp.inf)
        bias_first = jnp.where(first_cols, bias, -jnp.inf)

        for r in range(dil):
            for sb in range(per // blk):
                lo = r * per + sb * blk
                if sb == 0:
                    last = slice((r + 1) * per - blk, (r + 1) * per)
                    keys = jnp.concatenate([kprev[head, last, :], k_ref[lo:lo + blk, cols]], axis=0)
                    values = jnp.concatenate([vprev[head, last, :], v_ref[lo:lo + blk, cols]], axis=0)
                else:
                    keys = k_ref[lo - blk:lo + blk, cols]
                    values = v_ref[lo - blk:lo + blk, cols]
                s = _dot_nt(q_ref[lo:lo + blk, cols], keys) * scale + (bias_first if sb == 0 else bias)
                m = jnp.max(s, axis=-1, keepdims=True)
                e = jnp.exp(s - m)
                den = jnp.sum(e, axis=-1, keepdims=True)
                o_ref[hh, token_rows(r, sb), :] = _dot((e / den).astype(_BF16), values)
                prev = jnp.where(head == 0, 0.0, stats[lo:lo + blk, :])
                stats[lo:lo + blk, :] = jnp.where(lane == head, m + jnp.log(den), prev)

        kprev[head] = k_ref[:, cols]
        vprev[head] = v_ref[:, cols]

    @pl.when(hp == HPG // BAND_HEADS - 1)
    def _():
        for r in range(dil):
            for sb in range(per // blk):
                lo = r * per + sb * blk
                lse_ref[token_rows(r, sb), :] = stats[lo:lo + blk, :]


def _band_attention(z3, group):
    b, t, _ = z3.shape
    _, dil = ATT_GROUPS[group]
    tiles = t // ATT_TILE
    steps = HPG // BAND_HEADS
    spec = lambda which: pl.BlockSpec((None, ATT_TILE, BAND_HEADS * ATT_HEAD_DIM),
                                      lambda bi, i, hp: (bi, i, (3 * group + which) * steps + hp))
    carry = pltpu.VMEM((HPG, ATT_TILE, ATT_HEAD_DIM), _BF16)
    return pl.pallas_call(
        functools.partial(_band_attn_kernel, dil=dil, group=group),
        grid=(b, tiles, steps),
        in_specs=[spec(ATT_Q), spec(ATT_K), spec(ATT_V)],
        out_specs=[
            pl.BlockSpec((BAND_HEADS, ATT_TILE, ATT_HEAD_DIM), lambda bi, i, hp: (hp, bi * tiles + i, 0)),
            pl.BlockSpec((ATT_TILE, LSE_LANES), lambda bi, i, hp: (bi * tiles + i, 0)),
        ],
        out_shape=[
            jax.ShapeDtypeStruct((HPG, b * t, ATT_HEAD_DIM), _F32),
            jax.ShapeDtypeStruct((b * t, LSE_LANES), _F32),
        ],
        scratch_shapes=[carry, carry, pltpu.VMEM((ATT_TILE, LSE_LANES), _F32)],
        compiler_params=_params("parallel", "arbitrary", "arbitrary"),
        name="band_attention_g%d" % group,
    )(z3, z3, z3)


def _step_attn_kernel(q_ref, kn_ref, vn_ref, kc_ref, vc_ref, o_ref, lse_ref, ko_ref, vo_ref, *, batches, t_new, **kw):
    for bb in range(batches):
        tokens = pl.ds(bb * t_new, t_new)
        _step_attn_one(q_ref.at[bb], kn_ref.at[bb], vn_ref.at[bb], kc_ref.at[bb], vc_ref.at[bb],
                       o_ref.at[:, tokens, :], lse_ref.at[tokens, :], ko_ref.at[bb], vo_ref.at[bb], t_new=t_new, **kw)


def _step_attn_one(q_ref, kn_ref, vn_ref, kc_ref, vc_ref, o_ref, lse_ref, ko_ref, vo_ref, *,
                   t_new, cache_len, dil, group):
    n_res = min(dil, t_new)
    taps = cache_len // dil
    rows_all = HPG * t_new
    head_cols = [slice(h * ATT_HEAD_DIM, (h + 1) * ATT_HEAD_DIM) for h in range(HPG)]
    head_rows = [slice(h * t_new, (h + 1) * t_new) for h in range(HPG)]
    q = [q_ref[:, c].astype(_BF16) for c in head_cols]
    k_new = kn_ref[...]
    v_new = vn_ref[...]

    def cache_taps(ref, r, h):
        return ref[pl.ds(r * HPG + h, taps, stride=HPG * dil), :].astype(_BF16)

    scale = ATT_HEAD_DIM ** -0.5
    log2_dil = dil.bit_length() - 1
    log2_new = t_new.bit_length() - 1

    def row_terms(width):
        row_id = lax.broadcasted_iota(jnp.int32, (rows_all, width), 0)
        slope = jnp.zeros((rows_all, width), _F32)
        for h in range(HPG):
            slope = jnp.where((row_id >> log2_new) == h, _alibi_slope(group * HPG + h), slope)
        return row_id & (t_new - 1), slope

    query, slope = row_terms(taps)
    residue = query & (dil - 1)
    taps_back = taps + (query >> log2_dil) - lax.broadcasted_iota(jnp.int32, (rows_all, taps), 1)
    s_cache = None
    for r in range(n_res):
        s_r = jnp.concatenate([_dot_nt(q[h], cache_taps(kc_ref, r, h)) for h in range(HPG)], axis=0)
        s_cache = s_r if s_cache is None else jnp.where(residue == r, s_r, s_cache)
    s_cache = s_cache * scale - slope * (taps_back << log2_dil).astype(_F32)
    s_cache = jnp.where(taps_back <= ATT_TAPS, s_cache, -jnp.inf)

    query_n, slope_n = row_terms(t_new)
    back = query_n - lax.broadcasted_iota(jnp.int32, (rows_all, t_new), 1)
    s_new = jnp.concatenate([_dot_nt(q[h], k_new[:, head_cols[h]].astype(_BF16)) for h in range(HPG)], axis=0)
    s_new = s_new * scale - slope_n * back.astype(_F32)
    s_new = jnp.where((back >= 0) & ((back & (dil - 1)) == 0), s_new, -jnp.inf)

    m = jnp.maximum(jnp.max(s_cache, axis=-1, keepdims=True), jnp.max(s_new, axis=-1, keepdims=True))
    e_cache = jnp.exp(s_cache - m)
    e_new = jnp.exp(s_new - m)
    den = jnp.sum(e_cache, axis=-1, keepdims=True) + jnp.sum(e_new, axis=-1, keepdims=True)
    p_cache = e_cache / den
    p_new = (e_new / den).astype(_BF16)
    lse = m + jnp.log(den)

    for h in range(HPG):
        out = _dot(p_new[head_rows[h]], v_new[:, head_cols[h]].astype(_BF16))
        for r in range(n_res):
            p_r = p_cache[head_rows[h]]
            if n_res > 1:
                p_r = jnp.where(residue[head_rows[h]] == r, p_r, 0.0)
            out = out + _dot(p_r.astype(_BF16), cache_taps(vc_ref, r, h))
        o_ref[h] = out
    lse_ref[...] = _pack_head_stats([lse[rows] for rows in head_rows])

    keep = (cache_len - t_new) * HPG
    ko_ref[0:keep, :] = kc_ref[t_new * HPG:, :]
    vo_ref[0:keep, :] = vc_ref[t_new * HPG:, :]
    for h in range(HPG):
        ko_ref[pl.ds(keep + h, t_new, stride=HPG), :] = k_new[:, head_cols[h]]
        vo_ref[pl.ds(keep + h, t_new, stride=HPG), :] = v_new[:, head_cols[h]]


def _step_attention(z3, cache_k, cache_v, prev_k, prev_v, layer, group):
    b, t_new, _ = z3.shape
    cache_len = cache_k.shape[2] // HPG
    _, dil = ATT_GROUPS[group]
    cq, ck, cv = (3 * group + which for which in (ATT_Q, ATT_K, ATT_V))
    batches = max(1, min(b, STEP_CACHE_ROWS // (cache_len * HPG)))
    zspec = lambda c: pl.BlockSpec((batches, t_new, GROUP_W), lambda bi: (bi, 0, c))
    cache_spec = pl.BlockSpec((None, batches, cache_len * HPG, ATT_HEAD_DIM), lambda bi: (layer, bi, 0, 0))
    kernel = functools.partial(_step_attn_kernel, batches=batches, t_new=t_new, cache_len=cache_len, dil=dil,
                               group=group)
    in_specs = [zspec(cq), zspec(ck), zspec(cv), cache_spec, cache_spec]
    args = [z3, z3, z3, cache_k, cache_v]
    aliases = {}
    if prev_k is not None:
        in_specs += [pl.BlockSpec(memory_space=pl.ANY)] * 2
        args += [prev_k, prev_v]
        aliases = {5: 2, 6: 3}
        kernel = functools.partial(_drop_refs, kernel, 5, 2)
    cache_shape = jax.ShapeDtypeStruct(cache_k.shape, cache_k.dtype)
    return pl.pallas_call(
        kernel,
        grid=(b // batches,),
        in_specs=in_specs,
        out_specs=[
            pl.BlockSpec((HPG, batches * t_new, ATT_HEAD_DIM), lambda bi: (0, bi, 0)),
            pl.BlockSpec((batches * t_new, LSE_LANES), lambda bi: (bi, 0)),
            cache_spec, cache_spec,
        ],
        out_shape=[
            jax.ShapeDtypeStruct((HPG, b * t_new, ATT_HEAD_DIM), _F32),
            jax.ShapeDtypeStruct((b * t_new, LSE_LANES), _F32),
            cache_shape, cache_shape,
        ],
        input_output_aliases=aliases,
        compiler_params=_params("parallel"),
        name="step_attention_g%d" % group,
    )(*args)


def _drop_refs(kernel, start, count, *refs):
    return kernel(*refs[:start], *refs[start + count:])


def _mix_ffn_kernel(x_ref, oret_ref, og0_ref, og1_ref, og2_ref, l0_ref, l1_ref, l2_ref, gates_ref,
                    wret_ref, watt_ref, wout_ref, g_ref, wup_ref, wdown_ref, p_ref, wple_ref, wgate_ref, gfin_ref,
                    y_ref, *, final_norm):
    l0, l1, l2 = l0_ref[...], l1_ref[...], l2_ref[...]
    mx = jnp.maximum(jnp.maximum(l0, l1), l2)
    e0, e1, e2 = jnp.exp(l0 - mx), jnp.exp(l1 - mx), jnp.exp(l2 - mx)
    tot = e0 + e1 + e2
    w0, w1, w2 = e0 / tot, e1 / tot, e2 / tot
    heads = []
    for h in range(HPG):
        heads.append(w0[:, h:h + 1] * og0_ref[h] + w1[:, h:h + 1] * og1_ref[h] + w2[:, h:h + 1] * og2_ref[h])
    o_att = jnp.concatenate(heads, axis=-1).astype(_BF16)
    br_ret = _dot(oret_ref[...], wret_ref[...])
    br_att = _dot(o_att, watt_ref[...])
    gate_ret = gates_ref[:, 0:D_MODEL].astype(_F32)
    gate_att = gates_ref[:, D_MODEL:2 * D_MODEL].astype(_F32)
    mix = gate_ret * br_ret + gate_att * br_att
    x = x_ref[...] + _dot(mix.astype(_BF16), wout_ref[...])

    h = _rmsnorm(x, g_ref[...]).astype(_BF16)
    for c in range(D_FF // FF_SLICE):
        cols = slice(c * FF_SLICE, (c + 1) * FF_SLICE)
        u = jnp.maximum(_dot(h, wup_ref[:, cols]), 0.0)
        x = x + _dot((u * u).astype(_BF16), wdown_ref[cols, :])

    gate = jax.nn.sigmoid(_dot(x.astype(_BF16), wgate_ref[...]))
    x = x + _dot(p_ref[...].astype(_BF16), wple_ref[...]) * gate
    if final_norm:
        x = _rmsnorm(x, gfin_ref[...])
    y_ref[...] = x


def _mix_ffn(x, o_ret, ogs, lses, z_mix, p_all, layer, lw, g_final, *, tm, final_norm):
    n, d = x.shape
    row = lambda w: pl.BlockSpec((tm, w), lambda i: (i, 0))
    heads = pl.BlockSpec((HPG, tm, ATT_HEAD_DIM), lambda i: (0, i, 0))
    once = lambda a: pl.BlockSpec(a.shape, lambda i: (0, 0), pipeline_mode=pl.Buffered(1))
    weights = [lw[k] for k in ("w_ret_br", "w_att_br", "w_out", "norm_ffn", "w_up", "w_down")]
    tail = [lw["w_ple"], lw["w_ple_gate"], g_final]
    return pl.pallas_call(
        functools.partial(_mix_ffn_kernel, final_norm=final_norm),
        grid=(n // tm,),
        in_specs=[row(d), row(RET_V_W), heads, heads, heads,
                  row(LSE_LANES), row(LSE_LANES), row(LSE_LANES), row(2 * D_MODEL)]
        + [once(w) for w in weights]
        + [pl.BlockSpec((None, tm, D_PLE), lambda i: (layer, i, 0))]
        + [once(w) for w in tail],
        out_specs=row(d),
        out_shape=jax.ShapeDtypeStruct((n, d), _F32),
        compiler_params=_params("parallel", vmem_limit=VMEM_LIMIT_MIX_FFN),
        name="mix_ffn",
    )(x, o_ret, *ogs, *lses, z_mix, *weights, p_all, *tail)


def _layer_weights(norm_mix, w_in, w_ret_br, w_att_br, w_out, norm_ffn, w_up, w_down, w_ple, w_ple_gate, i):
    w = w_in[i]
    att_start = 2 * RET_QK_W + 2 * RET_V_W
    w_mix = jnp.concatenate([w[:, D_IN - 2 * D_MODEL:], w[:, :att_start]], axis=1)
    return dict(
        norm_mix=norm_mix[i][None, :], w_in_mix=w_mix.astype(_BF16),
        w_in_att_grouped=jnp.concatenate(
            [w[:, att_start + which * ATT_W + g * GROUP_W:att_start + which * ATT_W + (g + 1) * GROUP_W]
             for g in range(N_GROUPS) for which in range(3)], axis=1).astype(_BF16),
        w_ret_br=w_ret_br[i].astype(_BF16), w_att_br=w_att_br[i].astype(_BF16), w_out=w_out[i].astype(_BF16),
        norm_ffn=norm_ffn[i][None, :], w_up=w_up[i].astype(_BF16), w_down=w_down[i].astype(_BF16),
        w_ple=w_ple[i].astype(_BF16), w_ple_gate=w_ple_gate[i].astype(_BF16))


def kernel(x_prompt, x_sample, cache_win_k0, cache_win_v0, cache_win_k1, cache_win_v1, cache_win_k2, cache_win_v2,
           state_ret, p_prompt, p_sample, norm_mix, w_in, w_ret_br, w_att_br, w_out, norm_ffn, w_up, w_down,
           w_ple, w_ple_gate, norm_final):
    depth = w_in.shape[0]
    bp, tp, d = x_prompt.shape
    bs, ts, _ = x_sample.shape
    xp = x_prompt.reshape(bp * tp, d)
    xs = x_sample.reshape(bs * ts, d)
    g_final = norm_final[None, :]
    pp = p_prompt.reshape(depth, bp * tp, D_PLE)
    ps = p_sample.reshape(depth, bs * ts, D_PLE)
    pos_head_rows = lambda c: c.reshape(c.shape[:2] + (c.shape[2] * HPG, ATT_HEAD_DIM))
    caches_k = [pos_head_rows(c) for c in (cache_win_k0, cache_win_k1, cache_win_k2)]
    caches_v = [pos_head_rows(c) for c in (cache_win_v0, cache_win_v1, cache_win_v2)]
    new_k = [None] * N_GROUPS
    new_v = [None] * N_GROUPS
    windows = [None] * N_GROUPS
    prompt_ret = sample_ret = None
    zero_state = jnp.zeros((1, bp, RET_HEADS, RET_DK, RET_DV), _F32)
    sample_chunk = math.gcd(ts, RET_CHUNK)

    for i in range(depth):
        lw = _layer_weights(norm_mix, w_in, w_ret_br, w_att_br, w_out, norm_ffn, w_up, w_down, w_ple, w_ple_gate, i)
        last = i == depth - 1

        z_mix = _norm_proj(xp, lw["norm_mix"], lw["w_in_mix"], _BF16, tm=PROJ_ROWS, tn=PROJ_COLS,
                           gate_cols=2 * D_MODEL)
        z_att = _norm_proj_att(xp, lw["norm_mix"], lw["w_in_att_grouped"]).reshape(bp, tp, 3 * ATT_W)
        o_ret, prompt_ret = _retention(z_mix.reshape(bp, tp, MIX_W), zero_state, 0, prompt_ret, i, depth,
                                       chunk=RET_CHUNK, n_chunks=RET_CHUNKS_PER_STEP)
        ogs, lses = zip(*[_band_attention(z_att, g) for g in range(N_GROUPS)])
        for g, (window, _) in enumerate(ATT_GROUPS):
            windows[g] = _kv_window(xp.reshape(bp, tp, d), lw["norm_mix"], lw["w_in_att_grouped"], g, min(window, tp),
                                    windows[g], i, depth)
        xp = _mix_ffn(xp, o_ret.reshape(bp * tp, RET_V_W), ogs, lses, z_mix, pp, i, lw, g_final,
                      tm=MIX_FFN_ROWS, final_norm=last)

        z_mix = _norm_proj(xs, lw["norm_mix"], lw["w_in_mix"], _BF16, tm=bs * ts, tn=PROJ_COLS,
                           gate_cols=2 * D_MODEL)
        z3 = _norm_proj(xs, lw["norm_mix"], lw["w_in_att_grouped"], _F32, tm=bs * ts, tn=ATT_W).reshape(bs, ts, 3 * ATT_W)
        o_ret, sample_ret = _retention(z_mix.reshape(bs, ts, MIX_W), state_ret, i, sample_ret, i, depth,
                                       chunk=sample_chunk, n_chunks=ts // sample_chunk)
        ogs, lses = [], []
        for g in range(N_GROUPS):
            o_g, lse_g, new_k[g], new_v[g] = _step_attention(z3, caches_k[g], caches_v[g], new_k[g], new_v[g], i, g)
            ogs.append(o_g)
            lses.append(lse_g)
        xs = _mix_ffn(xs, o_ret.reshape(bs * ts, RET_V_W), ogs, lses, z_mix, ps, i, lw, g_final,
                      tm=bs * ts, final_norm=last)

    as_heads = lambda a: a.reshape(a.shape[:2] + (a.shape[2] // HPG, HPG, ATT_HEAD_DIM))
    prompt_windows = [as_heads(a) for kv in windows for a in kv]
    sample_windows = [as_heads(a) for g in range(N_GROUPS) for a in (new_k[g], new_v[g])]
    return (xp.reshape(bp, tp, d), xs.reshape(bs, ts, d), *prompt_windows, prompt_ret, *sample_windows, sample_ret)
```

```python
import functools
import math

import jax
import jax.numpy as jnp
from jax import lax
from jax.experimental import pallas as pl
from jax.experimental.pallas import tpu as pltpu

D_MODEL = 1024
D_PLE = 256
RET_HEADS = 4
RET_DK = 128
RET_DV = 256
RET_CHUNK = 128
ATT_GROUPS = ((128, 1), (512, 4), (2048, 16))
N_GROUPS = 3
HPG = 4
ATT_HEAD_DIM = 128
ATT_HEADS = N_GROUPS * HPG
ATT_TAPS = 128
Q_BLOCK = 128
ATT_TILE = 2048
BAND_HEADS = 4
PROJ_PARTS = 4
NORM_PARTS = 4
FF_SLICE = 2048
SPLIT_STRIDE = 4
D_FF = 4 * D_MODEL
EPS = 1e-6

RET_QK_W = RET_HEADS * RET_DK
RET_V_W = RET_HEADS * RET_DV
GROUP_W = HPG * ATT_HEAD_DIM
ATT_W = ATT_HEADS * ATT_HEAD_DIM
D_IN = 2 * RET_QK_W + 2 * RET_V_W + 3 * ATT_W + 2 * D_MODEL

MIX_W = 2 * D_MODEL + 2 * RET_QK_W + 2 * RET_V_W
COL_RQ = 2 * D_MODEL
COL_RK = COL_RQ + RET_QK_W
COL_RV = COL_RK + RET_QK_W
COL_RG = COL_RV + RET_V_W
ATT_Q, ATT_K, ATT_V = 0, 1, 2
COL_BLOCK = 512

LSE_LANES = 128

PROJ_ROWS = 2048
PROJ_COLS = 1024
MIX_FFN_ROWS = 512
RET_CHUNKS_PER_STEP = 4
STEP_CACHE_ROWS = 8192
RET_ROWS_PER_STEP = 64
VMEM_LIMIT = 48 * 1024 * 1024
VMEM_LIMIT_MIX_FFN = 56 * 1024 * 1024

_BF16 = jnp.bfloat16
_F32 = jnp.float32


def _params(*sem, vmem_limit=VMEM_LIMIT):
    return pltpu.CompilerParams(dimension_semantics=sem, vmem_limit_bytes=vmem_limit)


def _rmsnorm(x, g):
    return x * lax.rsqrt(jnp.mean(x * x, axis=-1, keepdims=True) + EPS) * g


def _dot(a, b):
    return jnp.dot(a, b, preferred_element_type=_F32)


def _dot_nt(a, b):
    return lax.dot_general(a, b, (((1,), (1,)), ((), ())), preferred_element_type=_F32)


def _dot_tn(a, b):
    return lax.dot_general(a, b, (((0,), (0,)), ((), ())), preferred_element_type=_F32)


def _norm_proj_kernel(x_ref, g_ref, w_ref, z_ref, h_ref, *, gate_blocks):
    j = pl.program_id(1)
    tm = x_ref.shape[0]
    part = tm // NORM_PARTS if tm % (NORM_PARTS * 16) == 0 else tm

    def by_parts(with_norm, with_sigmoid):
        for a in range(tm // part):
            rows = slice(a * part, (a + 1) * part)
            if with_norm:
                h_ref[rows, :] = _rmsnorm(x_ref[rows, :], g_ref[...]).astype(_BF16)
            z = _dot(h_ref[rows, :], w_ref[...])
            z_ref[rows, :] = (jax.nn.sigmoid(z) if with_sigmoid else z).astype(z_ref.dtype)

    @pl.when(j == 0)
    def _():
        by_parts(True, gate_blocks > 0)

    if gate_blocks > 1:
        @pl.when((j > 0) & (j < gate_blocks))
        def _():
            by_parts(False, True)

    @pl.when(j >= max(gate_blocks, 1))
    def _():
        z_ref[...] = _dot(h_ref[...], w_ref[...]).astype(z_ref.dtype)


def _norm_proj(x, g, w, out_dtype, *, tm, tn=COL_BLOCK, gate_cols=0):
    n, d = x.shape
    d_out = w.shape[1]
    return pl.pallas_call(
        functools.partial(_norm_proj_kernel, gate_blocks=gate_cols // tn),
        grid=(n // tm, d_out // tn),
        in_specs=[
            pl.BlockSpec((tm, d), lambda i, j: (i, 0)),
            pl.BlockSpec((1, d), lambda i, j: (0, 0)),
            pl.BlockSpec((d, tn), lambda i, j: (0, j)),
        ],
        out_specs=pl.BlockSpec((tm, tn), lambda i, j: (i, j)),
        out_shape=jax.ShapeDtypeStruct((n, d_out), out_dtype),
        scratch_shapes=[pltpu.VMEM((tm, d), _BF16)],
        compiler_params=_params("parallel", "arbitrary"),
        name="norm_proj",
    )(x, g, w)


def _norm_proj_att_kernel(x_ref, g_ref, w_ref, z_ref, h_ref, stage_ref, stage2_ref):
    j = pl.program_id(1)
    part = ATT_TILE // PROJ_PARTS
    for group, (_, dil) in enumerate(ATT_GROUPS):
        @pl.when(j == group)
        def _(group=group, dil=dil):
            if group == 0:
                units = [(a, which) for a in range(PROJ_PARTS) for which in range(3)]
            else:
                units = [(a, which) for which in range(3) for a in range(PROJ_PARTS)]
            for a, which in units:
                rows = slice(a * part, (a + 1) * part)
                col0 = which * GROUP_W
                if group == 0 and which == 0:
                    h_ref[rows, :] = _rmsnorm(x_ref[rows, :], g_ref[...]).astype(_BF16)
                res = _dot(h_ref[rows, :], w_ref[:, col0:col0 + GROUP_W])
                if dil == 1:
                    z_ref[rows, col0:col0 + GROUP_W] = res.astype(_BF16)
                    continue
                per, sub = ATT_TILE // dil, part // dil
                for c in range(HPG):
                    stage_ref[c, rows, :] = res[:, c * ATT_HEAD_DIM:(c + 1) * ATT_HEAD_DIM]
                if dil > SPLIT_STRIDE:
                    seg = part // SPLIT_STRIDE
                    for c in range(HPG):
                        for lo in range(SPLIT_STRIDE):
                            stage2_ref[c, lo * seg:(lo + 1) * seg, :] = (
                                stage_ref[c, pl.ds(a * part + lo, seg, stride=SPLIT_STRIDE), :])
                for r in range(dil):
                    for c in range(HPG):
                        cols = slice(col0 + c * ATT_HEAD_DIM, col0 + (c + 1) * ATT_HEAD_DIM)
                        if dil > SPLIT_STRIDE:
                            lo, hi = r % SPLIT_STRIDE, r // SPLIT_STRIDE
                            piece = stage2_ref[c, pl.ds(lo * seg + hi, sub, stride=dil // SPLIT_STRIDE), :]
                        else:
                            piece = stage_ref[c, pl.ds(a * part + r, sub, stride=dil), :]
                        z_ref[r * per + a * sub:r * per + (a + 1) * sub, cols] = piece.astype(_BF16)


def _norm_proj_att(x, g, w):
    n, d = x.shape
    d_out = w.shape[1]
    return pl.pallas_call(
        _norm_proj_att_kernel,
        grid=(n // ATT_TILE, d_out // (3 * GROUP_W)),
        in_specs=[
            pl.BlockSpec((ATT_TILE, d), lambda i, j: (i, 0)),
            pl.BlockSpec((1, d), lambda i, j: (0, 0)),
            pl.BlockSpec((d, 3 * GROUP_W), lambda i, j: (0, j)),
        ],
        out_specs=pl.BlockSpec((ATT_TILE, 3 * GROUP_W), lambda i, j: (i, j)),
        out_shape=jax.ShapeDtypeStruct((n, d_out), _BF16),
        scratch_shapes=[pltpu.VMEM((ATT_TILE, d), _BF16), pltpu.VMEM((HPG, ATT_TILE, ATT_HEAD_DIM), _F32),
                        pltpu.VMEM((HPG, ATT_TILE // PROJ_PARTS, ATT_HEAD_DIM), _F32)],
        compiler_params=_params("parallel", "arbitrary"),
        name="norm_proj_att",
    )(x, g, w)


def _kv_window_kernel(x_ref, g_ref, wk_ref, wv_ref, ko_ref, vo_ref):
    tm = x_ref.shape[0]
    h = _rmsnorm(x_ref[...], g_ref[...]).astype(_BF16)
    for w_ref, o_ref in ((wk_ref, ko_ref), (wv_ref, vo_ref)):
        res = _dot(h, w_ref[...])
        for hd in range(HPG):
            o_ref[pl.ds(hd, tm, stride=HPG), :] = res[:, hd * ATT_HEAD_DIM:(hd + 1) * ATT_HEAD_DIM]


def _kv_window(x3, g, w_grouped, group, keep, prev, layer, depth):
    b, t, d = x3.shape
    tm = min(keep, 1024)
    first = (t - keep) // tm
    wspec = lambda which: pl.BlockSpec((d, GROUP_W), lambda bi, i: (0, 3 * group + which))
    ospec = pl.BlockSpec((None, None, tm * HPG, ATT_HEAD_DIM), lambda bi, i: (layer, bi, i, 0))
    oshape = jax.ShapeDtypeStruct((depth, b, keep * HPG, ATT_HEAD_DIM), _F32)
    kernel, args, extra_specs, aliases = _kv_window_kernel, [x3, g, w_grouped, w_grouped], [], {}
    if prev is not None:
        extra_specs = [pl.BlockSpec(memory_space=pl.ANY)] * 2
        aliases = {4: 0, 5: 1}
        kernel = functools.partial(_drop_refs, kernel, 4, 2)
        args += list(prev)
    return pl.pallas_call(
        kernel,
        grid=(b, keep // tm),
        in_specs=[pl.BlockSpec((None, tm, d), lambda bi, i: (bi, first + i, 0)),
                  pl.BlockSpec((1, d), lambda bi, i: (0, 0)), wspec(ATT_K), wspec(ATT_V)] + extra_specs,
        out_specs=[ospec, ospec],
        out_shape=[oshape, oshape],
        input_output_aliases=aliases,
        compiler_params=_params("parallel", "parallel"),
        name="kv_window_g%d" % group,
    )(*args)


def _ret_log_gamma():
    return jnp.log1p(-jnp.exp(jnp.linspace(math.log(1.0 / 32), math.log(1.0 / 512), RET_HEADS))).astype(_F32)


def _retention_tables(chunk):
    lg = _ret_log_gamma()
    pos = jnp.arange(chunk, dtype=_F32)
    diff = pos[:, None] - pos[None, :]
    intra = jnp.where(diff[None] >= 0, jnp.exp(lg[:, None, None] * jnp.maximum(diff, 0.0)[None]), 0.0)
    xi = jnp.exp(lg[:, None] * (pos[None] + 1.0))
    zeta = jnp.exp(lg[:, None] * (chunk - 1.0 - pos)[None])
    decay = jnp.exp(lg * chunk)
    return (intra * (RET_DK ** -0.5),
            jnp.broadcast_to(xi[:, :, None], (RET_HEADS, chunk, RET_DK)),
            jnp.broadcast_to(zeta[:, :, None], (RET_HEADS, chunk, RET_DV)),
            jnp.broadcast_to(decay[:, None, None], (RET_HEADS, 8, RET_DV)))


def _retention_kernel(q_ref, k_ref, v_ref, g_ref, s0_ref, intra_ref, xi_ref, zeta_ref, decay_ref,
                      o_ref, sfin_ref, state, *, batches, **kw):
    for bb in range(batches):
        _retention_one(q_ref.at[bb], k_ref.at[bb], v_ref.at[bb], g_ref.at[bb], s0_ref.at[bb], intra_ref, xi_ref,
                       zeta_ref, decay_ref, o_ref.at[bb], sfin_ref.at[bb], state.at[bb], **kw)


def _retention_one(q_ref, k_ref, v_ref, g_ref, s0_ref, intra_ref, xi_ref, zeta_ref, decay_ref,
                   o_ref, sfin_ref, state, *, chunk, n_chunks):
    j = pl.program_id(1)

    @pl.when(j == 0)
    def _():
        state[...] = s0_ref[...]

    for c in range(n_chunks):
        rows = slice(c * chunk, (c + 1) * chunk)
        for h in range(RET_HEADS):
            qk_cols = slice(h * RET_DK, (h + 1) * RET_DK)
            v_cols = slice(h * RET_DV, (h + 1) * RET_DV)
            q = q_ref[rows, qk_cols]
            kb = k_ref[rows, qk_cols]
            v = v_ref[rows, v_cols]
            g = g_ref[rows, v_cols].astype(_F32)
            r_prev = state[h]
            scores = _dot_nt(q, kb) * intra_ref[h]
            o = _dot(scores.astype(_BF16), v)
            o = o + _dot((q * xi_ref[h]).astype(_BF16), r_prev.astype(_BF16))
            u = _dot_tn(kb, (v * zeta_ref[h]).astype(_BF16)) * (RET_DK ** -0.5)
            state[h] = decay_ref[h, 0:1, :] * r_prev + u
            mu = jnp.mean(o, axis=-1, keepdims=True)
            oc = o - mu
            var = jnp.mean(oc * oc, axis=-1, keepdims=True)
            on = oc * lax.rsqrt(var + EPS)
            o_ref[rows, v_cols] = (on * (g * jax.nn.sigmoid(g))).astype(o_ref.dtype)

    @pl.when(j == pl.num_programs(1) - 1)
    def _():
        sfin_ref[...] = state[...]


def _retention(z3, states0, layer0, prev_states, layer, depth, *, chunk, n_chunks):
    b, t, _ = z3.shape
    tc = chunk * n_chunks
    batches = max(1, min(b, RET_ROWS_PER_STEP // tc))
    intra, xi, zeta, decay = _retention_tables(chunk)
    const = lambda shape: pl.BlockSpec(shape, lambda bi, j: (0,) * len(shape))
    state_block = (None, batches, RET_HEADS, RET_DK, RET_DV)
    kernel = functools.partial(_retention_kernel, batches=batches, chunk=chunk, n_chunks=n_chunks)
    args = [z3, z3, z3, z3, states0, intra, xi, zeta, decay]
    extra_specs, aliases = [], {}
    if prev_states is not None:
        extra_specs = [pl.BlockSpec(memory_space=pl.ANY)]
        aliases = {len(args): 1}
        kernel = functools.partial(_drop_refs, kernel, len(args), 1)
        args.append(prev_states)
    return pl.pallas_call(
        kernel,
        grid=(b // batches, t // tc),
        in_specs=[
            pl.BlockSpec((batches, tc, RET_QK_W), lambda bi, j: (bi, j, COL_RQ // RET_QK_W)),
            pl.BlockSpec((batches, tc, RET_QK_W), lambda bi, j: (bi, j, COL_RK // RET_QK_W)),
            pl.BlockSpec((batches, tc, RET_V_W), lambda bi, j: (bi, j, COL_RV // RET_V_W)),
            pl.BlockSpec((batches, tc, RET_V_W), lambda bi, j: (bi, j, COL_RG // RET_V_W)),
            pl.BlockSpec(state_block, lambda bi, j: (layer0, bi, 0, 0, 0)),
            const((RET_HEADS, chunk, chunk)),
            const((RET_HEADS, chunk, RET_DK)),
            const((RET_HEADS, chunk, RET_DV)),
            const((RET_HEADS, 8, RET_DV)),
        ] + extra_specs,
        out_specs=[
            pl.BlockSpec((batches, tc, RET_V_W), lambda bi, j: (bi, j, 0)),
            pl.BlockSpec(state_block, lambda bi, j: (layer, bi, 0, 0, 0)),
        ],
        out_shape=[
            jax.ShapeDtypeStruct((b, t, RET_V_W), _BF16),
            jax.ShapeDtypeStruct((depth, b, RET_HEADS, RET_DK, RET_DV), _F32),
        ],
        scratch_shapes=[pltpu.VMEM((batches, RET_HEADS, RET_DK, RET_DV), _F32)],
        input_output_aliases=aliases,
        compiler_params=_params("parallel", "arbitrary"),
        name="retention",
    )(*args)


def _alibi_slope(head):
    return 2.0 ** (-8.0 * (head + 1.0) / ATT_HEADS)


def _pack_head_stats(cols):
    rows = cols[0].shape[0]
    lane = lax.broadcasted_iota(jnp.int32, (rows, LSE_LANES), 1)
    out = jnp.zeros((rows, LSE_LANES), _F32)
    for h, c in enumerate(cols):
        out = jnp.where(lane == h, c, out)
    return out


def _band_attn_kernel(q_ref, k_ref, v_ref, o_ref, lse_ref, kprev, vprev, stats, *, dil, group):
    i = pl.program_id(1)
    hp = pl.program_id(2)
    blk = Q_BLOCK
    per = ATT_TILE // dil
    row = lax.broadcasted_iota(jnp.int32, (blk, 2 * blk), 0)
    col = lax.broadcasted_iota(jnp.int32, (blk, 2 * blk), 1)
    delta = row + blk - col
    in_band = (delta >= 0) & (delta <= ATT_TAPS)
    first_cols = col >= jnp.where(i > 0, 0, blk)
    lane = lax.broadcasted_iota(jnp.int32, (blk, LSE_LANES), 1)
    scale = ATT_HEAD_DIM ** -0.5

    def token_rows(r, sb):
        start = r + sb * blk * dil
        return pl.ds(start, blk, stride=dil) if dil > 1 else pl.ds(start, blk)

    @pl.when(i == 0)
    def _():
        for hh in range(BAND_HEADS):
            kprev[hp * BAND_HEADS + hh] = jnp.zeros((ATT_TILE, ATT_HEAD_DIM), _BF16)
            vprev[hp * BAND_HEADS + hh] = jnp.zeros((ATT_TILE, ATT_HEAD_DIM), _BF16)

    for hh in range(BAND_HEADS):
        head = hp * BAND_HEADS + hh
        cols = slice(hh * ATT_HEAD_DIM, (hh + 1) * ATT_HEAD_DIM)
        slope = jnp.float32(0.0)
        for p in range(HPG // BAND_HEADS):
            slope = jnp.where(hp == p, jnp.float32(_alibi_slope(group * HPG + p * BAND_HEADS + hh)), slope)
        bias = jnp.where(in_band, -slope * (delta * dil).astype(_F32), -jnp.inf)
        bias_first = jnp.where(first_cols, bias, -jnp.inf)

        for r in range(dil):
            for sb in range(per // blk):
                lo = r * per + sb * blk
                if sb == 0:
                    last = slice((r + 1) * per - blk, (r + 1) * per)
                    keys = jnp.concatenate([kprev[head, last, :], k_ref[lo:lo + blk, cols]], axis=0)
                    values = jnp.concatenate([vprev[head, last, :], v_ref[lo:lo + blk, cols]], axis=0)
                else:
                    keys = k_ref[lo - blk:lo + blk, cols]
                    values = v_ref[lo - blk:lo + blk, cols]
                s = _dot_nt(q_ref[lo:lo + blk, cols], keys) * scale + (bias_first if sb == 0 else bias)
                m = jnp.max(s, axis=-1, keepdims=True)
                e = jnp.exp(s - m)
                den = jnp.sum(e, axis=-1, keepdims=True)
                o_ref[hh, token_rows(r, sb), :] = _dot((e / den).astype(_BF16), values)
                prev = jnp.where(head == 0, 0.0, stats[lo:lo + blk, :])
                stats[lo:lo + blk, :] = jnp.where(lane == head, m + jnp.log(den), prev)

        kprev[head] = k_ref[:, cols]
        vprev[head] = v_ref[:, cols]

    @pl.when(hp == HPG // BAND_HEADS - 1)
    def _():
        for r in range(dil):
            for sb in range(per // blk):
                lo = r * per + sb * blk
                lse_ref[token_rows(r, sb), :] = stats[lo:lo + blk, :]


def _band_attention(z3, group):
    b, t, _ = z3.shape
    _, dil = ATT_GROUPS[group]
    tiles = t // ATT_TILE
    steps = HPG // BAND_HEADS
    spec = lambda which: pl.BlockSpec((None, ATT_TILE, BAND_HEADS * ATT_HEAD_DIM),
                                      lambda bi, i, hp: (bi, i, (3 * group + which) * steps + hp))
    carry = pltpu.VMEM((HPG, ATT_TILE, ATT_HEAD_DIM), _BF16)
    return pl.pallas_call(
        functools.partial(_band_attn_kernel, dil=dil, group=group),
        grid=(b, tiles, steps),
        in_specs=[spec(ATT_Q), spec(ATT_K), spec(ATT_V)],
        out_specs=[
            pl.BlockSpec((BAND_HEADS, ATT_TILE, ATT_HEAD_DIM), lambda bi, i, hp: (hp, bi * tiles + i, 0)),
            pl.BlockSpec((ATT_TILE, LSE_LANES), lambda bi, i, hp: (bi * tiles + i, 0)),
        ],
        out_shape=[
            jax.ShapeDtypeStruct((HPG, b * t, ATT_HEAD_DIM), _F32),
            jax.ShapeDtypeStruct((b * t, LSE_LANES), _F32),
        ],
        scratch_shapes=[carry, carry, pltpu.VMEM((ATT_TILE, LSE_LANES), _F32)],
        compiler_params=_params("parallel", "arbitrary", "arbitrary"),
        name="band_attention_g%d" % group,
    )(z3, z3, z3)


def _step_attn_kernel(q_ref, kn_ref, vn_ref, kc_ref, vc_ref, o_ref, lse_ref, ko_ref, vo_ref, *, batches, t_new, **kw):
    for bb in range(batches):
        tokens = pl.ds(bb * t_new, t_new)
        _step_attn_one(q_ref.at[bb], kn_ref.at[bb], vn_ref.at[bb], kc_ref.at[bb], vc_ref.at[bb],
                       o_ref.at[:, tokens, :], lse_ref.at[tokens, :], ko_ref.at[bb], vo_ref.at[bb], t_new=t_new, **kw)


def _step_attn_one(q_ref, kn_ref, vn_ref, kc_ref, vc_ref, o_ref, lse_ref, ko_ref, vo_ref, *,
                   t_new, cache_len, dil, group):
    n_res = min(dil, t_new)
    taps = cache_len // dil
    rows_all = HPG * t_new
    head_cols = [slice(h * ATT_HEAD_DIM, (h + 1) * ATT_HEAD_DIM) for h in range(HPG)]
    head_rows = [slice(h * t_new, (h + 1) * t_new) for h in range(HPG)]
    q = [q_ref[:, c].astype(_BF16) for c in head_cols]
    k_new = kn_ref[...]
    v_new = vn_ref[...]

    def cache_taps(ref, r, h):
        return ref[pl.ds(r * HPG + h, taps, stride=HPG * dil), :].astype(_BF16)

    scale = ATT_HEAD_DIM ** -0.5
    log2_dil = dil.bit_length() - 1
    log2_new = t_new.bit_length() - 1

    def row_terms(width):
        row_id = lax.broadcasted_iota(jnp.int32, (rows_all, width), 0)
        slope = jnp.zeros((rows_all, width), _F32)
        for h in range(HPG):
            slope = jnp.where((row_id >> log2_new) == h, _alibi_slope(group * HPG + h), slope)
        return row_id & (t_new - 1), slope

    query, slope = row_terms(taps)
    residue = query & (dil - 1)
    taps_back = taps + (query >> log2_dil) - lax.broadcasted_iota(jnp.int32, (rows_all, taps), 1)
    s_cache = None
    for r in range(n_res):
        s_r = jnp.concatenate([_dot_nt(q[h], cache_taps(kc_ref, r, h)) for h in range(HPG)], axis=0)
        s_cache = s_r if s_cache is None else jnp.where(residue == r, s_r, s_cache)
    s_cache = s_cache * scale - slope * (taps_back << log2_dil).astype(_F32)
    s_cache = jnp.where(taps_back <= ATT_TAPS, s_cache, -jnp.inf)

    query_n, slope_n = row_terms(t_new)
    back = query_n - lax.broadcasted_iota(jnp.int32, (rows_all, t_new), 1)
    s_new = jnp.concatenate([_dot_nt(q[h], k_new[:, head_cols[h]].astype(_BF16)) for h in range(HPG)], axis=0)
    s_new = s_new * scale - slope_n * back.astype(_F32)
    s_new = jnp.where((back >= 0) & ((back & (dil - 1)) == 0), s_new, -jnp.inf)

    m = jnp.maximum(jnp.max(s_cache, axis=-1, keepdims=True), jnp.max(s_new, axis=-1, keepdims=True))
    e_cache = jnp.exp(s_cache - m)
    e_new = jnp.exp(s_new - m)
    den = jnp.sum(e_cache, axis=-1, keepdims=True) + jnp.sum(e_new, axis=-1, keepdims=True)
    p_cache = e_cache / den
    p_new = (e_new / den).astype(_BF16)
    lse = m + jnp.log(den)

    for h in range(HPG):
        out = _dot(p_new[head_rows[h]], v_new[:, head_cols[h]].astype(_BF16))
        for r in range(n_res):
            p_r = p_cache[head_rows[h]]
            if n_res > 1:
                p_r = jnp.where(residue[head_rows[h]] == r, p_r, 0.0)
            out = out + _dot(p_r.astype(_BF16), cache_taps(vc_ref, r, h))
        o_ref[h] = out
    lse_ref[...] = _pack_head_stats([lse[rows] for rows in head_rows])

    keep = (cache_len - t_new) * HPG
    ko_ref[0:keep, :] = kc_ref[t_new * HPG:, :]
    vo_ref[0:keep, :] = vc_ref[t_new * HPG:, :]
    for h in range(HPG):
        ko_ref[pl.ds(keep + h, t_new, stride=HPG), :] = k_new[:, head_cols[h]]
        vo_ref[pl.ds(keep + h, t_new, stride=HPG), :] = v_new[:, head_cols[h]]


def _step_attention(z3, cache_k, cache_v, prev_k, prev_v, layer, group):
    b, t_new, _ = z3.shape
    cache_len = cache_k.shape[2] // HPG
    _, dil = ATT_GROUPS[group]
    cq, ck, cv = (3 * group + which for which in (ATT_Q, ATT_K, ATT_V))
    batches = max(1, min(b, STEP_CACHE_ROWS // (cache_len * HPG)))
    zspec = lambda c: pl.BlockSpec((batches, t_new, GROUP_W), lambda bi: (bi, 0, c))
    cache_spec = pl.BlockSpec((None, batches, cache_len * HPG, ATT_HEAD_DIM), lambda bi: (layer, bi, 0, 0))
    kernel = functools.partial(_step_attn_kernel, batches=batches, t_new=t_new, cache_len=cache_len, dil=dil,
                               group=group)
    in_specs = [zspec(cq), zspec(ck), zspec(cv), cache_spec, cache_spec]
    args = [z3, z3, z3, cache_k, cache_v]
    aliases = {}
    if prev_k is not None:
        in_specs += [pl.BlockSpec(memory_space=pl.ANY)] * 2
        args += [prev_k, prev_v]
        aliases = {5: 2, 6: 3}
        kernel = functools.partial(_drop_refs, kernel, 5, 2)
    cache_shape = jax.ShapeDtypeStruct(cache_k.shape, cache_k.dtype)
    return pl.pallas_call(
        kernel,
        grid=(b // batches,),
        in_specs=in_specs,
        out_specs=[
            pl.BlockSpec((HPG, batches * t_new, ATT_HEAD_DIM), lambda bi: (0, bi, 0)),
            pl.BlockSpec((batches * t_new, LSE_LANES), lambda bi: (bi, 0)),
            cache_spec, cache_spec,
        ],
        out_shape=[
            jax.ShapeDtypeStruct((HPG, b * t_new, ATT_HEAD_DIM), _F32),
            jax.ShapeDtypeStruct((b * t_new, LSE_LANES), _F32),
            cache_shape, cache_shape,
        ],
        input_output_aliases=aliases,
        compiler_params=_params("parallel"),
        name="step_attention_g%d" % group,
    )(*args)


def _drop_refs(kernel, start, count, *refs):
    return kernel(*refs[:start], *refs[start + count:])


def _mix_ffn_kernel(x_ref, oret_ref, og0_ref, og1_ref, og2_ref, l0_ref, l1_ref, l2_ref, gates_ref,
                    wret_ref, watt_ref, wout_ref, g_ref, wup_ref, wdown_ref, p_ref, wple_ref, wgate_ref, gfin_ref,
                    y_ref, *, final_norm):
    l0, l1, l2 = l0_ref[...], l1_ref[...], l2_ref[...]
    mx = jnp.maximum(jnp.maximum(l0, l1), l2)
    e0, e1, e2 = jnp.exp(l0 - mx), jnp.exp(l1 - mx), jnp.exp(l2 - mx)
    tot = e0 + e1 + e2
    w0, w1, w2 = e0 / tot, e1 / tot, e2 / tot
    heads = []
    for h in range(HPG):
        heads.append(w0[:, h:h + 1] * og0_ref[h] + w1[:, h:h + 1] * og1_ref[h] + w2[:, h:h + 1] * og2_ref[h])
    o_att = jnp.concatenate(heads, axis=-1).astype(_BF16)
    br_ret = _dot(oret_ref[...], wret_ref[...])
    br_att = _dot(o_att, watt_ref[...])
    gate_ret = gates_ref[:, 0:D_MODEL].astype(_F32)
    gate_att = gates_ref[:, D_MODEL:2 * D_MODEL].astype(_F32)
    mix = gate_ret * br_ret + gate_att * br_att
    x = x_ref[...] + _dot(mix.astype(_BF16), wout_ref[...])

    h = _rmsnorm(x, g_ref[...]).astype(_BF16)
    for c in range(D_FF // FF_SLICE):
        cols = slice(c * FF_SLICE, (c + 1) * FF_SLICE)
        u = jnp.maximum(_dot(h, wup_ref[:, cols]), 0.0)
        x = x + _dot((u * u).astype(_BF16), wdown_ref[cols, :])

    gate = jax.nn.sigmoid(_dot(x.astype(_BF16), wgate_ref[...]))
    x = x + _dot(p_ref[...].astype(_BF16), wple_ref[...]) * gate
    if final_norm:
        x = _rmsnorm(x, gfin_ref[...])
    y_ref[...] = x


def _mix_ffn(x, o_ret, ogs, lses, z_mix, p_all, layer, lw, g_final, *, tm, final_norm):
    n, d = x.shape
    row = lambda w: pl.BlockSpec((tm, w), lambda i: (i, 0))
    heads = pl.BlockSpec((HPG, tm, ATT_HEAD_DIM), lambda i: (0, i, 0))
    once = lambda a: pl.BlockSpec(a.shape, lambda i: (0, 0), pipeline_mode=pl.Buffered(1))
    weights = [lw[k] for k in ("w_ret_br", "w_att_br", "w_out", "norm_ffn", "w_up", "w_down")]
    tail = [lw["w_ple"], lw["w_ple_gate"], g_final]
    return pl.pallas_call(
        functools.partial(_mix_ffn_kernel, final_norm=final_norm),
        grid=(n // tm,),
        in_specs=[row(d), row(RET_V_W), heads, heads, heads,
                  row(LSE_LANES), row(LSE_LANES), row(LSE_LANES), row(2 * D_MODEL)]
        + [once(w) for w in weights]
        + [pl.BlockSpec((None, tm, D_PLE), lambda i: (layer, i, 0))]
        + [once(w) for w in tail],
        out_specs=row(d),
        out_shape=jax.ShapeDtypeStruct((n, d), _F32),
        compiler_params=_params("parallel", vmem_limit=VMEM_LIMIT_MIX_FFN),
        name="mix_ffn",
    )(x, o_ret, *ogs, *lses, z_mix, *weights, p_all, *tail)


def _layer_weights(norm_mix, w_in, w_ret_br, w_att_br, w_out, norm_ffn, w_up, w_down, w_ple, w_ple_gate, i):
    w = w_in[i]
    att_start = 2 * RET_QK_W + 2 * RET_V_W
    w_mix = jnp.concatenate([w[:, D_IN - 2 * D_MODEL:], w[:, :att_start]], axis=1)
    return dict(
        norm_mix=norm_mix[i][None, :], w_in_mix=w_mix.astype(_BF16),
        w_in_att_grouped=jnp.concatenate(
            [w[:, att_start + which * ATT_W + g * GROUP_W:att_start + which * ATT_W + (g + 1) * GROUP_W]
             for g in range(N_GROUPS) for which in range(3)], axis=1).astype(_BF16),
        w_ret_br=w_ret_br[i].astype(_BF16), w_att_br=w_att_br[i].astype(_BF16), w_out=w_out[i].astype(_BF16),
        norm_ffn=norm_ffn[i][None, :], w_up=w_up[i].astype(_BF16), w_down=w_down[i].astype(_BF16),
        w_ple=w_ple[i].astype(_BF16), w_ple_gate=w_ple_gate[i].astype(_BF16))


def kernel(x_prompt, x_sample, cache_win_k0, cache_win_v0, cache_win_k1, cache_win_v1, cache_win_k2, cache_win_v2,
           state_ret, p_prompt, p_sample, norm_mix, w_in, w_ret_br, w_att_br, w_out, norm_ffn, w_up, w_down,
           w_ple, w_ple_gate, norm_final):
    depth = w_in.shape[0]
    bp, tp, d = x_prompt.shape
    bs, ts, _ = x_sample.shape
    xp = x_prompt.reshape(bp * tp, d)
    xs = x_sample.reshape(bs * ts, d)
    g_final = norm_final[None, :]
    pp = p_prompt.reshape(depth, bp * tp, D_PLE)
    ps = p_sample.reshape(depth, bs * ts, D_PLE)
    pos_head_rows = lambda c: c.reshape(c.shape[:2] + (c.shape[2] * HPG, ATT_HEAD_DIM))
    caches_k = [pos_head_rows(c) for c in (cache_win_k0, cache_win_k1, cache_win_k2)]
    caches_v = [pos_head_rows(c) for c in (cache_win_v0, cache_win_v1, cache_win_v2)]
    new_k = [None] * N_GROUPS
    new_v = [None] * N_GROUPS
    windows = [None] * N_GROUPS
    prompt_ret = sample_ret = None
    zero_state = jnp.zeros((1, bp, RET_HEADS, RET_DK, RET_DV), _F32)
    sample_chunk = math.gcd(ts, RET_CHUNK)

    for i in range(depth):
        lw = _layer_weights(norm_mix, w_in, w_ret_br, w_att_br, w_out, norm_ffn, w_up, w_down, w_ple, w_ple_gate, i)
        last = i == depth - 1

        z_mix = _norm_proj(xp, lw["norm_mix"], lw["w_in_mix"], _BF16, tm=PROJ_ROWS, tn=PROJ_COLS,
                           gate_cols=2 * D_MODEL)
        z_att = _norm_proj_att(xp, lw["norm_mix"], lw["w_in_att_grouped"]).reshape(bp, tp, 3 * ATT_W)
        o_ret, prompt_ret = _retention(z_mix.reshape(bp, tp, MIX_W), zero_state, 0, prompt_ret, i, depth,
                                       chunk=RET_CHUNK, n_chunks=RET_CHUNKS_PER_STEP)
        ogs, lses = zip(*[_band_attention(z_att, g) for g in range(N_GROUPS)])
        for g, (window, _) in enumerate(ATT_GROUPS):
            windows[g] = _kv_window(xp.reshape(bp, tp, d), lw["norm_mix"], lw["w_in_att_grouped"], g, min(window, tp),
                                    windows[g], i, depth)
        xp = _mix_ffn(xp, o_ret.reshape(bp * tp, RET_V_W), ogs, lses, z_mix, pp, i, lw, g_final,
                      tm=MIX_FFN_ROWS, final_norm=last)

        z_mix = _norm_proj(xs, lw["norm_mix"], lw["w_in_mix"], _BF16, tm=bs * ts, tn=PROJ_COLS,
                           gate_cols=2 * D_MODEL)
        z3 = _norm_proj(xs, lw["norm_mix"], lw["w_in_att_grouped"], _F32, tm=bs * ts, tn=ATT_W).reshape(bs, ts, 3 * ATT_W)
        o_ret, sample_ret = _retention(z_mix.reshape(bs, ts, MIX_W), state_ret, i, sample_ret, i, depth,
                                       chunk=sample_chunk, n_chunks=ts // sample_chunk)
        ogs, lses = [], []
        for g in range(N_GROUPS):
            o_g, lse_g, new_k[g], new_v[g] = _step_attention(z3, caches_k[g], caches_v[g], new_k[g], new_v[g], i, g)
            ogs.append(o_g)
            lses.append(lse_g)
        xs = _mix_ffn(xs, o_ret.reshape(bs * ts, RET_V_W), ogs, lses, z_mix, ps, i, lw, g_final,
                      tm=bs * ts, final_norm=last)

    as_heads = lambda a: a.reshape(a.shape[:2] + (a.shape[2] // HPG, HPG, ATT_HEAD_DIM))
    prompt_windows = [as_heads(a) for kv in windows for a in kv]
    sample_windows = [as_heads(a) for g in range(N_GROUPS) for a in (new_k[g], new_v[g])]
    return (xp.reshape(bp, tp, d), xs.reshape(bs, ts, d), *prompt_windows, prompt_ret, *sample_windows, sample_ret)
```
